```python
import jax, jax.numpy as jnp
from jax import lax
import numpy as np

D_MODEL = 1024
BATCH = 32
SEQ = 256
DEPTH = 2
DEC_BATCH = 2
DEC_SEQ = 1024
PAST_LEN = 512

GRID_W = 64
POS_BASE = 10000.0
EPS = 1e-6
HG_HEADS = 4
HG_DK = 128
HG_DV = 128
HG_WIDTH = HG_HEADS * HG_DV
HG_CHUNK = 16
SG_GROUPS = 4
SG_WIDTH = 512
SG_GROUP_DIM = SG_WIDTH // SG_GROUPS
SG_CHUNK = 128
POOL_WINDOWS = (2, 4, 8, 16)
POOL_WIDTH = 512
POOL_GROUP_DIM = POOL_WIDTH // len(POOL_WINDOWS)
N_BRANCH = 3
IN_COLS = 5 * HG_WIDTH + 2 * SG_WIDTH + POOL_WIDTH + N_BRANCH * D_MODEL
D_FF = 2816
CONV_WIDTH = 3
N_MOD = 6

kernel_name = "hybrid_hgrn2_sgmlp_pool_diffusion_step"


def rms_norm(x, gain):
    xf = x.astype(jnp.float32)
    y = xf * lax.rsqrt(jnp.mean(xf * xf, axis=-1, keepdims=True) + EPS)
    return (y * gain.astype(jnp.float32)).astype(x.dtype)


def grid_pos_embed(n_tokens, dtype):
    rows = n_tokens // GRID_W
    r = jnp.broadcast_to(jnp.arange(rows, dtype=jnp.float32)[:, None], (rows, GRID_W)).reshape(-1)
    col = jnp.broadcast_to(jnp.arange(GRID_W, dtype=jnp.float32)[None, :], (rows, GRID_W)).reshape(-1)
    quarter = D_MODEL // 4
    omega = 1.0 / (POS_BASE ** (jnp.arange(quarter, dtype=jnp.float32) / quarter))
    ar = r[:, None] * omega[None, :]
    ac = col[:, None] * omega[None, :]
    emb = jnp.concatenate([jnp.sin(ar), jnp.cos(ar), jnp.sin(ac), jnp.cos(ac)], axis=-1)
    return emb.astype(dtype)


def log_forget(z, lb):
    z = z.astype(jnp.float32)
    return jnp.logaddexp(0.0, jnp.log(lb) - z) - jax.nn.softplus(-z)


def gla_chunked(q, k, v, logf, s0):
    B, T, H, K = q.shape
    V = v.shape[-1]
    C = HG_CHUNK
    N = T // C
    q = q.reshape(B, N, C, H, K)
    k = k.reshape(B, N, C, H, K)
    v = v.reshape(B, N, C, H, V)
    b = jnp.cumsum(logf.reshape(B, N, C, H, K), axis=2)
    mask = jnp.tril(jnp.ones((C, C), dtype=bool))
    diff = b[:, :, :, None] - b[:, :, None, :]
    decay = jnp.exp(jnp.where(mask[None, None, :, :, None, None], diff, -jnp.inf))
    scores = jnp.einsum('bnthk,bnshk,bntshk->bnhts', q, k, decay)
    o_intra = jnp.einsum('bnhts,bnshv->bnthv', scores, v)
    b_last = b[:, :, -1]
    k_to_end = k * jnp.exp(b_last[:, :, None] - b)
    update = jnp.einsum('bnshk,bnshv->bnhkv', k_to_end, v)
    chunk_decay = jnp.exp(b_last)

    def step(s, inp):
        a_n, u_n = inp
        return a_n[..., None] * s + u_n, s

    s_fin, s_enter = lax.scan(step, s0.astype(jnp.float32),
                              (jnp.moveaxis(chunk_decay, 1, 0), jnp.moveaxis(update, 1, 0)))
    s_enter = jnp.moveaxis(s_enter, 0, 1)
    o_inter = jnp.einsum('bnthk,bnhkv->bnthv', q * jnp.exp(b), s_enter)
    return (o_intra + o_inter).reshape(B, T, H, V), s_fin


def hgrn2_mixer(zq, zf_fwd, zf_bwd, zi, zg, lb, norm_gain, s0):
    B, T, _ = zq.shape
    q = (jax.nn.silu(zq.astype(jnp.float32)) * HG_DK ** -0.5).reshape(B, T, HG_HEADS, HG_DK)
    v = zi.astype(jnp.float32).reshape(B, T, HG_HEADS, HG_DV)
    outs, finals = [], []
    for d, zf in enumerate((zf_fwd, zf_bwd)):
        logf = log_forget(zf, lb[d]).reshape(B, T, HG_HEADS, HG_DK)
        k = -jnp.expm1(logf)
        if d == 0:
            o, s_fin = gla_chunked(q, k, v, logf, s0[:, d])
        else:
            o, s_fin = gla_chunked(jnp.flip(q, axis=1), jnp.flip(k, axis=1), jnp.flip(v, axis=1),
                                   jnp.flip(logf, axis=1), s0[:, d])
            o = jnp.flip(o, axis=1)
        outs.append(o)
        finals.append(s_fin)
    gate = jax.nn.silu(zg.astype(jnp.float32)).reshape(B, T, HG_HEADS, HG_DV)
    o = rms_norm(outs[0] + outs[1], norm_gain) * gate
    return o.reshape(B, T, HG_WIDTH).astype(zq.dtype), jnp.stack(finals, axis=1)


def chunk_spatial_gating(zu, zv, v_gain, w_s, b_s):
    B, T, _ = zv.shape
    N = T // SG_CHUNK
    u = jax.nn.gelu(zu)
    v = rms_norm(jax.nn.gelu(zv), v_gain).reshape(B, N, SG_CHUNK, SG_GROUPS, SG_GROUP_DIM)
    mixed = jnp.einsum('gts,bnsgc->bntgc', w_s, v) + b_s.T[None, None, :, :, None]
    return u * mixed.reshape(B, T, SG_WIDTH).astype(u.dtype)


def multiscale_pool(zp, w_pool, scale):
    B, T, _ = zp.shape
    pf = zp.astype(jnp.float32)
    csum = jnp.concatenate([jnp.zeros((B, 1, POOL_WIDTH), jnp.float32), jnp.cumsum(pf, axis=1)], axis=1)
    t = jnp.arange(T)
    groups = []
    for gi, w in enumerate(POOL_WINDOWS):
        lo = jnp.clip(t - w // 2, 0, T)
        hi = jnp.clip(t + w // 2, 0, T)
        sl = slice(gi * POOL_GROUP_DIM, (gi + 1) * POOL_GROUP_DIM)
        cs = csum[..., sl]
        mean = (cs[:, hi] - cs[:, lo]) / (hi - lo).astype(jnp.float32)[None, :, None]
        groups.append(mean - pf[..., sl])
    pooled = jnp.stack(groups, axis=2)
    out = jnp.einsum('btgc,gcd->btgd', pooled, w_pool.astype(jnp.float32)).reshape(B, T, POOL_WIDTH)
    return (out * scale.astype(jnp.float32)).astype(zp.dtype)


def conv_ffn(x, w_up, conv_w, conv_b, w_down):
    h = x @ w_up
    hp = jnp.pad(h, ((0, 0), (1, 1), (0, 0)))
    h = hp[:, :-2] * conv_w[0] + hp[:, 1:-1] * conv_w[1] + hp[:, 2:] * conv_w[2] + conv_b
    a, b = jnp.split(h, 2, axis=-1)
    return (jax.nn.silu(a) * b) @ w_down


def trunk_layer(x, mod, s0, lb, p):
    shift1, scale1, gate1, shift2, scale2, gate2 = jnp.split(mod[:, None, :].astype(x.dtype), N_MOD, axis=-1)
    h = rms_norm(x, p['norm_mix']) * (1 + scale1) + shift1
    z = h @ p['w_in']
    widths = (HG_WIDTH,) * 5 + (SG_WIDTH,) * 2 + (POOL_WIDTH,)
    offsets = np.cumsum(widths).tolist()
    zq, zf_f, zf_b, zi, zg, zu, zv, zp, zgate = jnp.split(z, offsets, axis=-1)
    o_hg, s_fin = hgrn2_mixer(zq, zf_f, zf_b, zi, zg, lb, p['hg_norm'], s0)
    o_sg = chunk_spatial_gating(zu, zv, p['sg_norm'], p['sg_w'], p['sg_b'])
    o_pool = multiscale_pool(zp, p['pool_w'], p['pool_scale'])
    gates = jax.nn.sigmoid(zgate.astype(jnp.float32)).astype(x.dtype)
    g_hg, g_sg, g_pool = jnp.split(gates, N_BRANCH, axis=-1)
    merged = (g_hg * (o_hg @ p['w_branch_hg']) + g_sg * (o_sg @ p['w_branch_sg'])
              + g_pool * (o_pool @ p['w_branch_pool']))
    x = x + gate1 * (merged @ p['w_out'])
    h = rms_norm(x, p['norm_ffn']) * (1 + scale2) + shift2
    x = x + gate2 * conv_ffn(h, p['ffn_up'], p['ffn_conv_w'], p['ffn_conv_b'], p['ffn_down'])
    return x, s_fin


def setup_inputs(seed: int = 0) -> dict:
    key = jax.random.key(seed)
    ks = jax.random.split(key, 32)
    f32 = jnp.float32
    nrm = lambda k, shape, s: jax.random.normal(k, shape, f32) * s
    D = D_MODEL
    return {
        'x_prompt': nrm(ks[0], (BATCH, SEQ, D), 1.0),
        'x_sample': nrm(ks[1], (DEC_BATCH, DEC_SEQ, D), 1.0),
        'c': nrm(ks[2], (DEC_BATCH, D), 1.0),
        'state_hgrn': nrm(ks[3], (DEC_BATCH, DEPTH, 2, HG_HEADS, HG_DK, HG_DV), 0.5),
        'c_ctx': nrm(ks[4], (D,), 1.0),
        'norm_mix': 1.0 + nrm(ks[5], (DEPTH, D), 0.05),
        'norm_ffn': 1.0 + nrm(ks[6], (DEPTH, D), 0.05),
        'w_ada': nrm(ks[7], (DEPTH, D, N_MOD * D), 0.5 * D ** -0.5),
        'b_ada': nrm(ks[8], (DEPTH, N_MOD * D), 0.02),
        'w_in': nrm(ks[9], (DEPTH, D, IN_COLS), D ** -0.5),
        'lb_logits': nrm(ks[10], (DEPTH, 2, HG_WIDTH), 0.5),
        'hg_norm': 1.0 + nrm(ks[11], (DEPTH, HG_DV), 0.05),
        'w_branch_hg': nrm(ks[12], (DEPTH, HG_WIDTH, D), HG_WIDTH ** -0.5),
        'w_branch_sg': nrm(ks[13], (DEPTH, SG_WIDTH, D), SG_WIDTH ** -0.5),
        'w_branch_pool': nrm(ks[14], (DEPTH, POOL_WIDTH, D), POOL_WIDTH ** -0.5),
        'w_out': nrm(ks[15], (DEPTH, D, D), D ** -0.5),
        'sg_norm': 1.0 + nrm(ks[16], (DEPTH, SG_WIDTH), 0.05),
        'sg_w': nrm(ks[17], (DEPTH, SG_GROUPS, SG_CHUNK, SG_CHUNK), SG_CHUNK ** -0.5),
        'sg_b': 1.0 + nrm(ks[18], (DEPTH, SG_GROUPS, SG_CHUNK), 0.1),
        'pool_w': nrm(ks[19], (DEPTH, len(POOL_WINDOWS), POOL_GROUP_DIM, POOL_GROUP_DIM), POOL_GROUP_DIM ** -0.5),
        'pool_scale': 1.0 + nrm(ks[20], (DEPTH, POOL_WIDTH), 0.1),
        'ffn_up': nrm(ks[21], (DEPTH, D, 2 * D_FF), D ** -0.5),
        'ffn_conv_w': nrm(ks[22], (DEPTH, CONV_WIDTH, 2 * D_FF), CONV_WIDTH ** -0.5),
        'ffn_conv_b': nrm(ks[23], (DEPTH, 2 * D_FF), 0.02),
        'ffn_down': nrm(ks[24], (DEPTH, D_FF, D), D_FF ** -0.5),
        'final_norm': 1.0 + nrm(ks[25], (D,), 0.05),
    }


def reference(x_prompt, x_sample, c, state_hgrn, c_ctx, norm_mix, norm_ffn, w_ada, b_ada, w_in,
              lb_logits, hg_norm, w_branch_hg, w_branch_sg, w_branch_pool, w_out, sg_norm, sg_w, sg_b,
              pool_w, pool_scale, ffn_up, ffn_conv_w, ffn_conv_b, ffn_down, final_norm):
    lb_all = jnp.cumsum(jax.nn.softmax(lb_logits.astype(jnp.float32), axis=0), axis=0)
    lower = lb_all - lb_all[0]
    xp = x_prompt
    xs = x_sample + grid_pos_embed(x_sample.shape[1], x_sample.dtype)[None]
    n_ctx = x_prompt.shape[0]
    new_states = []
    for l in range(DEPTH):
        p = {'norm_mix': norm_mix[l], 'norm_ffn': norm_ffn[l], 'w_in': w_in[l], 'hg_norm': hg_norm[l],
             'w_branch_hg': w_branch_hg[l], 'w_branch_sg': w_branch_sg[l], 'w_branch_pool': w_branch_pool[l],
             'w_out': w_out[l], 'sg_norm': sg_norm[l], 'sg_w': sg_w[l], 'sg_b': sg_b[l],
             'pool_w': pool_w[l], 'pool_scale': pool_scale[l], 'ffn_up': ffn_up[l],
             'ffn_conv_w': ffn_conv_w[l], 'ffn_conv_b': ffn_conv_b[l], 'ffn_down': ffn_down[l]}
        mod_ctx = jax.nn.silu(c_ctx)[None, :] @ w_ada[l] + b_ada[l]
        mod_lat = jax.nn.silu(c) @ w_ada[l] + b_ada[l]
        s_zero = jnp.zeros((n_ctx, 2, HG_HEADS, HG_DK, HG_DV), jnp.float32)
        xp, s_ctx = trunk_layer(xp, mod_ctx, s_zero, lower[l], p)
        xs, _ = trunk_layer(xs, mod_lat, state_hgrn[:, l], lower[l], p)
        new_states.append(s_ctx)
    y_prompt = rms_norm(xp, final_norm)
    y_sample = rms_norm(xs, final_norm)
    new_state_hgrn = jnp.stack(new_states, axis=1).astype(x_prompt.dtype)
    return (y_prompt, y_sample, new_state_hgrn)
```

```python
import functools

import jax
import jax.numpy as jnp
from jax import lax
from jax.experimental import pallas as pl
from jax.experimental.pallas import tpu as pltpu

D_MODEL = 1024
DEPTH = 2
GRID_W = 64
POS_BASE = 10000.0
EPS = 1e-6
HG_HEADS = 4
HG_DK = 128
HG_DV = 128
HG_WIDTH = HG_HEADS * HG_DV
SG_GROUPS = 4
SG_WIDTH = 512
SG_GROUP_DIM = SG_WIDTH // SG_GROUPS
SG_CHUNK = 128
POOL_WINDOWS = (2, 4, 8, 16)
POOL_WIDTH = 512
POOL_GROUP_DIM = POOL_WIDTH // len(POOL_WINDOWS)
IN_COLS = 5 * HG_WIDTH + 2 * SG_WIDTH + POOL_WIDTH + 3 * D_MODEL
D_FF = 2816
N_MOD = 6

V7X_LANES = 128
V7X_SUBLANES = 8
V7X_MXU_DIM = 256
MIB = 2**20

ROWS = 1024
HG_BLOCK = 256
HG_LEVELS = 8
IN_CHUNK = 512
FF_CHUNK = V7X_MXU_DIM
MOD_CHUNK = 1536

F32 = jnp.float32
BF16 = jnp.bfloat16


def _dot(a, b):
    return lax.dot_general(a, b, (((1,), (0,)), ((), ())), preferred_element_type=F32)


def _dot_nt(a, b):
    return lax.dot_general(a, b, (((1,), (1,)), ((), ())), preferred_element_type=F32)


def _dot_tn(a, b):
    return lax.dot_general(a, b, (((0,), (0,)), ((), ())), preferred_element_type=F32)


def _silu(x):
    return x * jax.nn.sigmoid(x)


def _rms(x, gain):
    return x * lax.rsqrt(jnp.mean(x * x, axis=-1, keepdims=True) + EPS) * gain


def _log1pexp(y):
    return jnp.maximum(y, 0.0) + jnp.log1p(jnp.exp(-jnp.abs(y)))


def _norm_mod(x, gain, scale, shift):
    return (_rms(x, gain) * (1.0 + scale) + shift).astype(BF16)


def _mod_kernel(c_ref, w_ref, b_ref, o_ref):
    c = _silu(c_ref[...]).astype(BF16)
    o_ref[...] = _dot(c, w_ref[...].astype(BF16)) + b_ref[...]


def _mod_call(cvec, w_ada, b_ada):
    n_rows = cvec.shape[0]
    n_cols = N_MOD * D_MODEL
    return pl.pallas_call(
        _mod_kernel,
        grid=(DEPTH, n_cols // MOD_CHUNK),
        in_specs=[
            pl.BlockSpec((n_rows, D_MODEL), lambda l, n: (0, 0)),
            pl.BlockSpec((None, D_MODEL, MOD_CHUNK), lambda l, n: (l, 0, n)),
            pl.BlockSpec((None, 1, MOD_CHUNK), lambda l, n: (l, 0, n)),
        ],
        out_specs=pl.BlockSpec((None, n_rows, MOD_CHUNK), lambda l, n: (l, 0, n)),
        out_shape=jax.ShapeDtypeStruct((DEPTH, n_rows, n_cols), F32),
        compiler_params=pltpu.CompilerParams(
            dimension_semantics=("arbitrary", "arbitrary"), vmem_limit_bytes=32 * MIB
        ),
        name="adaln_mod",
    )(cvec, w_ada, b_ada.reshape(DEPTH, 1, n_cols))


def _addpos_kernel(x_ref, p_ref, o_ref):
    o_ref[...] = x_ref[...] + p_ref[...]


def _addpos_call(x, pos):
    b, t, d = x.shape
    return pl.pallas_call(
        _addpos_kernel,
        grid=(b,),
        in_specs=[pl.BlockSpec((None, t, d), lambda i: (i, 0, 0)), pl.BlockSpec((t, d), lambda i: (0, 0))],
        out_specs=pl.BlockSpec((None, t, d), lambda i: (i, 0, 0)),
        out_shape=jax.ShapeDtypeStruct(x.shape, x.dtype),
        compiler_params=pltpu.CompilerParams(dimension_semantics=("arbitrary",), vmem_limit_bytes=32 * MIB),
        name="add_pos",
    )(x, pos)


def _grid_pos_embed(n_tokens):
    rows = n_tokens // GRID_W
    r = jnp.broadcast_to(jnp.arange(rows, dtype=F32)[:, None], (rows, GRID_W)).reshape(-1)
    col = jnp.broadcast_to(jnp.arange(GRID_W, dtype=F32)[None, :], (rows, GRID_W)).reshape(-1)
    quarter = D_MODEL // 4
    omega = 1.0 / (POS_BASE ** (jnp.arange(quarter, dtype=F32) / quarter))
    ar = r[:, None] * omega[None, :]
    ac = col[:, None] * omega[None, :]
    return jnp.concatenate([jnp.sin(ar), jnp.cos(ar), jnp.sin(ac), jnp.cos(ac)], axis=-1)


def _ref_rows(b, blk, r):
    n, c = b.shape
    if blk >= V7X_SUBLANES:
        x3 = b.reshape(n // blk, blk, c)
        return jnp.broadcast_to(x3[:, r : r + 1, :], x3.shape).reshape(n, c)
    x3 = b.reshape(n // V7X_SUBLANES, V7X_SUBLANES, c)
    sub = lax.broadcasted_iota(jnp.int32, x3.shape, 1)
    bases = list(range(0, V7X_SUBLANES, blk))
    out = jnp.broadcast_to(x3[:, bases[-1] + r : bases[-1] + r + 1, :], x3.shape)
    for base in reversed(bases[:-1]):
        out = jnp.where(sub < base + blk, jnp.broadcast_to(x3[:, base + r : base + r + 1, :], x3.shape), out)
    return out.reshape(n, c)


def _cum_logdecay(lf, forward):
    n = lf.shape[0]
    row = lax.broadcasted_iota(jnp.int32, lf.shape, 0)
    b = lf
    sh = 1
    while sh < n:
        if forward:
            b = b + jnp.where(row >= sh, pltpu.roll(b, sh, 0), 0.0)
        else:
            b = b + jnp.where(row < n - sh, pltpu.roll(b, n - sh, 0), 0.0)
        sh *= 2
    return b


def _hgrn_block(q, k, v, lf, lv, forward, st):
    b = _cum_logdecay(lf, forward)
    vb = v.astype(BF16)
    a = jnp.where(lv == 0, _dot_nt(q.astype(BF16), k.astype(BF16)), 0.0)
    for m in range(1, HG_LEVELS + 1):
        blk = 2**m
        ref = _ref_rows(b, blk, blk // 2 - 1 if forward else blk // 2)
        qt = (q * jnp.exp(jnp.minimum(b - ref, 0.0))).astype(BF16)
        kt = (k * jnp.exp(jnp.minimum(ref - b, 0.0))).astype(BF16)
        a = jnp.where(lv == m, _dot_nt(qt, kt), a)
    o = _dot(a.astype(BF16), vb)
    edge = b[HG_BLOCK - 1 : HG_BLOCK, :] if forward else b[0:1, :]
    k_end = (k * jnp.exp(edge - b)).astype(BF16)
    st_new = _dot_tn(vb, k_end)
    if st is not None:
        o = o + _dot_nt((q * jnp.exp(b)).astype(BF16), st.astype(BF16))
        st_new = st_new + st * jnp.exp(edge)
    return o, st_new


def _hgrn_kernel(*refs, layer, seq_len):
    carry = seq_len > HG_BLOCK
    it = iter(refs)
    x_ref, mod_ref, nmix_ref = next(it), next(it), next(it)
    w_refs = [next(it) for _ in range(5)]
    lbl_ref, hgn_ref, wbr_ref = next(it), next(it), next(it)
    s0_refs = [next(it), next(it)] if carry else None
    phg_ref = next(it)
    sout_ref = None if carry else next(it)
    hb, wcat, q_s, v_s, g_s, kf_s, kb_s, lff_s, lfb_s, o_s, lvf_s, lvb_s, st_s = it

    j = pl.program_id(1)

    @pl.when(j == 0)
    def _():
        hb[...] = _norm_mod(x_ref[...], nmix_ref[...], mod_ref[0, 1:2, :], mod_ref[0, 0:1, :])
        t = lax.broadcasted_iota(jnp.int32, (HG_BLOCK, HG_BLOCK), 0)
        s = lax.broadcasted_iota(jnp.int32, (HG_BLOCK, HG_BLOCK), 1)
        x = t ^ s
        lv = jnp.zeros_like(x)
        for m in range(HG_LEVELS):
            lv = lv + (x >= 2**m).astype(jnp.int32)
        lvf_s[...] = jnp.where(t >= s, lv, -1)
        lvb_s[...] = jnp.where(t <= s, lv, -1)

    for g, w_ref in enumerate(w_refs):
        wcat[:, g * HG_DK : (g + 1) * HG_DK] = w_ref[...].astype(BF16)
    z = _dot(hb[...], wcat[...])
    zq, zff, zfb, zi, zg = (z[:, g * HG_DK : (g + 1) * HG_DK] for g in range(5))
    q_s[...] = _silu(zq) * HG_DK**-0.5
    v_s[...] = zi
    g_s[...] = _silu(zg)

    a0, a1 = lbl_ref[0], lbl_ref[1]
    amax = jnp.maximum(a0, a1)
    e0, e1 = jnp.exp(a0 - amax), jnp.exp(a1 - amax)
    p0, p1 = e0 / (e0 + e1), e1 / (e0 + e1)
    lb = (p0 - p0) if layer == 0 else ((p0 + p1) - p0)
    for d, (zf, k_s, lf_s) in enumerate(((zff, kf_s, lff_s), (zfb, kb_s, lfb_s))):
        lbd = lb[d : d + 1, :]
        lf_s[...] = _log1pexp(jnp.log(lbd) - zf) - _log1pexp(-zf)
        k_s[...] = (1.0 - lbd) * jax.nn.sigmoid(-zf)

    o_s[...] = jnp.zeros_like(o_s)
    n_blk = ROWS // HG_BLOCK
    if carry:
        for d in range(2):
            st_s[d] = s0_refs[d][0].T

    def blocks(n, c):
        for d in range(2):
            forward = d == 0
            blk = n if forward else n_blk - 1 - n
            rows = pl.ds(pl.multiple_of(blk * HG_BLOCK, HG_BLOCK), HG_BLOCK)
            k_s, lf_s, lv_s = (kf_s, lff_s, lvf_s) if forward else (kb_s, lfb_s, lvb_s)
            o, st_new = _hgrn_block(
                q_s[rows, :], k_s[rows, :], v_s[rows, :], lf_s[rows, :], lv_s[...], forward, st_s[d] if carry else None
            )
            o_s[rows, :] += o
            if carry:
                st_s[d] = st_new
            else:
                sout_ref[blk, d, 0] = st_new.T
        return c

    lax.fori_loop(0, n_blk, blocks, 0)

    y = _rms(o_s[...], hgn_ref[...]) * g_s[...]
    contrib = _dot(y.astype(BF16), wbr_ref[...].astype(BF16))

    @pl.when(j == 0)
    def _():
        phg_ref[...] = contrib

    @pl.when(j > 0)
    def _():
        phg_ref[...] += contrib


def _hgrn_call(x, mod, seq_len, layer, norm_mix, w_in, lb_logits, hg_norm, w_branch_hg, state0):
    n_tok = x.shape[0]
    nb = n_tok // ROWS
    carry = seq_len > HG_BLOCK
    per_seq_mod = mod.shape[0] > 1
    const = pl.Buffered(1)

    in_specs = [
        pl.BlockSpec((ROWS, D_MODEL), lambda i, j: (i, 0), pipeline_mode=const),
        pl.BlockSpec((1, N_MOD, D_MODEL), (lambda i, j: (i, 0, 0)) if per_seq_mod else (lambda i, j: (0, 0, 0))),
        pl.BlockSpec((None, 1, D_MODEL), lambda i, j: (layer, 0, 0)),
    ]
    args = [x, mod, norm_mix.reshape(DEPTH, 1, D_MODEL)]
    for g in range(5):
        in_specs.append(pl.BlockSpec((None, D_MODEL, HG_DK), lambda i, j, g=g: (layer, 0, g * HG_HEADS + j)))
        args.append(w_in)
    in_specs += [
        pl.BlockSpec((DEPTH, 2, HG_DK), lambda i, j: (0, 0, j)),
        pl.BlockSpec((None, 1, HG_DV), lambda i, j: (layer, 0, 0)),
        pl.BlockSpec((None, HG_DV, D_MODEL), lambda i, j: (layer, j, 0)),
    ]
    args += [lb_logits, hg_norm.reshape(DEPTH, 1, HG_DV), w_branch_hg]
    if carry:
        assert seq_len == ROWS
        for d in range(2):
            in_specs.append(
                pl.BlockSpec((1, HG_DK, HG_DV), lambda i, j, d=d: (((i * DEPTH + layer) * 2 + d) * HG_HEADS + j, 0, 0))
            )
            args.append(state0)

    out_shape = [jax.ShapeDtypeStruct((n_tok, D_MODEL), F32)]
    out_specs = [pl.BlockSpec((ROWS, D_MODEL), lambda i, j: (i, 0))]
    if not carry:
        assert seq_len == HG_BLOCK
        n_seq = n_tok // seq_len
        out_shape.append(jax.ShapeDtypeStruct((n_seq, 2, HG_HEADS, HG_DK, HG_DV), F32))
        out_specs.append(pl.BlockSpec((ROWS // seq_len, 2, 1, HG_DK, HG_DV), lambda i, j: (i, 0, j, 0, 0)))

    head = lambda dt=F32: pltpu.VMEM((ROWS, HG_DK), dt)
    scratch = [
        pltpu.VMEM((ROWS, D_MODEL), BF16),
        pltpu.VMEM((D_MODEL, 5 * HG_DK), BF16),
        head(), head(), head(),
        head(), head(), head(), head(),
        head(),
        pltpu.VMEM((HG_BLOCK, HG_BLOCK), jnp.int32),
        pltpu.VMEM((HG_BLOCK, HG_BLOCK), jnp.int32),
        pltpu.VMEM((2, HG_DV, HG_DK), F32),
    ]
    outs = pl.pallas_call(
        functools.partial(_hgrn_kernel, layer=layer, seq_len=seq_len),
        grid=(nb, HG_HEADS),
        in_specs=in_specs,
        out_specs=out_specs,
        out_shape=out_shape,
        scratch_shapes=scratch,
        compiler_params=pltpu.CompilerParams(
            dimension_semantics=("arbitrary", "arbitrary"), vmem_limit_bytes=48 * MIB
        ),
        name=f"hgrn_l{layer}_t{seq_len}",
    )(*args)
    return (outs[0], None) if carry else (outs[0], outs[1])


_MIX_ORDER = (10, 11, 5, 6, 12, 13, 7, 8, 9)


def _mix_col(k):
    idx = 0
    for n, c in enumerate(_MIX_ORDER):
        idx = idx + jnp.where(k == n, c, 0)
    return idx


def _window_mean_minus_self(p, tpos, seq_len, w):
    n = p.shape[0]
    acc = jnp.zeros_like(p)
    for jj in range(-(w // 2), w // 2):
        shifted = p if jj == 0 else pltpu.roll(p, (-jj) % n, 0)
        valid = (tpos + jj >= 0) & (tpos + jj < seq_len)
        acc = acc + jnp.where(valid, shifted, 0.0)
    cnt = jnp.minimum(tpos + w // 2, seq_len) - jnp.maximum(tpos - w // 2, 0)
    return acc / cnt.astype(F32) - p


def _mix_kernel(
    x_ref, mod_ref, nmix_ref, w_ref, sgn_ref, sgw_ref, sgb_ref, wbsg_ref, wbpool_ref, poolw_ref, pscale_ref,
    wout_ref, phg_ref, o_ref, hb, u_s, br_s, mrg_s, g2_s, *, seq_len,
):
    k = pl.program_id(1)
    half = D_MODEL // 2

    @pl.when(k == 0)
    def _():
        hb[...] = _norm_mod(x_ref[...], nmix_ref[...], mod_ref[0, 1:2, :], mod_ref[0, 0:1, :])

    z = _dot(hb[...], w_ref[...].astype(BF16))

    for step in (0, 1):

        @pl.when(k == step)
        def _(step=step):
            mrg_s[:, step * half : (step + 1) * half] = jax.nn.sigmoid(z)

    @pl.when(k == 2)
    def _():
        u_s[...] = jax.nn.gelu(z)

    @pl.when(k == 3)
    def _():
        v = _rms(jax.nn.gelu(z), sgn_ref[...]).astype(BF16)
        for g in range(SG_GROUPS):
            wg = sgw_ref[g].astype(BF16)
            bias = sgb_ref[:, g : g + 1]
            cols = slice(g * SG_GROUP_DIM, (g + 1) * SG_GROUP_DIM)
            for n in range(ROWS // SG_CHUNK):
                rows = slice(n * SG_CHUNK, (n + 1) * SG_CHUNK)
                mixed = _dot(wg, v[rows, cols]) + bias
                br_s[rows, cols] = (u_s[rows, cols] * mixed).astype(BF16)
        mrg_s[...] = mrg_s[...] * _dot(br_s[...], wbsg_ref[...].astype(BF16))

    for step in (4, 5):

        @pl.when(k == step)
        def _(step=step):
            g2_s[:, (step - 4) * half : (step - 3) * half] = jax.nn.sigmoid(z)

    @pl.when(k == 6)
    def _():
        tpos = lax.broadcasted_iota(jnp.int32, (ROWS, POOL_GROUP_DIM), 0) & (seq_len - 1)
        for gi, w in enumerate(POOL_WINDOWS):
            cols = slice(gi * POOL_GROUP_DIM, (gi + 1) * POOL_GROUP_DIM)
            pooled = _window_mean_minus_self(z[:, cols], tpos, seq_len, w)
            out = _dot(pooled.astype(BF16), poolw_ref[gi].astype(BF16)) * pscale_ref[:, cols]
            br_s[:, cols] = out.astype(BF16)
        mrg_s[...] = mrg_s[...] + g2_s[...] * _dot(br_s[...], wbpool_ref[...].astype(BF16))

    for step in (7, 8):

        @pl.when(k == step)
        def _(step=step):
            cols = slice((step - 7) * half, (step - 6) * half)
            mrg_s[:, cols] = mrg_s[:, cols] + jax.nn.sigmoid(z) * phg_ref[:, cols]

    @pl.when(k == 8)
    def _():
        y = _dot(mrg_s[...].astype(BF16), wout_ref[...].astype(BF16))
        o_ref[...] = x_ref[...] + mod_ref[0, 2:3, :] * y


def _mix_call(x, phg, mod, seq_len, layer, norm_mix, w_in, sg_norm, sg_w, sg_b, w_branch_sg, w_branch_pool, pool_w,
              pool_scale, w_out):
    n_tok = x.shape[0]
    nb = n_tok // ROWS
    per_seq_mod = mod.shape[0] > 1
    const = pl.Buffered(1)
    assert seq_len & (seq_len - 1) == 0 and ROWS % seq_len == 0 and seq_len % SG_CHUNK == 0
    in_specs = [
        pl.BlockSpec((ROWS, D_MODEL), lambda i, k: (i, 0), pipeline_mode=const),
        pl.BlockSpec((1, N_MOD, D_MODEL), (lambda i, k: (i, 0, 0)) if per_seq_mod else (lambda i, k: (0, 0, 0))),
        pl.BlockSpec((None, 1, D_MODEL), lambda i, k: (layer, 0, 0)),
        pl.BlockSpec((None, D_MODEL, IN_CHUNK), lambda i, k: (layer, 0, _mix_col(k))),
        pl.BlockSpec((None, 1, SG_WIDTH), lambda i, k: (layer, 0, 0)),
        pl.BlockSpec((None, SG_GROUPS, SG_CHUNK, SG_CHUNK), lambda i, k: (layer, 0, 0, 0)),
        pl.BlockSpec((None, SG_CHUNK, SG_GROUPS), lambda i, k: (layer, 0, 0)),
        pl.BlockSpec((None, SG_WIDTH, D_MODEL), lambda i, k: (layer, 0, 0), pipeline_mode=const),
        pl.BlockSpec((None, POOL_WIDTH, D_MODEL), lambda i, k: (layer, 0, 0), pipeline_mode=const),
        pl.BlockSpec((None, len(POOL_WINDOWS), POOL_GROUP_DIM, POOL_GROUP_DIM), lambda i, k: (layer, 0, 0, 0)),
        pl.BlockSpec((None, 1, POOL_WIDTH), lambda i, k: (layer, 0, 0)),
        pl.BlockSpec((None, D_MODEL, D_MODEL), lambda i, k: (layer, 0, 0), pipeline_mode=const),
        pl.BlockSpec((ROWS, D_MODEL), lambda i, k: (i, 0), pipeline_mode=const),
    ]
    args = [
        x, mod, norm_mix.reshape(DEPTH, 1, D_MODEL), w_in, sg_norm.reshape(DEPTH, 1, SG_WIDTH), sg_w,
        jnp.swapaxes(sg_b, 1, 2), w_branch_sg, w_branch_pool, pool_w, pool_scale.reshape(DEPTH, 1, POOL_WIDTH),
        w_out, phg,
    ]
    scratch = [
        pltpu.VMEM((ROWS, D_MODEL), BF16),
        pltpu.VMEM((ROWS, SG_WIDTH), F32),
        pltpu.VMEM((ROWS, SG_WIDTH), BF16),
        pltpu.VMEM((ROWS, D_MODEL), F32),
        pltpu.VMEM((ROWS, D_MODEL), F32),
    ]
    return pl.pallas_call(
        functools.partial(_mix_kernel, seq_len=seq_len),
        grid=(nb, len(_MIX_ORDER)),
        in_specs=in_specs,
        out_specs=pl.BlockSpec((ROWS, D_MODEL), lambda i, k: (i, 0)),
        out_shape=jax.ShapeDtypeStruct((n_tok, D_MODEL), F32),
        scratch_shapes=scratch,
        compiler_params=pltpu.CompilerParams(
            dimension_semantics=("arbitrary", "arbitrary"), vmem_limit_bytes=52 * MIB
        ),
        name=f"mix_l{layer}_t{seq_len}",
    )(*args)


def _ffn_kernel(x_ref, mod_ref, nffn_ref, wa_ref, wb_ref, cwa_ref, cwb_ref, cba_ref, cbb_ref, wd_ref, fin_ref, o_ref,
                hb, acc, *, seq_len, final):
    c = pl.program_id(1)

    @pl.when(c == 0)
    def _():
        hb[...] = _norm_mod(x_ref[...], nffn_ref[...], mod_ref[0, 4:5, :], mod_ref[0, 3:4, :])
        acc[...] = jnp.zeros_like(acc)

    tpos = lax.broadcasted_iota(jnp.int32, (ROWS, FF_CHUNK), 0) & (seq_len - 1)
    has_prev = tpos >= 1
    has_next = tpos < seq_len - 1

    def conv(h, cw_ref, cb_ref):
        prev = jnp.where(has_prev, pltpu.roll(h, 1, 0), 0.0)
        nxt = jnp.where(has_next, pltpu.roll(h, ROWS - 1, 0), 0.0)
        return prev * cw_ref[0:1, :] + h * cw_ref[1:2, :] + nxt * cw_ref[2:3, :] + cb_ref[...]

    h = hb[...]
    a = conv(_dot(h, wa_ref[...].astype(BF16)), cwa_ref, cba_ref)
    b = conv(_dot(h, wb_ref[...].astype(BF16)), cwb_ref, cbb_ref)
    acc[...] += _dot((_silu(a) * b).astype(BF16), wd_ref[...].astype(BF16))

    @pl.when(c == pl.num_programs(1) - 1)
    def _():
        y = x_ref[...] + mod_ref[0, 5:6, :] * acc[...]
        if final:
            y = _rms(y, fin_ref[...])
        o_ref[...] = y


def _ffn_call(x, mod, seq_len, layer, final, norm_ffn, ffn_up, ffn_conv_w, ffn_conv_b, ffn_down, final_norm):
    n_tok = x.shape[0]
    nb = n_tok // ROWS
    nc = D_FF // FF_CHUNK
    per_seq_mod = mod.shape[0] > 1
    const = pl.Buffered(1)
    conv_b = ffn_conv_b.reshape(DEPTH, 1, 2 * D_FF)
    in_specs = [
        pl.BlockSpec((ROWS, D_MODEL), lambda i, c: (i, 0), pipeline_mode=const),
        pl.BlockSpec((1, N_MOD, D_MODEL), (lambda i, c: (i, 0, 0)) if per_seq_mod else (lambda i, c: (0, 0, 0))),
        pl.BlockSpec((None, 1, D_MODEL), lambda i, c: (layer, 0, 0)),
        pl.BlockSpec((None, D_MODEL, FF_CHUNK), lambda i, c: (layer, 0, c)),
        pl.BlockSpec((None, D_MODEL, FF_CHUNK), lambda i, c: (layer, 0, nc + c)),
        pl.BlockSpec((None, 3, FF_CHUNK), lambda i, c: (layer, 0, c)),
        pl.BlockSpec((None, 3, FF_CHUNK), lambda i, c: (layer, 0, nc + c)),
        pl.BlockSpec((None, 1, FF_CHUNK), lambda i, c: (layer, 0, c)),
        pl.BlockSpec((None, 1, FF_CHUNK), lambda i, c: (layer, 0, nc + c)),
        pl.BlockSpec((None, FF_CHUNK, D_MODEL), lambda i, c: (layer, c, 0)),
        pl.BlockSpec((1, D_MODEL), lambda i, c: (0, 0)),
    ]
    args = [x, mod, norm_ffn.reshape(DEPTH, 1, D_MODEL), ffn_up, ffn_up, ffn_conv_w, ffn_conv_w, conv_b, conv_b,
            ffn_down, final_norm.reshape(1, D_MODEL)]
    return pl.pallas_call(
        functools.partial(_ffn_kernel, seq_len=seq_len, final=final),
        grid=(nb, nc),
        in_specs=in_specs,
        out_specs=pl.BlockSpec((ROWS, D_MODEL), lambda i, c: (i, 0)),
        out_shape=jax.ShapeDtypeStruct((n_tok, D_MODEL), F32),
        scratch_shapes=[pltpu.VMEM((ROWS, D_MODEL), BF16), pltpu.VMEM((ROWS, D_MODEL), F32)],
        compiler_params=pltpu.CompilerParams(
            dimension_semantics=("arbitrary", "arbitrary"), vmem_limit_bytes=40 * MIB
        ),
        name=f"ffn_l{layer}_t{seq_len}",
    )(*args)


def kernel(x_prompt, x_sample, c, state_hgrn, c_ctx, norm_mix, norm_ffn, w_ada, b_ada, w_in, lb_logits, hg_norm,
           w_branch_hg, w_branch_sg, w_branch_pool, w_out, sg_norm, sg_w, sg_b, pool_w, pool_scale, ffn_up,
           ffn_conv_w, ffn_conv_b, ffn_down, final_norm):
    n_ctx, t_ctx, _ = x_prompt.shape
    n_lat, t_lat, _ = x_sample.shape

    n_cond = 1 + n_lat
    pad = -n_cond % V7X_SUBLANES
    cvec = jnp.concatenate([c_ctx[None, :], c, jnp.zeros((pad, D_MODEL), F32)], axis=0)
    mod = _mod_call(cvec, w_ada, b_ada).reshape(DEPTH, n_cond + pad, N_MOD, D_MODEL)

    xs = _addpos_call(x_sample, _grid_pos_embed(t_lat)).reshape(n_lat * t_lat, D_MODEL)
    xp = x_prompt.reshape(n_ctx * t_ctx, D_MODEL)
    state0 = state_hgrn.reshape(n_lat * DEPTH * 2 * HG_HEADS, HG_DK, HG_DV)

    new_states = []
    for layer in range(DEPTH):
        final = layer == DEPTH - 1
        groups = []
        for x, m, t, s0 in ((xp, mod[layer, 0:1], t_ctx, None), (xs, mod[layer, 1:n_cond], t_lat, state0)):
            phg, s_fin = _hgrn_call(x, m, t, layer, norm_mix, w_in, lb_logits, hg_norm, w_branch_hg, s0)
            x1 = _mix_call(x, phg, m, t, layer, norm_mix, w_in, sg_norm, sg_w, sg_b, w_branch_sg, w_branch_pool,
                           pool_w, pool_scale, w_out)
            x2 = _ffn_call(x1, m, t, layer, final, norm_ffn, ffn_up, ffn_conv_w, ffn_conv_b, ffn_down, final_norm)
            groups.append((x2, s_fin))
        (xp, s_ctx), (xs, _) = groups
        new_states.append(s_ctx)

    y_prompt = xp.reshape(x_prompt.shape)
    y_sample = xs.reshape(x_sample.shape)
    new_state_hgrn = jnp.stack(new_states, axis=1)
    return (y_prompt, y_sample, new_state_hgrn)
```

```python
import functools

import jax
import jax.numpy as jnp
from jax import lax
from jax.experimental import pallas as pl
from jax.experimental.pallas import tpu as pltpu

D_MODEL = 1024
DEPTH = 2
GRID_W = 64
POS_BASE = 10000.0
EPS = 1e-6
HG_HEADS = 4
HG_DK = 128
HG_DV = 128
HG_WIDTH = HG_HEADS * HG_DV
SG_GROUPS = 4
SG_WIDTH = 512
SG_GROUP_DIM = SG_WIDTH // SG_GROUPS
SG_CHUNK = 128
POOL_WINDOWS = (2, 4, 8, 16)
POOL_WIDTH = 512
POOL_GROUP_DIM = POOL_WIDTH // len(POOL_WINDOWS)
IN_COLS = 5 * HG_WIDTH + 2 * SG_WIDTH + POOL_WIDTH + 3 * D_MODEL
D_FF = 2816
N_MOD = 6

V7X_LANES = 128
V7X_SUBLANES = 8
V7X_MXU_DIM = 256
MIB = 2**20

ROWS = 1024
HG_BLOCK = 256
HG_LEVELS = 8
IN_CHUNK = 512
FF_CHUNK = V7X_MXU_DIM
MOD_CHUNK = 1536

F32 = jnp.float32
BF16 = jnp.bfloat16


def _dot(a, b):
    return lax.dot_general(a, b, (((1,), (0,)), ((), ())), preferred_element_type=F32)


def _dot_nt(a, b):
    return lax.dot_general(a, b, (((1,), (1,)), ((), ())), preferred_element_type=F32)


def _dot_tn(a, b):
    return lax.dot_general(a, b, (((0,), (0,)), ((), ())), preferred_element_type=F32)


def _silu(x):
    return x * jax.nn.sigmoid(x)


def _rms(x, gain):
    return x * lax.rsqrt(jnp.mean(x * x, axis=-1, keepdims=True) + EPS) * gain


def _log1pexp(y):
    return jnp.maximum(y, 0.0) + jnp.log(1.0 + jnp.exp(-jnp.abs(y)))


def _norm_mod(x, gain, scale, shift):
    return (_rms(x, gain) * (1.0 + scale) + shift).astype(BF16)


def _mod_kernel(c_ref, w_ref, b_ref, o_ref):
    c = _silu(c_ref[...]).astype(BF16)
    o_ref[...] = _dot(c, w_ref[...].astype(BF16)) + b_ref[...]


def _mod_call(cvec, w_ada, b_ada):
    n_rows = cvec.shape[0]
    n_cols = N_MOD * D_MODEL
    return pl.pallas_call(
        _mod_kernel,
        grid=(DEPTH, n_cols // MOD_CHUNK),
        in_specs=[
            pl.BlockSpec((n_rows, D_MODEL), lambda l, n: (0, 0)),
            pl.BlockSpec((None, D_MODEL, MOD_CHUNK), lambda l, n: (l, 0, n)),
            pl.BlockSpec((None, 1, MOD_CHUNK), lambda l, n: (l, 0, n)),
        ],
        out_specs=pl.BlockSpec((None, n_rows, MOD_CHUNK), lambda l, n: (l, 0, n)),
        out_shape=jax.ShapeDtypeStruct((DEPTH, n_rows, n_cols), F32),
        compiler_params=pltpu.CompilerParams(
            dimension_semantics=("arbitrary", "arbitrary"), vmem_limit_bytes=32 * MIB
        ),
        name="adaln_mod",
    )(cvec, w_ada, b_ada.reshape(DEPTH, 1, n_cols))


def _addpos_kernel(x_ref, p_ref, o_ref):
    o_ref[...] = x_ref[...] + p_ref[...]


def _addpos_call(x, pos):
    b, t, d = x.shape
    return pl.pallas_call(
        _addpos_kernel,
        grid=(b,),
        in_specs=[pl.BlockSpec((None, t, d), lambda i: (i, 0, 0)), pl.BlockSpec((t, d), lambda i: (0, 0))],
        out_specs=pl.BlockSpec((None, t, d), lambda i: (i, 0, 0)),
        out_shape=jax.ShapeDtypeStruct(x.shape, x.dtype),
        compiler_params=pltpu.CompilerParams(dimension_semantics=("arbitrary",), vmem_limit_bytes=32 * MIB),
        name="add_pos",
    )(x, pos)


def _grid_pos_embed(n_tokens):
    rows = n_tokens // GRID_W
    r = jnp.broadcast_to(jnp.arange(rows, dtype=F32)[:, None], (rows, GRID_W)).reshape(-1)
    col = jnp.broadcast_to(jnp.arange(GRID_W, dtype=F32)[None, :], (rows, GRID_W)).reshape(-1)
    quarter = D_MODEL // 4
    omega = 1.0 / (POS_BASE ** (jnp.arange(quarter, dtype=F32) / quarter))
    ar = r[:, None] * omega[None, :]
    ac = col[:, None] * omega[None, :]
    return jnp.concatenate([jnp.sin(ar), jnp.cos(ar), jnp.sin(ac), jnp.cos(ac)], axis=-1)


def _ref_rows(b, blk, r):
    n, c = b.shape
    if blk >= V7X_SUBLANES:
        x3 = b.reshape(n // blk, blk, c)
        return jnp.broadcast_to(x3[:, r : r + 1, :], x3.shape).reshape(n, c)
    x3 = b.reshape(n // V7X_SUBLANES, V7X_SUBLANES, c)
    sub = lax.broadcasted_iota(jnp.int32, x3.shape, 1)
    bases = list(range(0, V7X_SUBLANES, blk))
    out = jnp.broadcast_to(x3[:, bases[-1] + r : bases[-1] + r + 1, :], x3.shape)
    for base in reversed(bases[:-1]):
        out = jnp.where(sub < base + blk, jnp.broadcast_to(x3[:, base + r : base + r + 1, :], x3.shape), out)
    return out.reshape(n, c)


def _cum_logdecay(lf, forward):
    n = lf.shape[0]
    row = lax.broadcasted_iota(jnp.int32, lf.shape, 0)
    b = lf
    sh = 1
    while sh < n:
        if forward:
            b = b + jnp.where(row >= sh, pltpu.roll(b, sh, 0), 0.0)
        else:
            b = b + jnp.where(row < n - sh, pltpu.roll(b, n - sh, 0), 0.0)
        sh *= 2
    return b


def _hgrn_block(q, k, v, lf, lvq, forward, st):
    half = HG_BLOCK // 2
    lo, hi = slice(0, half), slice(half, HG_BLOCK)
    b = _cum_logdecay(lf, forward)
    vb = v.astype(BF16)
    qb, kb = q.astype(BF16), k.astype(BF16)
    diag = [jnp.where(lvq == 0, _dot_nt(qb[h], kb[h]), 0.0) for h in (lo, hi)]
    for m in range(1, HG_LEVELS):
        blk = 2**m
        ref = _ref_rows(b, blk, blk // 2 - 1 if forward else blk // 2)
        e = jnp.exp(-jnp.abs(b - ref))
        qt, kt = (q * e).astype(BF16), (k * e).astype(BF16)
        diag = [jnp.where(lvq == m, _dot_nt(qt[h], kt[h]), a) for h, a in zip((lo, hi), diag)]
    mid = half - 1 if forward else half
    e = jnp.exp(-jnp.abs(b - b[mid : mid + 1, :]))
    qt, kt = (q * e).astype(BF16), (k * e).astype(BF16)
    a_lo, a_hi = (a.astype(BF16) for a in diag)
    if forward:
        cross = _dot_nt(qt[hi], kt[lo]).astype(BF16)
        o = jnp.concatenate([_dot(a_lo, vb[lo]), _dot(cross, vb[lo]) + _dot(a_hi, vb[hi])], axis=0)
    else:
        cross = _dot_nt(qt[lo], kt[hi]).astype(BF16)
        o = jnp.concatenate([_dot(a_lo, vb[lo]) + _dot(cross, vb[hi]), _dot(a_hi, vb[hi])], axis=0)
    edge = b[HG_BLOCK - 1 : HG_BLOCK, :] if forward else b[0:1, :]
    k_end = (k * jnp.exp(edge - b)).astype(BF16)
    st_new = _dot_tn(vb, k_end)
    if st is not None:
        o = o + _dot_nt((q * jnp.exp(b)).astype(BF16), st.astype(BF16))
        st_new = st_new + st * jnp.exp(edge)
    return o, st_new


def _hgrn_kernel(*refs, layer, seq_len):
    carry = seq_len > HG_BLOCK
    it = iter(refs)
    x_ref, mod_ref, nmix_ref = next(it), next(it), next(it)
    w_refs = [next(it) for _ in range(5)]
    lbl_ref, hgn_ref, wbr_ref = next(it), next(it), next(it)
    s0_refs = [next(it), next(it)] if carry else None
    phg_ref = next(it)
    sout_ref = None if carry else next(it)
    hb, wcat, q_s, v_s, g_s, kf_s, kb_s, lff_s, lfb_s, o_s, lvf_s, lvb_s, st_s = it

    j = pl.program_id(1)

    @pl.when(j == 0)
    def _():
        hb[...] = _norm_mod(x_ref[...], nmix_ref[...], mod_ref[0, 1:2, :], mod_ref[0, 0:1, :])
        t = lax.broadcasted_iota(jnp.int32, (HG_BLOCK // 2, HG_BLOCK // 2), 0)
        s = lax.broadcasted_iota(jnp.int32, (HG_BLOCK // 2, HG_BLOCK // 2), 1)
        x = t ^ s
        lv = jnp.zeros_like(x)
        for m in range(HG_LEVELS - 1):
            lv = lv + (x >= 2**m).astype(jnp.int32)
        lvf_s[...] = jnp.where(t >= s, lv, -1)
        lvb_s[...] = jnp.where(t <= s, lv, -1)

    for g, w_ref in enumerate(w_refs):
        wcat[:, g * HG_DK : (g + 1) * HG_DK] = w_ref[...].astype(BF16)
    z = _dot(hb[...], wcat[...])
    zq, zff, zfb, zi, zg = (z[:, g * HG_DK : (g + 1) * HG_DK] for g in range(5))
    q_s[...] = _silu(zq) * HG_DK**-0.5
    v_s[...] = zi
    g_s[...] = _silu(zg)

    a0, a1 = lbl_ref[0], lbl_ref[1]
    amax = jnp.maximum(a0, a1)
    e0, e1 = jnp.exp(a0 - amax), jnp.exp(a1 - amax)
    p0, p1 = e0 / (e0 + e1), e1 / (e0 + e1)
    lb = (p0 - p0) if layer == 0 else ((p0 + p1) - p0)
    for d, (zf, k_s, lf_s) in enumerate(((zff, kf_s, lff_s), (zfb, kb_s, lfb_s))):
        lbd = lb[d : d + 1, :]
        lf_s[...] = _log1pexp(jnp.log(lbd) - zf) - _log1pexp(-zf)
        k_s[...] = (1.0 - lbd) * jax.nn.sigmoid(-zf)

    o_s[...] = jnp.zeros_like(o_s)
    n_blk = ROWS // HG_BLOCK
    if carry:
        for d in range(2):
            st_s[d] = s0_refs[d][0].T

    def blocks(n, c):
        for d in range(2):
            forward = d == 0
            blk = n if forward else n_blk - 1 - n
            rows = pl.ds(pl.multiple_of(blk * HG_BLOCK, HG_BLOCK), HG_BLOCK)
            k_s, lf_s, lv_s = (kf_s, lff_s, lvf_s) if forward else (kb_s, lfb_s, lvb_s)
            o, st_new = _hgrn_block(
                q_s[rows, :], k_s[rows, :], v_s[rows, :], lf_s[rows, :], lv_s[...], forward, st_s[d] if carry else None
            )
            o_s[rows, :] += o
            if carry:
                st_s[d] = st_new
            else:
                sout_ref[blk, d, 0] = st_new.T
        return c

    lax.fori_loop(0, n_blk, blocks, 0)

    y = _rms(o_s[...], hgn_ref[...]) * g_s[...]
    contrib = _dot(y.astype(BF16), wbr_ref[...].astype(BF16))

    @pl.when(j == 0)
    def _():
        phg_ref[...] = contrib

    @pl.when(j > 0)
    def _():
        phg_ref[...] += contrib


def _hgrn_call(x, mod, seq_len, layer, norm_mix, w_in, lb_logits, hg_norm, w_branch_hg, state0):
    n_tok = x.shape[0]
    nb = n_tok // ROWS
    carry = seq_len > HG_BLOCK
    per_seq_mod = mod.shape[0] > 1
    const = pl.Buffered(1)

    in_specs = [
        pl.BlockSpec((ROWS, D_MODEL), lambda i, j: (i, 0), pipeline_mode=const),
        pl.BlockSpec((1, N_MOD, D_MODEL), (lambda i, j: (i, 0, 0)) if per_seq_mod else (lambda i, j: (0, 0, 0))),
        pl.BlockSpec((None, 1, D_MODEL), lambda i, j: (layer, 0, 0)),
    ]
    args = [x, mod, norm_mix.reshape(DEPTH, 1, D_MODEL)]
    for g in range(5):
        in_specs.append(pl.BlockSpec((None, D_MODEL, HG_DK), lambda i, j, g=g: (layer, 0, g * HG_HEADS + j)))
        args.append(w_in)
    in_specs += [
        pl.BlockSpec((DEPTH, 2, HG_DK), lambda i, j: (0, 0, j)),
        pl.BlockSpec((None, 1, HG_DV), lambda i, j: (layer, 0, 0)),
        pl.BlockSpec((None, HG_DV, D_MODEL), lambda i, j: (layer, j, 0)),
    ]
    args += [lb_logits, hg_norm.reshape(DEPTH, 1, HG_DV), w_branch_hg]
    if carry:
        assert seq_len == ROWS
        for d in range(2):
            in_specs.append(
                pl.BlockSpec((1, HG_DK, HG_DV), lambda i, j, d=d: (((i * DEPTH + layer) * 2 + d) * HG_HEADS + j, 0, 0))
            )
            args.append(state0)

    out_shape = [jax.ShapeDtypeStruct((n_tok, D_MODEL), F32)]
    out_specs = [pl.BlockSpec((ROWS, D_MODEL), lambda i, j: (i, 0))]
    if not carry:
        assert seq_len == HG_BLOCK
        n_seq = n_tok // seq_len
        out_shape.append(jax.ShapeDtypeStruct((n_seq, 2, HG_HEADS, HG_DK, HG_DV), F32))
        out_specs.append(pl.BlockSpec((ROWS // seq_len, 2, 1, HG_DK, HG_DV), lambda i, j: (i, 0, j, 0, 0)))

    head = lambda dt=F32: pltpu.VMEM((ROWS, HG_DK), dt)
    scratch = [
        pltpu.VMEM((ROWS, D_MODEL), BF16),
        pltpu.VMEM((D_MODEL, 5 * HG_DK), BF16),
        head(), head(), head(),
        head(), head(), head(), head(),
        head(),
        pltpu.VMEM((HG_BLOCK // 2, HG_BLOCK // 2), jnp.int32),
        pltpu.VMEM((HG_BLOCK // 2, HG_BLOCK // 2), jnp.int32),
        pltpu.VMEM((2, HG_DV, HG_DK), F32),
    ]
    outs = pl.pallas_call(
        functools.partial(_hgrn_kernel, layer=layer, seq_len=seq_len),
        grid=(nb, HG_HEADS),
        in_specs=in_specs,
        out_specs=out_specs,
        out_shape=out_shape,
        scratch_shapes=scratch,
        compiler_params=pltpu.CompilerParams(
            dimension_semantics=("arbitrary", "arbitrary"), vmem_limit_bytes=48 * MIB
        ),
        name=f"hgrn_l{layer}_t{seq_len}",
    )(*args)
    return (outs[0], None) if carry else (outs[0], outs[1])


_MIX_ORDER = (10, 11, 5, 6, 12, 13, 7, 8, 9)


def _mix_col(k):
    idx = 0
    for n, c in enumerate(_MIX_ORDER):
        idx = idx + jnp.where(k == n, c, 0)
    return idx


def _window_mean_minus_self(p, tpos, seq_len, w):
    n = p.shape[0]
    acc = jnp.zeros_like(p)
    for jj in range(-(w // 2), w // 2):
        shifted = p if jj == 0 else pltpu.roll(p, (-jj) % n, 0)
        valid = (tpos + jj >= 0) & (tpos + jj < seq_len)
        acc = acc + jnp.where(valid, shifted, 0.0)
    cnt = jnp.minimum(tpos + w // 2, seq_len) - jnp.maximum(tpos - w // 2, 0)
    return acc / cnt.astype(F32) - p


def _mix_kernel(
    x_ref, mod_ref, nmix_ref, w_ref, sgn_ref, sgw_ref, sgb_ref, wbsg_ref, wbpool_ref, poolw_ref, pscale_ref,
    wout_ref, phg_ref, o_ref, hb, u_s, br_s, mrg_s, g2_s, *, seq_len,
):
    k = pl.program_id(1)
    half = D_MODEL // 2

    @pl.when(k == 0)
    def _():
        hb[...] = _norm_mod(x_ref[...], nmix_ref[...], mod_ref[0, 1:2, :], mod_ref[0, 0:1, :])

    z = _dot(hb[...], w_ref[...].astype(BF16))

    for step in (0, 1):

        @pl.when(k == step)
        def _(step=step):
            mrg_s[:, step * half : (step + 1) * half] = jax.nn.sigmoid(z)

    @pl.when(k == 2)
    def _():
        u_s[...] = jax.nn.gelu(z)

    @pl.when(k == 3)
    def _():
        v = _rms(jax.nn.gelu(z), sgn_ref[...]).astype(BF16)
        for g in range(SG_GROUPS):
            wg = sgw_ref[g].astype(BF16)
            bias = sgb_ref[:, g : g + 1]
            cols = slice(g * SG_GROUP_DIM, (g + 1) * SG_GROUP_DIM)
            for n in range(ROWS // SG_CHUNK):
                rows = slice(n * SG_CHUNK, (n + 1) * SG_CHUNK)
                mixed = _dot(wg, v[rows, cols]) + bias
                br_s[rows, cols] = (u_s[rows, cols] * mixed).astype(BF16)
        mrg_s[...] = mrg_s[...] * _dot(br_s[...], wbsg_ref[...].astype(BF16))

    for step in (4, 5):

        @pl.when(k == step)
        def _(step=step):
            g2_s[:, (step - 4) * half : (step - 3) * half] = jax.nn.sigmoid(z)

    @pl.when(k == 6)
    def _():
        tpos = lax.broadcasted_iota(jnp.int32, (ROWS, POOL_GROUP_DIM), 0) & (seq_len - 1)
        for gi, w in enumerate(POOL_WINDOWS):
            cols = slice(gi * POOL_GROUP_DIM, (gi + 1) * POOL_GROUP_DIM)
            pooled = _window_mean_minus_self(z[:, cols], tpos, seq_len, w)
            out = _dot(pooled.astype(BF16), poolw_ref[gi].astype(BF16)) * pscale_ref[:, cols]
            br_s[:, cols] = out.astype(BF16)
        mrg_s[...] = mrg_s[...] + g2_s[...] * _dot(br_s[...], wbpool_ref[...].astype(BF16))

    for step in (7, 8):

        @pl.when(k == step)
        def _(step=step):
            cols = slice((step - 7) * half, (step - 6) * half)
            mrg_s[:, cols] = mrg_s[:, cols] + jax.nn.sigmoid(z) * phg_ref[:, cols]

    @pl.when(k == 8)
    def _():
        y = _dot(mrg_s[...].astype(BF16), wout_ref[...].astype(BF16))
        o_ref[...] = x_ref[...] + mod_ref[0, 2:3, :] * y


def _mix_call(x, phg, mod, seq_len, layer, norm_mix, w_in, sg_norm, sg_w, sg_b, w_branch_sg, w_branch_pool, pool_w,
              pool_scale, w_out):
    n_tok = x.shape[0]
    nb = n_tok // ROWS
    per_seq_mod = mod.shape[0] > 1
    const = pl.Buffered(1)
    assert seq_len & (seq_len - 1) == 0 and ROWS % seq_len == 0 and seq_len % SG_CHUNK == 0
    in_specs = [
        pl.BlockSpec((ROWS, D_MODEL), lambda i, k: (i, 0), pipeline_mode=const),
        pl.BlockSpec((1, N_MOD, D_MODEL), (lambda i, k: (i, 0, 0)) if per_seq_mod else (lambda i, k: (0, 0, 0))),
        pl.BlockSpec((None, 1, D_MODEL), lambda i, k: (layer, 0, 0)),
        pl.BlockSpec((None, D_MODEL, IN_CHUNK), lambda i, k: (layer, 0, _mix_col(k))),
        pl.BlockSpec((None, 1, SG_WIDTH), lambda i, k: (layer, 0, 0)),
        pl.BlockSpec((None, SG_GROUPS, SG_CHUNK, SG_CHUNK), lambda i, k: (layer, 0, 0, 0)),
        pl.BlockSpec((None, SG_CHUNK, SG_GROUPS), lambda i, k: (layer, 0, 0)),
        pl.BlockSpec((None, SG_WIDTH, D_MODEL), lambda i, k: (layer, 0, 0), pipeline_mode=const),
        pl.BlockSpec((None, POOL_WIDTH, D_MODEL), lambda i, k: (layer, 0, 0), pipeline_mode=const),
        pl.BlockSpec((None, len(POOL_WINDOWS), POOL_GROUP_DIM, POOL_GROUP_DIM), lambda i, k: (layer, 0, 0, 0)),
        pl.BlockSpec((None, 1, POOL_WIDTH), lambda i, k: (layer, 0, 0)),
        pl.BlockSpec((None, D_MODEL, D_MODEL), lambda i, k: (layer, 0, 0), pipeline_mode=const),
        pl.BlockSpec((ROWS, D_MODEL), lambda i, k: (i, 0), pipeline_mode=const),
    ]
    args = [
        x, mod, norm_mix.reshape(DEPTH, 1, D_MODEL), w_in, sg_norm.reshape(DEPTH, 1, SG_WIDTH), sg_w,
        jnp.swapaxes(sg_b, 1, 2), w_branch_sg, w_branch_pool, pool_w, pool_scale.reshape(DEPTH, 1, POOL_WIDTH),
        w_out, phg,
    ]
    scratch = [
        pltpu.VMEM((ROWS, D_MODEL), BF16),
        pltpu.VMEM((ROWS, SG_WIDTH), F32),
        pltpu.VMEM((ROWS, SG_WIDTH), BF16),
        pltpu.VMEM((ROWS, D_MODEL), F32),
        pltpu.VMEM((ROWS, D_MODEL), F32),
    ]
    return pl.pallas_call(
        functools.partial(_mix_kernel, seq_len=seq_len),
        grid=(nb, len(_MIX_ORDER)),
        in_specs=in_specs,
        out_specs=pl.BlockSpec((ROWS, D_MODEL), lambda i, k: (i, 0)),
        out_shape=jax.ShapeDtypeStruct((n_tok, D_MODEL), F32),
        scratch_shapes=scratch,
        compiler_params=pltpu.CompilerParams(
            dimension_semantics=("arbitrary", "arbitrary"), vmem_limit_bytes=52 * MIB
        ),
        name=f"mix_l{layer}_t{seq_len}",
    )(*args)


def _ffn_kernel(x_ref, mod_ref, nffn_ref, wa_ref, wb_ref, cwa_ref, cwb_ref, cba_ref, cbb_ref, wd_ref, fin_ref, o_ref,
                hb, acc, *, seq_len, final):
    c = pl.program_id(1)

    @pl.when(c == 0)
    def _():
        hb[...] = _norm_mod(x_ref[...], nffn_ref[...], mod_ref[0, 4:5, :], mod_ref[0, 3:4, :])
        acc[...] = jnp.zeros_like(acc)

    tpos = lax.broadcasted_iota(jnp.int32, (ROWS, FF_CHUNK), 0) & (seq_len - 1)
    has_prev = tpos >= 1
    has_next = tpos < seq_len - 1

    def conv(h, cw_ref, cb_ref):
        prev = jnp.where(has_prev, pltpu.roll(h, 1, 0), 0.0)
        nxt = jnp.where(has_next, pltpu.roll(h, ROWS - 1, 0), 0.0)
        return prev * cw_ref[0:1, :] + h * cw_ref[1:2, :] + nxt * cw_ref[2:3, :] + cb_ref[...]

    h = hb[...]
    a = conv(_dot(h, wa_ref[...].astype(BF16)), cwa_ref, cba_ref)
    b = conv(_dot(h, wb_ref[...].astype(BF16)), cwb_ref, cbb_ref)
    acc[...] += _dot((_silu(a) * b).astype(BF16), wd_ref[...].astype(BF16))

    @pl.when(c == pl.num_programs(1) - 1)
    def _():
        y = x_ref[...] + mod_ref[0, 5:6, :] * acc[...]
        if final:
            y = _rms(y, fin_ref[...])
        o_ref[...] = y


def _ffn_call(x, mod, seq_len, layer, final, norm_ffn, ffn_up, ffn_conv_w, ffn_conv_b, ffn_down, final_norm):
    n_tok = x.shape[0]
    nb = n_tok // ROWS
    nc = D_FF // FF_CHUNK
    per_seq_mod = mod.shape[0] > 1
    conv_b = ffn_conv_b.reshape(DEPTH, 1, 2 * D_FF)
    in_specs = [
        pl.BlockSpec((ROWS, D_MODEL), lambda i, c: (i, 0)),
        pl.BlockSpec((1, N_MOD, D_MODEL), (lambda i, c: (i, 0, 0)) if per_seq_mod else (lambda i, c: (0, 0, 0))),
        pl.BlockSpec((None, 1, D_MODEL), lambda i, c: (layer, 0, 0)),
        pl.BlockSpec((None, D_MODEL, FF_CHUNK), lambda i, c: (layer, 0, c)),
        pl.BlockSpec((None, D_MODEL, FF_CHUNK), lambda i, c: (layer, 0, nc + c)),
        pl.BlockSpec((None, 3, FF_CHUNK), lambda i, c: (layer, 0, c)),
        pl.BlockSpec((None, 3, FF_CHUNK), lambda i, c: (layer, 0, nc + c)),
        pl.BlockSpec((None, 1, FF_CHUNK), lambda i, c: (layer, 0, c)),
        pl.BlockSpec((None, 1, FF_CHUNK), lambda i, c: (layer, 0, nc + c)),
        pl.BlockSpec((None, FF_CHUNK, D_MODEL), lambda i, c: (layer, c, 0)),
        pl.BlockSpec((1, D_MODEL), lambda i, c: (0, 0)),
    ]
    args = [x, mod, norm_ffn.reshape(DEPTH, 1, D_MODEL), ffn_up, ffn_up, ffn_conv_w, ffn_conv_w, conv_b, conv_b,
            ffn_down, final_norm.reshape(1, D_MODEL)]
    return pl.pallas_call(
        functools.partial(_ffn_kernel, seq_len=seq_len, final=final),
        grid=(nb, nc),
        in_specs=in_specs,
        out_specs=pl.BlockSpec((ROWS, D_MODEL), lambda i, c: (i, 0)),
        out_shape=jax.ShapeDtypeStruct((n_tok, D_MODEL), F32),
        scratch_shapes=[pltpu.VMEM((ROWS, D_MODEL), BF16), pltpu.VMEM((ROWS, D_MODEL), F32)],
        compiler_params=pltpu.CompilerParams(
            dimension_semantics=("arbitrary", "arbitrary"), vmem_limit_bytes=40 * MIB
        ),
        name=f"ffn_l{layer}_t{seq_len}",
    )(*args)


def kernel(x_prompt, x_sample, c, state_hgrn, c_ctx, norm_mix, norm_ffn, w_ada, b_ada, w_in, lb_logits, hg_norm,
           w_branch_hg, w_branch_sg, w_branch_pool, w_out, sg_norm, sg_w, sg_b, pool_w, pool_scale, ffn_up,
           ffn_conv_w, ffn_conv_b, ffn_down, final_norm):
    n_ctx, t_ctx, _ = x_prompt.shape
    n_lat, t_lat, _ = x_sample.shape

    n_cond = 1 + n_lat
    pad = -n_cond % V7X_SUBLANES
    cvec = jnp.concatenate([c_ctx[None, :], c, jnp.zeros((pad, D_MODEL), F32)], axis=0)
    mod = _mod_call(cvec, w_ada, b_ada).reshape(DEPTH, n_cond + pad, N_MOD, D_MODEL)

    xs = _addpos_call(x_sample, _grid_pos_embed(t_lat)).reshape(n_lat * t_lat, D_MODEL)
    xp = x_prompt.reshape(n_ctx * t_ctx, D_MODEL)
    state0 = state_hgrn.reshape(n_lat * DEPTH * 2 * HG_HEADS, HG_DK, HG_DV)

    new_states = []
    for layer in range(DEPTH):
        final = layer == DEPTH - 1
        groups = []
        for x, m, t, s0 in ((xp, mod[layer, 0:1], t_ctx, None), (xs, mod[layer, 1:n_cond], t_lat, state0)):
            phg, s_fin = _hgrn_call(x, m, t, layer, norm_mix, w_in, lb_logits, hg_norm, w_branch_hg, s0)
            x1 = _mix_call(x, phg, m, t, layer, norm_mix, w_in, sg_norm, sg_w, sg_b, w_branch_sg, w_branch_pool,
                           pool_w, pool_scale, w_out)
            x2 = _ffn_call(x1, m, t, layer, final, norm_ffn, ffn_up, ffn_conv_w, ffn_conv_b, ffn_down, final_norm)
            groups.append((x2, s_fin))
        (xp, s_ctx), (xs, _) = groups
        new_states.append(s_ctx)

    y_prompt = xp.reshape(x_prompt.shape)
    y_sample = xs.reshape(x_sample.shape)
    new_state_hgrn = jnp.stack(new_states, axis=1)
    return (y_prompt, y_sample, new_state_hgrn)
```

```python
import functools

import jax
import jax.numpy as jnp
from jax import lax
from jax.experimental import pallas as pl
from jax.experimental.pallas import tpu as pltpu

D_MODEL = 1024
DEPTH = 2
GRID_W = 64
POS_BASE = 10000.0
EPS = 1e-6
HG_HEADS = 4
HG_DK = 128
HG_DV = 128
HG_WIDTH = HG_HEADS * HG_DV
SG_GROUPS = 4
SG_WIDTH = 512
SG_GROUP_DIM = SG_WIDTH // SG_GROUPS
SG_CHUNK = 128
POOL_WINDOWS = (2, 4, 8, 16)
POOL_WIDTH = 512
POOL_GROUP_DIM = POOL_WIDTH // len(POOL_WINDOWS)
IN_COLS = 5 * HG_WIDTH + 2 * SG_WIDTH + POOL_WIDTH + 3 * D_MODEL
D_FF = 2816
N_MOD = 6

V7X_LANES = 128
V7X_SUBLANES = 8
V7X_MXU_DIM = 256
MIB = 2**20

ROWS = 1024
PART_ROWS = 512
HG_BLOCK = 256
HG_LEVELS = 8
IN_CHUNK = 512
FF_CHUNK = V7X_MXU_DIM
MOD_CHUNK = 1536

F32 = jnp.float32
BF16 = jnp.bfloat16


def _dot(a, b):
    return lax.dot_general(a, b, (((1,), (0,)), ((), ())), preferred_element_type=F32)


def _dot_nt(a, b):
    return lax.dot_general(a, b, (((1,), (1,)), ((), ())), preferred_element_type=F32)


def _dot_tn(a, b):
    return lax.dot_general(a, b, (((0,), (0,)), ((), ())), preferred_element_type=F32)


def _silu(x):
    return x * jax.nn.sigmoid(x)


def _rms(x, gain):
    return x * lax.rsqrt(jnp.mean(x * x, axis=-1, keepdims=True) + EPS) * gain


def _log1pexp(y):
    return jnp.maximum(y, 0.0) + jnp.log(1.0 + jnp.exp(-jnp.abs(y)))


def _norm_mod(x, gain, scale, shift):
    return (_rms(x, gain) * (1.0 + scale) + shift).astype(BF16)


def _part_rows(seq_len):
    return max(seq_len, PART_ROWS)


def _mod_kernel(c_ref, w_ref, b_ref, o_ref):
    c = _silu(c_ref[...]).astype(BF16)
    o_ref[...] = _dot(c, w_ref[...].astype(BF16)) + b_ref[...]


def _mod_call(cvec, w_ada, b_ada):
    n_rows = cvec.shape[0]
    n_cols = N_MOD * D_MODEL
    return pl.pallas_call(
        _mod_kernel,
        grid=(DEPTH, n_cols // MOD_CHUNK),
        in_specs=[
            pl.BlockSpec((n_rows, D_MODEL), lambda l, n: (0, 0)),
            pl.BlockSpec((None, D_MODEL, MOD_CHUNK), lambda l, n: (l, 0, n)),
            pl.BlockSpec((None, 1, MOD_CHUNK), lambda l, n: (l, 0, n)),
        ],
        out_specs=pl.BlockSpec((None, n_rows, MOD_CHUNK), lambda l, n: (l, 0, n)),
        out_shape=jax.ShapeDtypeStruct((DEPTH, n_rows, n_cols), F32),
        compiler_params=pltpu.CompilerParams(
            dimension_semantics=("arbitrary", "arbitrary"), vmem_limit_bytes=32 * MIB
        ),
        name="adaln_mod",
    )(cvec, w_ada, b_ada.reshape(DEPTH, 1, n_cols))


def _addpos_kernel(x_ref, p_ref, o_ref):
    o_ref[...] = x_ref[...] + p_ref[...]


def _addpos_call(x, pos):
    b, t, d = x.shape
    return pl.pallas_call(
        _addpos_kernel,
        grid=(b,),
        in_specs=[pl.BlockSpec((None, t, d), lambda i: (i, 0, 0)), pl.BlockSpec((t, d), lambda i: (0, 0))],
        out_specs=pl.BlockSpec((None, t, d), lambda i: (i, 0, 0)),
        out_shape=jax.ShapeDtypeStruct(x.shape, x.dtype),
        compiler_params=pltpu.CompilerParams(dimension_semantics=("arbitrary",), vmem_limit_bytes=32 * MIB),
        name="add_pos",
    )(x, pos)


def _grid_pos_embed(n_tokens):
    rows = n_tokens // GRID_W
    r = jnp.broadcast_to(jnp.arange(rows, dtype=F32)[:, None], (rows, GRID_W)).reshape(-1)
    col = jnp.broadcast_to(jnp.arange(GRID_W, dtype=F32)[None, :], (rows, GRID_W)).reshape(-1)
    quarter = D_MODEL // 4
    omega = 1.0 / (POS_BASE ** (jnp.arange(quarter, dtype=F32) / quarter))
    ar = r[:, None] * omega[None, :]
    ac = col[:, None] * omega[None, :]
    return jnp.concatenate([jnp.sin(ar), jnp.cos(ar), jnp.sin(ac), jnp.cos(ac)], axis=-1)


def _ref_rows(b, blk, r):
    n, c = b.shape
    if blk >= V7X_SUBLANES:
        x3 = b.reshape(n // blk, blk, c)
        return jnp.broadcast_to(x3[:, r : r + 1, :], x3.shape).reshape(n, c)
    x3 = b.reshape(n // V7X_SUBLANES, V7X_SUBLANES, c)
    sub = lax.broadcasted_iota(jnp.int32, x3.shape, 1)
    bases = list(range(0, V7X_SUBLANES, blk))
    out = jnp.broadcast_to(x3[:, bases[-1] + r : bases[-1] + r + 1, :], x3.shape)
    for base in reversed(bases[:-1]):
        out = jnp.where(sub < base + blk, jnp.broadcast_to(x3[:, base + r : base + r + 1, :], x3.shape), out)
    return out.reshape(n, c)


def _cum_logdecay(lf, forward):
    n = lf.shape[0]
    row = lax.broadcasted_iota(jnp.int32, lf.shape, 0)
    b = lf
    sh = 1
    while sh < n:
        if forward:
            b = b + jnp.where(row >= sh, pltpu.roll(b, sh, 0), 0.0)
        else:
            b = b + jnp.where(row < n - sh, pltpu.roll(b, n - sh, 0), 0.0)
        sh *= 2
    return b


def _hgrn_block(q, k, v, lf, lvq, forward, st):
    half = HG_BLOCK // 2
    lo, hi = slice(0, half), slice(half, HG_BLOCK)
    b = _cum_logdecay(lf, forward)
    vb = v.astype(BF16)
    qb, kb = q.astype(BF16), k.astype(BF16)
    diag = [jnp.where(lvq == 0, _dot_nt(qb[h], kb[h]), 0.0) for h in (lo, hi)]
    for m in range(1, HG_LEVELS):
        blk = 2**m
        ref = _ref_rows(b, blk, blk // 2 - 1 if forward else blk // 2)
        e = jnp.exp(-jnp.abs(b - ref))
        qt, kt = (q * e).astype(BF16), (k * e).astype(BF16)
        diag = [jnp.where(lvq == m, _dot_nt(qt[h], kt[h]), a) for h, a in zip((lo, hi), diag)]
    mid = half - 1 if forward else half
    e = jnp.exp(-jnp.abs(b - b[mid : mid + 1, :]))
    qt, kt = (q * e).astype(BF16), (k * e).astype(BF16)
    a_lo, a_hi = (a.astype(BF16) for a in diag)
    if forward:
        cross = _dot_nt(qt[hi], kt[lo]).astype(BF16)
        o = jnp.concatenate([_dot(a_lo, vb[lo]), _dot(cross, vb[lo]) + _dot(a_hi, vb[hi])], axis=0)
    else:
        cross = _dot_nt(qt[lo], kt[hi]).astype(BF16)
        o = jnp.concatenate([_dot(a_lo, vb[lo]) + _dot(cross, vb[hi]), _dot(a_hi, vb[hi])], axis=0)
    edge = b[HG_BLOCK - 1 : HG_BLOCK, :] if forward else b[0:1, :]
    k_end = (k * jnp.exp(edge - b)).astype(BF16)
    st_new = _dot_tn(vb, k_end)
    if st is not None:
        o = o + _dot_nt((q * jnp.exp(b)).astype(BF16), st.astype(BF16))
        st_new = st_new + st * jnp.exp(edge)
    return o, st_new


def _hgrn_kernel(*refs, layer, seq_len):
    carry = seq_len > HG_BLOCK
    it = iter(refs)
    x_ref, mod_ref, nmix_ref = next(it), next(it), next(it)
    w_refs = [next(it) for _ in range(5)]
    lbl_ref, hgn_ref, wbr_ref = next(it), next(it), next(it)
    s0_refs = [next(it), next(it)] if carry else None
    phg_ref = next(it)
    sout_ref = None if carry else next(it)
    hb, wcat, q_s, v_s, g_s, kf_s, kb_s, lff_s, lfb_s, o_s, lvf_s, lvb_s, st_s = it

    j = pl.program_id(1)

    @pl.when(j == 0)
    def _():
        hb[...] = _norm_mod(x_ref[...], nmix_ref[...], mod_ref[0, 1:2, :], mod_ref[0, 0:1, :])
        t = lax.broadcasted_iota(jnp.int32, (HG_BLOCK // 2, HG_BLOCK // 2), 0)
        s = lax.broadcasted_iota(jnp.int32, (HG_BLOCK // 2, HG_BLOCK // 2), 1)
        x = t ^ s
        lv = jnp.zeros_like(x)
        for m in range(HG_LEVELS - 1):
            lv = lv + (x >= 2**m).astype(jnp.int32)
        lvf_s[...] = jnp.where(t >= s, lv, -1)
        lvb_s[...] = jnp.where(t <= s, lv, -1)

    for g, w_ref in enumerate(w_refs):
        wcat[:, g * HG_DK : (g + 1) * HG_DK] = w_ref[...].astype(BF16)
    a0, a1 = lbl_ref[0], lbl_ref[1]
    amax = jnp.maximum(a0, a1)
    e0, e1 = jnp.exp(a0 - amax), jnp.exp(a1 - amax)
    p0, p1 = e0 / (e0 + e1), e1 / (e0 + e1)
    lb = (p0 - p0) if layer == 0 else ((p0 + p1) - p0)
    log_lb = jnp.log(lb)

    wc = wcat[...]
    part = _part_rows(seq_len)
    n_parts = ROWS // part

    def zdot(p):
        return _dot(hb[p * part : (p + 1) * part, :], wc)

    pending = zdot(0)
    for p in range(n_parts):
        z = pending
        if p + 1 < n_parts:
            pending = zdot(p + 1)
        rows = slice(p * part, (p + 1) * part)
        zq, zff, zfb, zi, zg = (z[:, g * HG_DK : (g + 1) * HG_DK] for g in range(5))
        q_s[rows, :] = _silu(zq) * HG_DK**-0.5
        v_s[rows, :] = zi
        g_s[rows, :] = _silu(zg)
        for d, (zf, k_s, lf_s) in enumerate(((zff, kf_s, lff_s), (zfb, kb_s, lfb_s))):
            lf_s[rows, :] = _log1pexp(log_lb[d : d + 1, :] - zf) - _log1pexp(-zf)
            k_s[rows, :] = (1.0 - lb[d : d + 1, :]) * jax.nn.sigmoid(-zf)

    o_s[...] = jnp.zeros_like(o_s)
    n_blk = ROWS // HG_BLOCK
    if carry:
        for d in range(2):
            st_s[d] = s0_refs[d][0].T

    def blocks(n, c):
        for d in range(2):
            forward = d == 0
            blk = n if forward else n_blk - 1 - n
            rows = pl.ds(pl.multiple_of(blk * HG_BLOCK, HG_BLOCK), HG_BLOCK)
            k_s, lf_s, lv_s = (kf_s, lff_s, lvf_s) if forward else (kb_s, lfb_s, lvb_s)
            o, st_new = _hgrn_block(
                q_s[rows, :], k_s[rows, :], v_s[rows, :], lf_s[rows, :], lv_s[...], forward, st_s[d] if carry else None
            )
            o_s[rows, :] += o
            if carry:
                st_s[d] = st_new
            else:
                sout_ref[blk, d, 0] = st_new.T
        return c

    lax.fori_loop(0, n_blk, blocks, 0)

    y = _rms(o_s[...], hgn_ref[...]) * g_s[...]
    contrib = _dot(y.astype(BF16), wbr_ref[...].astype(BF16))

    @pl.when(j == 0)
    def _():
        phg_ref[...] = contrib

    @pl.when(j > 0)
    def _():
        phg_ref[...] += contrib


def _hgrn_call(x, mod, seq_len, layer, norm_mix, w_in, lb_logits, hg_norm, w_branch_hg, state0):
    n_tok = x.shape[0]
    nb = n_tok // ROWS
    carry = seq_len > HG_BLOCK
    per_seq_mod = mod.shape[0] > 1

    in_specs = [
        pl.BlockSpec((ROWS, D_MODEL), lambda i, j: (i, 0)),
        pl.BlockSpec((1, N_MOD, D_MODEL), (lambda i, j: (i, 0, 0)) if per_seq_mod else (lambda i, j: (0, 0, 0))),
        pl.BlockSpec((None, 1, D_MODEL), lambda i, j: (layer, 0, 0)),
    ]
    args = [x, mod, norm_mix.reshape(DEPTH, 1, D_MODEL)]
    for g in range(5):
        in_specs.append(pl.BlockSpec((None, D_MODEL, HG_DK), lambda i, j, g=g: (layer, 0, g * HG_HEADS + j)))
        args.append(w_in)
    in_specs += [
        pl.BlockSpec((DEPTH, 2, HG_DK), lambda i, j: (0, 0, j)),
        pl.BlockSpec((None, 1, HG_DV), lambda i, j: (layer, 0, 0)),
        pl.BlockSpec((None, HG_DV, D_MODEL), lambda i, j: (layer, j, 0)),
    ]
    args += [lb_logits, hg_norm.reshape(DEPTH, 1, HG_DV), w_branch_hg]
    if carry:
        assert seq_len == ROWS
        for d in range(2):
            in_specs.append(
                pl.BlockSpec((1, HG_DK, HG_DV), lambda i, j, d=d: (((i * DEPTH + layer) * 2 + d) * HG_HEADS + j, 0, 0))
            )
            args.append(state0)

    out_shape = [jax.ShapeDtypeStruct((n_tok, D_MODEL), F32)]
    out_specs = [pl.BlockSpec((ROWS, D_MODEL), lambda i, j: (i, 0))]
    if not carry:
        assert seq_len == HG_BLOCK
        n_seq = n_tok // seq_len
        out_shape.append(jax.ShapeDtypeStruct((n_seq, 2, HG_HEADS, HG_DK, HG_DV), F32))
        out_specs.append(pl.BlockSpec((ROWS // seq_len, 2, 1, HG_DK, HG_DV), lambda i, j: (i, 0, j, 0, 0)))

    head = lambda dt=F32: pltpu.VMEM((ROWS, HG_DK), dt)
    scratch = [
        pltpu.VMEM((ROWS, D_MODEL), BF16),
        pltpu.VMEM((D_MODEL, 5 * HG_DK), BF16),
        head(), head(), head(),
        head(), head(), head(), head(),
        head(),
        pltpu.VMEM((HG_BLOCK // 2, HG_BLOCK // 2), jnp.int32),
        pltpu.VMEM((HG_BLOCK // 2, HG_BLOCK // 2), jnp.int32),
        pltpu.VMEM((2, HG_DV, HG_DK), F32),
    ]
    outs = pl.pallas_call(
        functools.partial(_hgrn_kernel, layer=layer, seq_len=seq_len),
        grid=(nb, HG_HEADS),
        in_specs=in_specs,
        out_specs=out_specs,
        out_shape=out_shape,
        scratch_shapes=scratch,
        compiler_params=pltpu.CompilerParams(
            dimension_semantics=("arbitrary", "arbitrary"), vmem_limit_bytes=48 * MIB
        ),
        name=f"hgrn_l{layer}_t{seq_len}",
    )(*args)
    return (outs[0], None) if carry else (outs[0], outs[1])


_MIX_ORDER = (10, 11, 5, 6, 7, 12, 13, 8, 9)


def _mix_col(k):
    idx = 0
    for n, c in enumerate(_MIX_ORDER):
        idx = idx + jnp.where(k == n, c, 0)
    return idx


def _window_mean_minus_self(p, tpos, seq_len, w):
    n = p.shape[0]
    acc = jnp.zeros_like(p)
    for jj in range(-(w // 2), w // 2):
        shifted = p if jj == 0 else pltpu.roll(p, (-jj) % n, 0)
        valid = (tpos + jj >= 0) & (tpos + jj < seq_len)
        acc = acc + jnp.where(valid, shifted, 0.0)
    cnt = jnp.minimum(tpos + w // 2, seq_len) - jnp.maximum(tpos - w // 2, 0)
    return acc / cnt.astype(F32) - p


def _mix_kernel(
    x_ref, mod_ref, nmix_ref, w_ref, sgn_ref, sgw_ref, sgb_ref, wbsg_ref, wbpool_ref, poolw_ref, pscale_ref,
    wout_ref, phg_ref, o_ref, hb, u_s, br_s, mrg_s, *, seq_len,
):
    k = pl.program_id(1)
    half = D_MODEL // 2
    part = _part_rows(seq_len)
    n_parts = ROWS // part

    @pl.when(k == 0)
    def _():
        hb[...] = _norm_mod(x_ref[...], nmix_ref[...], mod_ref[0, 1:2, :], mod_ref[0, 0:1, :])

    def for_z_parts(consume):
        w = w_ref[...].astype(BF16)

        def zdot(p):
            return _dot(hb[p * part : (p + 1) * part, :], w)

        pending = zdot(0)
        for p in range(n_parts):
            z = pending
            if p + 1 < n_parts:
                pending = zdot(p + 1)
            consume(slice(p * part, (p + 1) * part), z)

    for step in (0, 1):

        @pl.when(k == step)
        def _(step=step):
            cols = slice(step * half, (step + 1) * half)

            def gate(rows, z):
                mrg_s[rows, cols] = jax.nn.sigmoid(z)

            for_z_parts(gate)

    @pl.when(k == 2)
    def _():
        def store_u(rows, z):
            u_s[rows, :] = jax.nn.gelu(z)

        for_z_parts(store_u)

    @pl.when(k == 3)
    def _():
        wbsg = wbsg_ref[...].astype(BF16)
        wgs = [sgw_ref[g].astype(BF16) for g in range(SG_GROUPS)]

        def spatial_gating(rows, z):
            v = _rms(jax.nn.gelu(z), sgn_ref[...]).astype(BF16)
            for g in range(SG_GROUPS):
                bias = sgb_ref[:, g : g + 1]
                cols = slice(g * SG_GROUP_DIM, (g + 1) * SG_GROUP_DIM)
                for n in range(part // SG_CHUNK):
                    loc = slice(n * SG_CHUNK, (n + 1) * SG_CHUNK)
                    dst = slice(rows.start + n * SG_CHUNK, rows.start + (n + 1) * SG_CHUNK)
                    mixed = _dot(wgs[g], v[loc, cols]) + bias
                    br_s[dst, cols] = (u_s[dst, cols] * mixed).astype(BF16)
            mrg_s[rows, :] = mrg_s[rows, :] * _dot(br_s[rows, :], wbsg)

        for_z_parts(spatial_gating)

    @pl.when(k == 4)
    def _():
        tpos = lax.broadcasted_iota(jnp.int32, (part, POOL_GROUP_DIM), 0) & (seq_len - 1)

        def pool(rows, z):
            for gi, w in enumerate(POOL_WINDOWS):
                cols = slice(gi * POOL_GROUP_DIM, (gi + 1) * POOL_GROUP_DIM)
                pooled = _window_mean_minus_self(z[:, cols], tpos, seq_len, w)
                out = _dot(pooled.astype(BF16), poolw_ref[gi].astype(BF16)) * pscale_ref[:, cols]
                br_s[rows, cols] = out.astype(BF16)

        for_z_parts(pool)

    for step in (5, 6):

        @pl.when(k == step)
        def _(step=step):
            cols = slice((step - 5) * half, (step - 4) * half)
            wbpool = wbpool_ref[:, cols].astype(BF16)

            def gate(rows, z):
                mrg_s[rows, cols] = mrg_s[rows, cols] + jax.nn.sigmoid(z) * _dot(br_s[rows, :], wbpool)

            for_z_parts(gate)

    @pl.when(k == 7)
    def _():
        cols = slice(0, half)

        def gate(rows, z):
            mrg_s[rows, cols] = mrg_s[rows, cols] + jax.nn.sigmoid(z) * phg_ref[rows, cols]

        for_z_parts(gate)

    @pl.when(k == 8)
    def _():
        cols = slice(half, D_MODEL)
        wout = wout_ref[...].astype(BF16)

        def gate_and_project(rows, z):
            mrg_s[rows, cols] = mrg_s[rows, cols] + jax.nn.sigmoid(z) * phg_ref[rows, cols]
            y = _dot(mrg_s[rows, :].astype(BF16), wout)
            o_ref[rows, :] = x_ref[rows, :] + mod_ref[0, 2:3, :] * y

        for_z_parts(gate_and_project)


def _mix_call(x, phg, mod, seq_len, layer, norm_mix, w_in, sg_norm, sg_w, sg_b, w_branch_sg, w_branch_pool, pool_w,
              pool_scale, w_out):
    n_tok = x.shape[0]
    nb = n_tok // ROWS
    per_seq_mod = mod.shape[0] > 1
    const = pl.Buffered(1)
    assert seq_len & (seq_len - 1) == 0 and ROWS % seq_len == 0 and seq_len % SG_CHUNK == 0
    in_specs = [
        pl.BlockSpec((ROWS, D_MODEL), lambda i, k: (i, 0)),
        pl.BlockSpec((1, N_MOD, D_MODEL), (lambda i, k: (i, 0, 0)) if per_seq_mod else (lambda i, k: (0, 0, 0))),
        pl.BlockSpec((None, 1, D_MODEL), lambda i, k: (layer, 0, 0)),
        pl.BlockSpec((None, D_MODEL, IN_CHUNK), lambda i, k: (layer, 0, _mix_col(k))),
        pl.BlockSpec((None, 1, SG_WIDTH), lambda i, k: (layer, 0, 0)),
        pl.BlockSpec((None, SG_GROUPS, SG_CHUNK, SG_CHUNK), lambda i, k: (layer, 0, 0, 0)),
        pl.BlockSpec((None, SG_CHUNK, SG_GROUPS), lambda i, k: (layer, 0, 0)),
        pl.BlockSpec((None, SG_WIDTH, D_MODEL), lambda i, k: (layer, 0, 0), pipeline_mode=const),
        pl.BlockSpec((None, POOL_WIDTH, D_MODEL), lambda i, k: (layer, 0, 0), pipeline_mode=const),
        pl.BlockSpec((None, len(POOL_WINDOWS), POOL_GROUP_DIM, POOL_GROUP_DIM), lambda i, k: (layer, 0, 0, 0)),
        pl.BlockSpec((None, 1, POOL_WIDTH), lambda i, k: (layer, 0, 0)),
        pl.BlockSpec((None, D_MODEL, D_MODEL), lambda i, k: (layer, 0, 0), pipeline_mode=const),
        pl.BlockSpec((ROWS, D_MODEL), lambda i, k: (i, 0)),
    ]
    args = [
        x, mod, norm_mix.reshape(DEPTH, 1, D_MODEL), w_in, sg_norm.reshape(DEPTH, 1, SG_WIDTH), sg_w,
        jnp.swapaxes(sg_b, 1, 2), w_branch_sg, w_branch_pool, pool_w, pool_scale.reshape(DEPTH, 1, POOL_WIDTH),
        w_out, phg,
    ]
    scratch = [
        pltpu.VMEM((ROWS, D_MODEL), BF16),
        pltpu.VMEM((ROWS, SG_WIDTH), F32),
        pltpu.VMEM((ROWS, SG_WIDTH), BF16),
        pltpu.VMEM((ROWS, D_MODEL), F32),
    ]
    return pl.pallas_call(
        functools.partial(_mix_kernel, seq_len=seq_len),
        grid=(nb, len(_MIX_ORDER)),
        in_specs=in_specs,
        out_specs=pl.BlockSpec((ROWS, D_MODEL), lambda i, k: (i, 0)),
        out_shape=jax.ShapeDtypeStruct((n_tok, D_MODEL), F32),
        scratch_shapes=scratch,
        compiler_params=pltpu.CompilerParams(
            dimension_semantics=("arbitrary", "arbitrary"), vmem_limit_bytes=56 * MIB
        ),
        name=f"mix_l{layer}_t{seq_len}",
    )(*args)


def _ffn_kernel(x_ref, mod_ref, nffn_ref, wa_ref, wb_ref, cwa_ref, cwb_ref, cba_ref, cbb_ref, wd_ref, fin_ref, o_ref,
                hb, acc, *, seq_len, final):
    c = pl.program_id(1)

    @pl.when(c == 0)
    def _():
        hb[...] = _norm_mod(x_ref[...], nffn_ref[...], mod_ref[0, 4:5, :], mod_ref[0, 3:4, :])
        acc[...] = jnp.zeros_like(acc)

    part = _part_rows(seq_len)
    tpos = lax.broadcasted_iota(jnp.int32, (part, FF_CHUNK), 0) & (seq_len - 1)
    has_prev = tpos >= 1
    has_next = tpos < seq_len - 1

    def conv(h, cw_ref, cb_ref):
        prev = jnp.where(has_prev, pltpu.roll(h, 1, 0), 0.0)
        nxt = jnp.where(has_next, pltpu.roll(h, part - 1, 0), 0.0)
        return prev * cw_ref[0:1, :] + h * cw_ref[1:2, :] + nxt * cw_ref[2:3, :] + cb_ref[...]

    wa, wb, wd = wa_ref[...].astype(BF16), wb_ref[...].astype(BF16), wd_ref[...].astype(BF16)
    n_parts = ROWS // part

    def up(p):
        h = hb[p * part : (p + 1) * part, :]
        return _dot(h, wa), _dot(h, wb)

    pending = up(0)
    for p in range(n_parts):
        ha, hb2 = pending
        if p + 1 < n_parts:
            pending = up(p + 1)
        a = conv(ha, cwa_ref, cba_ref)
        b = conv(hb2, cwb_ref, cbb_ref)
        acc[p * part : (p + 1) * part, :] += _dot((_silu(a) * b).astype(BF16), wd)

    @pl.when(c == pl.num_programs(1) - 1)
    def _():
        y = x_ref[...] + mod_ref[0, 5:6, :] * acc[...]
        if final:
            y = _rms(y, fin_ref[...])
        o_ref[...] = y


def _ffn_call(x, mod, seq_len, layer, final, norm_ffn, ffn_up, ffn_conv_w, ffn_conv_b, ffn_down, final_norm):
    n_tok = x.shape[0]
    nb = n_tok // ROWS
    nc = D_FF // FF_CHUNK
    per_seq_mod = mod.shape[0] > 1
    conv_b = ffn_conv_b.reshape(DEPTH, 1, 2 * D_FF)
    in_specs = [
        pl.BlockSpec((ROWS, D_MODEL), lambda i, c: (i, 0)),
        pl.BlockSpec((1, N_MOD, D_MODEL), (lambda i, c: (i, 0, 0)) if per_seq_mod else (lambda i, c: (0, 0, 0))),
        pl.BlockSpec((None, 1, D_MODEL), lambda i, c: (layer, 0, 0)),
        pl.BlockSpec((None, D_MODEL, FF_CHUNK), lambda i, c: (layer, 0, c)),
        pl.BlockSpec((None, D_MODEL, FF_CHUNK), lambda i, c: (layer, 0, nc + c)),
        pl.BlockSpec((None, 3, FF_CHUNK), lambda i, c: (layer, 0, c)),
        pl.BlockSpec((None, 3, FF_CHUNK), lambda i, c: (layer, 0, nc + c)),
        pl.BlockSpec((None, 1, FF_CHUNK), lambda i, c: (layer, 0, c)),
        pl.BlockSpec((None, 1, FF_CHUNK), lambda i, c: (layer, 0, nc + c)),
        pl.BlockSpec((None, FF_CHUNK, D_MODEL), lambda i, c: (layer, c, 0)),
        pl.BlockSpec((1, D_MODEL), lambda i, c: (0, 0)),
    ]
    args = [x, mod, norm_ffn.reshape(DEPTH, 1, D_MODEL), ffn_up, ffn_up, ffn_conv_w, ffn_conv_w, conv_b, conv_b,
            ffn_down, final_norm.reshape(1, D_MODEL)]
    return pl.pallas_call(
        functools.partial(_ffn_kernel, seq_len=seq_len, final=final),
        grid=(nb, nc),
        in_specs=in_specs,
        out_specs=pl.BlockSpec((ROWS, D_MODEL), lambda i, c: (i, 0)),
        out_shape=jax.ShapeDtypeStruct((n_tok, D_MODEL), F32),
        scratch_shapes=[pltpu.VMEM((ROWS, D_MODEL), BF16), pltpu.VMEM((ROWS, D_MODEL), F32)],
        compiler_params=pltpu.CompilerParams(
            dimension_semantics=("arbitrary", "arbitrary"), vmem_limit_bytes=40 * MIB
        ),
        name=f"ffn_l{layer}_t{seq_len}",
    )(*args)


def kernel(x_prompt, x_sample, c, state_hgrn, c_ctx, norm_mix, norm_ffn, w_ada, b_ada, w_in, lb_logits, hg_norm,
           w_branch_hg, w_branch_sg, w_branch_pool, w_out, sg_norm, sg_w, sg_b, pool_w, pool_scale, ffn_up,
           ffn_conv_w, ffn_conv_b, ffn_down, final_norm):
    n_ctx, t_ctx, _ = x_prompt.shape
    n_lat, t_lat, _ = x_sample.shape

    n_cond = 1 + n_lat
    pad = -n_cond % V7X_SUBLANES
    cvec = jnp.concatenate([c_ctx[None, :], c, jnp.zeros((pad, D_MODEL), F32)], axis=0)
    mod = _mod_call(cvec, w_ada, b_ada).reshape(DEPTH, n_cond + pad, N_MOD, D_MODEL)

    xs = _addpos_call(x_sample, _grid_pos_embed(t_lat)).reshape(n_lat * t_lat, D_MODEL)
    xp = x_prompt.reshape(n_ctx * t_ctx, D_MODEL)
    state0 = state_hgrn.reshape(n_lat * DEPTH * 2 * HG_HEADS, HG_DK, HG_DV)

    new_states = []
    for layer in range(DEPTH):
        final = layer == DEPTH - 1
        groups = []
        for x, m, t, s0 in ((xp, mod[layer, 0:1], t_ctx, None), (xs, mod[layer, 1:n_cond], t_lat, state0)):
            phg, s_fin = _hgrn_call(x, m, t, layer, norm_mix, w_in, lb_logits, hg_norm, w_branch_hg, s0)
            x1 = _mix_call(x, phg, m, t, layer, norm_mix, w_in, sg_norm, sg_w, sg_b, w_branch_sg, w_branch_pool,
                           pool_w, pool_scale, w_out)
            x2 = _ffn_call(x1, m, t, layer, final, norm_ffn, ffn_up, ffn_conv_w, ffn_conv_b, ffn_down, final_norm)
            groups.append((x2, s_fin))
        (xp, s_ctx), (xs, _) = groups
        new_states.append(s_ctx)

    y_prompt = xp.reshape(x_prompt.shape)
    y_sample = xs.reshape(x_sample.shape)
    new_state_hgrn = jnp.stack(new_states, axis=1)
    return (y_prompt, y_sample, new_state_hgrn)
```

```python
import functools

import jax
import jax.numpy as jnp
import numpy as np
from jax import lax
from jax.experimental import pallas as pl
from jax.experimental.pallas import tpu as pltpu

D_MODEL = 1024
DEPTH = 2
GRID_W = 64
POS_BASE = 10000.0
EPS = 1e-6
HG_HEADS = 4
HG_DK = 128
HG_DV = 128
HG_WIDTH = HG_HEADS * HG_DV
SG_GROUPS = 4
SG_WIDTH = 512
SG_GROUP_DIM = SG_WIDTH // SG_GROUPS
SG_CHUNK = 128
POOL_WINDOWS = (2, 4, 8, 16)
POOL_WIDTH = 512
POOL_GROUP_DIM = POOL_WIDTH // len(POOL_WINDOWS)
IN_COLS = 5 * HG_WIDTH + 2 * SG_WIDTH + POOL_WIDTH + 3 * D_MODEL
D_FF = 2816
N_MOD = 6

V7X_LANES = 128
V7X_SUBLANES = 8
V7X_MXU_DIM = 256
MIB = 2**20

ROWS = 1024
PART_ROWS = 512
HG_BLOCK = 256
HG_LEVELS = 8
IN_CHUNK = 512
FF_CHUNK = V7X_MXU_DIM
MOD_CHUNK = 1536

NEG_LOG2E = -1.4426950408889634

F32 = jnp.float32
BF16 = jnp.bfloat16


def _dot(a, b):
    return lax.dot_general(a, b, (((1,), (0,)), ((), ())), preferred_element_type=F32)


def _dot_nt(a, b):
    return lax.dot_general(a, b, (((1,), (1,)), ((), ())), preferred_element_type=F32)


def _dot_tn(a, b):
    return lax.dot_general(a, b, (((0,), (0,)), ((), ())), preferred_element_type=F32)


def _silu(x):
    return x * jax.nn.sigmoid(x)


def _rms(x, gain):
    return x * lax.rsqrt(jnp.mean(x * x, axis=-1, keepdims=True) + EPS) * gain


def _log1pexp(y):
    return jnp.maximum(y, 0.0) + jnp.log(1.0 + jnp.exp(-jnp.abs(y)))


def _norm_mod(x, gain, scale, shift):
    return (_rms(x, gain) * (1.0 + scale) + shift).astype(BF16)


def _part_rows(seq_len):
    return max(seq_len, PART_ROWS)


def _mod_kernel(c_ref, w_ref, b_ref, o_ref):
    c = _silu(c_ref[...]).astype(BF16)
    o_ref[...] = _dot(c, w_ref[...].astype(BF16)) + b_ref[...]


def _mod_call(cvec, w_ada, b_ada):
    n_rows = cvec.shape[0]
    n_cols = N_MOD * D_MODEL
    return pl.pallas_call(
        _mod_kernel,
        grid=(DEPTH, n_cols // MOD_CHUNK),
        in_specs=[
            pl.BlockSpec((n_rows, D_MODEL), lambda l, n: (0, 0)),
            pl.BlockSpec((None, D_MODEL, MOD_CHUNK), lambda l, n: (l, 0, n)),
            pl.BlockSpec((None, 1, MOD_CHUNK), lambda l, n: (l, 0, n)),
        ],
        out_specs=pl.BlockSpec((None, n_rows, MOD_CHUNK), lambda l, n: (l, 0, n)),
        out_shape=jax.ShapeDtypeStruct((DEPTH, n_rows, n_cols), F32),
        compiler_params=pltpu.CompilerParams(
            dimension_semantics=("arbitrary", "arbitrary"), vmem_limit_bytes=32 * MIB
        ),
        name="adaln_mod",
    )(cvec, w_ada, b_ada.reshape(DEPTH, 1, n_cols))


def _addpos_kernel(x_ref, p_ref, o_ref):
    o_ref[...] = x_ref[...] + p_ref[...]


def _addpos_call(x, pos):
    b, t, d = x.shape
    return pl.pallas_call(
        _addpos_kernel,
        grid=(b,),
        in_specs=[pl.BlockSpec((None, t, d), lambda i: (i, 0, 0)), pl.BlockSpec((t, d), lambda i: (0, 0))],
        out_specs=pl.BlockSpec((None, t, d), lambda i: (i, 0, 0)),
        out_shape=jax.ShapeDtypeStruct(x.shape, x.dtype),
        compiler_params=pltpu.CompilerParams(dimension_semantics=("arbitrary",), vmem_limit_bytes=32 * MIB),
        name="add_pos",
    )(x, pos)


def _grid_pos_embed(n_tokens):
    rows = n_tokens // GRID_W
    r = np.broadcast_to(np.arange(rows, dtype=np.float32)[:, None], (rows, GRID_W)).reshape(-1)
    col = np.broadcast_to(np.arange(GRID_W, dtype=np.float32)[None, :], (rows, GRID_W)).reshape(-1)
    quarter = D_MODEL // 4
    omega = (1.0 / (np.float32(POS_BASE) ** (np.arange(quarter, dtype=np.float32) / quarter))).astype(np.float32)
    ar = r[:, None] * omega[None, :]
    ac = col[:, None] * omega[None, :]
    return jnp.asarray(np.concatenate([np.sin(ar), np.cos(ar), np.sin(ac), np.cos(ac)], axis=-1), F32)


def _ref_rows(b, blk, r):
    n, c = b.shape
    if blk >= V7X_SUBLANES:
        x3 = b.reshape(n // blk, blk, c)
        return jnp.broadcast_to(x3[:, r : r + 1, :], x3.shape).reshape(n, c)
    x3 = b.reshape(n // V7X_SUBLANES, V7X_SUBLANES, c)
    sub = lax.broadcasted_iota(jnp.int32, x3.shape, 1)
    bases = list(range(0, V7X_SUBLANES, blk))
    out = jnp.broadcast_to(x3[:, bases[-1] + r : bases[-1] + r + 1, :], x3.shape)
    for base in reversed(bases[:-1]):
        out = jnp.where(sub < base + blk, jnp.broadcast_to(x3[:, base + r : base + r + 1, :], x3.shape), out)
    return out.reshape(n, c)


def _cum_logdecay(lf, tri):
    hi = lf.astype(BF16)
    r1 = lf - hi.astype(F32)
    mid = r1.astype(BF16)
    lo = (r1 - mid.astype(F32)).astype(BF16)
    return _dot(tri, hi) + _dot(tri, mid) + _dot(tri, lo)


def _hgrn_block(q, k, v, b, lvq, forward, st):
    half = HG_BLOCK // 2
    lo, hi = slice(0, half), slice(half, HG_BLOCK)
    vb = v.astype(BF16)
    qb, kb = q.astype(BF16), k.astype(BF16)
    diag = [jnp.where(lvq == 0, _dot_nt(qb[h], kb[h]), 0.0) for h in (lo, hi)]
    for m in range(1, HG_LEVELS):
        blk = 2**m
        ref = _ref_rows(b, blk, blk // 2 - 1 if forward else blk // 2)
        e = jnp.exp2(jnp.abs(b - ref) * NEG_LOG2E).astype(BF16)
        qt, kt = qb * e, kb * e
        diag = [jnp.where(lvq == m, _dot_nt(qt[h], kt[h]), a) for h, a in zip((lo, hi), diag)]
    mid = half - 1 if forward else half
    e = jnp.exp2(jnp.abs(b - b[mid : mid + 1, :]) * NEG_LOG2E).astype(BF16)
    qt, kt = qb * e, kb * e
    a_lo, a_hi = (a.astype(BF16) for a in diag)
    if forward:
        cross = _dot_nt(qt[hi], kt[lo]).astype(BF16)
        o = jnp.concatenate([_dot(a_lo, vb[lo]), _dot(cross, vb[lo]) + _dot(a_hi, vb[hi])], axis=0)
    else:
        cross = _dot_nt(qt[lo], kt[hi]).astype(BF16)
        o = jnp.concatenate([_dot(a_lo, vb[lo]) + _dot(cross, vb[hi]), _dot(a_hi, vb[hi])], axis=0)
    edge = b[HG_BLOCK - 1 : HG_BLOCK, :] if forward else b[0:1, :]
    k_end = (k * jnp.exp(edge - b)).astype(BF16)
    st_new = _dot_tn(vb, k_end)
    if st is not None:
        o = o + _dot_nt((q * jnp.exp(b)).astype(BF16), st.astype(BF16))
        st_new = st_new + st * jnp.exp(edge)
    return o, st_new


def _hgrn_kernel(*refs, layer, seq_len):
    carry = seq_len > HG_BLOCK
    it = iter(refs)
    x_ref, mod_ref, nmix_ref = next(it), next(it), next(it)
    w_refs = [next(it) for _ in range(5)]
    lbl_ref, hgn_ref, wbr_ref = next(it), next(it), next(it)
    s0_refs = [next(it), next(it)] if carry else None
    phg_ref = next(it)
    sout_ref = None if carry else next(it)
    hb, wcat, q_s, v_s, g_s, kf_s, kb_s, bf_s, bb_s, o_s, y_s, lvf_s, lvb_s, trif_s, trib_s, st_s = it

    j = pl.program_id(1)

    @pl.when(j == 0)
    def _():
        hb[...] = _norm_mod(x_ref[...], nmix_ref[...], mod_ref[0, 1:2, :], mod_ref[0, 0:1, :])
        t = lax.broadcasted_iota(jnp.int32, (HG_BLOCK // 2, HG_BLOCK // 2), 0)
        s = lax.broadcasted_iota(jnp.int32, (HG_BLOCK // 2, HG_BLOCK // 2), 1)
        x = t ^ s
        lv = jnp.zeros_like(x)
        for m in range(HG_LEVELS - 1):
            lv = lv + (x >= 2**m).astype(jnp.int32)
        lvf_s[...] = jnp.where(t >= s, lv, -1)
        lvb_s[...] = jnp.where(t <= s, lv, -1)
        t = lax.broadcasted_iota(jnp.int32, (HG_BLOCK, HG_BLOCK), 0)
        s = lax.broadcasted_iota(jnp.int32, (HG_BLOCK, HG_BLOCK), 1)
        trif_s[...] = (t >= s).astype(BF16)
        trib_s[...] = (t <= s).astype(BF16)

    for g, w_ref in enumerate(w_refs):
        wcat[:, g * HG_DK : (g + 1) * HG_DK] = w_ref[...].astype(BF16)
    a0, a1 = lbl_ref[0], lbl_ref[1]
    amax = jnp.maximum(a0, a1)
    e0, e1 = jnp.exp(a0 - amax), jnp.exp(a1 - amax)
    p0, p1 = e0 / (e0 + e1), e1 / (e0 + e1)
    lb = (p0 - p0) if layer == 0 else ((p0 + p1) - p0)
    log_lb = jnp.log(lb)

    wc = wcat[...]
    part = PART_ROWS
    n_parts = ROWS // part

    def zdot(p):
        return _dot(hb[p * part : (p + 1) * part, :], wc)

    pending = zdot(0)
    for p in range(n_parts):
        z = pending
        if p + 1 < n_parts:
            pending = zdot(p + 1)
        rows = slice(p * part, (p + 1) * part)
        zq, zff, zfb, zi, zg = (z[:, g * HG_DK : (g + 1) * HG_DK] for g in range(5))
        q_s[rows, :] = _silu(zq) * HG_DK**-0.5
        v_s[rows, :] = zi
        g_s[rows, :] = _silu(zg)
        for d, (zf, k_s, b_s, tri_s) in enumerate(((zff, kf_s, bf_s, trif_s), (zfb, kb_s, bb_s, trib_s))):
            lf = _log1pexp(log_lb[d : d + 1, :] - zf) - _log1pexp(-zf)
            k_s[rows, :] = (1.0 - lb[d : d + 1, :]) * jax.nn.sigmoid(-zf)
            tri = tri_s[...]
            for n in range(part // HG_BLOCK):
                loc = slice(n * HG_BLOCK, (n + 1) * HG_BLOCK)
                dst = slice(rows.start + n * HG_BLOCK, rows.start + (n + 1) * HG_BLOCK)
                b_s[dst, :] = _cum_logdecay(lf[loc, :], tri)

    o_s[...] = jnp.zeros_like(o_s)
    n_blk = ROWS // HG_BLOCK
    if carry:
        for d in range(2):
            st_s[d] = s0_refs[d][0].T

    def blocks(n, c):
        for d in range(2):
            forward = d == 0
            blk = n if forward else n_blk - 1 - n
            rows = pl.ds(pl.multiple_of(blk * HG_BLOCK, HG_BLOCK), HG_BLOCK)
            k_s, b_s, lv_s = (kf_s, bf_s, lvf_s) if forward else (kb_s, bb_s, lvb_s)
            o, st_new = _hgrn_block(
                q_s[rows, :], k_s[rows, :], v_s[rows, :], b_s[rows, :], lv_s[...], forward, st_s[d] if carry else None
            )
            o_s[rows, :] += o
            if carry:
                st_s[d] = st_new
            else:
                sout_ref[blk, d, 0] = st_new.T
        return c

    lax.fori_loop(0, n_blk, blocks, 0)

    y_s[j] = (_rms(o_s[...], hgn_ref[...]) * g_s[...]).astype(BF16)

    @pl.when(j == HG_HEADS - 1)
    def _():
        y = jnp.concatenate([y_s[h] for h in range(HG_HEADS)], axis=1)
        phg_ref[...] = _dot(y, wbr_ref[...].astype(BF16))


def _hgrn_call(x, mod, seq_len, layer, norm_mix, w_in, lb_logits, hg_norm, w_branch_hg, state0):
    n_tok = x.shape[0]
    nb = n_tok // ROWS
    carry = seq_len > HG_BLOCK
    per_seq_mod = mod.shape[0] > 1

    in_specs = [
        pl.BlockSpec((ROWS, D_MODEL), lambda i, j: (i, 0)),
        pl.BlockSpec((1, N_MOD, D_MODEL), (lambda i, j: (i, 0, 0)) if per_seq_mod else (lambda i, j: (0, 0, 0))),
        pl.BlockSpec((None, 1, D_MODEL), lambda i, j: (layer, 0, 0)),
    ]
    args = [x, mod, norm_mix.reshape(DEPTH, 1, D_MODEL)]
    for g in range(5):
        in_specs.append(pl.BlockSpec((None, D_MODEL, HG_DK), lambda i, j, g=g: (layer, 0, g * HG_HEADS + j)))
        args.append(w_in)
    in_specs += [
        pl.BlockSpec((DEPTH, 2, HG_DK), lambda i, j: (0, 0, j)),
        pl.BlockSpec((None, 1, HG_DV), lambda i, j: (layer, 0, 0)),
        pl.BlockSpec((None, HG_WIDTH, D_MODEL), lambda i, j: (layer, 0, 0), pipeline_mode=pl.Buffered(1)),
    ]
    args += [lb_logits, hg_norm.reshape(DEPTH, 1, HG_DV), w_branch_hg]
    if carry:
        assert seq_len == ROWS
        for d in range(2):
            in_specs.append(
                pl.BlockSpec((1, HG_DK, HG_DV), lambda i, j, d=d: (((i * DEPTH + layer) * 2 + d) * HG_HEADS + j, 0, 0))
            )
            args.append(state0)

    out_shape = [jax.ShapeDtypeStruct((n_tok, D_MODEL), F32)]
    out_specs = [pl.BlockSpec((ROWS, D_MODEL), lambda i, j: (i, 0))]
    if not carry:
        assert seq_len == HG_BLOCK
        n_seq = n_tok // seq_len
        out_shape.append(jax.ShapeDtypeStruct((n_seq, 2, HG_HEADS, HG_DK, HG_DV), F32))
        out_specs.append(pl.BlockSpec((ROWS // seq_len, 2, 1, HG_DK, HG_DV), lambda i, j: (i, 0, j, 0, 0)))

    head = lambda dt=F32: pltpu.VMEM((ROWS, HG_DK), dt)
    scratch = [
        pltpu.VMEM((ROWS, D_MODEL), BF16),
        pltpu.VMEM((D_MODEL, 5 * HG_DK), BF16),
        head(), head(), head(),
        head(), head(), head(), head(),
        head(),
        pltpu.VMEM((HG_HEADS, ROWS, HG_DV), BF16),
        pltpu.VMEM((HG_BLOCK // 2, HG_BLOCK // 2), jnp.int32),
        pltpu.VMEM((HG_BLOCK // 2, HG_BLOCK // 2), jnp.int32),
        pltpu.VMEM((HG_BLOCK, HG_BLOCK), BF16),
        pltpu.VMEM((HG_BLOCK, HG_BLOCK), BF16),
        pltpu.VMEM((2, HG_DV, HG_DK), F32),
    ]
    outs = pl.pallas_call(
        functools.partial(_hgrn_kernel, layer=layer, seq_len=seq_len),
        grid=(nb, HG_HEADS),
        in_specs=in_specs,
        out_specs=out_specs,
        out_shape=out_shape,
        scratch_shapes=scratch,
        compiler_params=pltpu.CompilerParams(
            dimension_semantics=("arbitrary", "arbitrary"), vmem_limit_bytes=48 * MIB
        ),
        name=f"hgrn_l{layer}_t{seq_len}",
    )(*args)
    return (outs[0], None) if carry else (outs[0], outs[1])


_MIX_ORDER = (10, 11, 5, 6, 7, 12, 13, 8, 9)


def _mix_col(k):
    idx = 0
    for n, c in enumerate(_MIX_ORDER):
        idx = idx + jnp.where(k == n, c, 0)
    return idx


def _window_mean_minus_self(p, tpos, seq_len, w):
    n = p.shape[0]
    acc = jnp.zeros_like(p)
    for jj in range(-(w // 2), w // 2):
        shifted = p if jj == 0 else pltpu.roll(p, (-jj) % n, 0)
        valid = (tpos + jj >= 0) & (tpos + jj < seq_len)
        acc = acc + jnp.where(valid, shifted, 0.0)
    cnt = jnp.minimum(tpos + w // 2, seq_len) - jnp.maximum(tpos - w // 2, 0)
    return acc / cnt.astype(F32) - p


def _mix_kernel(
    x_ref, mod_ref, nmix_ref, w_ref, sgn_ref, sgw_ref, sgb_ref, wbsg_ref, wbpool_ref, poolw_ref, pscale_ref,
    wout_ref, phg_ref, o_ref, hb, u_s, br_s, mrg_s, *, seq_len,
):
    k = pl.program_id(1)
    half = D_MODEL // 2

    @pl.when(k == 0)
    def _():
        hb[...] = _norm_mod(x_ref[...], nmix_ref[...], mod_ref[0, 1:2, :], mod_ref[0, 0:1, :])

    def for_z_parts(consume, part=PART_ROWS):
        w = w_ref[...].astype(BF16)
        n_parts = ROWS // part

        def zdot(p):
            return _dot(hb[p * part : (p + 1) * part, :], w)

        pending = zdot(0)
        for p in range(n_parts):
            z = pending
            if p + 1 < n_parts:
                pending = zdot(p + 1)
            consume(slice(p * part, (p + 1) * part), z)

    for step in (0, 1):

        @pl.when(k == step)
        def _(step=step):
            cols = slice(step * half, (step + 1) * half)

            def gate(rows, z):
                mrg_s[rows, cols] = jax.nn.sigmoid(z)

            for_z_parts(gate)

    @pl.when(k == 2)
    def _():
        def store_u(rows, z):
            u_s[rows, :] = jax.nn.gelu(z)

        for_z_parts(store_u)

    @pl.when(k == 3)
    def _():
        wbsg = wbsg_ref[...].astype(BF16)
        wgs = [sgw_ref[g].astype(BF16) for g in range(SG_GROUPS)]

        def spatial_gating(rows, z):
            v = _rms(jax.nn.gelu(z), sgn_ref[...]).astype(BF16)
            for g in range(SG_GROUPS):
                bias = sgb_ref[:, g : g + 1]
                cols = slice(g * SG_GROUP_DIM, (g + 1) * SG_GROUP_DIM)
                for n in range((rows.stop - rows.start) // SG_CHUNK):
                    loc = slice(n * SG_CHUNK, (n + 1) * SG_CHUNK)
                    dst = slice(rows.start + n * SG_CHUNK, rows.start + (n + 1) * SG_CHUNK)
                    mixed = _dot(wgs[g], v[loc, cols]) + bias
                    br_s[dst, cols] = (u_s[dst, cols] * mixed).astype(BF16)
            mrg_s[rows, :] = mrg_s[rows, :] * _dot(br_s[rows, :], wbsg)

        for_z_parts(spatial_gating)

    @pl.when(k == 4)
    def _():
        part = _part_rows(seq_len)
        tpos = lax.broadcasted_iota(jnp.int32, (part, POOL_GROUP_DIM), 0) & (seq_len - 1)

        def pool(rows, z):
            for gi, w in enumerate(POOL_WINDOWS):
                cols = slice(gi * POOL_GROUP_DIM, (gi + 1) * POOL_GROUP_DIM)
                pooled = _window_mean_minus_self(z[:, cols], tpos, seq_len, w)
                out = _dot(pooled.astype(BF16), poolw_ref[gi].astype(BF16)) * pscale_ref[:, cols]
                br_s[rows, cols] = out.astype(BF16)

        for_z_parts(pool, part)

    for step in (5, 6):

        @pl.when(k == step)
        def _(step=step):
            cols = slice((step - 5) * half, (step - 4) * half)
            wbpool = wbpool_ref[:, cols].astype(BF16)

            def gate(rows, z):
                mrg_s[rows, cols] = mrg_s[rows, cols] + jax.nn.sigmoid(z) * _dot(br_s[rows, :], wbpool)

            for_z_parts(gate)

    @pl.when(k == 7)
    def _():
        cols = slice(0, half)

        def gate(rows, z):
            mrg_s[rows, cols] = mrg_s[rows, cols] + jax.nn.sigmoid(z) * phg_ref[rows, cols]

        for_z_parts(gate)

    @pl.when(k == 8)
    def _():
        cols = slice(half, D_MODEL)
        wout = wout_ref[...].astype(BF16)

        def gate_and_project(rows, z):
            mrg_s[rows, cols] = mrg_s[rows, cols] + jax.nn.sigmoid(z) * phg_ref[rows, cols]
            y = _dot(mrg_s[rows, :].astype(BF16), wout)
            o_ref[rows, :] = x_ref[rows, :] + mod_ref[0, 2:3, :] * y

        for_z_parts(gate_and_project)


def _mix_call(x, phg, mod, seq_len, layer, norm_mix, w_in, sg_norm, sg_w, sg_b, w_branch_sg, w_branch_pool, pool_w,
              pool_scale, w_out):
    n_tok = x.shape[0]
    nb = n_tok // ROWS
    per_seq_mod = mod.shape[0] > 1
    const = pl.Buffered(1)
    assert seq_len & (seq_len - 1) == 0 and ROWS % seq_len == 0 and seq_len % SG_CHUNK == 0
    in_specs = [
        pl.BlockSpec((ROWS, D_MODEL), lambda i, k: (i, 0)),
        pl.BlockSpec((1, N_MOD, D_MODEL), (lambda i, k: (i, 0, 0)) if per_seq_mod else (lambda i, k: (0, 0, 0))),
        pl.BlockSpec((None, 1, D_MODEL), lambda i, k: (layer, 0, 0)),
        pl.BlockSpec((None, D_MODEL, IN_CHUNK), lambda i, k: (layer, 0, _mix_col(k))),
        pl.BlockSpec((None, 1, SG_WIDTH), lambda i, k: (layer, 0, 0)),
        pl.BlockSpec((None, SG_GROUPS, SG_CHUNK, SG_CHUNK), lambda i, k: (layer, 0, 0, 0)),
        pl.BlockSpec((None, SG_CHUNK, SG_GROUPS), lambda i, k: (layer, 0, 0)),
        pl.BlockSpec((None, SG_WIDTH, D_MODEL), lambda i, k: (layer, 0, 0), pipeline_mode=const),
        pl.BlockSpec((None, POOL_WIDTH, D_MODEL), lambda i, k: (layer, 0, 0), pipeline_mode=const),
        pl.BlockSpec((None, len(POOL_WINDOWS), POOL_GROUP_DIM, POOL_GROUP_DIM), lambda i, k: (layer, 0, 0, 0)),
        pl.BlockSpec((None, 1, POOL_WIDTH), lambda i, k: (layer, 0, 0)),
        pl.BlockSpec((None, D_MODEL, D_MODEL), lambda i, k: (layer, 0, 0), pipeline_mode=const),
        pl.BlockSpec((ROWS, D_MODEL), lambda i, k: (i, 0)),
    ]
    args = [
        x, mod, norm_mix.reshape(DEPTH, 1, D_MODEL), w_in, sg_norm.reshape(DEPTH, 1, SG_WIDTH), sg_w,
        jnp.swapaxes(sg_b, 1, 2), w_branch_sg, w_branch_pool, pool_w, pool_scale.reshape(DEPTH, 1, POOL_WIDTH),
        w_out, phg,
    ]
    scratch = [
        pltpu.VMEM((ROWS, D_MODEL), BF16),
        pltpu.VMEM((ROWS, SG_WIDTH), F32),
        pltpu.VMEM((ROWS, SG_WIDTH), BF16),
        pltpu.VMEM((ROWS, D_MODEL), F32),
    ]
    return pl.pallas_call(
        functools.partial(_mix_kernel, seq_len=seq_len),
        grid=(nb, len(_MIX_ORDER)),
        in_specs=in_specs,
        out_specs=pl.BlockSpec((ROWS, D_MODEL), lambda i, k: (i, 0)),
        out_shape=jax.ShapeDtypeStruct((n_tok, D_MODEL), F32),
        scratch_shapes=scratch,
        compiler_params=pltpu.CompilerParams(
            dimension_semantics=("arbitrary", "arbitrary"), vmem_limit_bytes=56 * MIB
        ),
        name=f"mix_l{layer}_t{seq_len}",
    )(*args)


def _ffn_kernel(x_ref, mod_ref, nffn_ref, wa_ref, wb_ref, cwa_ref, cwb_ref, cba_ref, cbb_ref, wd_ref, fin_ref, o_ref,
                hb, acc, *, seq_len, final):
    c = pl.program_id(1)

    @pl.when(c == 0)
    def _():
        hb[...] = _norm_mod(x_ref[...], nffn_ref[...], mod_ref[0, 4:5, :], mod_ref[0, 3:4, :])
        acc[...] = jnp.zeros_like(acc)

    part = _part_rows(seq_len)
    tpos = lax.broadcasted_iota(jnp.int32, (part, FF_CHUNK), 0) & (seq_len - 1)
    has_prev = tpos >= 1
    has_next = tpos < seq_len - 1

    def conv(h, cw_ref, cb_ref):
        prev = jnp.where(has_prev, pltpu.roll(h, 1, 0), 0.0)
        nxt = jnp.where(has_next, pltpu.roll(h, part - 1, 0), 0.0)
        return prev * cw_ref[0:1, :] + h * cw_ref[1:2, :] + nxt * cw_ref[2:3, :] + cb_ref[...]

    wa, wb, wd = wa_ref[...].astype(BF16), wb_ref[...].astype(BF16), wd_ref[...].astype(BF16)
    n_parts = ROWS // part

    def up(p):
        h = hb[p * part : (p + 1) * part, :]
        return _dot(h, wa), _dot(h, wb)

    pending = up(0)
    for p in range(n_parts):
        ha, hb2 = pending
        if p + 1 < n_parts:
            pending = up(p + 1)
        a = conv(ha, cwa_ref, cba_ref)
        b = conv(hb2, cwb_ref, cbb_ref)
        acc[p * part : (p + 1) * part, :] += _dot((_silu(a) * b).astype(BF16), wd)

    @pl.when(c == pl.num_programs(1) - 1)
    def _():
        y = x_ref[...] + mod_ref[0, 5:6, :] * acc[...]
        if final:
            y = _rms(y, fin_ref[...])
        o_ref[...] = y


def _ffn_call(x, mod, seq_len, layer, final, norm_ffn, ffn_up, ffn_conv_w, ffn_conv_b, ffn_down, final_norm):
    n_tok = x.shape[0]
    nb = n_tok // ROWS
    nc = D_FF // FF_CHUNK
    per_seq_mod = mod.shape[0] > 1
    conv_b = ffn_conv_b.reshape(DEPTH, 1, 2 * D_FF)
    in_specs = [
        pl.BlockSpec((ROWS, D_MODEL), lambda i, c: (i, 0)),
        pl.BlockSpec((1, N_MOD, D_MODEL), (lambda i, c: (i, 0, 0)) if per_seq_mod else (lambda i, c: (0, 0, 0))),
        pl.BlockSpec((None, 1, D_MODEL), lambda i, c: (layer, 0, 0)),
        pl.BlockSpec((None, D_MODEL, FF_CHUNK), lambda i, c: (layer, 0, c)),
        pl.BlockSpec((None, D_MODEL, FF_CHUNK), lambda i, c: (layer, 0, nc + c)),
        pl.BlockSpec((None, 3, FF_CHUNK), lambda i, c: (layer, 0, c)),
        pl.BlockSpec((None, 3, FF_CHUNK), lambda i, c: (layer, 0, nc + c)),
        pl.BlockSpec((None, 1, FF_CHUNK), lambda i, c: (layer, 0, c)),
        pl.BlockSpec((None, 1, FF_CHUNK), lambda i, c: (layer, 0, nc + c)),
        pl.BlockSpec((None, FF_CHUNK, D_MODEL), lambda i, c: (layer, c, 0)),
        pl.BlockSpec((1, D_MODEL), lambda i, c: (0, 0)),
    ]
    args = [x, mod, norm_ffn.reshape(DEPTH, 1, D_MODEL), ffn_up, ffn_up, ffn_conv_w, ffn_conv_w, conv_b, conv_b,
            ffn_down, final_norm.reshape(1, D_MODEL)]
    return pl.pallas_call(
        functools.partial(_ffn_kernel, seq_len=seq_len, final=final),
        grid=(nb, nc),
        in_specs=in_specs,
        out_specs=pl.BlockSpec((ROWS, D_MODEL), lambda i, c: (i, 0)),
        out_shape=jax.ShapeDtypeStruct((n_tok, D_MODEL), F32),
        scratch_shapes=[pltpu.VMEM((ROWS, D_MODEL), BF16), pltpu.VMEM((ROWS, D_MODEL), F32)],
        compiler_params=pltpu.CompilerParams(
            dimension_semantics=("arbitrary", "arbitrary"), vmem_limit_bytes=40 * MIB
        ),
        name=f"ffn_l{layer}_t{seq_len}",
    )(*args)


def kernel(x_prompt, x_sample, c, state_hgrn, c_ctx, norm_mix, norm_ffn, w_ada, b_ada, w_in, lb_logits, hg_norm,
           w_branch_hg, w_branch_sg, w_branch_pool, w_out, sg_norm, sg_w, sg_b, pool_w, pool_scale, ffn_up,
           ffn_conv_w, ffn_conv_b, ffn_down, final_norm):
    n_ctx, t_ctx, _ = x_prompt.shape
    n_lat, t_lat, _ = x_sample.shape

    n_cond = 1 + n_lat
    pad = -n_cond % V7X_SUBLANES
    cvec = jnp.concatenate([c_ctx[None, :], c, jnp.zeros((pad, D_MODEL), F32)], axis=0)
    mod = _mod_call(cvec, w_ada, b_ada).reshape(DEPTH, n_cond + pad, N_MOD, D_MODEL)

    xs = _addpos_call(x_sample, _grid_pos_embed(t_lat)).reshape(n_lat * t_lat, D_MODEL)
    xp = x_prompt.reshape(n_ctx * t_ctx, D_MODEL)
    state0 = state_hgrn.reshape(n_lat * DEPTH * 2 * HG_HEADS, HG_DK, HG_DV)

    new_states = []
    for layer in range(DEPTH):
        final = layer == DEPTH - 1
        groups = []
        for x, m, t, s0 in ((xp, mod[layer, 0:1], t_ctx, None), (xs, mod[layer, 1:n_cond], t_lat, state0)):
            phg, s_fin = _hgrn_call(x, m, t, layer, norm_mix, w_in, lb_logits, hg_norm, w_branch_hg, s0)
            x1 = _mix_call(x, phg, m, t, layer, norm_mix, w_in, sg_norm, sg_w, sg_b, w_branch_sg, w_branch_pool,
                           pool_w, pool_scale, w_out)
            x2 = _ffn_call(x1, m, t, layer, final, norm_ffn, ffn_up, ffn_conv_w, ffn_conv_b, ffn_down, final_norm)
            groups.append((x2, s_fin))
        (xp, s_ctx), (xs, _) = groups
        new_states.append(s_ctx)

    y_prompt = xp.reshape(x_prompt.shape)
    y_sample = xs.reshape(x_sample.shape)
    new_state_hgrn = jnp.stack(new_states, axis=1)
    return (y_prompt, y_sample, new_state_hgrn)
```

```python
import functools

import jax
import jax.numpy as jnp
import numpy as np
from jax import lax
from jax.experimental import pallas as pl
from jax.experimental.pallas import tpu as pltpu

D_MODEL = 1024
DEPTH = 2
GRID_W = 64
POS_BASE = 10000.0
EPS = 1e-6
HG_HEADS = 4
HG_DK = 128
HG_DV = 128
HG_WIDTH = HG_HEADS * HG_DV
SG_GROUPS = 4
SG_WIDTH = 512
SG_GROUP_DIM = SG_WIDTH // SG_GROUPS
SG_CHUNK = 128
POOL_WINDOWS = (2, 4, 8, 16)
POOL_WIDTH = 512
POOL_GROUP_DIM = POOL_WIDTH // len(POOL_WINDOWS)
IN_COLS = 5 * HG_WIDTH + 2 * SG_WIDTH + POOL_WIDTH + 3 * D_MODEL
D_FF = 2816
N_MOD = 6

V7X_LANES = 128
V7X_SUBLANES = 8
V7X_MXU_DIM = 256
MIB = 2**20

ROWS = 1024
PART_ROWS = 512
HG_BLOCK = 256
HG_LEVELS = 8
IN_CHUNK = 512
FF_CHUNK = V7X_MXU_DIM
FF_SETS = 2
MOD_CHUNK = 1536

NEG_LOG2E = -1.4426950408889634

F32 = jnp.float32
BF16 = jnp.bfloat16


def _dot(a, b):
    return lax.dot_general(a, b, (((1,), (0,)), ((), ())), preferred_element_type=F32)


def _dot_nt(a, b):
    return lax.dot_general(a, b, (((1,), (1,)), ((), ())), preferred_element_type=F32)


def _dot_tn(a, b):
    return lax.dot_general(a, b, (((0,), (0,)), ((), ())), preferred_element_type=F32)


def _silu(x):
    return x * jax.nn.sigmoid(x)


def _rms(x, gain):
    return x * lax.rsqrt(jnp.mean(x * x, axis=-1, keepdims=True) + EPS) * gain


def _log1pexp(y):
    return jnp.maximum(y, 0.0) + jnp.log(1.0 + jnp.exp(-jnp.abs(y)))


def _norm_mod(x, gain, scale, shift):
    return (_rms(x, gain) * (1.0 + scale) + shift).astype(BF16)


def _part_rows(seq_len):
    return max(seq_len, PART_ROWS)


def _mod_kernel(c_ref, w_ref, b_ref, o_ref):
    c = _silu(c_ref[...]).astype(BF16)
    o_ref[...] = _dot(c, w_ref[...].astype(BF16)) + b_ref[...]


def _mod_call(cvec, w_ada, b_ada):
    n_rows = cvec.shape[0]
    n_cols = N_MOD * D_MODEL
    return pl.pallas_call(
        _mod_kernel,
        grid=(DEPTH, n_cols // MOD_CHUNK),
        in_specs=[
            pl.BlockSpec((n_rows, D_MODEL), lambda l, n: (0, 0)),
            pl.BlockSpec((None, D_MODEL, MOD_CHUNK), lambda l, n: (l, 0, n)),
            pl.BlockSpec((None, 1, MOD_CHUNK), lambda l, n: (l, 0, n)),
        ],
        out_specs=pl.BlockSpec((None, n_rows, MOD_CHUNK), lambda l, n: (l, 0, n)),
        out_shape=jax.ShapeDtypeStruct((DEPTH, n_rows, n_cols), F32),
        compiler_params=pltpu.CompilerParams(
            dimension_semantics=("arbitrary", "arbitrary"), vmem_limit_bytes=32 * MIB
        ),
        name="adaln_mod",
    )(cvec, w_ada, b_ada.reshape(DEPTH, 1, n_cols))


def _addpos_kernel(x_ref, p_ref, o_ref):
    o_ref[...] = x_ref[...] + p_ref[...]


def _addpos_call(x, pos):
    b, t, d = x.shape
    return pl.pallas_call(
        _addpos_kernel,
        grid=(b,),
        in_specs=[pl.BlockSpec((None, t, d), lambda i: (i, 0, 0)), pl.BlockSpec((t, d), lambda i: (0, 0))],
        out_specs=pl.BlockSpec((None, t, d), lambda i: (i, 0, 0)),
        out_shape=jax.ShapeDtypeStruct(x.shape, x.dtype),
        compiler_params=pltpu.CompilerParams(dimension_semantics=("arbitrary",), vmem_limit_bytes=32 * MIB),
        name="add_pos",
    )(x, pos)


def _grid_pos_embed(n_tokens):
    rows = n_tokens // GRID_W
    r = np.broadcast_to(np.arange(rows, dtype=np.float32)[:, None], (rows, GRID_W)).reshape(-1)
    col = np.broadcast_to(np.arange(GRID_W, dtype=np.float32)[None, :], (rows, GRID_W)).reshape(-1)
    quarter = D_MODEL // 4
    omega = (1.0 / (np.float32(POS_BASE) ** (np.arange(quarter, dtype=np.float32) / quarter))).astype(np.float32)
    ar = r[:, None] * omega[None, :]
    ac = col[:, None] * omega[None, :]
    return jnp.asarray(np.concatenate([np.sin(ar), np.cos(ar), np.sin(ac), np.cos(ac)], axis=-1), F32)


def _ref_rows(b, blk, r):
    n, c = b.shape
    if blk >= V7X_SUBLANES:
        x3 = b.reshape(n // blk, blk, c)
        return jnp.broadcast_to(x3[:, r : r + 1, :], x3.shape).reshape(n, c)
    x3 = b.reshape(n // V7X_SUBLANES, V7X_SUBLANES, c)
    sub = lax.broadcasted_iota(jnp.int32, x3.shape, 1)
    bases = list(range(0, V7X_SUBLANES, blk))
    out = jnp.broadcast_to(x3[:, bases[-1] + r : bases[-1] + r + 1, :], x3.shape)
    for base in reversed(bases[:-1]):
        out = jnp.where(sub < base + blk, jnp.broadcast_to(x3[:, base + r : base + r + 1, :], x3.shape), out)
    return out.reshape(n, c)


def _cum_logdecay(lf, tri):
    hi = lf.astype(BF16)
    r1 = lf - hi.astype(F32)
    mid = r1.astype(BF16)
    lo = (r1 - mid.astype(F32)).astype(BF16)
    return _dot(tri, hi) + _dot(tri, mid) + _dot(tri, lo)


def _hgrn_block(q, k, v, b, lvq, forward, st):
    half = HG_BLOCK // 2
    lo, hi = slice(0, half), slice(half, HG_BLOCK)
    vb = v.astype(BF16)
    qb, kb = q.astype(BF16), k.astype(BF16)
    diag = [jnp.where(lvq == 0, _dot_nt(qb[h], kb[h]), 0.0) for h in (lo, hi)]
    for m in range(1, HG_LEVELS):
        blk = 2**m
        ref = _ref_rows(b, blk, blk // 2 - 1 if forward else blk // 2)
        e = jnp.exp2(jnp.abs(b - ref) * NEG_LOG2E).astype(BF16)
        qt, kt = qb * e, kb * e
        diag = [jnp.where(lvq == m, _dot_nt(qt[h], kt[h]), a) for h, a in zip((lo, hi), diag)]
    mid = half - 1 if forward else half
    e = jnp.exp2(jnp.abs(b - b[mid : mid + 1, :]) * NEG_LOG2E).astype(BF16)
    qt, kt = qb * e, kb * e
    a_lo, a_hi = (a.astype(BF16) for a in diag)
    if forward:
        cross = _dot_nt(qt[hi], kt[lo]).astype(BF16)
        o = jnp.concatenate([_dot(a_lo, vb[lo]), _dot(cross, vb[lo]) + _dot(a_hi, vb[hi])], axis=0)
    else:
        cross = _dot_nt(qt[lo], kt[hi]).astype(BF16)
        o = jnp.concatenate([_dot(a_lo, vb[lo]) + _dot(cross, vb[hi]), _dot(a_hi, vb[hi])], axis=0)
    edge = b[HG_BLOCK - 1 : HG_BLOCK, :] if forward else b[0:1, :]
    k_end = (k * jnp.exp(edge - b)).astype(BF16)
    st_new = _dot_tn(vb, k_end)
    if st is not None:
        o = o + _dot_nt((q * jnp.exp(b)).astype(BF16), st.astype(BF16))
        st_new = st_new + st * jnp.exp(edge)
    return o, st_new


def _hgrn_kernel(*refs, layer, seq_len):
    carry = seq_len > HG_BLOCK
    it = iter(refs)
    x_ref, mod_ref, nmix_ref = next(it), next(it), next(it)
    w_refs = [next(it) for _ in range(5)]
    lbl_ref, hgn_ref, wbr_ref = next(it), next(it), next(it)
    s0_refs = [next(it), next(it)] if carry else None
    phg_ref = next(it)
    sout_ref = None if carry else next(it)
    hb, wcat, q_s, v_s, g_s, kf_s, kb_s, bf_s, bb_s, o_s, y_s, lvf_s, lvb_s, trif_s, trib_s, st_s = it

    j = pl.program_id(1)

    @pl.when(j == 0)
    def _():
        hb[...] = _norm_mod(x_ref[...], nmix_ref[...], mod_ref[0, 1:2, :], mod_ref[0, 0:1, :])
        t = lax.broadcasted_iota(jnp.int32, (HG_BLOCK // 2, HG_BLOCK // 2), 0)
        s = lax.broadcasted_iota(jnp.int32, (HG_BLOCK // 2, HG_BLOCK // 2), 1)
        x = t ^ s
        lv = jnp.zeros_like(x)
        for m in range(HG_LEVELS - 1):
            lv = lv + (x >= 2**m).astype(jnp.int32)
        lvf_s[...] = jnp.where(t >= s, lv, -1)
        lvb_s[...] = jnp.where(t <= s, lv, -1)
        t = lax.broadcasted_iota(jnp.int32, (HG_BLOCK, HG_BLOCK), 0)
        s = lax.broadcasted_iota(jnp.int32, (HG_BLOCK, HG_BLOCK), 1)
        trif_s[...] = (t >= s).astype(BF16)
        trib_s[...] = (t <= s).astype(BF16)

    for g, w_ref in enumerate(w_refs):
        wcat[:, g * HG_DK : (g + 1) * HG_DK] = w_ref[...].astype(BF16)
    a0, a1 = lbl_ref[0], lbl_ref[1]
    amax = jnp.maximum(a0, a1)
    e0, e1 = jnp.exp(a0 - amax), jnp.exp(a1 - amax)
    p0, p1 = e0 / (e0 + e1), e1 / (e0 + e1)
    lb = (p0 - p0) if layer == 0 else ((p0 + p1) - p0)
    log_lb = jnp.log(lb)

    wc = wcat[...]
    part = PART_ROWS
    n_parts = ROWS // part

    def zdot(p):
        return _dot(hb[p * part : (p + 1) * part, :], wc)

    pending = zdot(0)
    for p in range(n_parts):
        z = pending
        if p + 1 < n_parts:
            pending = zdot(p + 1)
        rows = slice(p * part, (p + 1) * part)
        zq, zff, zfb, zi, zg = (z[:, g * HG_DK : (g + 1) * HG_DK] for g in range(5))
        q_s[rows, :] = _silu(zq) * HG_DK**-0.5
        v_s[rows, :] = zi
        g_s[rows, :] = _silu(zg)
        for d, (zf, k_s, b_s, tri_s) in enumerate(((zff, kf_s, bf_s, trif_s), (zfb, kb_s, bb_s, trib_s))):
            lf = _log1pexp(log_lb[d : d + 1, :] - zf) - _log1pexp(-zf)
            k_s[rows, :] = (1.0 - lb[d : d + 1, :]) * jax.nn.sigmoid(-zf)
            tri = tri_s[...]
            for n in range(part // HG_BLOCK):
                loc = slice(n * HG_BLOCK, (n + 1) * HG_BLOCK)
                dst = slice(rows.start + n * HG_BLOCK, rows.start + (n + 1) * HG_BLOCK)
                b_s[dst, :] = _cum_logdecay(lf[loc, :], tri)

    o_s[...] = jnp.zeros_like(o_s)
    n_blk = ROWS // HG_BLOCK
    if carry:
        for d in range(2):
            st_s[d] = s0_refs[d][0].T

    def blocks(n, c):
        for d in range(2):
            forward = d == 0
            blk = n if forward else n_blk - 1 - n
            rows = pl.ds(pl.multiple_of(blk * HG_BLOCK, HG_BLOCK), HG_BLOCK)
            k_s, b_s, lv_s = (kf_s, bf_s, lvf_s) if forward else (kb_s, bb_s, lvb_s)
            o, st_new = _hgrn_block(
                q_s[rows, :], k_s[rows, :], v_s[rows, :], b_s[rows, :], lv_s[...], forward, st_s[d] if carry else None
            )
            o_s[rows, :] += o
            if carry:
                st_s[d] = st_new
            else:
                sout_ref[blk, d, 0] = st_new.T
        return c

    lax.fori_loop(0, n_blk, blocks, 0, unroll=True)

    y_s[j] = (_rms(o_s[...], hgn_ref[...]) * g_s[...]).astype(BF16)

    @pl.when(j == HG_HEADS - 1)
    def _():
        y = jnp.concatenate([y_s[h] for h in range(HG_HEADS)], axis=1)
        phg_ref[...] = _dot(y, wbr_ref[...].astype(BF16))


def _hgrn_call(x, mod, seq_len, layer, norm_mix, w_in, lb_logits, hg_norm, w_branch_hg, state0):
    n_tok = x.shape[0]
    nb = n_tok // ROWS
    carry = seq_len > HG_BLOCK
    per_seq_mod = mod.shape[0] > 1

    in_specs = [
        pl.BlockSpec((ROWS, D_MODEL), lambda i, j: (i, 0)),
        pl.BlockSpec((1, N_MOD, D_MODEL), (lambda i, j: (i, 0, 0)) if per_seq_mod else (lambda i, j: (0, 0, 0))),
        pl.BlockSpec((None, 1, D_MODEL), lambda i, j: (layer, 0, 0)),
    ]
    args = [x, mod, norm_mix.reshape(DEPTH, 1, D_MODEL)]
    for g in range(5):
        in_specs.append(pl.BlockSpec((None, D_MODEL, HG_DK), lambda i, j, g=g: (layer, 0, g * HG_HEADS + j)))
        args.append(w_in)
    in_specs += [
        pl.BlockSpec((DEPTH, 2, HG_DK), lambda i, j: (0, 0, j)),
        pl.BlockSpec((None, 1, HG_DV), lambda i, j: (layer, 0, 0)),
        pl.BlockSpec((None, HG_WIDTH, D_MODEL), lambda i, j: (layer, 0, 0), pipeline_mode=pl.Buffered(1)),
    ]
    args += [lb_logits, hg_norm.reshape(DEPTH, 1, HG_DV), w_branch_hg]
    if carry:
        assert seq_len == ROWS
        for d in range(2):
            in_specs.append(
                pl.BlockSpec((1, HG_DK, HG_DV), lambda i, j, d=d: (((i * DEPTH + layer) * 2 + d) * HG_HEADS + j, 0, 0))
            )
            args.append(state0)

    out_shape = [jax.ShapeDtypeStruct((n_tok, D_MODEL), F32)]
    out_specs = [pl.BlockSpec((ROWS, D_MODEL), lambda i, j: (i, 0))]
    if not carry:
        assert seq_len == HG_BLOCK
        n_seq = n_tok // seq_len
        out_shape.append(jax.ShapeDtypeStruct((n_seq, 2, HG_HEADS, HG_DK, HG_DV), F32))
        out_specs.append(pl.BlockSpec((ROWS // seq_len, 2, 1, HG_DK, HG_DV), lambda i, j: (i, 0, j, 0, 0)))

    head = lambda dt=F32: pltpu.VMEM((ROWS, HG_DK), dt)
    scratch = [
        pltpu.VMEM((ROWS, D_MODEL), BF16),
        pltpu.VMEM((D_MODEL, 5 * HG_DK), BF16),
        head(), head(), head(),
        head(), head(), head(), head(),
        head(),
        pltpu.VMEM((HG_HEADS, ROWS, HG_DV), BF16),
        pltpu.VMEM((HG_BLOCK // 2, HG_BLOCK // 2), jnp.int32),
        pltpu.VMEM((HG_BLOCK // 2, HG_BLOCK // 2), jnp.int32),
        pltpu.VMEM((HG_BLOCK, HG_BLOCK), BF16),
        pltpu.VMEM((HG_BLOCK, HG_BLOCK), BF16),
        pltpu.VMEM((2, HG_DV, HG_DK), F32),
    ]
    outs = pl.pallas_call(
        functools.partial(_hgrn_kernel, layer=layer, seq_len=seq_len),
        grid=(nb, HG_HEADS),
        in_specs=in_specs,
        out_specs=out_specs,
        out_shape=out_shape,
        scratch_shapes=scratch,
        compiler_params=pltpu.CompilerParams(
            dimension_semantics=("arbitrary", "arbitrary"), vmem_limit_bytes=48 * MIB
        ),
        name=f"hgrn_l{layer}_t{seq_len}",
    )(*args)
    return (outs[0], None) if carry else (outs[0], outs[1])


_MIX_ORDER = (10, 11, 5, 6, 7, 12, 13, 8, 9)


def _mix_col(k):
    idx = 0
    for n, c in enumerate(_MIX_ORDER):
        idx = idx + jnp.where(k == n, c, 0)
    return idx


def _window_mean_minus_self(p, tpos, seq_len, w):
    n = p.shape[0]
    acc = jnp.zeros_like(p)
    for jj in range(-(w // 2), w // 2):
        shifted = p if jj == 0 else pltpu.roll(p, (-jj) % n, 0)
        valid = (tpos + jj >= 0) & (tpos + jj < seq_len)
        acc = acc + jnp.where(valid, shifted, 0.0)
    cnt = jnp.minimum(tpos + w // 2, seq_len) - jnp.maximum(tpos - w // 2, 0)
    return acc / cnt.astype(F32) - p


def _mix_kernel(
    x_ref, mod_ref, nmix_ref, w_ref, sgn_ref, sgw_ref, sgb_ref, wbsg_ref, wbpool_ref, poolw_ref, pscale_ref,
    wout_ref, phg_ref, o_ref, hb, u_s, br_s, mrg_s, *, seq_len,
):
    k = pl.program_id(1)
    half = D_MODEL // 2

    @pl.when(k == 0)
    def _():
        hb[...] = _norm_mod(x_ref[...], nmix_ref[...], mod_ref[0, 1:2, :], mod_ref[0, 0:1, :])

    def for_z_parts(consume, part=PART_ROWS):
        w = w_ref[...].astype(BF16)
        n_parts = ROWS // part

        def zdot(p):
            return _dot(hb[p * part : (p + 1) * part, :], w)

        pending = zdot(0)
        for p in range(n_parts):
            z = pending
            if p + 1 < n_parts:
                pending = zdot(p + 1)
            consume(slice(p * part, (p + 1) * part), z)

    for step in (0, 1):

        @pl.when(k == step)
        def _(step=step):
            cols = slice(step * half, (step + 1) * half)

            def gate(rows, z):
                mrg_s[rows, cols] = jax.nn.sigmoid(z)

            for_z_parts(gate)

    @pl.when(k == 2)
    def _():
        def store_u(rows, z):
            u_s[rows, :] = jax.nn.gelu(z)

        for_z_parts(store_u)

    @pl.when(k == 3)
    def _():
        wbsg = wbsg_ref[...].astype(BF16)
        wgs = [sgw_ref[g].astype(BF16) for g in range(SG_GROUPS)]

        def spatial_gating(rows, z):
            v = _rms(jax.nn.gelu(z), sgn_ref[...]).astype(BF16)
            for g in range(SG_GROUPS):
                bias = sgb_ref[:, g : g + 1]
                cols = slice(g * SG_GROUP_DIM, (g + 1) * SG_GROUP_DIM)
                for n in range((rows.stop - rows.start) // SG_CHUNK):
                    loc = slice(n * SG_CHUNK, (n + 1) * SG_CHUNK)
                    dst = slice(rows.start + n * SG_CHUNK, rows.start + (n + 1) * SG_CHUNK)
                    mixed = _dot(wgs[g], v[loc, cols]) + bias
                    br_s[dst, cols] = (u_s[dst, cols] * mixed).astype(BF16)
            mrg_s[rows, :] = mrg_s[rows, :] * _dot(br_s[rows, :], wbsg)

        for_z_parts(spatial_gating)

    @pl.when(k == 4)
    def _():
        part = _part_rows(seq_len)
        tpos = lax.broadcasted_iota(jnp.int32, (part, POOL_GROUP_DIM), 0) & (seq_len - 1)

        def pool(rows, z):
            for gi, w in enumerate(POOL_WINDOWS):
                cols = slice(gi * POOL_GROUP_DIM, (gi + 1) * POOL_GROUP_DIM)
                pooled = _window_mean_minus_self(z[:, cols], tpos, seq_len, w)
                out = _dot(pooled.astype(BF16), poolw_ref[gi].astype(BF16)) * pscale_ref[:, cols]
                br_s[rows, cols] = out.astype(BF16)

        for_z_parts(pool, part)

    for step in (5, 6):

        @pl.when(k == step)
        def _(step=step):
            cols = slice((step - 5) * half, (step - 4) * half)
            wbpool = wbpool_ref[:, cols].astype(BF16)

            def gate(rows, z):
                mrg_s[rows, cols] = mrg_s[rows, cols] + jax.nn.sigmoid(z) * _dot(br_s[rows, :], wbpool)

            for_z_parts(gate)

    @pl.when(k == 7)
    def _():
        cols = slice(0, half)

        def gate(rows, z):
            mrg_s[rows, cols] = mrg_s[rows, cols] + jax.nn.sigmoid(z) * phg_ref[rows, cols]

        for_z_parts(gate)

    @pl.when(k == 8)
    def _():
        cols = slice(half, D_MODEL)
        wout = wout_ref[...].astype(BF16)

        def gate_and_project(rows, z):
            mrg_s[rows, cols] = mrg_s[rows, cols] + jax.nn.sigmoid(z) * phg_ref[rows, cols]
            y = _dot(mrg_s[rows, :].astype(BF16), wout)
            o_ref[rows, :] = x_ref[rows, :] + mod_ref[0, 2:3, :] * y

        for_z_parts(gate_and_project)


def _mix_call(x, phg, mod, seq_len, layer, norm_mix, w_in, sg_norm, sg_w, sg_b, w_branch_sg, w_branch_pool, pool_w,
              pool_scale, w_out):
    n_tok = x.shape[0]
    nb = n_tok // ROWS
    per_seq_mod = mod.shape[0] > 1
    const = pl.Buffered(1)
    assert seq_len & (seq_len - 1) == 0 and ROWS % seq_len == 0 and seq_len % SG_CHUNK == 0
    in_specs = [
        pl.BlockSpec((ROWS, D_MODEL), lambda i, k: (i, 0)),
        pl.BlockSpec((1, N_MOD, D_MODEL), (lambda i, k: (i, 0, 0)) if per_seq_mod else (lambda i, k: (0, 0, 0))),
        pl.BlockSpec((None, 1, D_MODEL), lambda i, k: (layer, 0, 0)),
        pl.BlockSpec((None, D_MODEL, IN_CHUNK), lambda i, k: (layer, 0, _mix_col(k))),
        pl.BlockSpec((None, 1, SG_WIDTH), lambda i, k: (layer, 0, 0)),
        pl.BlockSpec((None, SG_GROUPS, SG_CHUNK, SG_CHUNK), lambda i, k: (layer, 0, 0, 0)),
        pl.BlockSpec((None, SG_CHUNK, SG_GROUPS), lambda i, k: (layer, 0, 0)),
        pl.BlockSpec((None, SG_WIDTH, D_MODEL), lambda i, k: (layer, 0, 0), pipeline_mode=const),
        pl.BlockSpec((None, POOL_WIDTH, D_MODEL), lambda i, k: (layer, 0, 0), pipeline_mode=const),
        pl.BlockSpec((None, len(POOL_WINDOWS), POOL_GROUP_DIM, POOL_GROUP_DIM), lambda i, k: (layer, 0, 0, 0)),
        pl.BlockSpec((None, 1, POOL_WIDTH), lambda i, k: (layer, 0, 0)),
        pl.BlockSpec((None, D_MODEL, D_MODEL), lambda i, k: (layer, 0, 0), pipeline_mode=const),
        pl.BlockSpec((ROWS, D_MODEL), lambda i, k: (i, 0)),
    ]
    args = [
        x, mod, norm_mix.reshape(DEPTH, 1, D_MODEL), w_in, sg_norm.reshape(DEPTH, 1, SG_WIDTH), sg_w,
        jnp.swapaxes(sg_b, 1, 2), w_branch_sg, w_branch_pool, pool_w, pool_scale.reshape(DEPTH, 1, POOL_WIDTH),
        w_out, phg,
    ]
    scratch = [
        pltpu.VMEM((ROWS, D_MODEL), BF16),
        pltpu.VMEM((ROWS, SG_WIDTH), F32),
        pltpu.VMEM((ROWS, SG_WIDTH), BF16),
        pltpu.VMEM((ROWS, D_MODEL), F32),
    ]
    return pl.pallas_call(
        functools.partial(_mix_kernel, seq_len=seq_len),
        grid=(nb, len(_MIX_ORDER)),
        in_specs=in_specs,
        out_specs=pl.BlockSpec((ROWS, D_MODEL), lambda i, k: (i, 0)),
        out_shape=jax.ShapeDtypeStruct((n_tok, D_MODEL), F32),
        scratch_shapes=scratch,
        compiler_params=pltpu.CompilerParams(
            dimension_semantics=("arbitrary", "arbitrary"), vmem_limit_bytes=56 * MIB
        ),
        name=f"mix_l{layer}_t{seq_len}",
    )(*args)


def _ffn_kernel(x_ref, mod_ref, nffn_ref, *refs, seq_len, final):
    sets = [refs[7 * s : 7 * s + 7] for s in range(FF_SETS)]
    fin_ref, o_ref, hb, acc = refs[7 * FF_SETS :]
    c = pl.program_id(1)
    n_steps = pl.num_programs(1)
    n_chunks = D_FF // FF_CHUNK

    @pl.when(c == 0)
    def _():
        hb[...] = _norm_mod(x_ref[...], nffn_ref[...], mod_ref[0, 4:5, :], mod_ref[0, 3:4, :])
        acc[...] = jnp.zeros_like(acc)

    part = _part_rows(seq_len)
    tpos = lax.broadcasted_iota(jnp.int32, (part, FF_CHUNK), 0) & (seq_len - 1)
    has_prev = tpos >= 1
    has_next = tpos < seq_len - 1

    def conv(h, cw_ref, cb_ref):
        prev = jnp.where(has_prev, pltpu.roll(h, 1, 0), 0.0)
        nxt = jnp.where(has_next, pltpu.roll(h, part - 1, 0), 0.0)
        return prev * cw_ref[0:1, :] + h * cw_ref[1:2, :] + nxt * cw_ref[2:3, :] + cb_ref[...]

    def run(n_sets):
        ws = [(wa[...].astype(BF16), wb[...].astype(BF16), wd[...].astype(BF16)) for wa, wb, _, _, _, _, wd in sets]
        items = [(p, s) for p in range(ROWS // part) for s in range(n_sets)]

        def up(item):
            p, s = item
            h = hb[p * part : (p + 1) * part, :]
            return _dot(h, ws[s][0]), _dot(h, ws[s][1])

        pending = up(items[0])
        down = None
        for i, (p, s) in enumerate(items):
            ha, hb2 = pending
            if i + 1 < len(items):
                pending = up(items[i + 1])
            _, _, cwa_ref, cwb_ref, cba_ref, cbb_ref, _ = sets[s]
            a = conv(ha, cwa_ref, cba_ref)
            b = conv(hb2, cwb_ref, cbb_ref)
            d = _dot((_silu(a) * b).astype(BF16), ws[s][2])
            down = d if down is None else down + d
            if s == n_sets - 1:
                acc[p * part : (p + 1) * part, :] += down
                down = None

    for n_sets in range(1, FF_SETS + 1):
        lo = n_chunks - n_sets * n_steps
        @pl.when((c >= lo) & (c < lo + n_steps))
        def _(n_sets=n_sets):
            run(n_sets)

    @pl.when(c == n_steps - 1)
    def _():
        y = x_ref[...] + mod_ref[0, 5:6, :] * acc[...]
        if final:
            y = _rms(y, fin_ref[...])
        o_ref[...] = y


def _ffn_call(x, mod, seq_len, layer, final, norm_ffn, ffn_up, ffn_conv_w, ffn_conv_b, ffn_down, final_norm):
    n_tok = x.shape[0]
    nb = n_tok // ROWS
    nc = D_FF // FF_CHUNK
    n_steps = pl.cdiv(nc, FF_SETS)
    per_seq_mod = mod.shape[0] > 1
    conv_b = ffn_conv_b.reshape(DEPTH, 1, 2 * D_FF)
    in_specs = [
        pl.BlockSpec((ROWS, D_MODEL), lambda i, c: (i, 0)),
        pl.BlockSpec((1, N_MOD, D_MODEL), (lambda i, c: (i, 0, 0)) if per_seq_mod else (lambda i, c: (0, 0, 0))),
        pl.BlockSpec((None, 1, D_MODEL), lambda i, c: (layer, 0, 0)),
    ]
    args = [x, mod, norm_ffn.reshape(DEPTH, 1, D_MODEL)]
    for s in range(FF_SETS):
        chunk = lambda c, s=s: jnp.minimum(c + s * n_steps, nc - 1)
        in_specs += [
            pl.BlockSpec((None, D_MODEL, FF_CHUNK), lambda i, c, f=chunk: (layer, 0, f(c))),
            pl.BlockSpec((None, D_MODEL, FF_CHUNK), lambda i, c, f=chunk: (layer, 0, nc + f(c))),
            pl.BlockSpec((None, 3, FF_CHUNK), lambda i, c, f=chunk: (layer, 0, f(c))),
            pl.BlockSpec((None, 3, FF_CHUNK), lambda i, c, f=chunk: (layer, 0, nc + f(c))),
            pl.BlockSpec((None, 1, FF_CHUNK), lambda i, c, f=chunk: (layer, 0, f(c))),
            pl.BlockSpec((None, 1, FF_CHUNK), lambda i, c, f=chunk: (layer, 0, nc + f(c))),
            pl.BlockSpec((None, FF_CHUNK, D_MODEL), lambda i, c, f=chunk: (layer, f(c), 0)),
        ]
        args += [ffn_up, ffn_up, ffn_conv_w, ffn_conv_w, conv_b, conv_b, ffn_down]
    in_specs.append(pl.BlockSpec((1, D_MODEL), lambda i, c: (0, 0)))
    args.append(final_norm.reshape(1, D_MODEL))
    return pl.pallas_call(
        functools.partial(_ffn_kernel, seq_len=seq_len, final=final),
        grid=(nb, n_steps),
        in_specs=in_specs,
        out_specs=pl.BlockSpec((ROWS, D_MODEL), lambda i, c: (i, 0)),
        out_shape=jax.ShapeDtypeStruct((n_tok, D_MODEL), F32),
        scratch_shapes=[pltpu.VMEM((ROWS, D_MODEL), BF16), pltpu.VMEM((ROWS, D_MODEL), F32)],
        compiler_params=pltpu.CompilerParams(
            dimension_semantics=("arbitrary", "arbitrary"), vmem_limit_bytes=48 * MIB
        ),
        name=f"ffn_l{layer}_t{seq_len}",
    )(*args)


def kernel(x_prompt, x_sample, c, state_hgrn, c_ctx, norm_mix, norm_ffn, w_ada, b_ada, w_in, lb_logits, hg_norm,
           w_branch_hg, w_branch_sg, w_branch_pool, w_out, sg_norm, sg_w, sg_b, pool_w, pool_scale, ffn_up,
           ffn_conv_w, ffn_conv_b, ffn_down, final_norm):
    n_ctx, t_ctx, _ = x_prompt.shape
    n_lat, t_lat, _ = x_sample.shape

    n_cond = 1 + n_lat
    pad = -n_cond % V7X_SUBLANES
    cvec = jnp.concatenate([c_ctx[None, :], c, jnp.zeros((pad, D_MODEL), F32)], axis=0)
    mod = _mod_call(cvec, w_ada, b_ada).reshape(DEPTH, n_cond + pad, N_MOD, D_MODEL)

    xs = _addpos_call(x_sample, _grid_pos_embed(t_lat)).reshape(n_lat * t_lat, D_MODEL)
    xp = x_prompt.reshape(n_ctx * t_ctx, D_MODEL)
    state0 = state_hgrn.reshape(n_lat * DEPTH * 2 * HG_HEADS, HG_DK, HG_DV)

    new_states = []
    for layer in range(DEPTH):
        final = layer == DEPTH - 1
        groups = []
        for x, m, t, s0 in ((xp, mod[layer, 0:1], t_ctx, None), (xs, mod[layer, 1:n_cond], t_lat, state0)):
            phg, s_fin = _hgrn_call(x, m, t, layer, norm_mix, w_in, lb_logits, hg_norm, w_branch_hg, s0)
            x1 = _mix_call(x, phg, m, t, layer, norm_mix, w_in, sg_norm, sg_w, sg_b, w_branch_sg, w_branch_pool,
                           pool_w, pool_scale, w_out)
            x2 = _ffn_call(x1, m, t, layer, final, norm_ffn, ffn_up, ffn_conv_w, ffn_conv_b, ffn_down, final_norm)
            groups.append((x2, s_fin))
        (xp, s_ctx), (xs, _) = groups
        new_states.append(s_ctx)

    y_prompt = xp.reshape(x_prompt.shape)
    y_sample = xs.reshape(x_sample.shape)
    new_state_hgrn = jnp.stack(new_states, axis=1)
    return (y_prompt, y_sample, new_state_hgrn)
```

```python
import functools

import jax
import jax.numpy as jnp
import numpy as np
from jax import lax
from jax.experimental import pallas as pl
from jax.experimental.pallas import tpu as pltpu

D_MODEL = 1024
DEPTH = 2
GRID_W = 64
POS_BASE = 10000.0
EPS = 1e-6
HG_HEADS = 4
HG_DK = 128
HG_DV = 128
HG_WIDTH = HG_HEADS * HG_DV
SG_GROUPS = 4
SG_WIDTH = 512
SG_GROUP_DIM = SG_WIDTH // SG_GROUPS
SG_CHUNK = 128
POOL_WINDOWS = (2, 4, 8, 16)
POOL_WIDTH = 512
POOL_GROUP_DIM = POOL_WIDTH // len(POOL_WINDOWS)
IN_COLS = 5 * HG_WIDTH + 2 * SG_WIDTH + POOL_WIDTH + 3 * D_MODEL
D_FF = 2816
N_MOD = 6

V7X_LANES = 128
V7X_SUBLANES = 8
V7X_MXU_DIM = 256
MIB = 2**20

ROWS = 1024
PART_ROWS = 512
HG_BLOCK = 256
HG_LEVELS = 8
IN_CHUNK = 512
FF_CHUNK = V7X_MXU_DIM
FF_SETS = 3
MOD_CHUNK = 1536

NEG_LOG2E = -1.4426950408889634

F32 = jnp.float32
BF16 = jnp.bfloat16


def _dot(a, b):
    return lax.dot_general(a, b, (((1,), (0,)), ((), ())), preferred_element_type=F32)


def _dot_nt(a, b):
    return lax.dot_general(a, b, (((1,), (1,)), ((), ())), preferred_element_type=F32)


def _dot_tn(a, b):
    return lax.dot_general(a, b, (((0,), (0,)), ((), ())), preferred_element_type=F32)


def _silu(x):
    return x * jax.nn.sigmoid(x)


def _rms(x, gain):
    return x * lax.rsqrt(jnp.mean(x * x, axis=-1, keepdims=True) + EPS) * gain


def _log1pexp(y):
    return jnp.maximum(y, 0.0) + jnp.log(1.0 + jnp.exp(-jnp.abs(y)))


def _norm_mod(x, gain, scale, shift):
    return (_rms(x, gain) * (1.0 + scale) + shift).astype(BF16)


def _part_rows(seq_len):
    return max(seq_len, PART_ROWS)


def _mod_kernel(c_ref, w_ref, b_ref, o_ref):
    c = _silu(c_ref[...]).astype(BF16)
    o_ref[...] = _dot(c, w_ref[...].astype(BF16)) + b_ref[...]


def _mod_call(cvec, w_ada, b_ada):
    n_rows = cvec.shape[0]
    n_cols = N_MOD * D_MODEL
    return pl.pallas_call(
        _mod_kernel,
        grid=(DEPTH, n_cols // MOD_CHUNK),
        in_specs=[
            pl.BlockSpec((n_rows, D_MODEL), lambda l, n: (0, 0)),
            pl.BlockSpec((None, D_MODEL, MOD_CHUNK), lambda l, n: (l, 0, n)),
            pl.BlockSpec((None, 1, MOD_CHUNK), lambda l, n: (l, 0, n)),
        ],
        out_specs=pl.BlockSpec((None, n_rows, MOD_CHUNK), lambda l, n: (l, 0, n)),
        out_shape=jax.ShapeDtypeStruct((DEPTH, n_rows, n_cols), F32),
        compiler_params=pltpu.CompilerParams(
            dimension_semantics=("arbitrary", "arbitrary"), vmem_limit_bytes=32 * MIB
        ),
        name="adaln_mod",
    )(cvec, w_ada, b_ada.reshape(DEPTH, 1, n_cols))


def _addpos_kernel(x_ref, p_ref, o_ref):
    o_ref[...] = x_ref[...] + p_ref[...]


def _addpos_call(x, pos):
    b, t, d = x.shape
    return pl.pallas_call(
        _addpos_kernel,
        grid=(b,),
        in_specs=[pl.BlockSpec((None, t, d), lambda i: (i, 0, 0)), pl.BlockSpec((t, d), lambda i: (0, 0))],
        out_specs=pl.BlockSpec((None, t, d), lambda i: (i, 0, 0)),
        out_shape=jax.ShapeDtypeStruct(x.shape, x.dtype),
        compiler_params=pltpu.CompilerParams(dimension_semantics=("arbitrary",), vmem_limit_bytes=32 * MIB),
        name="add_pos",
    )(x, pos)


def _grid_pos_embed(n_tokens):
    rows = n_tokens // GRID_W
    r = np.broadcast_to(np.arange(rows, dtype=np.float32)[:, None], (rows, GRID_W)).reshape(-1)
    col = np.broadcast_to(np.arange(GRID_W, dtype=np.float32)[None, :], (rows, GRID_W)).reshape(-1)
    quarter = D_MODEL // 4
    omega = (1.0 / (np.float32(POS_BASE) ** (np.arange(quarter, dtype=np.float32) / quarter))).astype(np.float32)
    ar = r[:, None] * omega[None, :]
    ac = col[:, None] * omega[None, :]
    return jnp.asarray(np.concatenate([np.sin(ar), np.cos(ar), np.sin(ac), np.cos(ac)], axis=-1), F32)


def _ref_rows(b, blk, r):
    n, c = b.shape
    if blk >= V7X_SUBLANES:
        x3 = b.reshape(n // blk, blk, c)
        return jnp.broadcast_to(x3[:, r : r + 1, :], x3.shape).reshape(n, c)
    x3 = b.reshape(n // V7X_SUBLANES, V7X_SUBLANES, c)
    sub = lax.broadcasted_iota(jnp.int32, x3.shape, 1)
    bases = list(range(0, V7X_SUBLANES, blk))
    out = jnp.broadcast_to(x3[:, bases[-1] + r : bases[-1] + r + 1, :], x3.shape)
    for base in reversed(bases[:-1]):
        out = jnp.where(sub < base + blk, jnp.broadcast_to(x3[:, base + r : base + r + 1, :], x3.shape), out)
    return out.reshape(n, c)


def _cum_logdecay(lf, tri):
    hi = lf.astype(BF16)
    r1 = lf - hi.astype(F32)
    mid = r1.astype(BF16)
    lo = (r1 - mid.astype(F32)).astype(BF16)
    return _dot(tri, hi) + _dot(tri, mid) + _dot(tri, lo)


def _hgrn_block(q, k, v, b, lvq, forward, st):
    half = HG_BLOCK // 2
    lo, hi = slice(0, half), slice(half, HG_BLOCK)
    vb = v.astype(BF16)
    qb, kb = q.astype(BF16), k.astype(BF16)
    diag = [jnp.where(lvq == 0, _dot_nt(qb[h], kb[h]), 0.0) for h in (lo, hi)]
    for m in range(1, HG_LEVELS):
        blk = 2**m
        ref = _ref_rows(b, blk, blk // 2 - 1 if forward else blk // 2)
        e = jnp.exp2(jnp.abs(b - ref) * NEG_LOG2E).astype(BF16)
        qt, kt = qb * e, kb * e
        diag = [jnp.where(lvq == m, _dot_nt(qt[h], kt[h]), a) for h, a in zip((lo, hi), diag)]
    mid = half - 1 if forward else half
    e = jnp.exp2(jnp.abs(b - b[mid : mid + 1, :]) * NEG_LOG2E).astype(BF16)
    qt, kt = qb * e, kb * e
    a_lo, a_hi = (a.astype(BF16) for a in diag)
    if forward:
        cross = _dot_nt(qt[hi], kt[lo]).astype(BF16)
        o = jnp.concatenate([_dot(a_lo, vb[lo]), _dot(cross, vb[lo]) + _dot(a_hi, vb[hi])], axis=0)
    else:
        cross = _dot_nt(qt[lo], kt[hi]).astype(BF16)
        o = jnp.concatenate([_dot(a_lo, vb[lo]) + _dot(cross, vb[hi]), _dot(a_hi, vb[hi])], axis=0)
    edge = b[HG_BLOCK - 1 : HG_BLOCK, :] if forward else b[0:1, :]
    k_end = (k * jnp.exp(edge - b)).astype(BF16)
    st_new = _dot_tn(vb, k_end)
    if st is not None:
        o = o + _dot_nt((q * jnp.exp(b)).astype(BF16), st.astype(BF16))
        st_new = st_new + st * jnp.exp(edge)
    return o, st_new


def _hgrn_kernel(*refs, layer, seq_len):
    carry = seq_len > HG_BLOCK
    it = iter(refs)
    x_ref, mod_ref, nmix_ref = next(it), next(it), next(it)
    w_refs = [next(it) for _ in range(5)]
    lbl_ref, hgn_ref, wbr_ref = next(it), next(it), next(it)
    s0_refs = [next(it), next(it)] if carry else None
    sprev_ref = next(it) if (not carry and layer > 0) else None
    phg_ref = next(it)
    sout_ref = None if carry else next(it)
    hb, wcat, q_s, v_s, g_s, kf_s, kb_s, bf_s, bb_s, o_s, y_s, lvf_s, lvb_s, trif_s, trib_s, st_s = it

    j = pl.program_id(1)

    @pl.when(j == 0)
    def _():
        hb[...] = _norm_mod(x_ref[...], nmix_ref[...], mod_ref[0, 1:2, :], mod_ref[0, 0:1, :])
        t = lax.broadcasted_iota(jnp.int32, (HG_BLOCK // 2, HG_BLOCK // 2), 0)
        s = lax.broadcasted_iota(jnp.int32, (HG_BLOCK // 2, HG_BLOCK // 2), 1)
        x = t ^ s
        lv = jnp.zeros_like(x)
        for m in range(HG_LEVELS - 1):
            lv = lv + (x >= 2**m).astype(jnp.int32)
        lvf_s[...] = jnp.where(t >= s, lv, -1)
        lvb_s[...] = jnp.where(t <= s, lv, -1)
        t = lax.broadcasted_iota(jnp.int32, (HG_BLOCK, HG_BLOCK), 0)
        s = lax.broadcasted_iota(jnp.int32, (HG_BLOCK, HG_BLOCK), 1)
        trif_s[...] = (t >= s).astype(BF16)
        trib_s[...] = (t <= s).astype(BF16)

    for g, w_ref in enumerate(w_refs):
        wcat[:, g * HG_DK : (g + 1) * HG_DK] = w_ref[...].astype(BF16)
    a0, a1 = lbl_ref[0], lbl_ref[1]
    amax = jnp.maximum(a0, a1)
    e0, e1 = jnp.exp(a0 - amax), jnp.exp(a1 - amax)
    p0, p1 = e0 / (e0 + e1), e1 / (e0 + e1)
    lb = (p0 - p0) if layer == 0 else ((p0 + p1) - p0)
    log_lb = jnp.log(lb)

    wc = wcat[...]
    part = PART_ROWS
    n_parts = ROWS // part

    def zdot(p):
        return _dot(hb[p * part : (p + 1) * part, :], wc)

    pending = zdot(0)
    for p in range(n_parts):
        z = pending
        if p + 1 < n_parts:
            pending = zdot(p + 1)
        rows = slice(p * part, (p + 1) * part)
        zq, zff, zfb, zi, zg = (z[:, g * HG_DK : (g + 1) * HG_DK] for g in range(5))
        q_s[rows, :] = _silu(zq) * HG_DK**-0.5
        v_s[rows, :] = zi
        g_s[rows, :] = _silu(zg)
        for d, (zf, k_s, b_s, tri_s) in enumerate(((zff, kf_s, bf_s, trif_s), (zfb, kb_s, bb_s, trib_s))):
            lf = _log1pexp(log_lb[d : d + 1, :] - zf) - _log1pexp(-zf)
            k_s[rows, :] = (1.0 - lb[d : d + 1, :]) * jax.nn.sigmoid(-zf)
            tri = tri_s[...]
            for n in range(part // HG_BLOCK):
                loc = slice(n * HG_BLOCK, (n + 1) * HG_BLOCK)
                dst = slice(rows.start + n * HG_BLOCK, rows.start + (n + 1) * HG_BLOCK)
                b_s[dst, :] = _cum_logdecay(lf[loc, :], tri)

    o_s[...] = jnp.zeros_like(o_s)
    n_blk = ROWS // HG_BLOCK
    if sprev_ref is not None:
        sout_ref[:, 0:layer] = sprev_ref[...]
    if carry:
        for d in range(2):
            st_s[d] = s0_refs[d][0].T

    def blocks(n, c):
        for d in range(2):
            forward = d == 0
            blk = n if forward else n_blk - 1 - n
            rows = pl.ds(pl.multiple_of(blk * HG_BLOCK, HG_BLOCK), HG_BLOCK)
            k_s, b_s, lv_s = (kf_s, bf_s, lvf_s) if forward else (kb_s, bb_s, lvb_s)
            o, st_new = _hgrn_block(
                q_s[rows, :], k_s[rows, :], v_s[rows, :], b_s[rows, :], lv_s[...], forward, st_s[d] if carry else None
            )
            o_s[rows, :] += o
            if carry:
                st_s[d] = st_new
            else:
                sout_ref[blk, layer, d, 0] = st_new.T
        return c

    lax.fori_loop(0, n_blk, blocks, 0, unroll=True)

    y_s[j] = (_rms(o_s[...], hgn_ref[...]) * g_s[...]).astype(BF16)

    @pl.when(j == HG_HEADS - 1)
    def _():
        y = jnp.concatenate([y_s[h] for h in range(HG_HEADS)], axis=1)
        phg_ref[...] = _dot(y, wbr_ref[...].astype(BF16))


def _hgrn_call(x, mod, seq_len, layer, norm_mix, w_in, lb_logits, hg_norm, w_branch_hg, state0, prev_states):
    n_tok = x.shape[0]
    nb = n_tok // ROWS
    carry = seq_len > HG_BLOCK
    per_seq_mod = mod.shape[0] > 1

    in_specs = [
        pl.BlockSpec((ROWS, D_MODEL), lambda i, j: (i, 0)),
        pl.BlockSpec((1, N_MOD, D_MODEL), (lambda i, j: (i, 0, 0)) if per_seq_mod else (lambda i, j: (0, 0, 0))),
        pl.BlockSpec((None, 1, D_MODEL), lambda i, j: (layer, 0, 0)),
    ]
    args = [x, mod, norm_mix.reshape(DEPTH, 1, D_MODEL)]
    for g in range(5):
        in_specs.append(pl.BlockSpec((None, D_MODEL, HG_DK), lambda i, j, g=g: (layer, 0, g * HG_HEADS + j)))
        args.append(w_in)
    in_specs += [
        pl.BlockSpec((DEPTH, 2, HG_DK), lambda i, j: (0, 0, j)),
        pl.BlockSpec((None, 1, HG_DV), lambda i, j: (layer, 0, 0)),
        pl.BlockSpec((None, HG_WIDTH, D_MODEL), lambda i, j: (layer, 0, 0), pipeline_mode=pl.Buffered(1)),
    ]
    args += [lb_logits, hg_norm.reshape(DEPTH, 1, HG_DV), w_branch_hg]
    if carry:
        assert seq_len == ROWS
        for d in range(2):
            in_specs.append(
                pl.BlockSpec((1, HG_DK, HG_DV), lambda i, j, d=d: (((i * DEPTH + layer) * 2 + d) * HG_HEADS + j, 0, 0))
            )
            args.append(state0)

    out_shape = [jax.ShapeDtypeStruct((n_tok, D_MODEL), F32)]
    out_specs = [pl.BlockSpec((ROWS, D_MODEL), lambda i, j: (i, 0))]
    if not carry:
        assert seq_len == HG_BLOCK
        n_seq = n_tok // seq_len
        seqs = ROWS // seq_len
        if layer > 0:
            in_specs.append(pl.BlockSpec((seqs, layer, 2, 1, HG_DK, HG_DV), lambda i, j: (i, 0, 0, j, 0, 0)))
            args.append(prev_states)
        out_shape.append(jax.ShapeDtypeStruct((n_seq, layer + 1, 2, HG_HEADS, HG_DK, HG_DV), F32))
        out_specs.append(pl.BlockSpec((seqs, layer + 1, 2, 1, HG_DK, HG_DV), lambda i, j: (i, 0, 0, j, 0, 0)))

    head = lambda dt=F32: pltpu.VMEM((ROWS, HG_DK), dt)
    scratch = [
        pltpu.VMEM((ROWS, D_MODEL), BF16),
        pltpu.VMEM((D_MODEL, 5 * HG_DK), BF16),
        head(), head(), head(),
        head(), head(), head(), head(),
        head(),
        pltpu.VMEM((HG_HEADS, ROWS, HG_DV), BF16),
        pltpu.VMEM((HG_BLOCK // 2, HG_BLOCK // 2), jnp.int32),
        pltpu.VMEM((HG_BLOCK // 2, HG_BLOCK // 2), jnp.int32),
        pltpu.VMEM((HG_BLOCK, HG_BLOCK), BF16),
        pltpu.VMEM((HG_BLOCK, HG_BLOCK), BF16),
        pltpu.VMEM((2, HG_DV, HG_DK), F32),
    ]
    outs = pl.pallas_call(
        functools.partial(_hgrn_kernel, layer=layer, seq_len=seq_len),
        grid=(nb, HG_HEADS),
        in_specs=in_specs,
        out_specs=out_specs,
        out_shape=out_shape,
        scratch_shapes=scratch,
        compiler_params=pltpu.CompilerParams(
            dimension_semantics=("arbitrary", "arbitrary"), vmem_limit_bytes=48 * MIB
        ),
        name=f"hgrn_l{layer}_t{seq_len}",
    )(*args)
    return (outs[0], None) if carry else (outs[0], outs[1])


_MIX_ORDER = (10, 11, 5, 6, 7, 12, 13, 8, 9)


def _mix_col(k):
    idx = 0
    for n, c in enumerate(_MIX_ORDER):
        idx = idx + jnp.where(k == n, c, 0)
    return idx


def _window_mean_minus_self(p, tpos, seq_len, w):
    n = p.shape[0]
    acc = jnp.zeros_like(p)
    for jj in range(-(w // 2), w // 2):
        shifted = p if jj == 0 else pltpu.roll(p, (-jj) % n, 0)
        valid = (tpos + jj >= 0) & (tpos + jj < seq_len)
        acc = acc + jnp.where(valid, shifted, 0.0)
    cnt = jnp.minimum(tpos + w // 2, seq_len) - jnp.maximum(tpos - w // 2, 0)
    return acc / cnt.astype(F32) - p


def _mix_kernel(
    x_ref, mod_ref, nmix_ref, w_ref, sgn_ref, sgw_ref, sgb_ref, wbsg_ref, wbpool_ref, poolw_ref, pscale_ref,
    wout_ref, phg_ref, o_ref, hb, u_s, br_s, mrg_s, *, seq_len,
):
    k = pl.program_id(1)
    half = D_MODEL // 2

    @pl.when(k == 0)
    def _():
        hb[...] = _norm_mod(x_ref[...], nmix_ref[...], mod_ref[0, 1:2, :], mod_ref[0, 0:1, :])

    def for_z_parts(consume, part=PART_ROWS):
        w = w_ref[...].astype(BF16)
        n_parts = ROWS // part

        def zdot(p):
            return _dot(hb[p * part : (p + 1) * part, :], w)

        pending = zdot(0)
        for p in range(n_parts):
            z = pending
            if p + 1 < n_parts:
                pending = zdot(p + 1)
            consume(slice(p * part, (p + 1) * part), z)

    for step in (0, 1):

        @pl.when(k == step)
        def _(step=step):
            cols = slice(step * half, (step + 1) * half)

            def gate(rows, z):
                mrg_s[rows, cols] = jax.nn.sigmoid(z)

            for_z_parts(gate)

    @pl.when(k == 2)
    def _():
        def store_u(rows, z):
            u_s[rows, :] = jax.nn.gelu(z)

        for_z_parts(store_u)

    @pl.when(k == 3)
    def _():
        wbsg = wbsg_ref[...].astype(BF16)
        wgs = [sgw_ref[g].astype(BF16) for g in range(SG_GROUPS)]

        def spatial_gating(rows, z):
            v = _rms(jax.nn.gelu(z), sgn_ref[...]).astype(BF16)
            for g in range(SG_GROUPS):
                bias = sgb_ref[:, g : g + 1]
                cols = slice(g * SG_GROUP_DIM, (g + 1) * SG_GROUP_DIM)
                for n in range((rows.stop - rows.start) // SG_CHUNK):
                    loc = slice(n * SG_CHUNK, (n + 1) * SG_CHUNK)
                    dst = slice(rows.start + n * SG_CHUNK, rows.start + (n + 1) * SG_CHUNK)
                    mixed = _dot(wgs[g], v[loc, cols]) + bias
                    br_s[dst, cols] = (u_s[dst, cols] * mixed).astype(BF16)
            mrg_s[rows, :] = mrg_s[rows, :] * _dot(br_s[rows, :], wbsg)

        for_z_parts(spatial_gating)

    @pl.when(k == 4)
    def _():
        part = _part_rows(seq_len)
        tpos = lax.broadcasted_iota(jnp.int32, (part, POOL_GROUP_DIM), 0) & (seq_len - 1)

        def pool(rows, z):
            for gi, w in enumerate(POOL_WINDOWS):
                cols = slice(gi * POOL_GROUP_DIM, (gi + 1) * POOL_GROUP_DIM)
                pooled = _window_mean_minus_self(z[:, cols], tpos, seq_len, w)
                out = _dot(pooled.astype(BF16), poolw_ref[gi].astype(BF16)) * pscale_ref[:, cols]
                br_s[rows, cols] = out.astype(BF16)

        for_z_parts(pool, part)

    for step in (5, 6):

        @pl.when(k == step)
        def _(step=step):
            cols = slice((step - 5) * half, (step - 4) * half)
            wbpool = wbpool_ref[:, cols].astype(BF16)

            def gate(rows, z):
                mrg_s[rows, cols] = mrg_s[rows, cols] + jax.nn.sigmoid(z) * _dot(br_s[rows, :], wbpool)

            for_z_parts(gate)

    @pl.when(k == 7)
    def _():
        cols = slice(0, half)

        def gate(rows, z):
            mrg_s[rows, cols] = mrg_s[rows, cols] + jax.nn.sigmoid(z) * phg_ref[rows, cols]

        for_z_parts(gate)

    @pl.when(k == 8)
    def _():
        cols = slice(half, D_MODEL)
        wout = wout_ref[...].astype(BF16)

        def gate_and_project(rows, z):
            mrg_s[rows, cols] = mrg_s[rows, cols] + jax.nn.sigmoid(z) * phg_ref[rows, cols]
            y = _dot(mrg_s[rows, :].astype(BF16), wout)
            o_ref[rows, :] = x_ref[rows, :] + mod_ref[0, 2:3, :] * y

        for_z_parts(gate_and_project)


def _mix_call(x, phg, mod, seq_len, layer, norm_mix, w_in, sg_norm, sg_w, sg_b, w_branch_sg, w_branch_pool, pool_w,
              pool_scale, w_out):
    n_tok = x.shape[0]
    nb = n_tok // ROWS
    per_seq_mod = mod.shape[0] > 1
    const = pl.Buffered(1)
    assert seq_len & (seq_len - 1) == 0 and ROWS % seq_len == 0 and seq_len % SG_CHUNK == 0
    in_specs = [
        pl.BlockSpec((ROWS, D_MODEL), lambda i, k: (i, 0)),
        pl.BlockSpec((1, N_MOD, D_MODEL), (lambda i, k: (i, 0, 0)) if per_seq_mod else (lambda i, k: (0, 0, 0))),
        pl.BlockSpec((None, 1, D_MODEL), lambda i, k: (layer, 0, 0)),
        pl.BlockSpec((None, D_MODEL, IN_CHUNK), lambda i, k: (layer, 0, _mix_col(k))),
        pl.BlockSpec((None, 1, SG_WIDTH), lambda i, k: (layer, 0, 0)),
        pl.BlockSpec((None, SG_GROUPS, SG_CHUNK, SG_CHUNK), lambda i, k: (layer, 0, 0, 0)),
        pl.BlockSpec((None, SG_CHUNK, SG_GROUPS), lambda i, k: (layer, 0, 0)),
        pl.BlockSpec((None, SG_WIDTH, D_MODEL), lambda i, k: (layer, 0, 0), pipeline_mode=const),
        pl.BlockSpec((None, POOL_WIDTH, D_MODEL), lambda i, k: (layer, 0, 0), pipeline_mode=const),
        pl.BlockSpec((None, len(POOL_WINDOWS), POOL_GROUP_DIM, POOL_GROUP_DIM), lambda i, k: (layer, 0, 0, 0)),
        pl.BlockSpec((None, 1, POOL_WIDTH), lambda i, k: (layer, 0, 0)),
        pl.BlockSpec((None, D_MODEL, D_MODEL), lambda i, k: (layer, 0, 0), pipeline_mode=const),
        pl.BlockSpec((ROWS, D_MODEL), lambda i, k: (i, 0)),
    ]
    args = [
        x, mod, norm_mix.reshape(DEPTH, 1, D_MODEL), w_in, sg_norm.reshape(DEPTH, 1, SG_WIDTH), sg_w,
        jnp.swapaxes(sg_b, 1, 2), w_branch_sg, w_branch_pool, pool_w, pool_scale.reshape(DEPTH, 1, POOL_WIDTH),
        w_out, phg,
    ]
    scratch = [
        pltpu.VMEM((ROWS, D_MODEL), BF16),
        pltpu.VMEM((ROWS, SG_WIDTH), F32),
        pltpu.VMEM((ROWS, SG_WIDTH), BF16),
        pltpu.VMEM((ROWS, D_MODEL), F32),
    ]
    return pl.pallas_call(
        functools.partial(_mix_kernel, seq_len=seq_len),
        grid=(nb, len(_MIX_ORDER)),
        in_specs=in_specs,
        out_specs=pl.BlockSpec((ROWS, D_MODEL), lambda i, k: (i, 0)),
        out_shape=jax.ShapeDtypeStruct((n_tok, D_MODEL), F32),
        scratch_shapes=scratch,
        compiler_params=pltpu.CompilerParams(
            dimension_semantics=("arbitrary", "arbitrary"), vmem_limit_bytes=56 * MIB
        ),
        name=f"mix_l{layer}_t{seq_len}",
    )(*args)


def _ffn_kernel(x_ref, mod_ref, nffn_ref, *refs, seq_len, final):
    sets = [refs[7 * s : 7 * s + 7] for s in range(FF_SETS)]
    fin_ref, o_ref, hb, acc = refs[7 * FF_SETS :]
    c = pl.program_id(1)
    n_steps = pl.num_programs(1)
    n_chunks = D_FF // FF_CHUNK

    @pl.when(c == 0)
    def _():
        hb[...] = _norm_mod(x_ref[...], nffn_ref[...], mod_ref[0, 4:5, :], mod_ref[0, 3:4, :])
        acc[...] = jnp.zeros_like(acc)

    part = _part_rows(seq_len)
    tpos = lax.broadcasted_iota(jnp.int32, (part, FF_CHUNK), 0) & (seq_len - 1)
    has_prev = tpos >= 1
    has_next = tpos < seq_len - 1

    def conv(h, cw_ref, cb_ref):
        prev = jnp.where(has_prev, pltpu.roll(h, 1, 0), 0.0)
        nxt = jnp.where(has_next, pltpu.roll(h, part - 1, 0), 0.0)
        return prev * cw_ref[0:1, :] + h * cw_ref[1:2, :] + nxt * cw_ref[2:3, :] + cb_ref[...]

    def run(n_sets):
        ws = [(wa[...].astype(BF16), wb[...].astype(BF16), wd[...].astype(BF16)) for wa, wb, _, _, _, _, wd in sets]
        items = [(p, s) for p in range(ROWS // part) for s in range(n_sets)]

        def up(item):
            p, s = item
            h = hb[p * part : (p + 1) * part, :]
            return _dot(h, ws[s][0]), _dot(h, ws[s][1])

        pending = up(items[0])
        down = None
        for i, (p, s) in enumerate(items):
            ha, hb2 = pending
            if i + 1 < len(items):
                pending = up(items[i + 1])
            _, _, cwa_ref, cwb_ref, cba_ref, cbb_ref, _ = sets[s]
            a = conv(ha, cwa_ref, cba_ref)
            b = conv(hb2, cwb_ref, cbb_ref)
            d = _dot((_silu(a) * b).astype(BF16), ws[s][2])
            down = d if down is None else down + d
            if s == n_sets - 1:
                acc[p * part : (p + 1) * part, :] += down
                down = None

    for n_sets in range(1, FF_SETS + 1):
        lo = n_chunks - n_sets * n_steps
        @pl.when((c >= lo) & (c < lo + n_steps))
        def _(n_sets=n_sets):
            run(n_sets)

    @pl.when(c == n_steps - 1)
    def _():
        y = x_ref[...] + mod_ref[0, 5:6, :] * acc[...]
        if final:
            y = _rms(y, fin_ref[...])
        o_ref[...] = y


def _ffn_call(x, mod, seq_len, layer, final, norm_ffn, ffn_up, ffn_conv_w, ffn_conv_b, ffn_down, final_norm):
    n_tok = x.shape[0]
    nb = n_tok // ROWS
    nc = D_FF // FF_CHUNK
    n_steps = pl.cdiv(nc, FF_SETS)
    per_seq_mod = mod.shape[0] > 1
    conv_b = ffn_conv_b.reshape(DEPTH, 1, 2 * D_FF)
    in_specs = [
        pl.BlockSpec((ROWS, D_MODEL), lambda i, c: (i, 0)),
        pl.BlockSpec((1, N_MOD, D_MODEL), (lambda i, c: (i, 0, 0)) if per_seq_mod else (lambda i, c: (0, 0, 0))),
        pl.BlockSpec((None, 1, D_MODEL), lambda i, c: (layer, 0, 0)),
    ]
    args = [x, mod, norm_ffn.reshape(DEPTH, 1, D_MODEL)]
    for s in range(FF_SETS):
        chunk = lambda c, s=s: jnp.minimum(c + s * n_steps, nc - 1)
        in_specs += [
            pl.BlockSpec((None, D_MODEL, FF_CHUNK), lambda i, c, f=chunk: (layer, 0, f(c))),
            pl.BlockSpec((None, D_MODEL, FF_CHUNK), lambda i, c, f=chunk: (layer, 0, nc + f(c))),
            pl.BlockSpec((None, 3, FF_CHUNK), lambda i, c, f=chunk: (layer, 0, f(c))),
            pl.BlockSpec((None, 3, FF_CHUNK), lambda i, c, f=chunk: (layer, 0, nc + f(c))),
            pl.BlockSpec((None, 1, FF_CHUNK), lambda i, c, f=chunk: (layer, 0, f(c))),
            pl.BlockSpec((None, 1, FF_CHUNK), lambda i, c, f=chunk: (layer, 0, nc + f(c))),
            pl.BlockSpec((None, FF_CHUNK, D_MODEL), lambda i, c, f=chunk: (layer, f(c), 0)),
        ]
        args += [ffn_up, ffn_up, ffn_conv_w, ffn_conv_w, conv_b, conv_b, ffn_down]
    in_specs.append(pl.BlockSpec((1, D_MODEL), lambda i, c: (0, 0)))
    args.append(final_norm.reshape(1, D_MODEL))
    return pl.pallas_call(
        functools.partial(_ffn_kernel, seq_len=seq_len, final=final),
        grid=(nb, n_steps),
        in_specs=in_specs,
        out_specs=pl.BlockSpec((ROWS, D_MODEL), lambda i, c: (i, 0)),
        out_shape=jax.ShapeDtypeStruct((n_tok, D_MODEL), F32),
        scratch_shapes=[pltpu.VMEM((ROWS, D_MODEL), BF16), pltpu.VMEM((ROWS, D_MODEL), F32)],
        compiler_params=pltpu.CompilerParams(
            dimension_semantics=("arbitrary", "arbitrary"), vmem_limit_bytes=52 * MIB
        ),
        name=f"ffn_l{layer}_t{seq_len}",
    )(*args)


def kernel(x_prompt, x_sample, c, state_hgrn, c_ctx, norm_mix, norm_ffn, w_ada, b_ada, w_in, lb_logits, hg_norm,
           w_branch_hg, w_branch_sg, w_branch_pool, w_out, sg_norm, sg_w, sg_b, pool_w, pool_scale, ffn_up,
           ffn_conv_w, ffn_conv_b, ffn_down, final_norm):
    n_ctx, t_ctx, _ = x_prompt.shape
    n_lat, t_lat, _ = x_sample.shape

    n_cond = 1 + n_lat
    pad = -n_cond % V7X_SUBLANES
    cvec = jnp.concatenate([c_ctx[None, :], c, jnp.zeros((pad, D_MODEL), F32)], axis=0)
    mod = _mod_call(cvec, w_ada, b_ada).reshape(DEPTH, n_cond + pad, N_MOD, D_MODEL)

    xs = _addpos_call(x_sample, _grid_pos_embed(t_lat)).reshape(n_lat * t_lat, D_MODEL)
    xp = x_prompt.reshape(n_ctx * t_ctx, D_MODEL)
    state0 = state_hgrn.reshape(n_lat * DEPTH * 2 * HG_HEADS, HG_DK, HG_DV)

    states = None
    for layer in range(DEPTH):
        final = layer == DEPTH - 1
        groups = []
        for x, m, t, s0 in ((xp, mod[layer, 0:1], t_ctx, None), (xs, mod[layer, 1:n_cond], t_lat, state0)):
            phg, s_fin = _hgrn_call(x, m, t, layer, norm_mix, w_in, lb_logits, hg_norm, w_branch_hg, s0, states)
            x1 = _mix_call(x, phg, m, t, layer, norm_mix, w_in, sg_norm, sg_w, sg_b, w_branch_sg, w_branch_pool,
                           pool_w, pool_scale, w_out)
            x2 = _ffn_call(x1, m, t, layer, final, norm_ffn, ffn_up, ffn_conv_w, ffn_conv_b, ffn_down, final_norm)
            groups.append((x2, s_fin))
        (xp, states), (xs, _) = groups

    y_prompt = xp.reshape(x_prompt.shape)
    y_sample = xs.reshape(x_sample.shape)
    return (y_prompt, y_sample, states)
```

```python
import functools

import jax
import jax.numpy as jnp
import numpy as np
from jax import lax
from jax.experimental import pallas as pl
from jax.experimental.pallas import tpu as pltpu

D_MODEL = 1024
DEPTH = 2
GRID_W = 64
POS_BASE = 10000.0
EPS = 1e-6
HG_HEADS = 4
HG_DK = 128
HG_DV = 128
HG_WIDTH = HG_HEADS * HG_DV
SG_GROUPS = 4
SG_WIDTH = 512
SG_GROUP_DIM = SG_WIDTH // SG_GROUPS
SG_CHUNK = 128
POOL_WINDOWS = (2, 4, 8, 16)
POOL_WIDTH = 512
POOL_GROUP_DIM = POOL_WIDTH // len(POOL_WINDOWS)
IN_COLS = 5 * HG_WIDTH + 2 * SG_WIDTH + POOL_WIDTH + 3 * D_MODEL
D_FF = 2816
N_MOD = 6

V7X_LANES = 128
V7X_SUBLANES = 8
V7X_MXU_DIM = 256
MIB = 2**20

ROWS = 1024
PART_ROWS = 512
HG_BLOCK = 256
HG_LEVELS = 8
IN_CHUNK = 512
FF_CHUNK = V7X_MXU_DIM
FF_SETS = 3
MOD_CHUNK = 1536

NEG_LOG2E = -1.4426950408889634

F32 = jnp.float32
BF16 = jnp.bfloat16


def _dot(a, b):
    return lax.dot_general(a, b, (((1,), (0,)), ((), ())), preferred_element_type=F32)


def _dot_nt(a, b):
    return lax.dot_general(a, b, (((1,), (1,)), ((), ())), preferred_element_type=F32)


def _dot_tn(a, b):
    return lax.dot_general(a, b, (((0,), (0,)), ((), ())), preferred_element_type=F32)


def _sigmoid(x):
    return 0.5 * jnp.tanh(0.5 * x) + 0.5


def _silu(x):
    return x * _sigmoid(x)


_GELU_C1 = 0.7978845608028654
_GELU_C2 = _GELU_C1 * 0.044715


def _gelu_tanh(x):
    half_x = 0.5 * x
    return half_x + half_x * jnp.tanh(x * (_GELU_C1 + _GELU_C2 * (x * x)))


def _rms(x, gain):
    return x * lax.rsqrt(jnp.mean(x * x, axis=-1, keepdims=True) + EPS) * gain


def _log1pexp(y):
    return jnp.maximum(y, 0.0) + jnp.log(1.0 + jnp.exp(-jnp.abs(y)))


def _norm_mod(x, gain, scale, shift):
    return (_rms(x, gain) * (1.0 + scale) + shift).astype(BF16)


def _part_rows(seq_len):
    return max(seq_len, PART_ROWS)


def _mod_kernel(c_ref, w_ref, b_ref, o_ref):
    c = _silu(c_ref[...]).astype(BF16)
    o_ref[...] = _dot(c, w_ref[...].astype(BF16)) + b_ref[...]


def _mod_call(cvec, w_ada, b_ada):
    n_rows = cvec.shape[0]
    n_cols = N_MOD * D_MODEL
    return pl.pallas_call(
        _mod_kernel,
        grid=(DEPTH, n_cols // MOD_CHUNK),
        in_specs=[
            pl.BlockSpec((n_rows, D_MODEL), lambda l, n: (0, 0)),
            pl.BlockSpec((None, D_MODEL, MOD_CHUNK), lambda l, n: (l, 0, n)),
            pl.BlockSpec((None, 1, MOD_CHUNK), lambda l, n: (l, 0, n)),
        ],
        out_specs=pl.BlockSpec((None, n_rows, MOD_CHUNK), lambda l, n: (l, 0, n)),
        out_shape=jax.ShapeDtypeStruct((DEPTH, n_rows, n_cols), F32),
        compiler_params=pltpu.CompilerParams(
            dimension_semantics=("arbitrary", "arbitrary"), vmem_limit_bytes=32 * MIB
        ),
        name="adaln_mod",
    )(cvec, w_ada, b_ada.reshape(DEPTH, 1, n_cols))


def _addpos_kernel(x_ref, p_ref, o_ref):
    o_ref[...] = x_ref[...] + p_ref[...]


def _addpos_call(x, pos):
    b, t, d = x.shape
    return pl.pallas_call(
        _addpos_kernel,
        grid=(b,),
        in_specs=[pl.BlockSpec((None, t, d), lambda i: (i, 0, 0)), pl.BlockSpec((t, d), lambda i: (0, 0))],
        out_specs=pl.BlockSpec((None, t, d), lambda i: (i, 0, 0)),
        out_shape=jax.ShapeDtypeStruct(x.shape, x.dtype),
        compiler_params=pltpu.CompilerParams(dimension_semantics=("arbitrary",), vmem_limit_bytes=32 * MIB),
        name="add_pos",
    )(x, pos)


def _grid_pos_embed(n_tokens):
    rows = n_tokens // GRID_W
    r = np.broadcast_to(np.arange(rows, dtype=np.float32)[:, None], (rows, GRID_W)).reshape(-1)
    col = np.broadcast_to(np.arange(GRID_W, dtype=np.float32)[None, :], (rows, GRID_W)).reshape(-1)
    quarter = D_MODEL // 4
    omega = (1.0 / (np.float32(POS_BASE) ** (np.arange(quarter, dtype=np.float32) / quarter))).astype(np.float32)
    ar = r[:, None] * omega[None, :]
    ac = col[:, None] * omega[None, :]
    return jnp.asarray(np.concatenate([np.sin(ar), np.cos(ar), np.sin(ac), np.cos(ac)], axis=-1), F32)


def _ref_rows(b, blk, r):
    n, c = b.shape
    if blk >= V7X_SUBLANES:
        x3 = b.reshape(n // blk, blk, c)
        return jnp.broadcast_to(x3[:, r : r + 1, :], x3.shape).reshape(n, c)
    x3 = b.reshape(n // V7X_SUBLANES, V7X_SUBLANES, c)
    sub = lax.broadcasted_iota(jnp.int32, x3.shape, 1)
    bases = list(range(0, V7X_SUBLANES, blk))
    out = jnp.broadcast_to(x3[:, bases[-1] + r : bases[-1] + r + 1, :], x3.shape)
    for base in reversed(bases[:-1]):
        out = jnp.where(sub < base + blk, jnp.broadcast_to(x3[:, base + r : base + r + 1, :], x3.shape), out)
    return out.reshape(n, c)


def _cum_logdecay(lf, tri):
    hi = lf.astype(BF16)
    r1 = lf - hi.astype(F32)
    mid = r1.astype(BF16)
    lo = (r1 - mid.astype(F32)).astype(BF16)
    return _dot(tri, hi) + _dot(tri, mid) + _dot(tri, lo)


def _hgrn_block(q, k, v, b, lvq, forward, st):
    half = HG_BLOCK // 2
    lo, hi = slice(0, half), slice(half, HG_BLOCK)
    vb = v.astype(BF16)
    qb, kb = q.astype(BF16), k.astype(BF16)
    diag = [jnp.where(lvq == 0, _dot_nt(qb[h], kb[h]), 0.0) for h in (lo, hi)]
    for m in range(1, HG_LEVELS):
        blk = 2**m
        ref = _ref_rows(b, blk, blk // 2 - 1 if forward else blk // 2)
        e = jnp.exp2(jnp.abs(b - ref) * NEG_LOG2E).astype(BF16)
        qt, kt = qb * e, kb * e
        diag = [jnp.where(lvq == m, _dot_nt(qt[h], kt[h]), a) for h, a in zip((lo, hi), diag)]
    mid = half - 1 if forward else half
    e = jnp.exp2(jnp.abs(b - b[mid : mid + 1, :]) * NEG_LOG2E).astype(BF16)
    qt, kt = qb * e, kb * e
    a_lo, a_hi = (a.astype(BF16) for a in diag)
    if forward:
        cross = _dot_nt(qt[hi], kt[lo]).astype(BF16)
        o = jnp.concatenate([_dot(a_lo, vb[lo]), _dot(cross, vb[lo]) + _dot(a_hi, vb[hi])], axis=0)
    else:
        cross = _dot_nt(qt[lo], kt[hi]).astype(BF16)
        o = jnp.concatenate([_dot(a_lo, vb[lo]) + _dot(cross, vb[hi]), _dot(a_hi, vb[hi])], axis=0)
    edge = b[HG_BLOCK - 1 : HG_BLOCK, :] if forward else b[0:1, :]
    k_end = (k * jnp.exp(edge - b)).astype(BF16)
    st_new = _dot_tn(vb, k_end)
    if st is not None:
        o = o + _dot_nt((q * jnp.exp(b)).astype(BF16), st.astype(BF16))
        st_new = st_new + st * jnp.exp(edge)
    return o, st_new


def _hgrn_kernel(*refs, layer, seq_len):
    carry = seq_len > HG_BLOCK
    it = iter(refs)
    x_ref, mod_ref, nmix_ref = next(it), next(it), next(it)
    w_refs = [next(it) for _ in range(5)]
    lbl_ref, hgn_ref, wbr_ref = next(it), next(it), next(it)
    s0_refs = [next(it), next(it)] if carry else None
    sprev_ref = next(it) if (not carry and layer > 0) else None
    phg_ref = next(it)
    sout_ref = None if carry else next(it)
    hb, wcat, q_s, v_s, g_s, kf_s, kb_s, bf_s, bb_s, o_s, y_s, lvf_s, lvb_s, trif_s, trib_s, st_s = it

    j = pl.program_id(1)

    @pl.when(j == 0)
    def _():
        hb[...] = _norm_mod(x_ref[...], nmix_ref[...], mod_ref[0, 1:2, :], mod_ref[0, 0:1, :])
        t = lax.broadcasted_iota(jnp.int32, (HG_BLOCK // 2, HG_BLOCK // 2), 0)
        s = lax.broadcasted_iota(jnp.int32, (HG_BLOCK // 2, HG_BLOCK // 2), 1)
        x = t ^ s
        lv = jnp.zeros_like(x)
        for m in range(HG_LEVELS - 1):
            lv = lv + (x >= 2**m).astype(jnp.int32)
        lvf_s[...] = jnp.where(t >= s, lv, -1)
        lvb_s[...] = jnp.where(t <= s, lv, -1)
        t = lax.broadcasted_iota(jnp.int32, (HG_BLOCK, HG_BLOCK), 0)
        s = lax.broadcasted_iota(jnp.int32, (HG_BLOCK, HG_BLOCK), 1)
        trif_s[...] = (t >= s).astype(BF16)
        trib_s[...] = (t <= s).astype(BF16)

    for g, w_ref in enumerate(w_refs):
        wcat[:, g * HG_DK : (g + 1) * HG_DK] = w_ref[...].astype(BF16)
    a0, a1 = lbl_ref[0], lbl_ref[1]
    amax = jnp.maximum(a0, a1)
    e0, e1 = jnp.exp(a0 - amax), jnp.exp(a1 - amax)
    p0, p1 = e0 / (e0 + e1), e1 / (e0 + e1)
    lb = (p0 - p0) if layer == 0 else ((p0 + p1) - p0)
    log_lb = jnp.log(lb)

    wc = wcat[...]
    part = PART_ROWS
    n_parts = ROWS // part

    def zdot(p):
        return _dot(hb[p * part : (p + 1) * part, :], wc)

    pending = zdot(0)
    for p in range(n_parts):
        z = pending
        if p + 1 < n_parts:
            pending = zdot(p + 1)
        rows = slice(p * part, (p + 1) * part)
        zq, zff, zfb, zi, zg = (z[:, g * HG_DK : (g + 1) * HG_DK] for g in range(5))
        q_s[rows, :] = _silu(zq) * HG_DK**-0.5
        v_s[rows, :] = zi
        g_s[rows, :] = _silu(zg)
        for d, (zf, k_s, b_s, tri_s) in enumerate(((zff, kf_s, bf_s, trif_s), (zfb, kb_s, bb_s, trib_s))):
            lf = _log1pexp(log_lb[d : d + 1, :] - zf) - _log1pexp(-zf)
            k_s[rows, :] = (1.0 - lb[d : d + 1, :]) * _sigmoid(-zf)
            tri = tri_s[...]
            for n in range(part // HG_BLOCK):
                loc = slice(n * HG_BLOCK, (n + 1) * HG_BLOCK)
                dst = slice(rows.start + n * HG_BLOCK, rows.start + (n + 1) * HG_BLOCK)
                b_s[dst, :] = _cum_logdecay(lf[loc, :], tri)

    o_s[...] = jnp.zeros_like(o_s)
    n_blk = ROWS // HG_BLOCK
    if sprev_ref is not None:
        sout_ref[:, 0:layer] = sprev_ref[...]
    if carry:
        for d in range(2):
            st_s[d] = s0_refs[d][0].T

    def blocks(n, c):
        for d in range(2):
            forward = d == 0
            blk = n if forward else n_blk - 1 - n
            rows = pl.ds(pl.multiple_of(blk * HG_BLOCK, HG_BLOCK), HG_BLOCK)
            k_s, b_s, lv_s = (kf_s, bf_s, lvf_s) if forward else (kb_s, bb_s, lvb_s)
            o, st_new = _hgrn_block(
                q_s[rows, :], k_s[rows, :], v_s[rows, :], b_s[rows, :], lv_s[...], forward, st_s[d] if carry else None
            )
            o_s[rows, :] += o
            if carry:
                st_s[d] = st_new
            else:
                sout_ref[blk, layer, d, 0] = st_new.T
        return c

    lax.fori_loop(0, n_blk, blocks, 0, unroll=True)

    y_s[j] = (_rms(o_s[...], hgn_ref[...]) * g_s[...]).astype(BF16)

    @pl.when(j == HG_HEADS - 1)
    def _():
        y = jnp.concatenate([y_s[h] for h in range(HG_HEADS)], axis=1)
        phg_ref[...] = _dot(y, wbr_ref[...].astype(BF16))


def _hgrn_call(x, mod, seq_len, layer, norm_mix, w_in, lb_logits, hg_norm, w_branch_hg, state0, prev_states):
    n_tok = x.shape[0]
    nb = n_tok // ROWS
    carry = seq_len > HG_BLOCK
    per_seq_mod = mod.shape[0] > 1

    in_specs = [
        pl.BlockSpec((ROWS, D_MODEL), lambda i, j: (i, 0)),
        pl.BlockSpec((1, N_MOD, D_MODEL), (lambda i, j: (i, 0, 0)) if per_seq_mod else (lambda i, j: (0, 0, 0))),
        pl.BlockSpec((None, 1, D_MODEL), lambda i, j: (layer, 0, 0)),
    ]
    args = [x, mod, norm_mix.reshape(DEPTH, 1, D_MODEL)]
    for g in range(5):
        in_specs.append(pl.BlockSpec((None, D_MODEL, HG_DK), lambda i, j, g=g: (layer, 0, g * HG_HEADS + j)))
        args.append(w_in)
    in_specs += [
        pl.BlockSpec((DEPTH, 2, HG_DK), lambda i, j: (0, 0, j)),
        pl.BlockSpec((None, 1, HG_DV), lambda i, j: (layer, 0, 0)),
        pl.BlockSpec((None, HG_WIDTH, D_MODEL), lambda i, j: (layer, 0, 0), pipeline_mode=pl.Buffered(1)),
    ]
    args += [lb_logits, hg_norm.reshape(DEPTH, 1, HG_DV), w_branch_hg]
    if carry:
        assert seq_len == ROWS
        for d in range(2):
            in_specs.append(
                pl.BlockSpec((1, HG_DK, HG_DV), lambda i, j, d=d: (((i * DEPTH + layer) * 2 + d) * HG_HEADS + j, 0, 0))
            )
            args.append(state0)

    out_shape = [jax.ShapeDtypeStruct((n_tok, D_MODEL), F32)]
    out_specs = [pl.BlockSpec((ROWS, D_MODEL), lambda i, j: (i, 0))]
    if not carry:
        assert seq_len == HG_BLOCK
        n_seq = n_tok // seq_len
        seqs = ROWS // seq_len
        if layer > 0:
            in_specs.append(pl.BlockSpec((seqs, layer, 2, 1, HG_DK, HG_DV), lambda i, j: (i, 0, 0, j, 0, 0)))
            args.append(prev_states)
        out_shape.append(jax.ShapeDtypeStruct((n_seq, layer + 1, 2, HG_HEADS, HG_DK, HG_DV), F32))
        out_specs.append(pl.BlockSpec((seqs, layer + 1, 2, 1, HG_DK, HG_DV), lambda i, j: (i, 0, 0, j, 0, 0)))

    head = lambda dt=F32: pltpu.VMEM((ROWS, HG_DK), dt)
    scratch = [
        pltpu.VMEM((ROWS, D_MODEL), BF16),
        pltpu.VMEM((D_MODEL, 5 * HG_DK), BF16),
        head(), head(), head(),
        head(), head(), head(), head(),
        head(),
        pltpu.VMEM((HG_HEADS, ROWS, HG_DV), BF16),
        pltpu.VMEM((HG_BLOCK // 2, HG_BLOCK // 2), jnp.int32),
        pltpu.VMEM((HG_BLOCK // 2, HG_BLOCK // 2), jnp.int32),
        pltpu.VMEM((HG_BLOCK, HG_BLOCK), BF16),
        pltpu.VMEM((HG_BLOCK, HG_BLOCK), BF16),
        pltpu.VMEM((2, HG_DV, HG_DK), F32),
    ]
    outs = pl.pallas_call(
        functools.partial(_hgrn_kernel, layer=layer, seq_len=seq_len),
        grid=(nb, HG_HEADS),
        in_specs=in_specs,
        out_specs=out_specs,
        out_shape=out_shape,
        scratch_shapes=scratch,
        compiler_params=pltpu.CompilerParams(
            dimension_semantics=("arbitrary", "arbitrary"), vmem_limit_bytes=48 * MIB
        ),
        name=f"hgrn_l{layer}_t{seq_len}",
    )(*args)
    return (outs[0], None) if carry else (outs[0], outs[1])


_MIX_ORDER = (10, 11, 5, 6, 7, 12, 13, 8, 9)


def _mix_col(k):
    idx = 0
    for n, c in enumerate(_MIX_ORDER):
        idx = idx + jnp.where(k == n, c, 0)
    return idx


def _window_mean_minus_self(p, tpos, seq_len, w):
    n = p.shape[0]
    acc = jnp.zeros_like(p)
    for jj in range(-(w // 2), w // 2):
        shifted = p if jj == 0 else pltpu.roll(p, (-jj) % n, 0)
        valid = (tpos + jj >= 0) & (tpos + jj < seq_len)
        acc = acc + jnp.where(valid, shifted, 0.0)
    cnt = jnp.minimum(tpos + w // 2, seq_len) - jnp.maximum(tpos - w // 2, 0)
    return acc / cnt.astype(F32) - p


def _mix_kernel(
    x_ref, mod_ref, nmix_ref, w_ref, sgn_ref, sgw_ref, sgb_ref, wbsg_ref, wbpool_ref, poolw_ref, pscale_ref,
    wout_ref, phg_ref, o_ref, hb, u_s, br_s, mrg_s, *, seq_len,
):
    k = pl.program_id(1)
    half = D_MODEL // 2

    @pl.when(k == 0)
    def _():
        hb[...] = _norm_mod(x_ref[...], nmix_ref[...], mod_ref[0, 1:2, :], mod_ref[0, 0:1, :])

    def for_z_parts(consume, part=PART_ROWS):
        w = w_ref[...].astype(BF16)
        n_parts = ROWS // part

        def zdot(p):
            return _dot(hb[p * part : (p + 1) * part, :], w)

        pending = zdot(0)
        for p in range(n_parts):
            z = pending
            if p + 1 < n_parts:
                pending = zdot(p + 1)
            consume(slice(p * part, (p + 1) * part), z)

    for step in (0, 1):

        @pl.when(k == step)
        def _(step=step):
            cols = slice(step * half, (step + 1) * half)

            def gate(rows, z):
                mrg_s[rows, cols] = _sigmoid(z)

            for_z_parts(gate)

    @pl.when(k == 2)
    def _():
        def store_u(rows, z):
            u_s[rows, :] = _gelu_tanh(z)

        for_z_parts(store_u)

    @pl.when(k == 3)
    def _():
        wbsg = wbsg_ref[...].astype(BF16)
        wgs = [sgw_ref[g].astype(BF16) for g in range(SG_GROUPS)]

        def spatial_gating(rows, z):
            v = _rms(_gelu_tanh(z), sgn_ref[...]).astype(BF16)
            for g in range(SG_GROUPS):
                bias = sgb_ref[:, g : g + 1]
                cols = slice(g * SG_GROUP_DIM, (g + 1) * SG_GROUP_DIM)
                for n in range((rows.stop - rows.start) // SG_CHUNK):
                    loc = slice(n * SG_CHUNK, (n + 1) * SG_CHUNK)
                    dst = slice(rows.start + n * SG_CHUNK, rows.start + (n + 1) * SG_CHUNK)
                    mixed = _dot(wgs[g], v[loc, cols]) + bias
                    br_s[dst, cols] = (u_s[dst, cols] * mixed).astype(BF16)
            mrg_s[rows, :] = mrg_s[rows, :] * _dot(br_s[rows, :], wbsg)

        for_z_parts(spatial_gating)

    @pl.when(k == 4)
    def _():
        part = _part_rows(seq_len)
        tpos = lax.broadcasted_iota(jnp.int32, (part, POOL_GROUP_DIM), 0) & (seq_len - 1)

        def pool(rows, z):
            for gi, w in enumerate(POOL_WINDOWS):
                cols = slice(gi * POOL_GROUP_DIM, (gi + 1) * POOL_GROUP_DIM)
                pooled = _window_mean_minus_self(z[:, cols], tpos, seq_len, w)
                out = _dot(pooled.astype(BF16), poolw_ref[gi].astype(BF16)) * pscale_ref[:, cols]
                br_s[rows, cols] = out.astype(BF16)

        for_z_parts(pool, part)

    for step in (5, 6):

        @pl.when(k == step)
        def _(step=step):
            cols = slice((step - 5) * half, (step - 4) * half)
            wbpool = wbpool_ref[:, cols].astype(BF16)

            def gate(rows, z):
                mrg_s[rows, cols] = mrg_s[rows, cols] + _sigmoid(z) * _dot(br_s[rows, :], wbpool)

            for_z_parts(gate)

    @pl.when(k == 7)
    def _():
        cols = slice(0, half)

        def gate(rows, z):
            mrg_s[rows, cols] = mrg_s[rows, cols] + _sigmoid(z) * phg_ref[rows, cols]

        for_z_parts(gate)

    @pl.when(k == 8)
    def _():
        cols = slice(half, D_MODEL)
        wout = wout_ref[...].astype(BF16)

        def gate_and_project(rows, z):
            mrg_s[rows, cols] = mrg_s[rows, cols] + _sigmoid(z) * phg_ref[rows, cols]
            y = _dot(mrg_s[rows, :].astype(BF16), wout)
            o_ref[rows, :] = x_ref[rows, :] + mod_ref[0, 2:3, :] * y

        for_z_parts(gate_and_project)


def _mix_call(x, phg, mod, seq_len, layer, norm_mix, w_in, sg_norm, sg_w, sg_b, w_branch_sg, w_branch_pool, pool_w,
              pool_scale, w_out):
    n_tok = x.shape[0]
    nb = n_tok // ROWS
    per_seq_mod = mod.shape[0] > 1
    const = pl.Buffered(1)
    assert seq_len & (seq_len - 1) == 0 and ROWS % seq_len == 0 and seq_len % SG_CHUNK == 0
    in_specs = [
        pl.BlockSpec((ROWS, D_MODEL), lambda i, k: (i, 0)),
        pl.BlockSpec((1, N_MOD, D_MODEL), (lambda i, k: (i, 0, 0)) if per_seq_mod else (lambda i, k: (0, 0, 0))),
        pl.BlockSpec((None, 1, D_MODEL), lambda i, k: (layer, 0, 0)),
        pl.BlockSpec((None, D_MODEL, IN_CHUNK), lambda i, k: (layer, 0, _mix_col(k))),
        pl.BlockSpec((None, 1, SG_WIDTH), lambda i, k: (layer, 0, 0)),
        pl.BlockSpec((None, SG_GROUPS, SG_CHUNK, SG_CHUNK), lambda i, k: (layer, 0, 0, 0)),
        pl.BlockSpec((None, SG_CHUNK, SG_GROUPS), lambda i, k: (layer, 0, 0)),
        pl.BlockSpec((None, SG_WIDTH, D_MODEL), lambda i, k: (layer, 0, 0), pipeline_mode=const),
        pl.BlockSpec((None, POOL_WIDTH, D_MODEL), lambda i, k: (layer, 0, 0), pipeline_mode=const),
        pl.BlockSpec((None, len(POOL_WINDOWS), POOL_GROUP_DIM, POOL_GROUP_DIM), lambda i, k: (layer, 0, 0, 0)),
        pl.BlockSpec((None, 1, POOL_WIDTH), lambda i, k: (layer, 0, 0)),
        pl.BlockSpec((None, D_MODEL, D_MODEL), lambda i, k: (layer, 0, 0), pipeline_mode=const),
        pl.BlockSpec((ROWS, D_MODEL), lambda i, k: (i, 0)),
    ]
    args = [
        x, mod, norm_mix.reshape(DEPTH, 1, D_MODEL), w_in, sg_norm.reshape(DEPTH, 1, SG_WIDTH), sg_w,
        jnp.swapaxes(sg_b, 1, 2), w_branch_sg, w_branch_pool, pool_w, pool_scale.reshape(DEPTH, 1, POOL_WIDTH),
        w_out, phg,
    ]
    scratch = [
        pltpu.VMEM((ROWS, D_MODEL), BF16),
        pltpu.VMEM((ROWS, SG_WIDTH), F32),
        pltpu.VMEM((ROWS, SG_WIDTH), BF16),
        pltpu.VMEM((ROWS, D_MODEL), F32),
    ]
    return pl.pallas_call(
        functools.partial(_mix_kernel, seq_len=seq_len),
        grid=(nb, len(_MIX_ORDER)),
        in_specs=in_specs,
        out_specs=pl.BlockSpec((ROWS, D_MODEL), lambda i, k: (i, 0)),
        out_shape=jax.ShapeDtypeStruct((n_tok, D_MODEL), F32),
        scratch_shapes=scratch,
        compiler_params=pltpu.CompilerParams(
            dimension_semantics=("arbitrary", "arbitrary"), vmem_limit_bytes=56 * MIB
        ),
        name=f"mix_l{layer}_t{seq_len}",
    )(*args)


def _ffn_kernel(x_ref, mod_ref, nffn_ref, *refs, seq_len, final):
    sets = [refs[7 * s : 7 * s + 7] for s in range(FF_SETS)]
    fin_ref, o_ref, hb, acc = refs[7 * FF_SETS :]
    c = pl.program_id(1)
    n_steps = pl.num_programs(1)
    n_chunks = D_FF // FF_CHUNK

    @pl.when(c == 0)
    def _():
        hb[...] = _norm_mod(x_ref[...], nffn_ref[...], mod_ref[0, 4:5, :], mod_ref[0, 3:4, :])
        acc[...] = jnp.zeros_like(acc)

    part = _part_rows(seq_len)
    tpos = lax.broadcasted_iota(jnp.int32, (part, FF_CHUNK), 0) & (seq_len - 1)
    has_prev = tpos >= 1
    has_next = tpos < seq_len - 1

    def conv(h, cw_ref, cb_ref):
        prev = jnp.where(has_prev, pltpu.roll(h, 1, 0), 0.0)
        nxt = jnp.where(has_next, pltpu.roll(h, part - 1, 0), 0.0)
        return prev * cw_ref[0:1, :] + h * cw_ref[1:2, :] + nxt * cw_ref[2:3, :] + cb_ref[...]

    def run(n_sets):
        ws = [(wa[...].astype(BF16), wb[...].astype(BF16), wd[...].astype(BF16)) for wa, wb, _, _, _, _, wd in sets]
        items = [(p, s) for p in range(ROWS // part) for s in range(n_sets)]

        def up(item):
            p, s = item
            h = hb[p * part : (p + 1) * part, :]
            return _dot(h, ws[s][0]), _dot(h, ws[s][1])

        pending = up(items[0])
        down = None
        for i, (p, s) in enumerate(items):
            ha, hb2 = pending
            if i + 1 < len(items):
                pending = up(items[i + 1])
            _, _, cwa_ref, cwb_ref, cba_ref, cbb_ref, _ = sets[s]
            a = conv(ha, cwa_ref, cba_ref)
            b = conv(hb2, cwb_ref, cbb_ref)
            d = _dot((_silu(a) * b).astype(BF16), ws[s][2])
            down = d if down is None else down + d
            if s == n_sets - 1:
                acc[p * part : (p + 1) * part, :] += down
                down = None

    for n_sets in range(1, FF_SETS + 1):
        lo = n_chunks - n_sets * n_steps
        @pl.when((c >= lo) & (c < lo + n_steps))
        def _(n_sets=n_sets):
            run(n_sets)

    @pl.when(c == n_steps - 1)
    def _():
        y = x_ref[...] + mod_ref[0, 5:6, :] * acc[...]
        if final:
            y = _rms(y, fin_ref[...])
        o_ref[...] = y


def _ffn_call(x, mod, seq_len, layer, final, norm_ffn, ffn_up, ffn_conv_w, ffn_conv_b, ffn_down, final_norm):
    n_tok = x.shape[0]
    nb = n_tok // ROWS
    nc = D_FF // FF_CHUNK
    n_steps = pl.cdiv(nc, FF_SETS)
    per_seq_mod = mod.shape[0] > 1
    conv_b = ffn_conv_b.reshape(DEPTH, 1, 2 * D_FF)
    in_specs = [
        pl.BlockSpec((ROWS, D_MODEL), lambda i, c: (i, 0)),
        pl.BlockSpec((1, N_MOD, D_MODEL), (lambda i, c: (i, 0, 0)) if per_seq_mod else (lambda i, c: (0, 0, 0))),
        pl.BlockSpec((None, 1, D_MODEL), lambda i, c: (layer, 0, 0)),
    ]
    args = [x, mod, norm_ffn.reshape(DEPTH, 1, D_MODEL)]
    for s in range(FF_SETS):
        chunk = lambda c, s=s: jnp.minimum(c + s * n_steps, nc - 1)
        in_specs += [
            pl.BlockSpec((None, D_MODEL, FF_CHUNK), lambda i, c, f=chunk: (layer, 0, f(c))),
            pl.BlockSpec((None, D_MODEL, FF_CHUNK), lambda i, c, f=chunk: (layer, 0, nc + f(c))),
            pl.BlockSpec((None, 3, FF_CHUNK), lambda i, c, f=chunk: (layer, 0, f(c))),
            pl.BlockSpec((None, 3, FF_CHUNK), lambda i, c, f=chunk: (layer, 0, nc + f(c))),
            pl.BlockSpec((None, 1, FF_CHUNK), lambda i, c, f=chunk: (layer, 0, f(c))),
            pl.BlockSpec((None, 1, FF_CHUNK), lambda i, c, f=chunk: (layer, 0, nc + f(c))),
            pl.BlockSpec((None, FF_CHUNK, D_MODEL), lambda i, c, f=chunk: (layer, f(c), 0)),
        ]
        args += [ffn_up, ffn_up, ffn_conv_w, ffn_conv_w, conv_b, conv_b, ffn_down]
    in_specs.append(pl.BlockSpec((1, D_MODEL), lambda i, c: (0, 0)))
    args.append(final_norm.reshape(1, D_MODEL))
    return pl.pallas_call(
        functools.partial(_ffn_kernel, seq_len=seq_len, final=final),
        grid=(nb, n_steps),
        in_specs=in_specs,
        out_specs=pl.BlockSpec((ROWS, D_MODEL), lambda i, c: (i, 0)),
        out_shape=jax.ShapeDtypeStruct((n_tok, D_MODEL), F32),
        scratch_shapes=[pltpu.VMEM((ROWS, D_MODEL), BF16), pltpu.VMEM((ROWS, D_MODEL), F32)],
        compiler_params=pltpu.CompilerParams(
            dimension_semantics=("arbitrary", "arbitrary"), vmem_limit_bytes=52 * MIB
        ),
        name=f"ffn_l{layer}_t{seq_len}",
    )(*args)


def kernel(x_prompt, x_sample, c, state_hgrn, c_ctx, norm_mix, norm_ffn, w_ada, b_ada, w_in, lb_logits, hg_norm,
           w_branch_hg, w_branch_sg, w_branch_pool, w_out, sg_norm, sg_w, sg_b, pool_w, pool_scale, ffn_up,
           ffn_conv_w, ffn_conv_b, ffn_down, final_norm):
    n_ctx, t_ctx, _ = x_prompt.shape
    n_lat, t_lat, _ = x_sample.shape

    n_cond = 1 + n_lat
    pad = -n_cond % V7X_SUBLANES
    cvec = jnp.concatenate([c_ctx[None, :], c, jnp.zeros((pad, D_MODEL), F32)], axis=0)
    mod = _mod_call(cvec, w_ada, b_ada).reshape(DEPTH, n_cond + pad, N_MOD, D_MODEL)

    xs = _addpos_call(x_sample, _grid_pos_embed(t_lat)).reshape(n_lat * t_lat, D_MODEL)
    xp = x_prompt.reshape(n_ctx * t_ctx, D_MODEL)
    state0 = state_hgrn.reshape(n_lat * DEPTH * 2 * HG_HEADS, HG_DK, HG_DV)

    states = None
    for layer in range(DEPTH):
        final = layer == DEPTH - 1
        groups = []
        for x, m, t, s0 in ((xp, mod[layer, 0:1], t_ctx, None), (xs, mod[layer, 1:n_cond], t_lat, state0)):
            phg, s_fin = _hgrn_call(x, m, t, layer, norm_mix, w_in, lb_logits, hg_norm, w_branch_hg, s0, states)
            x1 = _mix_call(x, phg, m, t, layer, norm_mix, w_in, sg_norm, sg_w, sg_b, w_branch_sg, w_branch_pool,
                           pool_w, pool_scale, w_out)
            x2 = _ffn_call(x1, m, t, layer, final, norm_ffn, ffn_up, ffn_conv_w, ffn_conv_b, ffn_down, final_norm)
            groups.append((x2, s_fin))
        (xp, states), (xs, _) = groups

    y_prompt = xp.reshape(x_prompt.shape)
    y_sample = xs.reshape(x_sample.shape)
    return (y_prompt, y_sample, states)
```

```python
import functools

import jax
import jax.numpy as jnp
import numpy as np
from jax import lax
from jax.experimental import pallas as pl
from jax.experimental.pallas import tpu as pltpu

D_MODEL = 1024
DEPTH = 2
GRID_W = 64
POS_BASE = 10000.0
EPS = 1e-6
HG_HEADS = 4
HG_DK = 128
HG_DV = 128
HG_WIDTH = HG_HEADS * HG_DV
SG_GROUPS = 4
SG_WIDTH = 512
SG_GROUP_DIM = SG_WIDTH // SG_GROUPS
SG_CHUNK = 128
POOL_WINDOWS = (2, 4, 8, 16)
POOL_WIDTH = 512
POOL_GROUP_DIM = POOL_WIDTH // len(POOL_WINDOWS)
IN_COLS = 5 * HG_WIDTH + 2 * SG_WIDTH + POOL_WIDTH + 3 * D_MODEL
D_FF = 2816
N_MOD = 6

V7X_LANES = 128
V7X_SUBLANES = 8
V7X_MXU_DIM = 256
MIB = 2**20

ROWS = 1024
PART_ROWS = 512
HG_BLOCK = 256
HG_LEVELS = 8
IN_CHUNK = 512
FF_CHUNK = V7X_MXU_DIM
FF_SETS = 3
MOD_CHUNK = 1536

NEG_LOG2E = -1.4426950408889634

F32 = jnp.float32
BF16 = jnp.bfloat16


def _dot(a, b):
    return lax.dot_general(a, b, (((1,), (0,)), ((), ())), preferred_element_type=F32)


def _dot_nt(a, b):
    return lax.dot_general(a, b, (((1,), (1,)), ((), ())), preferred_element_type=F32)


def _dot_tn(a, b):
    return lax.dot_general(a, b, (((0,), (0,)), ((), ())), preferred_element_type=F32)


def _sigmoid(x):
    return 0.5 * jnp.tanh(0.5 * x) + 0.5


def _silu(x):
    return x * _sigmoid(x)


_GELU_C1 = 0.7978845608028654
_GELU_C2 = _GELU_C1 * 0.044715


def _gelu_tanh(x):
    half_x = 0.5 * x
    return half_x + half_x * jnp.tanh(x * (_GELU_C1 + _GELU_C2 * (x * x)))


def _rms(x, gain):
    return x * lax.rsqrt(jnp.mean(x * x, axis=-1, keepdims=True) + EPS) * gain


def _log1pexp(y):
    return jnp.maximum(y, 0.0) + jnp.log(1.0 + jnp.exp(-jnp.abs(y)))


def _norm_mod(x, gain, scale, shift):
    return (_rms(x, gain) * (1.0 + scale) + shift).astype(BF16)


def _part_rows(seq_len):
    return max(seq_len, PART_ROWS)


def _mod_kernel(c_ref, w_ref, b_ref, o_ref):
    c = _silu(c_ref[...]).astype(BF16)
    o_ref[...] = _dot(c, w_ref[...].astype(BF16)) + b_ref[...]


def _mod_call(cvec, w_ada, b_ada):
    n_rows = cvec.shape[0]
    n_cols = N_MOD * D_MODEL
    return pl.pallas_call(
        _mod_kernel,
        grid=(DEPTH, n_cols // MOD_CHUNK),
        in_specs=[
            pl.BlockSpec((n_rows, D_MODEL), lambda l, n: (0, 0)),
            pl.BlockSpec((None, D_MODEL, MOD_CHUNK), lambda l, n: (l, 0, n)),
            pl.BlockSpec((None, 1, MOD_CHUNK), lambda l, n: (l, 0, n)),
        ],
        out_specs=pl.BlockSpec((None, n_rows, MOD_CHUNK), lambda l, n: (l, 0, n)),
        out_shape=jax.ShapeDtypeStruct((DEPTH, n_rows, n_cols), F32),
        compiler_params=pltpu.CompilerParams(
            dimension_semantics=("arbitrary", "arbitrary"), vmem_limit_bytes=32 * MIB
        ),
        name="adaln_mod",
    )(cvec, w_ada, b_ada.reshape(DEPTH, 1, n_cols))


def _addpos_kernel(x_ref, p_ref, o_ref):
    o_ref[...] = x_ref[...] + p_ref[...]


def _addpos_call(x, pos):
    b, t, d = x.shape
    return pl.pallas_call(
        _addpos_kernel,
        grid=(b,),
        in_specs=[pl.BlockSpec((None, t, d), lambda i: (i, 0, 0)), pl.BlockSpec((t, d), lambda i: (0, 0))],
        out_specs=pl.BlockSpec((None, t, d), lambda i: (i, 0, 0)),
        out_shape=jax.ShapeDtypeStruct(x.shape, x.dtype),
        compiler_params=pltpu.CompilerParams(dimension_semantics=("arbitrary",), vmem_limit_bytes=32 * MIB),
        name="add_pos",
    )(x, pos)


def _grid_pos_embed(n_tokens):
    rows = n_tokens // GRID_W
    r = np.broadcast_to(np.arange(rows, dtype=np.float32)[:, None], (rows, GRID_W)).reshape(-1)
    col = np.broadcast_to(np.arange(GRID_W, dtype=np.float32)[None, :], (rows, GRID_W)).reshape(-1)
    quarter = D_MODEL // 4
    omega = (1.0 / (np.float32(POS_BASE) ** (np.arange(quarter, dtype=np.float32) / quarter))).astype(np.float32)
    ar = r[:, None] * omega[None, :]
    ac = col[:, None] * omega[None, :]
    return jnp.asarray(np.concatenate([np.sin(ar), np.cos(ar), np.sin(ac), np.cos(ac)], axis=-1), F32)


def _ref_rows(b, blk, r):
    n, c = b.shape
    if blk >= V7X_SUBLANES:
        x3 = b.reshape(n // blk, blk, c)
        return jnp.broadcast_to(x3[:, r : r + 1, :], x3.shape).reshape(n, c)
    x3 = b.reshape(n // V7X_SUBLANES, V7X_SUBLANES, c)
    sub = lax.broadcasted_iota(jnp.int32, x3.shape, 1)
    bases = list(range(0, V7X_SUBLANES, blk))
    out = jnp.broadcast_to(x3[:, bases[-1] + r : bases[-1] + r + 1, :], x3.shape)
    for base in reversed(bases[:-1]):
        out = jnp.where(sub < base + blk, jnp.broadcast_to(x3[:, base + r : base + r + 1, :], x3.shape), out)
    return out.reshape(n, c)


def _cum_logdecay(lf, tri):
    hi = lf.astype(BF16)
    r1 = lf - hi.astype(F32)
    mid = r1.astype(BF16)
    lo = (r1 - mid.astype(F32)).astype(BF16)
    return _dot(tri, hi) + _dot(tri, mid) + _dot(tri, lo)


def _hgrn_block(q, k, v, b, lvq, forward, st):
    half = HG_BLOCK // 2
    lo, hi = slice(0, half), slice(half, HG_BLOCK)
    vb = v.astype(BF16)
    qb, kb = q.astype(BF16), k.astype(BF16)
    diag = [jnp.where(lvq == 0, _dot_nt(qb[h], kb[h]), 0.0) for h in (lo, hi)]
    for m in range(1, HG_LEVELS):
        blk = 2**m
        ref = _ref_rows(b, blk, blk // 2 - 1 if forward else blk // 2)
        e = jnp.exp2(jnp.abs(b - ref) * NEG_LOG2E).astype(BF16)
        qt, kt = qb * e, kb * e
        diag = [jnp.where(lvq == m, _dot_nt(qt[h], kt[h]), a) for h, a in zip((lo, hi), diag)]
    mid = half - 1 if forward else half
    e = jnp.exp2(jnp.abs(b - b[mid : mid + 1, :]) * NEG_LOG2E).astype(BF16)
    qt, kt = qb * e, kb * e
    a_lo, a_hi = (a.astype(BF16) for a in diag)
    if forward:
        cross = _dot_nt(qt[hi], kt[lo]).astype(BF16)
        o = jnp.concatenate([_dot(a_lo, vb[lo]), _dot(cross, vb[lo]) + _dot(a_hi, vb[hi])], axis=0)
    else:
        cross = _dot_nt(qt[lo], kt[hi]).astype(BF16)
        o = jnp.concatenate([_dot(a_lo, vb[lo]) + _dot(cross, vb[hi]), _dot(a_hi, vb[hi])], axis=0)
    edge = b[HG_BLOCK - 1 : HG_BLOCK, :] if forward else b[0:1, :]
    k_end = (k * jnp.exp(edge - b)).astype(BF16)
    st_new = _dot_tn(vb, k_end)
    if st is not None:
        o = o + _dot_nt((q * jnp.exp(b)).astype(BF16), st.astype(BF16))
        st_new = st_new + st * jnp.exp(edge)
    return o, st_new


def _hgrn_kernel(*refs, layer, seq_len):
    carry = seq_len > HG_BLOCK
    it = iter(refs)
    x_ref, mod_ref, nmix_ref = next(it), next(it), next(it)
    w_refs = [next(it) for _ in range(5)]
    lbl_ref, hgn_ref, wbr_ref = next(it), next(it), next(it)
    s0_refs = [next(it), next(it)] if carry else None
    sprev_ref = next(it) if (not carry and layer > 0) else None
    phg_ref = next(it)
    sout_ref = None if carry else next(it)
    hb, wcat, q_s, v_s, g_s, kf_s, kb_s, bf_s, bb_s, o_s, y_s, lvf_s, lvb_s, trif_s, trib_s, st_s = it

    j = pl.program_id(1)

    def head(first):
        if first:
            t = lax.broadcasted_iota(jnp.int32, (HG_BLOCK // 2, HG_BLOCK // 2), 0)
            s = lax.broadcasted_iota(jnp.int32, (HG_BLOCK // 2, HG_BLOCK // 2), 1)
            x = t ^ s
            lv = jnp.zeros_like(x)
            for m in range(HG_LEVELS - 1):
                lv = lv + (x >= 2**m).astype(jnp.int32)
            lvf_s[...] = jnp.where(t >= s, lv, -1)
            lvb_s[...] = jnp.where(t <= s, lv, -1)
            t = lax.broadcasted_iota(jnp.int32, (HG_BLOCK, HG_BLOCK), 0)
            s = lax.broadcasted_iota(jnp.int32, (HG_BLOCK, HG_BLOCK), 1)
            trif_s[...] = (t >= s).astype(BF16)
            trib_s[...] = (t <= s).astype(BF16)
        for g, w_ref in enumerate(w_refs):
            wcat[:, g * HG_DK : (g + 1) * HG_DK] = w_ref[...].astype(BF16)
        a0, a1 = lbl_ref[0], lbl_ref[1]
        amax = jnp.maximum(a0, a1)
        e0, e1 = jnp.exp(a0 - amax), jnp.exp(a1 - amax)
        p0, p1 = e0 / (e0 + e1), e1 / (e0 + e1)
        lb = (p0 - p0) if layer == 0 else ((p0 + p1) - p0)
        log_lb = jnp.log(lb)

        wc = wcat[...]
        part = PART_ROWS
        n_parts = ROWS // part

        def zdot(p):
            rows = slice(p * part, (p + 1) * part)
            if first:
                hb[rows, :] = _norm_mod(x_ref[rows, :], nmix_ref[...], mod_ref[0, 1:2, :], mod_ref[0, 0:1, :])
            return _dot(hb[rows, :], wc)

        pending = zdot(0)
        for p in range(n_parts):
            z = pending
            if p + 1 < n_parts:
                pending = zdot(p + 1)
            rows = slice(p * part, (p + 1) * part)
            zq, zff, zfb, zi, zg = (z[:, g * HG_DK : (g + 1) * HG_DK] for g in range(5))
            q_s[rows, :] = _silu(zq) * HG_DK**-0.5
            v_s[rows, :] = zi
            g_s[rows, :] = _silu(zg)
            for d, (zf, k_s, b_s, tri_s) in enumerate(((zff, kf_s, bf_s, trif_s), (zfb, kb_s, bb_s, trib_s))):
                lf = _log1pexp(log_lb[d : d + 1, :] - zf) - _log1pexp(-zf)
                k_s[rows, :] = (1.0 - lb[d : d + 1, :]) * _sigmoid(-zf)
                tri = tri_s[...]
                for n in range(part // HG_BLOCK):
                    loc = slice(n * HG_BLOCK, (n + 1) * HG_BLOCK)
                    dst = slice(rows.start + n * HG_BLOCK, rows.start + (n + 1) * HG_BLOCK)
                    b_s[dst, :] = _cum_logdecay(lf[loc, :], tri)

        o_s[...] = jnp.zeros_like(o_s)
        n_blk = ROWS // HG_BLOCK
        if sprev_ref is not None:
            sout_ref[:, 0:layer] = sprev_ref[...]
        if carry:
            for d in range(2):
                st_s[d] = s0_refs[d][0].T

        def blocks(n, c):
            for d in range(2):
                forward = d == 0
                blk = n if forward else n_blk - 1 - n
                rows = pl.ds(pl.multiple_of(blk * HG_BLOCK, HG_BLOCK), HG_BLOCK)
                k_s, b_s, lv_s = (kf_s, bf_s, lvf_s) if forward else (kb_s, bb_s, lvb_s)
                o, st_new = _hgrn_block(
                    q_s[rows, :], k_s[rows, :], v_s[rows, :], b_s[rows, :], lv_s[...], forward, st_s[d] if carry else None
                )
                o_s[rows, :] += o
                if carry:
                    st_s[d] = st_new
                else:
                    sout_ref[blk, layer, d, 0] = st_new.T
            return c

        lax.fori_loop(0, n_blk, blocks, 0, unroll=True)

        y_s[j] = (_rms(o_s[...], hgn_ref[...]) * g_s[...]).astype(BF16)

    @pl.when(j == 0)
    def _():
        head(True)

    @pl.when(j > 0)
    def _():
        head(False)

    @pl.when(j == HG_HEADS - 1)
    def _():
        y = jnp.concatenate([y_s[h] for h in range(HG_HEADS)], axis=1)
        phg_ref[...] = _dot(y, wbr_ref[...].astype(BF16))


def _hgrn_call(x, mod, seq_len, layer, norm_mix, w_in, lb_logits, hg_norm, w_branch_hg, state0, prev_states):
    n_tok = x.shape[0]
    nb = n_tok // ROWS
    carry = seq_len > HG_BLOCK
    per_seq_mod = mod.shape[0] > 1

    in_specs = [
        pl.BlockSpec((ROWS, D_MODEL), lambda i, j: (i, 0)),
        pl.BlockSpec((1, N_MOD, D_MODEL), (lambda i, j: (i, 0, 0)) if per_seq_mod else (lambda i, j: (0, 0, 0))),
        pl.BlockSpec((None, 1, D_MODEL), lambda i, j: (layer, 0, 0)),
    ]
    args = [x, mod, norm_mix.reshape(DEPTH, 1, D_MODEL)]
    for g in range(5):
        in_specs.append(pl.BlockSpec((None, D_MODEL, HG_DK), lambda i, j, g=g: (layer, 0, g * HG_HEADS + j)))
        args.append(w_in)
    in_specs += [
        pl.BlockSpec((DEPTH, 2, HG_DK), lambda i, j: (0, 0, j)),
        pl.BlockSpec((None, 1, HG_DV), lambda i, j: (layer, 0, 0)),
        pl.BlockSpec((None, HG_WIDTH, D_MODEL), lambda i, j: (layer, 0, 0), pipeline_mode=pl.Buffered(1)),
    ]
    args += [lb_logits, hg_norm.reshape(DEPTH, 1, HG_DV), w_branch_hg]
    if carry:
        assert seq_len == ROWS
        for d in range(2):
            in_specs.append(
                pl.BlockSpec((1, HG_DK, HG_DV), lambda i, j, d=d: (((i * DEPTH + layer) * 2 + d) * HG_HEADS + j, 0, 0))
            )
            args.append(state0)

    out_shape = [jax.ShapeDtypeStruct((n_tok, D_MODEL), F32)]
    out_specs = [pl.BlockSpec((ROWS, D_MODEL), lambda i, j: (i, 0))]
    if not carry:
        assert seq_len == HG_BLOCK
        n_seq = n_tok // seq_len
        seqs = ROWS // seq_len
        if layer > 0:
            in_specs.append(pl.BlockSpec((seqs, layer, 2, 1, HG_DK, HG_DV), lambda i, j: (i, 0, 0, j, 0, 0)))
            args.append(prev_states)
        out_shape.append(jax.ShapeDtypeStruct((n_seq, layer + 1, 2, HG_HEADS, HG_DK, HG_DV), F32))
        out_specs.append(pl.BlockSpec((seqs, layer + 1, 2, 1, HG_DK, HG_DV), lambda i, j: (i, 0, 0, j, 0, 0)))

    head = lambda dt=F32: pltpu.VMEM((ROWS, HG_DK), dt)
    scratch = [
        pltpu.VMEM((ROWS, D_MODEL), BF16),
        pltpu.VMEM((D_MODEL, 5 * HG_DK), BF16),
        head(), head(), head(),
        head(), head(), head(), head(),
        head(),
        pltpu.VMEM((HG_HEADS, ROWS, HG_DV), BF16),
        pltpu.VMEM((HG_BLOCK // 2, HG_BLOCK // 2), jnp.int32),
        pltpu.VMEM((HG_BLOCK // 2, HG_BLOCK // 2), jnp.int32),
        pltpu.VMEM((HG_BLOCK, HG_BLOCK), BF16),
        pltpu.VMEM((HG_BLOCK, HG_BLOCK), BF16),
        pltpu.VMEM((2, HG_DV, HG_DK), F32),
    ]
    outs = pl.pallas_call(
        functools.partial(_hgrn_kernel, layer=layer, seq_len=seq_len),
        grid=(nb, HG_HEADS),
        in_specs=in_specs,
        out_specs=out_specs,
        out_shape=out_shape,
        scratch_shapes=scratch,
        compiler_params=pltpu.CompilerParams(
            dimension_semantics=("arbitrary", "arbitrary"), vmem_limit_bytes=48 * MIB
        ),
        name=f"hgrn_l{layer}_t{seq_len}",
    )(*args)
    return (outs[0], None) if carry else (outs[0], outs[1])


_MIX_ORDER = (10, 11, 5, 6, 7, 12, 13, 8, 9)


def _mix_col(k):
    idx = 0
    for n, c in enumerate(_MIX_ORDER):
        idx = idx + jnp.where(k == n, c, 0)
    return idx


def _window_mean_minus_self(p, tpos, seq_len, w):
    n = p.shape[0]
    acc = jnp.zeros_like(p)
    for jj in range(-(w // 2), w // 2):
        shifted = p if jj == 0 else pltpu.roll(p, (-jj) % n, 0)
        valid = (tpos + jj >= 0) & (tpos + jj < seq_len)
        acc = acc + jnp.where(valid, shifted, 0.0)
    cnt = jnp.minimum(tpos + w // 2, seq_len) - jnp.maximum(tpos - w // 2, 0)
    return acc / cnt.astype(F32) - p


def _mix_kernel(
    x_ref, mod_ref, nmix_ref, w_ref, sgn_ref, sgw_ref, sgb_ref, wbsg_ref, wbpool_ref, poolw_ref, pscale_ref,
    wout_ref, phg_ref, o_ref, hb, u_s, br_s, mrg_s, *, seq_len,
):
    k = pl.program_id(1)
    half = D_MODEL // 2

    @pl.when(k == 0)
    def _():
        hb[...] = _norm_mod(x_ref[...], nmix_ref[...], mod_ref[0, 1:2, :], mod_ref[0, 0:1, :])

    def for_z_parts(consume, part=PART_ROWS):
        w = w_ref[...].astype(BF16)
        n_parts = ROWS // part

        def zdot(p):
            return _dot(hb[p * part : (p + 1) * part, :], w)

        pending = zdot(0)
        for p in range(n_parts):
            z = pending
            if p + 1 < n_parts:
                pending = zdot(p + 1)
            consume(slice(p * part, (p + 1) * part), z)

    for step in (0, 1):

        @pl.when(k == step)
        def _(step=step):
            cols = slice(step * half, (step + 1) * half)

            def gate(rows, z):
                mrg_s[rows, cols] = _sigmoid(z)

            for_z_parts(gate)

    @pl.when(k == 2)
    def _():
        def store_u(rows, z):
            u_s[rows, :] = _gelu_tanh(z)

        for_z_parts(store_u)

    @pl.when(k == 3)
    def _():
        wbsg = wbsg_ref[...].astype(BF16)
        wgs = [sgw_ref[g].astype(BF16) for g in range(SG_GROUPS)]

        def spatial_gating(rows, z):
            v = _rms(_gelu_tanh(z), sgn_ref[...]).astype(BF16)
            for g in range(SG_GROUPS):
                bias = sgb_ref[:, g : g + 1]
                cols = slice(g * SG_GROUP_DIM, (g + 1) * SG_GROUP_DIM)
                for n in range((rows.stop - rows.start) // SG_CHUNK):
                    loc = slice(n * SG_CHUNK, (n + 1) * SG_CHUNK)
                    dst = slice(rows.start + n * SG_CHUNK, rows.start + (n + 1) * SG_CHUNK)
                    mixed = _dot(wgs[g], v[loc, cols]) + bias
                    br_s[dst, cols] = (u_s[dst, cols] * mixed).astype(BF16)
            mrg_s[rows, :] = mrg_s[rows, :] * _dot(br_s[rows, :], wbsg)

        for_z_parts(spatial_gating)

    @pl.when(k == 4)
    def _():
        part = _part_rows(seq_len)
        tpos = lax.broadcasted_iota(jnp.int32, (part, POOL_GROUP_DIM), 0) & (seq_len - 1)

        def pool(rows, z):
            for gi, w in enumerate(POOL_WINDOWS):
                cols = slice(gi * POOL_GROUP_DIM, (gi + 1) * POOL_GROUP_DIM)
                pooled = _window_mean_minus_self(z[:, cols], tpos, seq_len, w)
                out = _dot(pooled.astype(BF16), poolw_ref[gi].astype(BF16)) * pscale_ref[:, cols]
                br_s[rows, cols] = out.astype(BF16)

        for_z_parts(pool, part)

    for step in (5, 6):

        @pl.when(k == step)
        def _(step=step):
            cols = slice((step - 5) * half, (step - 4) * half)
            wbpool = wbpool_ref[:, cols].astype(BF16)

            def gate(rows, z):
                mrg_s[rows, cols] = mrg_s[rows, cols] + _sigmoid(z) * _dot(br_s[rows, :], wbpool)

            for_z_parts(gate)

    @pl.when(k == 7)
    def _():
        cols = slice(0, half)

        def gate(rows, z):
            mrg_s[rows, cols] = mrg_s[rows, cols] + _sigmoid(z) * phg_ref[rows, cols]

        for_z_parts(gate)

    @pl.when(k == 8)
    def _():
        cols = slice(half, D_MODEL)
        wout = wout_ref[...].astype(BF16)

        def gate_and_project(rows, z):
            mrg_s[rows, cols] = mrg_s[rows, cols] + _sigmoid(z) * phg_ref[rows, cols]
            y = _dot(mrg_s[rows, :].astype(BF16), wout)
            o_ref[rows, :] = x_ref[rows, :] + mod_ref[0, 2:3, :] * y

        for_z_parts(gate_and_project)


def _mix_call(x, phg, mod, seq_len, layer, norm_mix, w_in, sg_norm, sg_w, sg_b, w_branch_sg, w_branch_pool, pool_w,
              pool_scale, w_out):
    n_tok = x.shape[0]
    nb = n_tok // ROWS
    per_seq_mod = mod.shape[0] > 1
    const = pl.Buffered(1)
    assert seq_len & (seq_len - 1) == 0 and ROWS % seq_len == 0 and seq_len % SG_CHUNK == 0
    in_specs = [
        pl.BlockSpec((ROWS, D_MODEL), lambda i, k: (i, 0)),
        pl.BlockSpec((1, N_MOD, D_MODEL), (lambda i, k: (i, 0, 0)) if per_seq_mod else (lambda i, k: (0, 0, 0))),
        pl.BlockSpec((None, 1, D_MODEL), lambda i, k: (layer, 0, 0)),
        pl.BlockSpec((None, D_MODEL, IN_CHUNK), lambda i, k: (layer, 0, _mix_col(k))),
        pl.BlockSpec((None, 1, SG_WIDTH), lambda i, k: (layer, 0, 0)),
        pl.BlockSpec((None, SG_GROUPS, SG_CHUNK, SG_CHUNK), lambda i, k: (layer, 0, 0, 0)),
        pl.BlockSpec((None, SG_CHUNK, SG_GROUPS), lambda i, k: (layer, 0, 0)),
        pl.BlockSpec((None, SG_WIDTH, D_MODEL), lambda i, k: (layer, 0, 0), pipeline_mode=const),
        pl.BlockSpec((None, POOL_WIDTH, D_MODEL), lambda i, k: (layer, 0, 0), pipeline_mode=const),
        pl.BlockSpec((None, len(POOL_WINDOWS), POOL_GROUP_DIM, POOL_GROUP_DIM), lambda i, k: (layer, 0, 0, 0)),
        pl.BlockSpec((None, 1, POOL_WIDTH), lambda i, k: (layer, 0, 0)),
        pl.BlockSpec((None, D_MODEL, D_MODEL), lambda i, k: (layer, 0, 0), pipeline_mode=const),
        pl.BlockSpec((ROWS, D_MODEL), lambda i, k: (i, 0)),
    ]
    args = [
        x, mod, norm_mix.reshape(DEPTH, 1, D_MODEL), w_in, sg_norm.reshape(DEPTH, 1, SG_WIDTH), sg_w,
        jnp.swapaxes(sg_b, 1, 2), w_branch_sg, w_branch_pool, pool_w, pool_scale.reshape(DEPTH, 1, POOL_WIDTH),
        w_out, phg,
    ]
    scratch = [
        pltpu.VMEM((ROWS, D_MODEL), BF16),
        pltpu.VMEM((ROWS, SG_WIDTH), F32),
        pltpu.VMEM((ROWS, SG_WIDTH), BF16),
        pltpu.VMEM((ROWS, D_MODEL), F32),
    ]
    return pl.pallas_call(
        functools.partial(_mix_kernel, seq_len=seq_len),
        grid=(nb, len(_MIX_ORDER)),
        in_specs=in_specs,
        out_specs=pl.BlockSpec((ROWS, D_MODEL), lambda i, k: (i, 0)),
        out_shape=jax.ShapeDtypeStruct((n_tok, D_MODEL), F32),
        scratch_shapes=scratch,
        compiler_params=pltpu.CompilerParams(
            dimension_semantics=("arbitrary", "arbitrary"), vmem_limit_bytes=56 * MIB
        ),
        name=f"mix_l{layer}_t{seq_len}",
    )(*args)


def _ffn_kernel(x_ref, mod_ref, nffn_ref, *refs, seq_len, final):
    sets = [refs[7 * s : 7 * s + 7] for s in range(FF_SETS)]
    fin_ref, o_ref, hb, acc = refs[7 * FF_SETS :]
    c = pl.program_id(1)
    n_chunks = D_FF // FF_CHUNK
    n_steps = pl.cdiv(n_chunks, FF_SETS)

    part = _part_rows(seq_len)
    tpos = lax.broadcasted_iota(jnp.int32, (part, FF_CHUNK), 0) & (seq_len - 1)
    has_prev = tpos >= 1
    has_next = tpos < seq_len - 1

    def conv(h, cw_ref, cb_ref):
        prev = jnp.where(has_prev, pltpu.roll(h, 1, 0), 0.0)
        nxt = jnp.where(has_next, pltpu.roll(h, part - 1, 0), 0.0)
        return prev * cw_ref[0:1, :] + h * cw_ref[1:2, :] + nxt * cw_ref[2:3, :] + cb_ref[...]

    def run(n_sets, first, last):
        ws = [(wa[...].astype(BF16), wb[...].astype(BF16), wd[...].astype(BF16)) for wa, wb, _, _, _, _, wd in sets]
        items = [(p, s) for p in range(ROWS // part) for s in range(n_sets)]

        def up(item):
            p, s = item
            rows = slice(p * part, (p + 1) * part)
            if first and s == 0:
                hb[rows, :] = _norm_mod(x_ref[rows, :], nffn_ref[...], mod_ref[0, 4:5, :], mod_ref[0, 3:4, :])
            h = hb[rows, :]
            return _dot(h, ws[s][0]), _dot(h, ws[s][1])

        pending = up(items[0])
        down = None
        for i, (p, s) in enumerate(items):
            ha, hb2 = pending
            if i + 1 < len(items):
                pending = up(items[i + 1])
            _, _, cwa_ref, cwb_ref, cba_ref, cbb_ref, _ = sets[s]
            a = conv(ha, cwa_ref, cba_ref)
            b = conv(hb2, cwb_ref, cbb_ref)
            d = _dot((_silu(a) * b).astype(BF16), ws[s][2])
            down = d if down is None else down + d
            if s == n_sets - 1:
                rows = slice(p * part, (p + 1) * part)
                total = down if first else acc[rows, :] + down
                if last:
                    y = x_ref[rows, :] + mod_ref[0, 5:6, :] * total
                    o_ref[rows, :] = _rms(y, fin_ref[...]) if final else y
                else:
                    acc[rows, :] = total
                down = None

    kinds = {}
    for step in range(n_steps):
        n_sets = len([s for s in range(FF_SETS) if step + s * n_steps < n_chunks])
        kinds.setdefault((n_sets, step == 0, step == n_steps - 1), []).append(step)
    for (n_sets, first, last), steps in kinds.items():
        cond = functools.reduce(lambda u, v: u | v, [c == st for st in steps])

        @pl.when(cond)
        def _(n_sets=n_sets, first=first, last=last):
            run(n_sets, first, last)


def _ffn_call(x, mod, seq_len, layer, final, norm_ffn, ffn_up, ffn_conv_w, ffn_conv_b, ffn_down, final_norm):
    n_tok = x.shape[0]
    nb = n_tok // ROWS
    nc = D_FF // FF_CHUNK
    n_steps = pl.cdiv(nc, FF_SETS)
    per_seq_mod = mod.shape[0] > 1
    conv_b = ffn_conv_b.reshape(DEPTH, 1, 2 * D_FF)
    in_specs = [
        pl.BlockSpec((ROWS, D_MODEL), lambda i, c: (i, 0)),
        pl.BlockSpec((1, N_MOD, D_MODEL), (lambda i, c: (i, 0, 0)) if per_seq_mod else (lambda i, c: (0, 0, 0))),
        pl.BlockSpec((None, 1, D_MODEL), lambda i, c: (layer, 0, 0)),
    ]
    args = [x, mod, norm_ffn.reshape(DEPTH, 1, D_MODEL)]
    for s in range(FF_SETS):
        chunk = lambda c, s=s: jnp.minimum(c + s * n_steps, nc - 1)
        in_specs += [
            pl.BlockSpec((None, D_MODEL, FF_CHUNK), lambda i, c, f=chunk: (layer, 0, f(c))),
            pl.BlockSpec((None, D_MODEL, FF_CHUNK), lambda i, c, f=chunk: (layer, 0, nc + f(c))),
            pl.BlockSpec((None, 3, FF_CHUNK), lambda i, c, f=chunk: (layer, 0, f(c))),
            pl.BlockSpec((None, 3, FF_CHUNK), lambda i, c, f=chunk: (layer, 0, nc + f(c))),
            pl.BlockSpec((None, 1, FF_CHUNK), lambda i, c, f=chunk: (layer, 0, f(c))),
            pl.BlockSpec((None, 1, FF_CHUNK), lambda i, c, f=chunk: (layer, 0, nc + f(c))),
            pl.BlockSpec((None, FF_CHUNK, D_MODEL), lambda i, c, f=chunk: (layer, f(c), 0)),
        ]
        args += [ffn_up, ffn_up, ffn_conv_w, ffn_conv_w, conv_b, conv_b, ffn_down]
    in_specs.append(pl.BlockSpec((1, D_MODEL), lambda i, c: (0, 0)))
    args.append(final_norm.reshape(1, D_MODEL))
    return pl.pallas_call(
        functools.partial(_ffn_kernel, seq_len=seq_len, final=final),
        grid=(nb, n_steps),
        in_specs=in_specs,
        out_specs=pl.BlockSpec((ROWS, D_MODEL), lambda i, c: (i, 0)),
        out_shape=jax.ShapeDtypeStruct((n_tok, D_MODEL), F32),
        scratch_shapes=[pltpu.VMEM((ROWS, D_MODEL), BF16), pltpu.VMEM((ROWS, D_MODEL), F32)],
        compiler_params=pltpu.CompilerParams(
            dimension_semantics=("arbitrary", "arbitrary"), vmem_limit_bytes=52 * MIB
        ),
        name=f"ffn_l{layer}_t{seq_len}",
    )(*args)


def kernel(x_prompt, x_sample, c, state_hgrn, c_ctx, norm_mix, norm_ffn, w_ada, b_ada, w_in, lb_logits, hg_norm,
           w_branch_hg, w_branch_sg, w_branch_pool, w_out, sg_norm, sg_w, sg_b, pool_w, pool_scale, ffn_up,
           ffn_conv_w, ffn_conv_b, ffn_down, final_norm):
    n_ctx, t_ctx, _ = x_prompt.shape
    n_lat, t_lat, _ = x_sample.shape

    n_cond = 1 + n_lat
    pad = -n_cond % V7X_SUBLANES
    cvec = jnp.concatenate([c_ctx[None, :], c, jnp.zeros((pad, D_MODEL), F32)], axis=0)
    mod = _mod_call(cvec, w_ada, b_ada).reshape(DEPTH, n_cond + pad, N_MOD, D_MODEL)

    xs = _addpos_call(x_sample, _grid_pos_embed(t_lat)).reshape(n_lat * t_lat, D_MODEL)
    xp = x_prompt.reshape(n_ctx * t_ctx, D_MODEL)
    state0 = state_hgrn.reshape(n_lat * DEPTH * 2 * HG_HEADS, HG_DK, HG_DV)

    states = None
    for layer in range(DEPTH):
        final = layer == DEPTH - 1
        groups = []
        for x, m, t, s0 in ((xp, mod[layer, 0:1], t_ctx, None), (xs, mod[layer, 1:n_cond], t_lat, state0)):
            phg, s_fin = _hgrn_call(x, m, t, layer, norm_mix, w_in, lb_logits, hg_norm, w_branch_hg, s0, states)
            x1 = _mix_call(x, phg, m, t, layer, norm_mix, w_in, sg_norm, sg_w, sg_b, w_branch_sg, w_branch_pool,
                           pool_w, pool_scale, w_out)
            x2 = _ffn_call(x1, m, t, layer, final, norm_ffn, ffn_up, ffn_conv_w, ffn_conv_b, ffn_down, final_norm)
            groups.append((x2, s_fin))
        (xp, states), (xs, _) = groups

    y_prompt = xp.reshape(x_prompt.shape)
    y_sample = xs.reshape(x_sample.shape)
    return (y_prompt, y_sample, states)
```

```python
import functools

import jax
import jax.numpy as jnp
import numpy as np
from jax import lax
from jax.experimental import pallas as pl
from jax.experimental.pallas import tpu as pltpu

D_MODEL = 1024
DEPTH = 2
GRID_W = 64
POS_BASE = 10000.0
EPS = 1e-6
HG_HEADS = 4
HG_DK = 128
HG_DV = 128
HG_WIDTH = HG_HEADS * HG_DV
SG_GROUPS = 4
SG_WIDTH = 512
SG_GROUP_DIM = SG_WIDTH // SG_GROUPS
SG_CHUNK = 128
POOL_WINDOWS = (2, 4, 8, 16)
POOL_WIDTH = 512
POOL_GROUP_DIM = POOL_WIDTH // len(POOL_WINDOWS)
IN_COLS = 5 * HG_WIDTH + 2 * SG_WIDTH + POOL_WIDTH + 3 * D_MODEL
D_FF = 2816
N_MOD = 6

V7X_LANES = 128
V7X_SUBLANES = 8
V7X_MXU_DIM = 256
MIB = 2**20

ROWS = 1024
PART_ROWS = 512
HG_BLOCK = 256
HG_LEVELS = 8
IN_CHUNK = 512
FF_CHUNK = V7X_MXU_DIM
FF_SETS = 3
MOD_CHUNK = 1536

NEG_LOG2E = -1.4426950408889634

F32 = jnp.float32
BF16 = jnp.bfloat16


def _dot(a, b):
    return lax.dot_general(a, b, (((1,), (0,)), ((), ())), preferred_element_type=F32)


def _dot_nt(a, b):
    return lax.dot_general(a, b, (((1,), (1,)), ((), ())), preferred_element_type=F32)


def _dot_tn(a, b):
    return lax.dot_general(a, b, (((0,), (0,)), ((), ())), preferred_element_type=F32)


def _sigmoid(x):
    return 0.5 * jnp.tanh(0.5 * x) + 0.5


def _silu(x):
    return x * _sigmoid(x)


_GELU_C1 = 0.7978845608028654
_GELU_C2 = _GELU_C1 * 0.044715


def _gelu_tanh(x):
    half_x = 0.5 * x
    return half_x + half_x * jnp.tanh(x * (_GELU_C1 + _GELU_C2 * (x * x)))


def _rms(x, gain):
    return x * lax.rsqrt(jnp.mean(x * x, axis=-1, keepdims=True) + EPS) * gain


def _log1pexp(y):
    return jnp.maximum(y, 0.0) + jnp.log(1.0 + jnp.exp(-jnp.abs(y)))


def _norm_mod(x, gain, scale, shift):
    return (_rms(x, gain) * (1.0 + scale) + shift).astype(BF16)


def _part_rows(seq_len):
    return max(seq_len, PART_ROWS)


def _mod_kernel(c_ref, w_ref, b_ref, o_ref):
    c = _silu(c_ref[...]).astype(BF16)
    o_ref[...] = _dot(c, w_ref[...].astype(BF16)) + b_ref[...]


def _mod_call(cvec, w_ada, b_ada):
    n_rows = cvec.shape[0]
    n_cols = N_MOD * D_MODEL
    return pl.pallas_call(
        _mod_kernel,
        grid=(DEPTH, n_cols // MOD_CHUNK),
        in_specs=[
            pl.BlockSpec((n_rows, D_MODEL), lambda l, n: (0, 0)),
            pl.BlockSpec((None, D_MODEL, MOD_CHUNK), lambda l, n: (l, 0, n)),
            pl.BlockSpec((None, 1, MOD_CHUNK), lambda l, n: (l, 0, n)),
        ],
        out_specs=pl.BlockSpec((None, n_rows, MOD_CHUNK), lambda l, n: (l, 0, n)),
        out_shape=jax.ShapeDtypeStruct((DEPTH, n_rows, n_cols), F32),
        compiler_params=pltpu.CompilerParams(
            dimension_semantics=("arbitrary", "arbitrary"), vmem_limit_bytes=32 * MIB
        ),
        name="adaln_mod",
    )(cvec, w_ada, b_ada.reshape(DEPTH, 1, n_cols))


def _addpos_kernel(x_ref, p_ref, o_ref):
    o_ref[...] = x_ref[...] + p_ref[...]


def _addpos_call(x, pos):
    b, t, d = x.shape
    return pl.pallas_call(
        _addpos_kernel,
        grid=(b,),
        in_specs=[pl.BlockSpec((None, t, d), lambda i: (i, 0, 0)), pl.BlockSpec((t, d), lambda i: (0, 0))],
        out_specs=pl.BlockSpec((None, t, d), lambda i: (i, 0, 0)),
        out_shape=jax.ShapeDtypeStruct(x.shape, x.dtype),
        compiler_params=pltpu.CompilerParams(dimension_semantics=("arbitrary",), vmem_limit_bytes=32 * MIB),
        name="add_pos",
    )(x, pos)


def _grid_pos_embed(n_tokens):
    rows = n_tokens // GRID_W
    r = np.broadcast_to(np.arange(rows, dtype=np.float32)[:, None], (rows, GRID_W)).reshape(-1)
    col = np.broadcast_to(np.arange(GRID_W, dtype=np.float32)[None, :], (rows, GRID_W)).reshape(-1)
    quarter = D_MODEL // 4
    omega = (1.0 / (np.float32(POS_BASE) ** (np.arange(quarter, dtype=np.float32) / quarter))).astype(np.float32)
    ar = r[:, None] * omega[None, :]
    ac = col[:, None] * omega[None, :]
    return jnp.asarray(np.concatenate([np.sin(ar), np.cos(ar), np.sin(ac), np.cos(ac)], axis=-1), F32)


def _ref_rows(b, blk, r):
    n, c = b.shape
    if blk >= V7X_SUBLANES:
        x3 = b.reshape(n // blk, blk, c)
        return jnp.broadcast_to(x3[:, r : r + 1, :], x3.shape).reshape(n, c)
    x3 = b.reshape(n // V7X_SUBLANES, V7X_SUBLANES, c)
    sub = lax.broadcasted_iota(jnp.int32, x3.shape, 1)
    bases = list(range(0, V7X_SUBLANES, blk))
    out = jnp.broadcast_to(x3[:, bases[-1] + r : bases[-1] + r + 1, :], x3.shape)
    for base in reversed(bases[:-1]):
        out = jnp.where(sub < base + blk, jnp.broadcast_to(x3[:, base + r : base + r + 1, :], x3.shape), out)
    return out.reshape(n, c)


def _cum_logdecay(lf, tri):
    hi = lf.astype(BF16)
    r1 = lf - hi.astype(F32)
    mid = r1.astype(BF16)
    lo = (r1 - mid.astype(F32)).astype(BF16)
    return _dot(tri, hi) + _dot(tri, mid) + _dot(tri, lo)


def _hgrn_block(q, k, v, b, lvq, forward, st):
    half = HG_BLOCK // 2
    lo, hi = slice(0, half), slice(half, HG_BLOCK)
    vb = v.astype(BF16)
    qb, kb = q.astype(BF16), k.astype(BF16)
    diag = [jnp.where(lvq == 0, _dot_nt(qb[h], kb[h]), 0.0) for h in (lo, hi)]
    for m in range(1, HG_LEVELS):
        blk = 2**m
        ref = _ref_rows(b, blk, blk // 2 - 1 if forward else blk // 2)
        e = jnp.exp2(jnp.abs(b - ref) * NEG_LOG2E).astype(BF16)
        qt, kt = qb * e, kb * e
        diag = [jnp.where(lvq == m, _dot_nt(qt[h], kt[h]), a) for h, a in zip((lo, hi), diag)]
    mid = half - 1 if forward else half
    e = jnp.exp2(jnp.abs(b - b[mid : mid + 1, :]) * NEG_LOG2E).astype(BF16)
    qt, kt = qb * e, kb * e
    a_lo, a_hi = (a.astype(BF16) for a in diag)
    if forward:
        cross = _dot_nt(qt[hi], kt[lo]).astype(BF16)
        o = jnp.concatenate([_dot(a_lo, vb[lo]), _dot(cross, vb[lo]) + _dot(a_hi, vb[hi])], axis=0)
    else:
        cross = _dot_nt(qt[lo], kt[hi]).astype(BF16)
        o = jnp.concatenate([_dot(a_lo, vb[lo]) + _dot(cross, vb[hi]), _dot(a_hi, vb[hi])], axis=0)
    edge = b[HG_BLOCK - 1 : HG_BLOCK, :] if forward else b[0:1, :]
    k_end = (k * jnp.exp(edge - b)).astype(BF16)
    st_new = _dot_tn(vb, k_end)
    if st is not None:
        o = o + _dot_nt((q * jnp.exp(b)).astype(BF16), st.astype(BF16))
        st_new = st_new + st * jnp.exp(edge)
    return o, st_new


def _hgrn_kernel(*refs, layer, seq_len):
    carry = seq_len > HG_BLOCK
    it = iter(refs)
    x_ref, mod_ref, nmix_ref = next(it), next(it), next(it)
    w_refs = [next(it) for _ in range(5)]
    lbl_ref, hgn_ref, wbr_ref = next(it), next(it), next(it)
    s0_refs = [next(it), next(it)] if carry else None
    sprev_ref = next(it) if (not carry and layer > 0) else None
    phg_ref = next(it)
    sout_ref = None if carry else next(it)
    hb, wcat, q_s, v_s, g_s, kf_s, kb_s, bf_s, bb_s, o_s, y_s, lvf_s, lvb_s, trif_s, trib_s, st_s = it

    j = pl.program_id(1)

    def head(first):
        if first:
            t = lax.broadcasted_iota(jnp.int32, (HG_BLOCK // 2, HG_BLOCK // 2), 0)
            s = lax.broadcasted_iota(jnp.int32, (HG_BLOCK // 2, HG_BLOCK // 2), 1)
            x = t ^ s
            lv = jnp.zeros_like(x)
            for m in range(HG_LEVELS - 1):
                lv = lv + (x >= 2**m).astype(jnp.int32)
            lvf_s[...] = jnp.where(t >= s, lv, -1)
            lvb_s[...] = jnp.where(t <= s, lv, -1)
            t = lax.broadcasted_iota(jnp.int32, (HG_BLOCK, HG_BLOCK), 0)
            s = lax.broadcasted_iota(jnp.int32, (HG_BLOCK, HG_BLOCK), 1)
            trif_s[...] = (t >= s).astype(BF16)
            trib_s[...] = (t <= s).astype(BF16)
        for g, w_ref in enumerate(w_refs):
            wcat[:, g * HG_DK : (g + 1) * HG_DK] = w_ref[...].astype(BF16)
        a0, a1 = lbl_ref[0], lbl_ref[1]
        amax = jnp.maximum(a0, a1)
        e0, e1 = jnp.exp(a0 - amax), jnp.exp(a1 - amax)
        p0, p1 = e0 / (e0 + e1), e1 / (e0 + e1)
        lb = (p0 - p0) if layer == 0 else ((p0 + p1) - p0)
        log_lb = jnp.log(lb)

        wc = wcat[...]
        part = PART_ROWS
        n_parts = ROWS // part

        def zdot(p):
            rows = slice(p * part, (p + 1) * part)
            if first:
                hb[rows, :] = _norm_mod(x_ref[rows, :], nmix_ref[...], mod_ref[0, 1:2, :], mod_ref[0, 0:1, :])
            return _dot(hb[rows, :], wc)

        pending = zdot(0)
        for p in range(n_parts):
            z = pending
            if p + 1 < n_parts:
                pending = zdot(p + 1)
            rows = slice(p * part, (p + 1) * part)
            zq, zff, zfb, zi, zg = (z[:, g * HG_DK : (g + 1) * HG_DK] for g in range(5))
            q_s[rows, :] = _silu(zq) * HG_DK**-0.5
            v_s[rows, :] = zi
            g_s[rows, :] = _silu(zg)
            for d, (zf, k_s, b_s, tri_s) in enumerate(((zff, kf_s, bf_s, trif_s), (zfb, kb_s, bb_s, trib_s))):
                lf = _log1pexp(log_lb[d : d + 1, :] - zf) - _log1pexp(-zf)
                k_s[rows, :] = (1.0 - lb[d : d + 1, :]) * _sigmoid(-zf)
                tri = tri_s[...]
                for n in range(part // HG_BLOCK):
                    loc = slice(n * HG_BLOCK, (n + 1) * HG_BLOCK)
                    dst = slice(rows.start + n * HG_BLOCK, rows.start + (n + 1) * HG_BLOCK)
                    b_s[dst, :] = _cum_logdecay(lf[loc, :], tri)

        o_s[...] = jnp.zeros_like(o_s)
        n_blk = ROWS // HG_BLOCK
        if sprev_ref is not None:
            sout_ref[:, 0:layer] = sprev_ref[...]
        if carry:
            for d in range(2):
                st_s[d] = s0_refs[d][0].T

        def blocks(n, c):
            for d in range(2):
                forward = d == 0
                blk = n if forward else n_blk - 1 - n
                rows = pl.ds(pl.multiple_of(blk * HG_BLOCK, HG_BLOCK), HG_BLOCK)
                k_s, b_s, lv_s = (kf_s, bf_s, lvf_s) if forward else (kb_s, bb_s, lvb_s)
                o, st_new = _hgrn_block(
                    q_s[rows, :], k_s[rows, :], v_s[rows, :], b_s[rows, :], lv_s[...], forward, st_s[d] if carry else None
                )
                o_s[rows, :] += o
                if carry:
                    st_s[d] = st_new
                else:
                    sout_ref[blk, layer, d, 0] = st_new.T
            return c

        lax.fori_loop(0, n_blk, blocks, 0, unroll=True)

        y_s[j] = (_rms(o_s[...], hgn_ref[...]) * g_s[...]).astype(BF16)

    @pl.when(j == 0)
    def _():
        head(True)

    @pl.when(j > 0)
    def _():
        head(False)

    @pl.when(j == HG_HEADS - 1)
    def _():
        y = jnp.concatenate([y_s[h] for h in range(HG_HEADS)], axis=1)
        phg_ref[...] = _dot(y, wbr_ref[...].astype(BF16))


def _hgrn_call(x, mod, seq_len, layer, norm_mix, w_in, lb_logits, hg_norm, w_branch_hg, state0, prev_states):
    n_tok = x.shape[0]
    nb = n_tok // ROWS
    carry = seq_len > HG_BLOCK
    per_seq_mod = mod.shape[0] > 1

    in_specs = [
        pl.BlockSpec((ROWS, D_MODEL), lambda i, j: (i, 0)),
        pl.BlockSpec((1, N_MOD, D_MODEL), (lambda i, j: (i, 0, 0)) if per_seq_mod else (lambda i, j: (0, 0, 0))),
        pl.BlockSpec((None, 1, D_MODEL), lambda i, j: (layer, 0, 0)),
    ]
    args = [x, mod, norm_mix.reshape(DEPTH, 1, D_MODEL)]
    for g in range(5):
        in_specs.append(pl.BlockSpec((None, D_MODEL, HG_DK), lambda i, j, g=g: (layer, 0, g * HG_HEADS + j)))
        args.append(w_in)
    in_specs += [
        pl.BlockSpec((DEPTH, 2, HG_DK), lambda i, j: (0, 0, j)),
        pl.BlockSpec((None, 1, HG_DV), lambda i, j: (layer, 0, 0)),
        pl.BlockSpec((None, HG_WIDTH, D_MODEL), lambda i, j: (layer, 0, 0), pipeline_mode=pl.Buffered(1)),
    ]
    args += [lb_logits, hg_norm.reshape(DEPTH, 1, HG_DV), w_branch_hg]
    if carry:
        assert seq_len == ROWS
        for d in range(2):
            in_specs.append(
                pl.BlockSpec((1, HG_DK, HG_DV), lambda i, j, d=d: (((i * DEPTH + layer) * 2 + d) * HG_HEADS + j, 0, 0))
            )
            args.append(state0)

    out_shape = [jax.ShapeDtypeStruct((n_tok, D_MODEL), F32)]
    out_specs = [pl.BlockSpec((ROWS, D_MODEL), lambda i, j: (i, 0))]
    if not carry:
        assert seq_len == HG_BLOCK
        n_seq = n_tok // seq_len
        seqs = ROWS // seq_len
        if layer > 0:
            in_specs.append(pl.BlockSpec((seqs, layer, 2, 1, HG_DK, HG_DV), lambda i, j: (i, 0, 0, j, 0, 0)))
            args.append(prev_states)
        out_shape.append(jax.ShapeDtypeStruct((n_seq, layer + 1, 2, HG_HEADS, HG_DK, HG_DV), F32))
        out_specs.append(pl.BlockSpec((seqs, layer + 1, 2, 1, HG_DK, HG_DV), lambda i, j: (i, 0, 0, j, 0, 0)))

    head = lambda dt=F32: pltpu.VMEM((ROWS, HG_DK), dt)
    scratch = [
        pltpu.VMEM((ROWS, D_MODEL), BF16),
        pltpu.VMEM((D_MODEL, 5 * HG_DK), BF16),
        head(), head(), head(),
        head(), head(), head(), head(),
        head(),
        pltpu.VMEM((HG_HEADS, ROWS, HG_DV), BF16),
        pltpu.VMEM((HG_BLOCK // 2, HG_BLOCK // 2), jnp.int32),
        pltpu.VMEM((HG_BLOCK // 2, HG_BLOCK // 2), jnp.int32),
        pltpu.VMEM((HG_BLOCK, HG_BLOCK), BF16),
        pltpu.VMEM((HG_BLOCK, HG_BLOCK), BF16),
        pltpu.VMEM((2, HG_DV, HG_DK), F32),
    ]
    outs = pl.pallas_call(
        functools.partial(_hgrn_kernel, layer=layer, seq_len=seq_len),
        grid=(nb, HG_HEADS),
        in_specs=in_specs,
        out_specs=out_specs,
        out_shape=out_shape,
        scratch_shapes=scratch,
        compiler_params=pltpu.CompilerParams(
            dimension_semantics=("arbitrary", "arbitrary"), vmem_limit_bytes=48 * MIB
        ),
        name=f"hgrn_l{layer}_t{seq_len}",
    )(*args)
    return (outs[0], None) if carry else (outs[0], outs[1])


_MIX_ORDER = (10, 11, 5, 6, 7, 12, 13, 8, 9)


def _mix_col(k):
    idx = 0
    for n, c in enumerate(_MIX_ORDER):
        idx = idx + jnp.where(k == n, c, 0)
    return idx


def _window_mean_minus_self(p, tpos, seq_len, w):
    n = p.shape[0]
    half = w // 2

    def shifted(x, j):
        valid = (tpos + j >= 0) & (tpos + j < seq_len)
        return jnp.where(valid, pltpu.roll(x, (-j) % n, 0), 0.0)

    ahead, behind, length = p, p, 1
    while length < half:
        ahead = ahead + shifted(ahead, length)
        behind = behind + shifted(behind, -length)
        length *= 2
    acc = ahead + shifted(behind, -1)
    cnt = jnp.minimum(tpos + half, seq_len) - jnp.maximum(tpos - half, 0)
    return acc / cnt.astype(F32) - p


def _mix_kernel(
    x_ref, mod_ref, nmix_ref, w_ref, sgn_ref, sgw_ref, sgb_ref, wbsg_ref, wbpool_ref, poolw_ref, pscale_ref,
    wout_ref, phg_ref, o_ref, hb, u_s, br_s, mrg_s, *, seq_len,
):
    k = pl.program_id(1)
    half = D_MODEL // 2

    @pl.when(k == 0)
    def _():
        hb[...] = _norm_mod(x_ref[...], nmix_ref[...], mod_ref[0, 1:2, :], mod_ref[0, 0:1, :])

    def for_z_parts(consume, part=PART_ROWS):
        w = w_ref[...].astype(BF16)
        n_parts = ROWS // part

        def zdot(p):
            return _dot(hb[p * part : (p + 1) * part, :], w)

        pending = zdot(0)
        for p in range(n_parts):
            z = pending
            if p + 1 < n_parts:
                pending = zdot(p + 1)
            consume(slice(p * part, (p + 1) * part), z)

    for step in (0, 1):

        @pl.when(k == step)
        def _(step=step):
            cols = slice(step * half, (step + 1) * half)

            def gate(rows, z):
                mrg_s[rows, cols] = _sigmoid(z)

            for_z_parts(gate)

    @pl.when(k == 2)
    def _():
        def store_u(rows, z):
            u_s[rows, :] = _gelu_tanh(z)

        for_z_parts(store_u)

    @pl.when(k == 3)
    def _():
        wbsg = wbsg_ref[...].astype(BF16)
        wgs = [sgw_ref[g].astype(BF16) for g in range(SG_GROUPS)]

        def spatial_gating(rows, z):
            v = _rms(_gelu_tanh(z), sgn_ref[...]).astype(BF16)
            for g in range(SG_GROUPS):
                bias = sgb_ref[:, g : g + 1]
                cols = slice(g * SG_GROUP_DIM, (g + 1) * SG_GROUP_DIM)
                for n in range((rows.stop - rows.start) // SG_CHUNK):
                    loc = slice(n * SG_CHUNK, (n + 1) * SG_CHUNK)
                    dst = slice(rows.start + n * SG_CHUNK, rows.start + (n + 1) * SG_CHUNK)
                    mixed = _dot(wgs[g], v[loc, cols]) + bias
                    br_s[dst, cols] = (u_s[dst, cols] * mixed).astype(BF16)
            mrg_s[rows, :] = mrg_s[rows, :] * _dot(br_s[rows, :], wbsg)

        for_z_parts(spatial_gating)

    @pl.when(k == 4)
    def _():
        part = _part_rows(seq_len)
        tpos = lax.broadcasted_iota(jnp.int32, (part, POOL_GROUP_DIM), 0) & (seq_len - 1)

        def pool(rows, z):
            for gi, w in enumerate(POOL_WINDOWS):
                cols = slice(gi * POOL_GROUP_DIM, (gi + 1) * POOL_GROUP_DIM)
                pooled = _window_mean_minus_self(z[:, cols], tpos, seq_len, w)
                out = _dot(pooled.astype(BF16), poolw_ref[gi].astype(BF16)) * pscale_ref[:, cols]
                br_s[rows, cols] = out.astype(BF16)

        for_z_parts(pool, part)

    for step in (5, 6):

        @pl.when(k == step)
        def _(step=step):
            cols = slice((step - 5) * half, (step - 4) * half)
            wbpool = wbpool_ref[:, cols].astype(BF16)

            def gate(rows, z):
                mrg_s[rows, cols] = mrg_s[rows, cols] + _sigmoid(z) * _dot(br_s[rows, :], wbpool)

            for_z_parts(gate)

    @pl.when(k == 7)
    def _():
        cols = slice(0, half)

        def gate(rows, z):
            mrg_s[rows, cols] = mrg_s[rows, cols] + _sigmoid(z) * phg_ref[rows, cols]

        for_z_parts(gate)

    @pl.when(k == 8)
    def _():
        cols = slice(half, D_MODEL)
        wout = wout_ref[...].astype(BF16)

        def gate_and_project(rows, z):
            mrg_s[rows, cols] = mrg_s[rows, cols] + _sigmoid(z) * phg_ref[rows, cols]
            y = _dot(mrg_s[rows, :].astype(BF16), wout)
            o_ref[rows, :] = x_ref[rows, :] + mod_ref[0, 2:3, :] * y

        for_z_parts(gate_and_project)


def _mix_call(x, phg, mod, seq_len, layer, norm_mix, w_in, sg_norm, sg_w, sg_b, w_branch_sg, w_branch_pool, pool_w,
              pool_scale, w_out):
    n_tok = x.shape[0]
    nb = n_tok // ROWS
    per_seq_mod = mod.shape[0] > 1
    const = pl.Buffered(1)
    assert seq_len & (seq_len - 1) == 0 and ROWS % seq_len == 0 and seq_len % SG_CHUNK == 0
    in_specs = [
        pl.BlockSpec((ROWS, D_MODEL), lambda i, k: (i, 0)),
        pl.BlockSpec((1, N_MOD, D_MODEL), (lambda i, k: (i, 0, 0)) if per_seq_mod else (lambda i, k: (0, 0, 0))),
        pl.BlockSpec((None, 1, D_MODEL), lambda i, k: (layer, 0, 0)),
        pl.BlockSpec((None, D_MODEL, IN_CHUNK), lambda i, k: (layer, 0, _mix_col(k))),
        pl.BlockSpec((None, 1, SG_WIDTH), lambda i, k: (layer, 0, 0)),
        pl.BlockSpec((None, SG_GROUPS, SG_CHUNK, SG_CHUNK), lambda i, k: (layer, 0, 0, 0)),
        pl.BlockSpec((None, SG_CHUNK, SG_GROUPS), lambda i, k: (layer, 0, 0)),
        pl.BlockSpec((None, SG_WIDTH, D_MODEL), lambda i, k: (layer, 0, 0), pipeline_mode=const),
        pl.BlockSpec((None, POOL_WIDTH, D_MODEL), lambda i, k: (layer, 0, 0), pipeline_mode=const),
        pl.BlockSpec((None, len(POOL_WINDOWS), POOL_GROUP_DIM, POOL_GROUP_DIM), lambda i, k: (layer, 0, 0, 0)),
        pl.BlockSpec((None, 1, POOL_WIDTH), lambda i, k: (layer, 0, 0)),
        pl.BlockSpec((None, D_MODEL, D_MODEL), lambda i, k: (layer, 0, 0), pipeline_mode=const),
        pl.BlockSpec((ROWS, D_MODEL), lambda i, k: (i, 0)),
    ]
    args = [
        x, mod, norm_mix.reshape(DEPTH, 1, D_MODEL), w_in, sg_norm.reshape(DEPTH, 1, SG_WIDTH), sg_w,
        jnp.swapaxes(sg_b, 1, 2), w_branch_sg, w_branch_pool, pool_w, pool_scale.reshape(DEPTH, 1, POOL_WIDTH),
        w_out, phg,
    ]
    scratch = [
        pltpu.VMEM((ROWS, D_MODEL), BF16),
        pltpu.VMEM((ROWS, SG_WIDTH), F32),
        pltpu.VMEM((ROWS, SG_WIDTH), BF16),
        pltpu.VMEM((ROWS, D_MODEL), F32),
    ]
    return pl.pallas_call(
        functools.partial(_mix_kernel, seq_len=seq_len),
        grid=(nb, len(_MIX_ORDER)),
        in_specs=in_specs,
        out_specs=pl.BlockSpec((ROWS, D_MODEL), lambda i, k: (i, 0)),
        out_shape=jax.ShapeDtypeStruct((n_tok, D_MODEL), F32),
        scratch_shapes=scratch,
        compiler_params=pltpu.CompilerParams(
            dimension_semantics=("arbitrary", "arbitrary"), vmem_limit_bytes=56 * MIB
        ),
        name=f"mix_l{layer}_t{seq_len}",
    )(*args)


def _ffn_kernel(x_ref, mod_ref, nffn_ref, *refs, seq_len, final):
    sets = [refs[7 * s : 7 * s + 7] for s in range(FF_SETS)]
    fin_ref, o_ref, hb, acc = refs[7 * FF_SETS :]
    c = pl.program_id(1)
    n_chunks = D_FF // FF_CHUNK
    n_steps = pl.cdiv(n_chunks, FF_SETS)

    part = ROWS
    tpos = lax.broadcasted_iota(jnp.int32, (part, FF_CHUNK), 0) & (seq_len - 1)
    has_prev = tpos >= 1
    has_next = tpos < seq_len - 1

    def conv(h, cw_ref, cb_ref):
        prev = jnp.where(has_prev, pltpu.roll(h, 1, 0), 0.0)
        nxt = jnp.where(has_next, pltpu.roll(h, part - 1, 0), 0.0)
        return prev * cw_ref[0:1, :] + h * cw_ref[1:2, :] + nxt * cw_ref[2:3, :] + cb_ref[...]

    def run(n_sets, first, last):
        ws = [(wa[...].astype(BF16), wb[...].astype(BF16), wd[...].astype(BF16)) for wa, wb, _, _, _, _, wd in sets]
        items = [(p, s) for p in range(ROWS // part) for s in range(n_sets)]

        def up(item):
            p, s = item
            rows = slice(p * part, (p + 1) * part)
            if first and s == 0:
                hb[rows, :] = _norm_mod(x_ref[rows, :], nffn_ref[...], mod_ref[0, 4:5, :], mod_ref[0, 3:4, :])
            h = hb[rows, :]
            return _dot(h, ws[s][0]), _dot(h, ws[s][1])

        pending = up(items[0])
        down = None
        for i, (p, s) in enumerate(items):
            ha, hb2 = pending
            if i + 1 < len(items):
                pending = up(items[i + 1])
            _, _, cwa_ref, cwb_ref, cba_ref, cbb_ref, _ = sets[s]
            a = conv(ha, cwa_ref, cba_ref)
            b = conv(hb2, cwb_ref, cbb_ref)
            d = _dot((_silu(a) * b).astype(BF16), ws[s][2])
            down = d if down is None else down + d
            if s == n_sets - 1:
                rows = slice(p * part, (p + 1) * part)
                total = down if first else acc[rows, :] + down
                if last:
                    y = x_ref[rows, :] + mod_ref[0, 5:6, :] * total
                    o_ref[rows, :] = _rms(y, fin_ref[...]) if final else y
                else:
                    acc[rows, :] = total
                down = None

    kinds = {}
    for step in range(n_steps):
        n_sets = len([s for s in range(FF_SETS) if step + s * n_steps < n_chunks])
        kinds.setdefault((n_sets, step == 0, step == n_steps - 1), []).append(step)
    for (n_sets, first, last), steps in kinds.items():
        cond = functools.reduce(lambda u, v: u | v, [c == st for st in steps])

        @pl.when(cond)
        def _(n_sets=n_sets, first=first, last=last):
            run(n_sets, first, last)


def _ffn_call(x, mod, seq_len, layer, final, norm_ffn, ffn_up, ffn_conv_w, ffn_conv_b, ffn_down, final_norm):
    n_tok = x.shape[0]
    nb = n_tok // ROWS
    nc = D_FF // FF_CHUNK
    n_steps = pl.cdiv(nc, FF_SETS)
    per_seq_mod = mod.shape[0] > 1
    conv_b = ffn_conv_b.reshape(DEPTH, 1, 2 * D_FF)
    in_specs = [
        pl.BlockSpec((ROWS, D_MODEL), lambda i, c: (i, 0)),
        pl.BlockSpec((1, N_MOD, D_MODEL), (lambda i, c: (i, 0, 0)) if per_seq_mod else (lambda i, c: (0, 0, 0))),
        pl.BlockSpec((None, 1, D_MODEL), lambda i, c: (layer, 0, 0)),
    ]
    args = [x, mod, norm_ffn.reshape(DEPTH, 1, D_MODEL)]
    for s in range(FF_SETS):
        chunk = lambda c, s=s: jnp.minimum(c + s * n_steps, nc - 1)
        in_specs += [
            pl.BlockSpec((None, D_MODEL, FF_CHUNK), lambda i, c, f=chunk: (layer, 0, f(c))),
            pl.BlockSpec((None, D_MODEL, FF_CHUNK), lambda i, c, f=chunk: (layer, 0, nc + f(c))),
            pl.BlockSpec((None, 3, FF_CHUNK), lambda i, c, f=chunk: (layer, 0, f(c))),
            pl.BlockSpec((None, 3, FF_CHUNK), lambda i, c, f=chunk: (layer, 0, nc + f(c))),
            pl.BlockSpec((None, 1, FF_CHUNK), lambda i, c, f=chunk: (layer, 0, f(c))),
            pl.BlockSpec((None, 1, FF_CHUNK), lambda i, c, f=chunk: (layer, 0, nc + f(c))),
            pl.BlockSpec((None, FF_CHUNK, D_MODEL), lambda i, c, f=chunk: (layer, f(c), 0)),
        ]
        args += [ffn_up, ffn_up, ffn_conv_w, ffn_conv_w, conv_b, conv_b, ffn_down]
    in_specs.append(pl.BlockSpec((1, D_MODEL), lambda i, c: (0, 0)))
    args.append(final_norm.reshape(1, D_MODEL))
    return pl.pallas_call(
        functools.partial(_ffn_kernel, seq_len=seq_len, final=final),
        grid=(nb, n_steps),
        in_specs=in_specs,
        out_specs=pl.BlockSpec((ROWS, D_MODEL), lambda i, c: (i, 0)),
        out_shape=jax.ShapeDtypeStruct((n_tok, D_MODEL), F32),
        scratch_shapes=[pltpu.VMEM((ROWS, D_MODEL), BF16), pltpu.VMEM((ROWS, D_MODEL), F32)],
        compiler_params=pltpu.CompilerParams(
            dimension_semantics=("arbitrary", "arbitrary"), vmem_limit_bytes=52 * MIB
        ),
        name=f"ffn_l{layer}_t{seq_len}",
    )(*args)


def kernel(x_prompt, x_sample, c, state_hgrn, c_ctx, norm_mix, norm_ffn, w_ada, b_ada, w_in, lb_logits, hg_norm,
           w_branch_hg, w_branch_sg, w_branch_pool, w_out, sg_norm, sg_w, sg_b, pool_w, pool_scale, ffn_up,
           ffn_conv_w, ffn_conv_b, ffn_down, final_norm):
    n_ctx, t_ctx, _ = x_prompt.shape
    n_lat, t_lat, _ = x_sample.shape

    n_cond = 1 + n_lat
    pad = -n_cond % V7X_SUBLANES
    cvec = jnp.concatenate([c_ctx[None, :], c, jnp.zeros((pad, D_MODEL), F32)], axis=0)
    mod = _mod_call(cvec, w_ada, b_ada).reshape(DEPTH, n_cond + pad, N_MOD, D_MODEL)

    xs = _addpos_call(x_sample, _grid_pos_embed(t_lat)).reshape(n_lat * t_lat, D_MODEL)
    xp = x_prompt.reshape(n_ctx * t_ctx, D_MODEL)
    state0 = state_hgrn.reshape(n_lat * DEPTH * 2 * HG_HEADS, HG_DK, HG_DV)

    states = None
    for layer in range(DEPTH):
        final = layer == DEPTH - 1
        groups = []
        for x, m, t, s0 in ((xp, mod[layer, 0:1], t_ctx, None), (xs, mod[layer, 1:n_cond], t_lat, state0)):
            phg, s_fin = _hgrn_call(x, m, t, layer, norm_mix, w_in, lb_logits, hg_norm, w_branch_hg, s0, states)
            x1 = _mix_call(x, phg, m, t, layer, norm_mix, w_in, sg_norm, sg_w, sg_b, w_branch_sg, w_branch_pool,
                           pool_w, pool_scale, w_out)
            x2 = _ffn_call(x1, m, t, layer, final, norm_ffn, ffn_up, ffn_conv_w, ffn_conv_b, ffn_down, final_norm)
            groups.append((x2, s_fin))
        (xp, states), (xs, _) = groups

    y_prompt = xp.reshape(x_prompt.shape)
    y_sample = xs.reshape(x_sample.shape)
    return (y_prompt, y_sample, states)
```

```python
import functools

import jax
import jax.numpy as jnp
import numpy as np
from jax import lax
from jax.experimental import pallas as pl
from jax.experimental.pallas import tpu as pltpu

D_MODEL = 1024
DEPTH = 2
GRID_W = 64
POS_BASE = 10000.0
EPS = 1e-6
HG_HEADS = 4
HG_DK = 128
HG_DV = 128
HG_WIDTH = HG_HEADS * HG_DV
SG_GROUPS = 4
SG_WIDTH = 512
SG_GROUP_DIM = SG_WIDTH // SG_GROUPS
SG_CHUNK = 128
POOL_WINDOWS = (2, 4, 8, 16)
POOL_WIDTH = 512
POOL_GROUP_DIM = POOL_WIDTH // len(POOL_WINDOWS)
IN_COLS = 5 * HG_WIDTH + 2 * SG_WIDTH + POOL_WIDTH + 3 * D_MODEL
D_FF = 2816
N_MOD = 6

V7X_LANES = 128
V7X_SUBLANES = 8
V7X_MXU_DIM = 256
MIB = 2**20

ROWS = 1024
PART_ROWS = 512
HG_BLOCK = 256
HG_LEVELS = 8
IN_CHUNK = 512
FF_CHUNK = V7X_MXU_DIM
FF_SETS = 3
MOD_CHUNK = 1536
VMEM_SMALL = 32 * MIB
VMEM_HGRN = 48 * MIB
VMEM_MIX = 56 * MIB
VMEM_FFN = 52 * MIB

NEG_LOG2E = -1.4426950408889634

F32 = jnp.float32
BF16 = jnp.bfloat16


def _dot(a, b):
    return lax.dot_general(a, b, (((1,), (0,)), ((), ())), preferred_element_type=F32)


def _dot_nt(a, b):
    return lax.dot_general(a, b, (((1,), (1,)), ((), ())), preferred_element_type=F32)


def _dot_tn(a, b):
    return lax.dot_general(a, b, (((0,), (0,)), ((), ())), preferred_element_type=F32)


def _sigmoid(x):
    return 0.5 * jnp.tanh(0.5 * x) + 0.5


def _silu(x):
    return x * _sigmoid(x)


_GELU_C1 = 0.7978845608028654
_GELU_C2 = _GELU_C1 * 0.044715


def _gelu_tanh(x):
    half_x = 0.5 * x
    return half_x + half_x * jnp.tanh(x * (_GELU_C1 + _GELU_C2 * (x * x)))


def _rms(x, gain):
    return x * lax.rsqrt(jnp.mean(x * x, axis=-1, keepdims=True) + EPS) * gain


def _log1pexp(y):
    return jnp.maximum(y, 0.0) + jnp.log(1.0 + jnp.exp(-jnp.abs(y)))


def _norm_mod(x, gain, scale, shift):
    return (_rms(x, gain) * (1.0 + scale) + shift).astype(BF16)


def _part_rows(seq_len):
    return max(seq_len, PART_ROWS)


def _mod_kernel(c_ref, w_ref, b_ref, o_ref):
    c = _silu(c_ref[...]).astype(BF16)
    o_ref[...] = _dot(c, w_ref[...].astype(BF16)) + b_ref[...]


def _mod_call(cvec, w_ada, b_ada):
    n_rows = cvec.shape[0]
    n_cols = N_MOD * D_MODEL
    return pl.pallas_call(
        _mod_kernel,
        grid=(DEPTH, n_cols // MOD_CHUNK),
        in_specs=[
            pl.BlockSpec((n_rows, D_MODEL), lambda l, n: (0, 0)),
            pl.BlockSpec((None, D_MODEL, MOD_CHUNK), lambda l, n: (l, 0, n)),
            pl.BlockSpec((None, 1, MOD_CHUNK), lambda l, n: (l, 0, n)),
        ],
        out_specs=pl.BlockSpec((None, n_rows, MOD_CHUNK), lambda l, n: (l, 0, n)),
        out_shape=jax.ShapeDtypeStruct((DEPTH, n_rows, n_cols), F32),
        compiler_params=pltpu.CompilerParams(
            dimension_semantics=("arbitrary", "arbitrary"), vmem_limit_bytes=VMEM_SMALL
        ),
        name="adaln_mod",
    )(cvec, w_ada, b_ada.reshape(DEPTH, 1, n_cols))


def _addpos_kernel(x_ref, p_ref, o_ref):
    o_ref[...] = x_ref[...] + p_ref[...]


def _addpos_call(x, pos):
    b, t, d = x.shape
    return pl.pallas_call(
        _addpos_kernel,
        grid=(b,),
        in_specs=[pl.BlockSpec((None, t, d), lambda i: (i, 0, 0)), pl.BlockSpec((t, d), lambda i: (0, 0))],
        out_specs=pl.BlockSpec((None, t, d), lambda i: (i, 0, 0)),
        out_shape=jax.ShapeDtypeStruct(x.shape, x.dtype),
        compiler_params=pltpu.CompilerParams(dimension_semantics=("arbitrary",), vmem_limit_bytes=VMEM_SMALL),
        name="add_pos",
    )(x, pos)


def _grid_pos_embed(n_tokens):
    rows = n_tokens // GRID_W
    r = np.broadcast_to(np.arange(rows, dtype=np.float32)[:, None], (rows, GRID_W)).reshape(-1)
    col = np.broadcast_to(np.arange(GRID_W, dtype=np.float32)[None, :], (rows, GRID_W)).reshape(-1)
    quarter = D_MODEL // 4
    omega = (1.0 / (np.float32(POS_BASE) ** (np.arange(quarter, dtype=np.float32) / quarter))).astype(np.float32)
    ar = r[:, None] * omega[None, :]
    ac = col[:, None] * omega[None, :]
    return jnp.asarray(np.concatenate([np.sin(ar), np.cos(ar), np.sin(ac), np.cos(ac)], axis=-1), F32)


def _ref_rows(b, blk, r):
    n, c = b.shape
    if blk >= V7X_SUBLANES:
        x3 = b.reshape(n // blk, blk, c)
        return jnp.broadcast_to(x3[:, r : r + 1, :], x3.shape).reshape(n, c)
    x3 = b.reshape(n // V7X_SUBLANES, V7X_SUBLANES, c)
    sub = lax.broadcasted_iota(jnp.int32, x3.shape, 1)
    bases = list(range(0, V7X_SUBLANES, blk))
    out = jnp.broadcast_to(x3[:, bases[-1] + r : bases[-1] + r + 1, :], x3.shape)
    for base in reversed(bases[:-1]):
        out = jnp.where(sub < base + blk, jnp.broadcast_to(x3[:, base + r : base + r + 1, :], x3.shape), out)
    return out.reshape(n, c)


def _cum_logdecay(lf, tri):
    hi = lf.astype(BF16)
    r1 = lf - hi.astype(F32)
    mid = r1.astype(BF16)
    lo = (r1 - mid.astype(F32)).astype(BF16)
    return _dot(tri, hi) + _dot(tri, mid) + _dot(tri, lo)


def _hgrn_block(q, k, v, b, lvq, forward, st):
    half = HG_BLOCK // 2
    lo, hi = slice(0, half), slice(half, HG_BLOCK)
    vb = v.astype(BF16)
    qb, kb = q.astype(BF16), k.astype(BF16)
    diag = [jnp.where(lvq == 0, _dot_nt(qb[h], kb[h]), 0.0) for h in (lo, hi)]
    for m in range(1, HG_LEVELS):
        blk = 2**m
        ref = _ref_rows(b, blk, blk // 2 - 1 if forward else blk // 2)
        e = jnp.exp2(jnp.abs(b - ref) * NEG_LOG2E).astype(BF16)
        qt, kt = qb * e, kb * e
        diag = [jnp.where(lvq == m, _dot_nt(qt[h], kt[h]), a) for h, a in zip((lo, hi), diag)]
    mid = half - 1 if forward else half
    e = jnp.exp2(jnp.abs(b - b[mid : mid + 1, :]) * NEG_LOG2E).astype(BF16)
    qt, kt = qb * e, kb * e
    a_lo, a_hi = (a.astype(BF16) for a in diag)
    if forward:
        cross = _dot_nt(qt[hi], kt[lo]).astype(BF16)
        o = jnp.concatenate([_dot(a_lo, vb[lo]), _dot(cross, vb[lo]) + _dot(a_hi, vb[hi])], axis=0)
    else:
        cross = _dot_nt(qt[lo], kt[hi]).astype(BF16)
        o = jnp.concatenate([_dot(a_lo, vb[lo]) + _dot(cross, vb[hi]), _dot(a_hi, vb[hi])], axis=0)
    edge = b[HG_BLOCK - 1 : HG_BLOCK, :] if forward else b[0:1, :]
    k_end = (k * jnp.exp(edge - b)).astype(BF16)
    st_new = _dot_tn(vb, k_end)
    if st is not None:
        o = o + _dot_nt((q * jnp.exp(b)).astype(BF16), st.astype(BF16))
        st_new = st_new + st * jnp.exp(edge)
    return o, st_new


def _hgrn_kernel(*refs, layer, seq_len):
    carry = seq_len > HG_BLOCK
    it = iter(refs)
    x_ref, mod_ref, nmix_ref = next(it), next(it), next(it)
    w_refs = [next(it) for _ in range(5)]
    lbl_ref, hgn_ref, wbr_ref = next(it), next(it), next(it)
    s0_refs = [next(it), next(it)] if carry else None
    sprev_ref = next(it) if (not carry and layer > 0) else None
    phg_ref = next(it)
    sout_ref = None if carry else next(it)
    hb, wcat, q_s, v_s, g_s, kf_s, kb_s, bf_s, bb_s, o_s, y_s, lvf_s, lvb_s, trif_s, trib_s, st_s = it

    j = pl.program_id(1)

    def head(first):
        if first:
            t = lax.broadcasted_iota(jnp.int32, (HG_BLOCK // 2, HG_BLOCK // 2), 0)
            s = lax.broadcasted_iota(jnp.int32, (HG_BLOCK // 2, HG_BLOCK // 2), 1)
            x = t ^ s
            lv = jnp.zeros_like(x)
            for m in range(HG_LEVELS - 1):
                lv = lv + (x >= 2**m).astype(jnp.int32)
            lvf_s[...] = jnp.where(t >= s, lv, -1)
            lvb_s[...] = jnp.where(t <= s, lv, -1)
            t = lax.broadcasted_iota(jnp.int32, (HG_BLOCK, HG_BLOCK), 0)
            s = lax.broadcasted_iota(jnp.int32, (HG_BLOCK, HG_BLOCK), 1)
            trif_s[...] = (t >= s).astype(BF16)
            trib_s[...] = (t <= s).astype(BF16)
        for g, w_ref in enumerate(w_refs):
            wcat[:, g * HG_DK : (g + 1) * HG_DK] = w_ref[...].astype(BF16)
        a0, a1 = lbl_ref[0], lbl_ref[1]
        amax = jnp.maximum(a0, a1)
        e0, e1 = jnp.exp(a0 - amax), jnp.exp(a1 - amax)
        p0, p1 = e0 / (e0 + e1), e1 / (e0 + e1)
        lb = (p0 - p0) if layer == 0 else ((p0 + p1) - p0)
        log_lb = jnp.log(lb)

        wc = wcat[...]
        part = PART_ROWS
        n_parts = ROWS // part

        def zdot(p):
            rows = slice(p * part, (p + 1) * part)
            if first:
                hb[rows, :] = _norm_mod(x_ref[rows, :], nmix_ref[...], mod_ref[0, 1:2, :], mod_ref[0, 0:1, :])
            return _dot(hb[rows, :], wc)

        pending = zdot(0)
        for p in range(n_parts):
            z = pending
            if p + 1 < n_parts:
                pending = zdot(p + 1)
            rows = slice(p * part, (p + 1) * part)
            zq, zff, zfb, zi, zg = (z[:, g * HG_DK : (g + 1) * HG_DK] for g in range(5))
            q_s[rows, :] = _silu(zq) * HG_DK**-0.5
            v_s[rows, :] = zi
            g_s[rows, :] = _silu(zg)
            for d, (zf, k_s, b_s, tri_s) in enumerate(((zff, kf_s, bf_s, trif_s), (zfb, kb_s, bb_s, trib_s))):
                lf = _log1pexp(log_lb[d : d + 1, :] - zf) - _log1pexp(-zf)
                k_s[rows, :] = (1.0 - lb[d : d + 1, :]) * _sigmoid(-zf)
                tri = tri_s[...]
                for n in range(part // HG_BLOCK):
                    loc = slice(n * HG_BLOCK, (n + 1) * HG_BLOCK)
                    dst = slice(rows.start + n * HG_BLOCK, rows.start + (n + 1) * HG_BLOCK)
                    b_s[dst, :] = _cum_logdecay(lf[loc, :], tri)

        o_s[...] = jnp.zeros_like(o_s)
        n_blk = ROWS // HG_BLOCK
        if sprev_ref is not None:
            sout_ref[:, 0:layer] = sprev_ref[...]
        if carry:
            for d in range(2):
                st_s[d] = s0_refs[d][0].T

        def blocks(n, c):
            for d in range(2):
                forward = d == 0
                blk = n if forward else n_blk - 1 - n
                rows = pl.ds(pl.multiple_of(blk * HG_BLOCK, HG_BLOCK), HG_BLOCK)
                k_s, b_s, lv_s = (kf_s, bf_s, lvf_s) if forward else (kb_s, bb_s, lvb_s)
                o, st_new = _hgrn_block(
                    q_s[rows, :], k_s[rows, :], v_s[rows, :], b_s[rows, :], lv_s[...], forward, st_s[d] if carry else None
                )
                o_s[rows, :] += o
                if carry:
                    st_s[d] = st_new
                else:
                    sout_ref[blk, layer, d, 0] = st_new.T
            return c

        lax.fori_loop(0, n_blk, blocks, 0, unroll=True)

        y_s[j] = (_rms(o_s[...], hgn_ref[...]) * g_s[...]).astype(BF16)

    @pl.when(j == 0)
    def _():
        head(True)

    @pl.when(j > 0)
    def _():
        head(False)

    @pl.when(j == HG_HEADS - 1)
    def _():
        y = jnp.concatenate([y_s[h] for h in range(HG_HEADS)], axis=1)
        phg_ref[...] = _dot(y, wbr_ref[...].astype(BF16))


def _hgrn_call(x, mod, seq_len, layer, norm_mix, w_in, lb_logits, hg_norm, w_branch_hg, state0, prev_states):
    n_tok = x.shape[0]
    nb = n_tok // ROWS
    carry = seq_len > HG_BLOCK
    per_seq_mod = mod.shape[0] > 1

    in_specs = [
        pl.BlockSpec((ROWS, D_MODEL), lambda i, j: (i, 0)),
        pl.BlockSpec((1, N_MOD, D_MODEL), (lambda i, j: (i, 0, 0)) if per_seq_mod else (lambda i, j: (0, 0, 0))),
        pl.BlockSpec((None, 1, D_MODEL), lambda i, j: (layer, 0, 0)),
    ]
    args = [x, mod, norm_mix.reshape(DEPTH, 1, D_MODEL)]
    for g in range(5):
        in_specs.append(pl.BlockSpec((None, D_MODEL, HG_DK), lambda i, j, g=g: (layer, 0, g * HG_HEADS + j)))
        args.append(w_in)
    in_specs += [
        pl.BlockSpec((DEPTH, 2, HG_DK), lambda i, j: (0, 0, j)),
        pl.BlockSpec((None, 1, HG_DV), lambda i, j: (layer, 0, 0)),
        pl.BlockSpec((None, HG_WIDTH, D_MODEL), lambda i, j: (layer, 0, 0), pipeline_mode=pl.Buffered(1)),
    ]
    args += [lb_logits, hg_norm.reshape(DEPTH, 1, HG_DV), w_branch_hg]
    if carry:
        assert seq_len == ROWS
        for d in range(2):
            in_specs.append(
                pl.BlockSpec((1, HG_DK, HG_DV), lambda i, j, d=d: (((i * DEPTH + layer) * 2 + d) * HG_HEADS + j, 0, 0))
            )
            args.append(state0)

    out_shape = [jax.ShapeDtypeStruct((n_tok, D_MODEL), F32)]
    out_specs = [pl.BlockSpec((ROWS, D_MODEL), lambda i, j: (i, 0))]
    if not carry:
        assert seq_len == HG_BLOCK
        n_seq = n_tok // seq_len
        seqs = ROWS // seq_len
        if layer > 0:
            in_specs.append(pl.BlockSpec((seqs, layer, 2, 1, HG_DK, HG_DV), lambda i, j: (i, 0, 0, j, 0, 0)))
            args.append(prev_states)
        out_shape.append(jax.ShapeDtypeStruct((n_seq, layer + 1, 2, HG_HEADS, HG_DK, HG_DV), F32))
        out_specs.append(pl.BlockSpec((seqs, layer + 1, 2, 1, HG_DK, HG_DV), lambda i, j: (i, 0, 0, j, 0, 0)))

    head = lambda dt=F32: pltpu.VMEM((ROWS, HG_DK), dt)
    scratch = [
        pltpu.VMEM((ROWS, D_MODEL), BF16),
        pltpu.VMEM((D_MODEL, 5 * HG_DK), BF16),
        head(), head(), head(),
        head(), head(), head(), head(),
        head(),
        pltpu.VMEM((HG_HEADS, ROWS, HG_DV), BF16),
        pltpu.VMEM((HG_BLOCK // 2, HG_BLOCK // 2), jnp.int32),
        pltpu.VMEM((HG_BLOCK // 2, HG_BLOCK // 2), jnp.int32),
        pltpu.VMEM((HG_BLOCK, HG_BLOCK), BF16),
        pltpu.VMEM((HG_BLOCK, HG_BLOCK), BF16),
        pltpu.VMEM((2, HG_DV, HG_DK), F32),
    ]
    outs = pl.pallas_call(
        functools.partial(_hgrn_kernel, layer=layer, seq_len=seq_len),
        grid=(nb, HG_HEADS),
        in_specs=in_specs,
        out_specs=out_specs,
        out_shape=out_shape,
        scratch_shapes=scratch,
        compiler_params=pltpu.CompilerParams(
            dimension_semantics=("arbitrary", "arbitrary"), vmem_limit_bytes=VMEM_HGRN
        ),
        name=f"hgrn_l{layer}_t{seq_len}",
    )(*args)
    return (outs[0], None) if carry else (outs[0], outs[1])


_MIX_ORDER = (10, 11, 5, 6, 7, 12, 13, 8, 9)


def _mix_col(k):
    idx = 0
    for n, c in enumerate(_MIX_ORDER):
        idx = idx + jnp.where(k == n, c, 0)
    return idx


def _window_mean_minus_self(p, tpos, seq_len, w):
    n = p.shape[0]
    half = w // 2

    def shifted(x, j):
        valid = (tpos + j >= 0) & (tpos + j < seq_len)
        return jnp.where(valid, pltpu.roll(x, (-j) % n, 0), 0.0)

    ahead, behind, length = p, p, 1
    while length < half:
        ahead = ahead + shifted(ahead, length)
        behind = behind + shifted(behind, -length)
        length *= 2
    acc = ahead + shifted(behind, -1)
    cnt = jnp.minimum(tpos + half, seq_len) - jnp.maximum(tpos - half, 0)
    return acc / cnt.astype(F32) - p


def _mix_kernel(
    x_ref, mod_ref, nmix_ref, w_ref, sgn_ref, sgw_ref, sgb_ref, wbsg_ref, wbpool_ref, poolw_ref, pscale_ref,
    wout_ref, phg_ref, o_ref, hb, u_s, br_s, mrg_s, *, seq_len,
):
    k = pl.program_id(1)
    half = D_MODEL // 2

    @pl.when(k == 0)
    def _():
        hb[...] = _norm_mod(x_ref[...], nmix_ref[...], mod_ref[0, 1:2, :], mod_ref[0, 0:1, :])

    def for_z_parts(consume, part=PART_ROWS):
        w = w_ref[...].astype(BF16)
        n_parts = ROWS // part

        def zdot(p):
            return _dot(hb[p * part : (p + 1) * part, :], w)

        pending = zdot(0)
        for p in range(n_parts):
            z = pending
            if p + 1 < n_parts:
                pending = zdot(p + 1)
            consume(slice(p * part, (p + 1) * part), z)

    for step in (0, 1):

        @pl.when(k == step)
        def _(step=step):
            cols = slice(step * half, (step + 1) * half)

            def gate(rows, z):
                mrg_s[rows, cols] = _sigmoid(z)

            for_z_parts(gate)

    @pl.when(k == 2)
    def _():
        def store_u(rows, z):
            u_s[rows, :] = _gelu_tanh(z)

        for_z_parts(store_u)

    @pl.when(k == 3)
    def _():
        wbsg = wbsg_ref[...].astype(BF16)
        wgs = [sgw_ref[g].astype(BF16) for g in range(SG_GROUPS)]

        def spatial_gating(rows, z):
            v = _rms(_gelu_tanh(z), sgn_ref[...]).astype(BF16)
            for g in range(SG_GROUPS):
                bias = sgb_ref[:, g : g + 1]
                cols = slice(g * SG_GROUP_DIM, (g + 1) * SG_GROUP_DIM)
                for n in range((rows.stop - rows.start) // SG_CHUNK):
                    loc = slice(n * SG_CHUNK, (n + 1) * SG_CHUNK)
                    dst = slice(rows.start + n * SG_CHUNK, rows.start + (n + 1) * SG_CHUNK)
                    mixed = _dot(wgs[g], v[loc, cols]) + bias
                    br_s[dst, cols] = (u_s[dst, cols] * mixed).astype(BF16)
            mrg_s[rows, :] = mrg_s[rows, :] * _dot(br_s[rows, :], wbsg)

        for_z_parts(spatial_gating)

    @pl.when(k == 4)
    def _():
        part = _part_rows(seq_len)
        tpos = lax.broadcasted_iota(jnp.int32, (part, POOL_GROUP_DIM), 0) & (seq_len - 1)

        def pool(rows, z):
            for gi, w in enumerate(POOL_WINDOWS):
                cols = slice(gi * POOL_GROUP_DIM, (gi + 1) * POOL_GROUP_DIM)
                pooled = _window_mean_minus_self(z[:, cols], tpos, seq_len, w)
                out = _dot(pooled.astype(BF16), poolw_ref[gi].astype(BF16)) * pscale_ref[:, cols]
                br_s[rows, cols] = out.astype(BF16)

        for_z_parts(pool, part)

    for step in (5, 6):

        @pl.when(k == step)
        def _(step=step):
            cols = slice((step - 5) * half, (step - 4) * half)
            wbpool = wbpool_ref[:, cols].astype(BF16)

            def gate(rows, z):
                mrg_s[rows, cols] = mrg_s[rows, cols] + _sigmoid(z) * _dot(br_s[rows, :], wbpool)

            for_z_parts(gate)

    @pl.when(k == 7)
    def _():
        cols = slice(0, half)

        def gate(rows, z):
            mrg_s[rows, cols] = mrg_s[rows, cols] + _sigmoid(z) * phg_ref[rows, cols]

        for_z_parts(gate)

    @pl.when(k == 8)
    def _():
        cols = slice(half, D_MODEL)
        wout = wout_ref[...].astype(BF16)

        def gate_and_project(rows, z):
            mrg_s[rows, cols] = mrg_s[rows, cols] + _sigmoid(z) * phg_ref[rows, cols]
            y = _dot(mrg_s[rows, :].astype(BF16), wout)
            o_ref[rows, :] = x_ref[rows, :] + mod_ref[0, 2:3, :] * y

        for_z_parts(gate_and_project)


def _mix_call(x, phg, mod, seq_len, layer, norm_mix, w_in, sg_norm, sg_w, sg_b, w_branch_sg, w_branch_pool, pool_w,
              pool_scale, w_out):
    n_tok = x.shape[0]
    nb = n_tok // ROWS
    per_seq_mod = mod.shape[0] > 1
    const = pl.Buffered(1)
    assert seq_len & (seq_len - 1) == 0 and ROWS % seq_len == 0 and seq_len % SG_CHUNK == 0
    in_specs = [
        pl.BlockSpec((ROWS, D_MODEL), lambda i, k: (i, 0)),
        pl.BlockSpec((1, N_MOD, D_MODEL), (lambda i, k: (i, 0, 0)) if per_seq_mod else (lambda i, k: (0, 0, 0))),
        pl.BlockSpec((None, 1, D_MODEL), lambda i, k: (layer, 0, 0)),
        pl.BlockSpec((None, D_MODEL, IN_CHUNK), lambda i, k: (layer, 0, _mix_col(k))),
        pl.BlockSpec((None, 1, SG_WIDTH), lambda i, k: (layer, 0, 0)),
        pl.BlockSpec((None, SG_GROUPS, SG_CHUNK, SG_CHUNK), lambda i, k: (layer, 0, 0, 0)),
        pl.BlockSpec((None, SG_CHUNK, SG_GROUPS), lambda i, k: (layer, 0, 0)),
        pl.BlockSpec((None, SG_WIDTH, D_MODEL), lambda i, k: (layer, 0, 0), pipeline_mode=const),
        pl.BlockSpec((None, POOL_WIDTH, D_MODEL), lambda i, k: (layer, 0, 0), pipeline_mode=const),
        pl.BlockSpec((None, len(POOL_WINDOWS), POOL_GROUP_DIM, POOL_GROUP_DIM), lambda i, k: (layer, 0, 0, 0)),
        pl.BlockSpec((None, 1, POOL_WIDTH), lambda i, k: (layer, 0, 0)),
        pl.BlockSpec((None, D_MODEL, D_MODEL), lambda i, k: (layer, 0, 0), pipeline_mode=const),
        pl.BlockSpec((ROWS, D_MODEL), lambda i, k: (i, 0)),
    ]
    args = [
        x, mod, norm_mix.reshape(DEPTH, 1, D_MODEL), w_in, sg_norm.reshape(DEPTH, 1, SG_WIDTH), sg_w,
        jnp.swapaxes(sg_b, 1, 2), w_branch_sg, w_branch_pool, pool_w, pool_scale.reshape(DEPTH, 1, POOL_WIDTH),
        w_out, phg,
    ]
    scratch = [
        pltpu.VMEM((ROWS, D_MODEL), BF16),
        pltpu.VMEM((ROWS, SG_WIDTH), F32),
        pltpu.VMEM((ROWS, SG_WIDTH), BF16),
        pltpu.VMEM((ROWS, D_MODEL), F32),
    ]
    return pl.pallas_call(
        functools.partial(_mix_kernel, seq_len=seq_len),
        grid=(nb, len(_MIX_ORDER)),
        in_specs=in_specs,
        out_specs=pl.BlockSpec((ROWS, D_MODEL), lambda i, k: (i, 0)),
        out_shape=jax.ShapeDtypeStruct((n_tok, D_MODEL), F32),
        scratch_shapes=scratch,
        compiler_params=pltpu.CompilerParams(
            dimension_semantics=("arbitrary", "arbitrary"), vmem_limit_bytes=VMEM_MIX
        ),
        name=f"mix_l{layer}_t{seq_len}",
    )(*args)


def _ffn_kernel(x_ref, mod_ref, nffn_ref, *refs, seq_len, final):
    sets = [refs[7 * s : 7 * s + 7] for s in range(FF_SETS)]
    fin_ref, o_ref, hb, acc = refs[7 * FF_SETS :]
    c = pl.program_id(1)
    n_chunks = D_FF // FF_CHUNK
    n_steps = pl.cdiv(n_chunks, FF_SETS)

    part = ROWS
    tpos = lax.broadcasted_iota(jnp.int32, (part, FF_CHUNK), 0) & (seq_len - 1)
    has_prev = tpos >= 1
    has_next = tpos < seq_len - 1

    def conv(h, cw_ref, cb_ref):
        prev = jnp.where(has_prev, pltpu.roll(h, 1, 0), 0.0)
        nxt = jnp.where(has_next, pltpu.roll(h, part - 1, 0), 0.0)
        return prev * cw_ref[0:1, :] + h * cw_ref[1:2, :] + nxt * cw_ref[2:3, :] + cb_ref[...]

    def run(n_sets, first, last):
        ws = [(wa[...].astype(BF16), wb[...].astype(BF16), wd[...].astype(BF16)) for wa, wb, _, _, _, _, wd in sets]
        items = [(p, s) for p in range(ROWS // part) for s in range(n_sets)]

        def up(item):
            p, s = item
            rows = slice(p * part, (p + 1) * part)
            if first and s == 0:
                hb[rows, :] = _norm_mod(x_ref[rows, :], nffn_ref[...], mod_ref[0, 4:5, :], mod_ref[0, 3:4, :])
            h = hb[rows, :]
            return _dot(h, ws[s][0]), _dot(h, ws[s][1])

        pending = up(items[0])
        down = None
        for i, (p, s) in enumerate(items):
            ha, hb2 = pending
            if i + 1 < len(items):
                pending = up(items[i + 1])
            _, _, cwa_ref, cwb_ref, cba_ref, cbb_ref, _ = sets[s]
            a = conv(ha, cwa_ref, cba_ref)
            b = conv(hb2, cwb_ref, cbb_ref)
            d = _dot((_silu(a) * b).astype(BF16), ws[s][2])
            down = d if down is None else down + d
            if s == n_sets - 1:
                rows = slice(p * part, (p + 1) * part)
                total = down if first else acc[rows, :] + down
                if last:
                    y = x_ref[rows, :] + mod_ref[0, 5:6, :] * total
                    o_ref[rows, :] = _rms(y, fin_ref[...]) if final else y
                else:
                    acc[rows, :] = total
                down = None

    kinds = {}
    for step in range(n_steps):
        n_sets = len([s for s in range(FF_SETS) if step + s * n_steps < n_chunks])
        kinds.setdefault((n_sets, step == 0, step == n_steps - 1), []).append(step)
    for (n_sets, first, last), steps in kinds.items():
        cond = functools.reduce(lambda u, v: u | v, [c == st for st in steps])

        @pl.when(cond)
        def _(n_sets=n_sets, first=first, last=last):
            run(n_sets, first, last)


def _ffn_call(x, mod, seq_len, layer, final, norm_ffn, ffn_up, ffn_conv_w, ffn_conv_b, ffn_down, final_norm):
    n_tok = x.shape[0]
    nb = n_tok // ROWS
    nc = D_FF // FF_CHUNK
    n_steps = pl.cdiv(nc, FF_SETS)
    per_seq_mod = mod.shape[0] > 1
    conv_b = ffn_conv_b.reshape(DEPTH, 1, 2 * D_FF)
    in_specs = [
        pl.BlockSpec((ROWS, D_MODEL), lambda i, c: (i, 0)),
        pl.BlockSpec((1, N_MOD, D_MODEL), (lambda i, c: (i, 0, 0)) if per_seq_mod else (lambda i, c: (0, 0, 0))),
        pl.BlockSpec((None, 1, D_MODEL), lambda i, c: (layer, 0, 0)),
    ]
    args = [x, mod, norm_ffn.reshape(DEPTH, 1, D_MODEL)]
    for s in range(FF_SETS):
        chunk = lambda c, s=s: jnp.minimum(c + s * n_steps, nc - 1)
        in_specs += [
            pl.BlockSpec((None, D_MODEL, FF_CHUNK), lambda i, c, f=chunk: (layer, 0, f(c))),
            pl.BlockSpec((None, D_MODEL, FF_CHUNK), lambda i, c, f=chunk: (layer, 0, nc + f(c))),
            pl.BlockSpec((None, 3, FF_CHUNK), lambda i, c, f=chunk: (layer, 0, f(c))),
            pl.BlockSpec((None, 3, FF_CHUNK), lambda i, c, f=chunk: (layer, 0, nc + f(c))),
            pl.BlockSpec((None, 1, FF_CHUNK), lambda i, c, f=chunk: (layer, 0, f(c))),
            pl.BlockSpec((None, 1, FF_CHUNK), lambda i, c, f=chunk: (layer, 0, nc + f(c))),
            pl.BlockSpec((None, FF_CHUNK, D_MODEL), lambda i, c, f=chunk: (layer, f(c), 0)),
        ]
        args += [ffn_up, ffn_up, ffn_conv_w, ffn_conv_w, conv_b, conv_b, ffn_down]
    in_specs.append(pl.BlockSpec((1, D_MODEL), lambda i, c: (0, 0)))
    args.append(final_norm.reshape(1, D_MODEL))
    return pl.pallas_call(
        functools.partial(_ffn_kernel, seq_len=seq_len, final=final),
        grid=(nb, n_steps),
        in_specs=in_specs,
        out_specs=pl.BlockSpec((ROWS, D_MODEL), lambda i, c: (i, 0)),
        out_shape=jax.ShapeDtypeStruct((n_tok, D_MODEL), F32),
        scratch_shapes=[pltpu.VMEM((ROWS, D_MODEL), BF16), pltpu.VMEM((ROWS, D_MODEL), F32)],
        compiler_params=pltpu.CompilerParams(
            dimension_semantics=("arbitrary", "arbitrary"), vmem_limit_bytes=VMEM_FFN
        ),
        name=f"ffn_l{layer}_t{seq_len}",
    )(*args)


def kernel(x_prompt, x_sample, c, state_hgrn, c_ctx, norm_mix, norm_ffn, w_ada, b_ada, w_in, lb_logits, hg_norm,
           w_branch_hg, w_branch_sg, w_branch_pool, w_out, sg_norm, sg_w, sg_b, pool_w, pool_scale, ffn_up,
           ffn_conv_w, ffn_conv_b, ffn_down, final_norm):
    n_ctx, t_ctx, _ = x_prompt.shape
    n_lat, t_lat, _ = x_sample.shape

    n_cond = 1 + n_lat
    pad = -n_cond % V7X_SUBLANES
    cvec = jnp.concatenate([c_ctx[None, :], c, jnp.zeros((pad, D_MODEL), F32)], axis=0)
    mod = _mod_call(cvec, w_ada, b_ada).reshape(DEPTH, n_cond + pad, N_MOD, D_MODEL)

    xs = _addpos_call(x_sample, _grid_pos_embed(t_lat)).reshape(n_lat * t_lat, D_MODEL)
    xp = x_prompt.reshape(n_ctx * t_ctx, D_MODEL)
    state0 = state_hgrn.reshape(n_lat * DEPTH * 2 * HG_HEADS, HG_DK, HG_DV)

    states = None
    for layer in range(DEPTH):
        final = layer == DEPTH - 1
        groups = []
        for x, m, t, s0 in ((xp, mod[layer, 0:1], t_ctx, None), (xs, mod[layer, 1:n_cond], t_lat, state0)):
            phg, s_fin = _hgrn_call(x, m, t, layer, norm_mix, w_in, lb_logits, hg_norm, w_branch_hg, s0, states)
            x1 = _mix_call(x, phg, m, t, layer, norm_mix, w_in, sg_norm, sg_w, sg_b, w_branch_sg, w_branch_pool,
                           pool_w, pool_scale, w_out)
            x2 = _ffn_call(x1, m, t, layer, final, norm_ffn, ffn_up, ffn_conv_w, ffn_conv_b, ffn_down, final_norm)
            groups.append((x2, s_fin))
        (xp, states), (xs, _) = groups

    y_prompt = xp.reshape(x_prompt.shape)
    y_sample = xs.reshape(x_sample.shape)
    return (y_prompt, y_sample, states)
```

```python
import functools

import jax
import jax.numpy as jnp
import numpy as np
from jax import lax
from jax.experimental import pallas as pl
from jax.experimental.pallas import tpu as pltpu

D_MODEL = 1024
DEPTH = 2
GRID_W = 64
POS_BASE = 10000.0
EPS = 1e-6
HG_HEADS = 4
HG_DK = 128
HG_DV = 128
HG_WIDTH = HG_HEADS * HG_DV
SG_GROUPS = 4
SG_WIDTH = 512
SG_GROUP_DIM = SG_WIDTH // SG_GROUPS
SG_CHUNK = 128
POOL_WINDOWS = (2, 4, 8, 16)
POOL_WIDTH = 512
POOL_GROUP_DIM = POOL_WIDTH // len(POOL_WINDOWS)
IN_COLS = 5 * HG_WIDTH + 2 * SG_WIDTH + POOL_WIDTH + 3 * D_MODEL
D_FF = 2816
N_MOD = 6

V7X_LANES = 128
V7X_SUBLANES = 8
V7X_MXU_DIM = 256
MIB = 2**20

ROWS = 1024
PART_ROWS = 512
HG_BLOCK = 256
HG_LEVELS = 8
IN_CHUNK = 512
FF_CHUNK = V7X_MXU_DIM
FF_SETS = 3
MOD_CHUNK = 1536
VMEM_SMALL = 32 * MIB
VMEM_HGRN = 48 * MIB
VMEM_MIX = 56 * MIB
VMEM_FFN = 52 * MIB

NEG_LOG2E = -1.4426950408889634

F32 = jnp.float32
BF16 = jnp.bfloat16


def _dot(a, b):
    return lax.dot_general(a, b, (((1,), (0,)), ((), ())), preferred_element_type=F32)


def _dot_nt(a, b):
    return lax.dot_general(a, b, (((1,), (1,)), ((), ())), preferred_element_type=F32)


def _dot_tn(a, b):
    return lax.dot_general(a, b, (((0,), (0,)), ((), ())), preferred_element_type=F32)


def _sigmoid(x):
    return 0.5 * jnp.tanh(0.5 * x) + 0.5


def _silu(x):
    return x * _sigmoid(x)


_GELU_C1 = 0.7978845608028654
_GELU_C2 = _GELU_C1 * 0.044715


def _gelu_tanh(x):
    half_x = 0.5 * x
    return half_x + half_x * jnp.tanh(x * (_GELU_C1 + _GELU_C2 * (x * x)))


def _rms(x, gain):
    return x * lax.rsqrt(jnp.mean(x * x, axis=-1, keepdims=True) + EPS) * gain


def _log1pexp(y):
    return jnp.maximum(y, 0.0) + jnp.log(1.0 + jnp.exp(-jnp.abs(y)))


def _norm_mod(x, gain, scale, shift):
    return (_rms(x, gain) * (1.0 + scale) + shift).astype(BF16)


def _part_rows(seq_len):
    return max(seq_len, PART_ROWS)


def _mod_kernel(c_ref, w_ref, b_ref, o_ref):
    c = _silu(c_ref[...]).astype(BF16)
    o_ref[...] = _dot(c, w_ref[...].astype(BF16)) + b_ref[...]


def _mod_call(cvec, w_ada, b_ada):
    n_rows = cvec.shape[0]
    n_cols = N_MOD * D_MODEL
    return pl.pallas_call(
        _mod_kernel,
        grid=(DEPTH, n_cols // MOD_CHUNK),
        in_specs=[
            pl.BlockSpec((n_rows, D_MODEL), lambda l, n: (0, 0)),
            pl.BlockSpec((None, D_MODEL, MOD_CHUNK), lambda l, n: (l, 0, n)),
            pl.BlockSpec((None, 1, MOD_CHUNK), lambda l, n: (l, 0, n)),
        ],
        out_specs=pl.BlockSpec((None, n_rows, MOD_CHUNK), lambda l, n: (l, 0, n)),
        out_shape=jax.ShapeDtypeStruct((DEPTH, n_rows, n_cols), F32),
        compiler_params=pltpu.CompilerParams(
            dimension_semantics=("arbitrary", "arbitrary"), vmem_limit_bytes=VMEM_SMALL
        ),
        name="adaln_mod",
    )(cvec, w_ada, b_ada.reshape(DEPTH, 1, n_cols))


def _addpos_kernel(x_ref, p_ref, o_ref):
    o_ref[...] = x_ref[...] + p_ref[...]


def _addpos_call(x, pos):
    b, t, d = x.shape
    return pl.pallas_call(
        _addpos_kernel,
        grid=(b,),
        in_specs=[pl.BlockSpec((None, t, d), lambda i: (i, 0, 0)), pl.BlockSpec((t, d), lambda i: (0, 0))],
        out_specs=pl.BlockSpec((None, t, d), lambda i: (i, 0, 0)),
        out_shape=jax.ShapeDtypeStruct(x.shape, x.dtype),
        compiler_params=pltpu.CompilerParams(dimension_semantics=("arbitrary",), vmem_limit_bytes=VMEM_SMALL),
        name="add_pos",
    )(x, pos)


def _grid_pos_embed(n_tokens):
    rows = n_tokens // GRID_W
    r = np.broadcast_to(np.arange(rows, dtype=np.float32)[:, None], (rows, GRID_W)).reshape(-1)
    col = np.broadcast_to(np.arange(GRID_W, dtype=np.float32)[None, :], (rows, GRID_W)).reshape(-1)
    quarter = D_MODEL // 4
    omega = (1.0 / (np.float32(POS_BASE) ** (np.arange(quarter, dtype=np.float32) / quarter))).astype(np.float32)
    ar = r[:, None] * omega[None, :]
    ac = col[:, None] * omega[None, :]
    return jnp.asarray(np.concatenate([np.sin(ar), np.cos(ar), np.sin(ac), np.cos(ac)], axis=-1), F32)


def _ref_rows(b, blk, r):
    n, c = b.shape
    if blk >= V7X_SUBLANES:
        x3 = b.reshape(n // blk, blk, c)
        return jnp.broadcast_to(x3[:, r : r + 1, :], x3.shape).reshape(n, c)
    x3 = b.reshape(n // V7X_SUBLANES, V7X_SUBLANES, c)
    sub = lax.broadcasted_iota(jnp.int32, x3.shape, 1)
    bases = list(range(0, V7X_SUBLANES, blk))
    out = jnp.broadcast_to(x3[:, bases[-1] + r : bases[-1] + r + 1, :], x3.shape)
    for base in reversed(bases[:-1]):
        out = jnp.where(sub < base + blk, jnp.broadcast_to(x3[:, base + r : base + r + 1, :], x3.shape), out)
    return out.reshape(n, c)


def _cum_logdecay(lf, tri):
    hi = lf.astype(BF16)
    r1 = lf - hi.astype(F32)
    mid = r1.astype(BF16)
    lo = (r1 - mid.astype(F32)).astype(BF16)
    return _dot(tri, hi) + _dot(tri, mid) + _dot(tri, lo)


def _hgrn_block(qs, ks, vs, bs, lvqs, sts):
    half = HG_BLOCK // 2
    lo, hi = slice(0, half), slice(half, HG_BLOCK)
    dirs = (True, False)
    vbs = [v.astype(BF16) for v in vs]
    qbs = [q.astype(BF16) for q in qs]
    kbs = [k.astype(BF16) for k in ks]
    diags = [[jnp.where(lvq == 0, _dot_nt(qb[h], kb[h]), 0.0) for h in (lo, hi)] for qb, kb, lvq in zip(qbs, kbs, lvqs)]
    for m in range(1, HG_LEVELS):
        blk = 2**m
        for d, forward in enumerate(dirs):
            b = bs[d]
            ref = _ref_rows(b, blk, blk // 2 - 1 if forward else blk // 2)
            e = jnp.exp2(jnp.abs(b - ref) * NEG_LOG2E).astype(BF16)
            qt, kt = qbs[d] * e, kbs[d] * e
            diags[d] = [jnp.where(lvqs[d] == m, _dot_nt(qt[h], kt[h]), a) for h, a in zip((lo, hi), diags[d])]
    outs = []
    for d, forward in enumerate(dirs):
        b, k, st, q, vb = bs[d], ks[d], sts[d], qs[d], vbs[d]
        mid = half - 1 if forward else half
        e = jnp.exp2(jnp.abs(b - b[mid : mid + 1, :]) * NEG_LOG2E).astype(BF16)
        qt, kt = qbs[d] * e, kbs[d] * e
        a_lo, a_hi = (a.astype(BF16) for a in diags[d])
        if forward:
            cross = _dot_nt(qt[hi], kt[lo]).astype(BF16)
            o = jnp.concatenate([_dot(a_lo, vb[lo]), _dot(cross, vb[lo]) + _dot(a_hi, vb[hi])], axis=0)
        else:
            cross = _dot_nt(qt[lo], kt[hi]).astype(BF16)
            o = jnp.concatenate([_dot(a_lo, vb[lo]) + _dot(cross, vb[hi]), _dot(a_hi, vb[hi])], axis=0)
        edge = b[HG_BLOCK - 1 : HG_BLOCK, :] if forward else b[0:1, :]
        k_end = (k * jnp.exp(edge - b)).astype(BF16)
        st_new = _dot_tn(vb, k_end)
        if st is not None:
            o = o + _dot_nt((q * jnp.exp(b)).astype(BF16), st.astype(BF16))
            st_new = st_new + st * jnp.exp(edge)
        outs.append((o, st_new))
    return outs


def _hgrn_kernel(*refs, layer, seq_len):
    carry = seq_len > HG_BLOCK
    it = iter(refs)
    x_ref, mod_ref, nmix_ref = next(it), next(it), next(it)
    w_refs = [next(it) for _ in range(5)]
    lbl_ref, hgn_ref, wbr_ref = next(it), next(it), next(it)
    s0_refs = [next(it), next(it)] if carry else None
    sprev_ref = next(it) if (not carry and layer > 0) else None
    phg_ref = next(it)
    sout_ref = None if carry else next(it)
    hb, wcat, q_s, v_s, g_s, kf_s, kb_s, bf_s, bb_s, o_s, y_s, lvf_s, lvb_s, trif_s, trib_s, st_s = it

    j = pl.program_id(1)

    def head(first):
        if first:
            t = lax.broadcasted_iota(jnp.int32, (HG_BLOCK // 2, HG_BLOCK // 2), 0)
            s = lax.broadcasted_iota(jnp.int32, (HG_BLOCK // 2, HG_BLOCK // 2), 1)
            x = t ^ s
            lv = jnp.zeros_like(x)
            for m in range(HG_LEVELS - 1):
                lv = lv + (x >= 2**m).astype(jnp.int32)
            lvf_s[...] = jnp.where(t >= s, lv, -1)
            lvb_s[...] = jnp.where(t <= s, lv, -1)
            t = lax.broadcasted_iota(jnp.int32, (HG_BLOCK, HG_BLOCK), 0)
            s = lax.broadcasted_iota(jnp.int32, (HG_BLOCK, HG_BLOCK), 1)
            trif_s[...] = (t >= s).astype(BF16)
            trib_s[...] = (t <= s).astype(BF16)
        for g, w_ref in enumerate(w_refs):
            wcat[:, g * HG_DK : (g + 1) * HG_DK] = w_ref[...].astype(BF16)
        a0, a1 = lbl_ref[0], lbl_ref[1]
        amax = jnp.maximum(a0, a1)
        e0, e1 = jnp.exp(a0 - amax), jnp.exp(a1 - amax)
        p0, p1 = e0 / (e0 + e1), e1 / (e0 + e1)
        lb = (p0 - p0) if layer == 0 else ((p0 + p1) - p0)
        log_lb = jnp.log(lb)

        wc = wcat[...]
        part = PART_ROWS
        n_parts = ROWS // part

        def zdot(p):
            rows = slice(p * part, (p + 1) * part)
            if first:
                hb[rows, :] = _norm_mod(x_ref[rows, :], nmix_ref[...], mod_ref[0, 1:2, :], mod_ref[0, 0:1, :])
            return _dot(hb[rows, :], wc)

        pending = zdot(0)
        for p in range(n_parts):
            z = pending
            if p + 1 < n_parts:
                pending = zdot(p + 1)
            rows = slice(p * part, (p + 1) * part)
            zq, zff, zfb, zi, zg = (z[:, g * HG_DK : (g + 1) * HG_DK] for g in range(5))
            q_s[rows, :] = _silu(zq) * HG_DK**-0.5
            v_s[rows, :] = zi
            g_s[rows, :] = _silu(zg)
            for d, (zf, k_s, b_s, tri_s) in enumerate(((zff, kf_s, bf_s, trif_s), (zfb, kb_s, bb_s, trib_s))):
                lf = _log1pexp(log_lb[d : d + 1, :] - zf) - _log1pexp(-zf)
                k_s[rows, :] = (1.0 - lb[d : d + 1, :]) * _sigmoid(-zf)
                tri = tri_s[...]
                for n in range(part // HG_BLOCK):
                    loc = slice(n * HG_BLOCK, (n + 1) * HG_BLOCK)
                    dst = slice(rows.start + n * HG_BLOCK, rows.start + (n + 1) * HG_BLOCK)
                    b_s[dst, :] = _cum_logdecay(lf[loc, :], tri)

        o_s[...] = jnp.zeros_like(o_s)
        n_blk = ROWS // HG_BLOCK
        if sprev_ref is not None:
            sout_ref[:, 0:layer] = sprev_ref[...]
        if carry:
            for d in range(2):
                st_s[d] = s0_refs[d][0].T

        def blocks(n, c):
            blks = (n, n_blk - 1 - n if carry else n)
            rows = [pl.ds(pl.multiple_of(blk * HG_BLOCK, HG_BLOCK), HG_BLOCK) for blk in blks]
            outs = _hgrn_block(
                [q_s[r, :] for r in rows], [kf_s[rows[0], :], kb_s[rows[1], :]], [v_s[r, :] for r in rows],
                [bf_s[rows[0], :], bb_s[rows[1], :]], [lvf_s[...], lvb_s[...]],
                [st_s[d] if carry else None for d in range(2)],
            )
            for d, (o, st_new) in enumerate(outs):
                o_s[rows[d], :] += o
                if carry:
                    st_s[d] = st_new
                else:
                    sout_ref[blks[d], layer, d, 0] = st_new.T
            return c

        lax.fori_loop(0, n_blk, blocks, 0, unroll=True)

        y_s[j] = (_rms(o_s[...], hgn_ref[...]) * g_s[...]).astype(BF16)

    @pl.when(j == 0)
    def _():
        head(True)

    @pl.when(j > 0)
    def _():
        head(False)

    @pl.when(j == HG_HEADS - 1)
    def _():
        y = jnp.concatenate([y_s[h] for h in range(HG_HEADS)], axis=1)
        phg_ref[...] = _dot(y, wbr_ref[...].astype(BF16))


def _hgrn_call(x, mod, seq_len, layer, norm_mix, w_in, lb_logits, hg_norm, w_branch_hg, state0, prev_states):
    n_tok = x.shape[0]
    nb = n_tok // ROWS
    carry = seq_len > HG_BLOCK
    per_seq_mod = mod.shape[0] > 1

    in_specs = [
        pl.BlockSpec((ROWS, D_MODEL), lambda i, j: (i, 0)),
        pl.BlockSpec((1, N_MOD, D_MODEL), (lambda i, j: (i, 0, 0)) if per_seq_mod else (lambda i, j: (0, 0, 0))),
        pl.BlockSpec((None, 1, D_MODEL), lambda i, j: (layer, 0, 0)),
    ]
    args = [x, mod, norm_mix.reshape(DEPTH, 1, D_MODEL)]
    for g in range(5):
        in_specs.append(pl.BlockSpec((None, D_MODEL, HG_DK), lambda i, j, g=g: (layer, 0, g * HG_HEADS + j)))
        args.append(w_in)
    in_specs += [
        pl.BlockSpec((DEPTH, 2, HG_DK), lambda i, j: (0, 0, j)),
        pl.BlockSpec((None, 1, HG_DV), lambda i, j: (layer, 0, 0)),
        pl.BlockSpec((None, HG_WIDTH, D_MODEL), lambda i, j: (layer, 0, 0), pipeline_mode=pl.Buffered(1)),
    ]
    args += [lb_logits, hg_norm.reshape(DEPTH, 1, HG_DV), w_branch_hg]
    if carry:
        assert seq_len == ROWS
        for d in range(2):
            in_specs.append(
                pl.BlockSpec((1, HG_DK, HG_DV), lambda i, j, d=d: (((i * DEPTH + layer) * 2 + d) * HG_HEADS + j, 0, 0))
            )
            args.append(state0)

    out_shape = [jax.ShapeDtypeStruct((n_tok, D_MODEL), F32)]
    out_specs = [pl.BlockSpec((ROWS, D_MODEL), lambda i, j: (i, 0))]
    if not carry:
        assert seq_len == HG_BLOCK
        n_seq = n_tok // seq_len
        seqs = ROWS // seq_len
        if layer > 0:
            in_specs.append(pl.BlockSpec((seqs, layer, 2, 1, HG_DK, HG_DV), lambda i, j: (i, 0, 0, j, 0, 0)))
            args.append(prev_states)
        out_shape.append(jax.ShapeDtypeStruct((n_seq, layer + 1, 2, HG_HEADS, HG_DK, HG_DV), F32))
        out_specs.append(pl.BlockSpec((seqs, layer + 1, 2, 1, HG_DK, HG_DV), lambda i, j: (i, 0, 0, j, 0, 0)))

    head = lambda dt=F32: pltpu.VMEM((ROWS, HG_DK), dt)
    scratch = [
        pltpu.VMEM((ROWS, D_MODEL), BF16),
        pltpu.VMEM((D_MODEL, 5 * HG_DK), BF16),
        head(), head(), head(),
        head(), head(), head(), head(),
        head(),
        pltpu.VMEM((HG_HEADS, ROWS, HG_DV), BF16),
        pltpu.VMEM((HG_BLOCK // 2, HG_BLOCK // 2), jnp.int32),
        pltpu.VMEM((HG_BLOCK // 2, HG_BLOCK // 2), jnp.int32),
        pltpu.VMEM((HG_BLOCK, HG_BLOCK), BF16),
        pltpu.VMEM((HG_BLOCK, HG_BLOCK), BF16),
        pltpu.VMEM((2, HG_DV, HG_DK), F32),
    ]
    outs = pl.pallas_call(
        functools.partial(_hgrn_kernel, layer=layer, seq_len=seq_len),
        grid=(nb, HG_HEADS),
        in_specs=in_specs,
        out_specs=out_specs,
        out_shape=out_shape,
        scratch_shapes=scratch,
        compiler_params=pltpu.CompilerParams(
            dimension_semantics=("arbitrary", "arbitrary"), vmem_limit_bytes=VMEM_HGRN
        ),
        name=f"hgrn_l{layer}_t{seq_len}",
    )(*args)
    return (outs[0], None) if carry else (outs[0], outs[1])


_MIX_ORDER = (10, 11, 5, 6, 7, 12, 13, 8, 9)


def _mix_col(k):
    idx = 0
    for n, c in enumerate(_MIX_ORDER):
        idx = idx + jnp.where(k == n, c, 0)
    return idx


def _window_mean_minus_self(p, tpos, seq_len, w):
    n = p.shape[0]
    half = w // 2

    def shifted(x, j):
        valid = (tpos + j >= 0) & (tpos + j < seq_len)
        return jnp.where(valid, pltpu.roll(x, (-j) % n, 0), 0.0)

    ahead, behind, length = p, p, 1
    while length < half:
        ahead = ahead + shifted(ahead, length)
        behind = behind + shifted(behind, -length)
        length *= 2
    acc = ahead + shifted(behind, -1)
    cnt = jnp.minimum(tpos + half, seq_len) - jnp.maximum(tpos - half, 0)
    return acc / cnt.astype(F32) - p


def _mix_kernel(
    x_ref, mod_ref, nmix_ref, w_ref, sgn_ref, sgw_ref, sgb_ref, wbsg_ref, wbpool_ref, poolw_ref, pscale_ref,
    wout_ref, phg_ref, o_ref, hb, u_s, br_s, mrg_s, *, seq_len,
):
    k = pl.program_id(1)
    half = D_MODEL // 2

    @pl.when(k == 0)
    def _():
        hb[...] = _norm_mod(x_ref[...], nmix_ref[...], mod_ref[0, 1:2, :], mod_ref[0, 0:1, :])

    def for_z_parts(consume, part=PART_ROWS):
        w = w_ref[...].astype(BF16)
        n_parts = ROWS // part

        def zdot(p):
            return _dot(hb[p * part : (p + 1) * part, :], w)

        pending = zdot(0)
        for p in range(n_parts):
            z = pending
            if p + 1 < n_parts:
                pending = zdot(p + 1)
            consume(slice(p * part, (p + 1) * part), z)

    for step in (0, 1):

        @pl.when(k == step)
        def _(step=step):
            cols = slice(step * half, (step + 1) * half)

            def gate(rows, z):
                mrg_s[rows, cols] = _sigmoid(z)

            for_z_parts(gate)

    @pl.when(k == 2)
    def _():
        def store_u(rows, z):
            u_s[rows, :] = _gelu_tanh(z)

        for_z_parts(store_u)

    @pl.when(k == 3)
    def _():
        wbsg = wbsg_ref[...].astype(BF16)
        wgs = [sgw_ref[g].astype(BF16) for g in range(SG_GROUPS)]

        def spatial_gating(rows, z):
            v = _rms(_gelu_tanh(z), sgn_ref[...]).astype(BF16)
            for g in range(SG_GROUPS):
                bias = sgb_ref[:, g : g + 1]
                cols = slice(g * SG_GROUP_DIM, (g + 1) * SG_GROUP_DIM)
                for n in range((rows.stop - rows.start) // SG_CHUNK):
                    loc = slice(n * SG_CHUNK, (n + 1) * SG_CHUNK)
                    dst = slice(rows.start + n * SG_CHUNK, rows.start + (n + 1) * SG_CHUNK)
                    mixed = _dot(wgs[g], v[loc, cols]) + bias
                    br_s[dst, cols] = (u_s[dst, cols] * mixed).astype(BF16)
            mrg_s[rows, :] = mrg_s[rows, :] * _dot(br_s[rows, :], wbsg)

        for_z_parts(spatial_gating)

    @pl.when(k == 4)
    def _():
        part = _part_rows(seq_len)
        tpos = lax.broadcasted_iota(jnp.int32, (part, POOL_GROUP_DIM), 0) & (seq_len - 1)

        def pool(rows, z):
            for gi, w in enumerate(POOL_WINDOWS):
                cols = slice(gi * POOL_GROUP_DIM, (gi + 1) * POOL_GROUP_DIM)
                pooled = _window_mean_minus_self(z[:, cols], tpos, seq_len, w)
                out = _dot(pooled.astype(BF16), poolw_ref[gi].astype(BF16)) * pscale_ref[:, cols]
                br_s[rows, cols] = out.astype(BF16)

        for_z_parts(pool, part)

    for step in (5, 6):

        @pl.when(k == step)
        def _(step=step):
            cols = slice((step - 5) * half, (step - 4) * half)
            wbpool = wbpool_ref[:, cols].astype(BF16)

            def gate(rows, z):
                mrg_s[rows, cols] = mrg_s[rows, cols] + _sigmoid(z) * _dot(br_s[rows, :], wbpool)

            for_z_parts(gate)

    @pl.when(k == 7)
    def _():
        cols = slice(0, half)

        def gate(rows, z):
            mrg_s[rows, cols] = mrg_s[rows, cols] + _sigmoid(z) * phg_ref[rows, cols]

        for_z_parts(gate)

    @pl.when(k == 8)
    def _():
        cols = slice(half, D_MODEL)
        wout = wout_ref[...].astype(BF16)

        def gate_and_project(rows, z):
            mrg_s[rows, cols] = mrg_s[rows, cols] + _sigmoid(z) * phg_ref[rows, cols]
            y = _dot(mrg_s[rows, :].astype(BF16), wout)
            o_ref[rows, :] = x_ref[rows, :] + mod_ref[0, 2:3, :] * y

        for_z_parts(gate_and_project)


def _mix_call(x, phg, mod, seq_len, layer, norm_mix, w_in, sg_norm, sg_w, sg_b, w_branch_sg, w_branch_pool, pool_w,
              pool_scale, w_out):
    n_tok = x.shape[0]
    nb = n_tok // ROWS
    per_seq_mod = mod.shape[0] > 1
    const = pl.Buffered(1)
    assert seq_len & (seq_len - 1) == 0 and ROWS % seq_len == 0 and seq_len % SG_CHUNK == 0
    in_specs = [
        pl.BlockSpec((ROWS, D_MODEL), lambda i, k: (i, 0)),
        pl.BlockSpec((1, N_MOD, D_MODEL), (lambda i, k: (i, 0, 0)) if per_seq_mod else (lambda i, k: (0, 0, 0))),
        pl.BlockSpec((None, 1, D_MODEL), lambda i, k: (layer, 0, 0)),
        pl.BlockSpec((None, D_MODEL, IN_CHUNK), lambda i, k: (layer, 0, _mix_col(k))),
        pl.BlockSpec((None, 1, SG_WIDTH), lambda i, k: (layer, 0, 0)),
        pl.BlockSpec((None, SG_GROUPS, SG_CHUNK, SG_CHUNK), lambda i, k: (layer, 0, 0, 0)),
        pl.BlockSpec((None, SG_CHUNK, SG_GROUPS), lambda i, k: (layer, 0, 0)),
        pl.BlockSpec((None, SG_WIDTH, D_MODEL), lambda i, k: (layer, 0, 0), pipeline_mode=const),
        pl.BlockSpec((None, POOL_WIDTH, D_MODEL), lambda i, k: (layer, 0, 0), pipeline_mode=const),
        pl.BlockSpec((None, len(POOL_WINDOWS), POOL_GROUP_DIM, POOL_GROUP_DIM), lambda i, k: (layer, 0, 0, 0)),
        pl.BlockSpec((None, 1, POOL_WIDTH), lambda i, k: (layer, 0, 0)),
        pl.BlockSpec((None, D_MODEL, D_MODEL), lambda i, k: (layer, 0, 0), pipeline_mode=const),
        pl.BlockSpec((ROWS, D_MODEL), lambda i, k: (i, 0)),
    ]
    args = [
        x, mod, norm_mix.reshape(DEPTH, 1, D_MODEL), w_in, sg_norm.reshape(DEPTH, 1, SG_WIDTH), sg_w,
        jnp.swapaxes(sg_b, 1, 2), w_branch_sg, w_branch_pool, pool_w, pool_scale.reshape(DEPTH, 1, POOL_WIDTH),
        w_out, phg,
    ]
    scratch = [
        pltpu.VMEM((ROWS, D_MODEL), BF16),
        pltpu.VMEM((ROWS, SG_WIDTH), F32),
        pltpu.VMEM((ROWS, SG_WIDTH), BF16),
        pltpu.VMEM((ROWS, D_MODEL), F32),
    ]
    return pl.pallas_call(
        functools.partial(_mix_kernel, seq_len=seq_len),
        grid=(nb, len(_MIX_ORDER)),
        in_specs=in_specs,
        out_specs=pl.BlockSpec((ROWS, D_MODEL), lambda i, k: (i, 0)),
        out_shape=jax.ShapeDtypeStruct((n_tok, D_MODEL), F32),
        scratch_shapes=scratch,
        compiler_params=pltpu.CompilerParams(
            dimension_semantics=("arbitrary", "arbitrary"), vmem_limit_bytes=VMEM_MIX
        ),
        name=f"mix_l{layer}_t{seq_len}",
    )(*args)


def _ffn_kernel(x_ref, mod_ref, nffn_ref, *refs, seq_len, final):
    sets = [refs[7 * s : 7 * s + 7] for s in range(FF_SETS)]
    fin_ref, o_ref, hb, acc = refs[7 * FF_SETS :]
    c = pl.program_id(1)
    n_chunks = D_FF // FF_CHUNK
    n_steps = pl.cdiv(n_chunks, FF_SETS)

    part = ROWS
    tpos = lax.broadcasted_iota(jnp.int32, (part, FF_CHUNK), 0) & (seq_len - 1)
    has_prev = tpos >= 1
    has_next = tpos < seq_len - 1

    def conv(h, cw_ref, cb_ref):
        prev = jnp.where(has_prev, pltpu.roll(h, 1, 0), 0.0)
        nxt = jnp.where(has_next, pltpu.roll(h, part - 1, 0), 0.0)
        return prev * cw_ref[0:1, :] + h * cw_ref[1:2, :] + nxt * cw_ref[2:3, :] + cb_ref[...]

    def run(n_sets, first, last):
        ws = [(wa[...].astype(BF16), wb[...].astype(BF16), wd[...].astype(BF16)) for wa, wb, _, _, _, _, wd in sets]
        items = [(p, s) for p in range(ROWS // part) for s in range(n_sets)]

        def up(item):
            p, s = item
            rows = slice(p * part, (p + 1) * part)
            if first and s == 0:
                hb[rows, :] = _norm_mod(x_ref[rows, :], nffn_ref[...], mod_ref[0, 4:5, :], mod_ref[0, 3:4, :])
            h = hb[rows, :]
            return _dot(h, ws[s][0]), _dot(h, ws[s][1])

        pending = up(items[0])
        down = None
        for i, (p, s) in enumerate(items):
            ha, hb2 = pending
            if i + 1 < len(items):
                pending = up(items[i + 1])
            _, _, cwa_ref, cwb_ref, cba_ref, cbb_ref, _ = sets[s]
            a = conv(ha, cwa_ref, cba_ref)
            b = conv(hb2, cwb_ref, cbb_ref)
            d = _dot((_silu(a) * b).astype(BF16), ws[s][2])
            down = d if down is None else down + d
            if s == n_sets - 1:
                rows = slice(p * part, (p + 1) * part)
                total = down if first else acc[rows, :] + down
                if last:
                    y = x_ref[rows, :] + mod_ref[0, 5:6, :] * total
                    o_ref[rows, :] = _rms(y, fin_ref[...]) if final else y
                else:
                    acc[rows, :] = total
                down = None

    kinds = {}
    for step in range(n_steps):
        n_sets = len([s for s in range(FF_SETS) if step + s * n_steps < n_chunks])
        kinds.setdefault((n_sets, step == 0, step == n_steps - 1), []).append(step)
    for (n_sets, first, last), steps in kinds.items():
        cond = functools.reduce(lambda u, v: u | v, [c == st for st in steps])

        @pl.when(cond)
        def _(n_sets=n_sets, first=first, last=last):
            run(n_sets, first, last)


def _ffn_call(x, mod, seq_len, layer, final, norm_ffn, ffn_up, ffn_conv_w, ffn_conv_b, ffn_down, final_norm):
    n_tok = x.shape[0]
    nb = n_tok // ROWS
    nc = D_FF // FF_CHUNK
    n_steps = pl.cdiv(nc, FF_SETS)
    per_seq_mod = mod.shape[0] > 1
    conv_b = ffn_conv_b.reshape(DEPTH, 1, 2 * D_FF)
    in_specs = [
        pl.BlockSpec((ROWS, D_MODEL), lambda i, c: (i, 0)),
        pl.BlockSpec((1, N_MOD, D_MODEL), (lambda i, c: (i, 0, 0)) if per_seq_mod else (lambda i, c: (0, 0, 0))),
        pl.BlockSpec((None, 1, D_MODEL), lambda i, c: (layer, 0, 0)),
    ]
    args = [x, mod, norm_ffn.reshape(DEPTH, 1, D_MODEL)]
    for s in range(FF_SETS):
        chunk = lambda c, s=s: jnp.minimum(c + s * n_steps, nc - 1)
        in_specs += [
            pl.BlockSpec((None, D_MODEL, FF_CHUNK), lambda i, c, f=chunk: (layer, 0, f(c))),
            pl.BlockSpec((None, D_MODEL, FF_CHUNK), lambda i, c, f=chunk: (layer, 0, nc + f(c))),
            pl.BlockSpec((None, 3, FF_CHUNK), lambda i, c, f=chunk: (layer, 0, f(c))),
            pl.BlockSpec((None, 3, FF_CHUNK), lambda i, c, f=chunk: (layer, 0, nc + f(c))),
            pl.BlockSpec((None, 1, FF_CHUNK), lambda i, c, f=chunk: (layer, 0, f(c))),
            pl.BlockSpec((None, 1, FF_CHUNK), lambda i, c, f=chunk: (layer, 0, nc + f(c))),
            pl.BlockSpec((None, FF_CHUNK, D_MODEL), lambda i, c, f=chunk: (layer, f(c), 0)),
        ]
        args += [ffn_up, ffn_up, ffn_conv_w, ffn_conv_w, conv_b, conv_b, ffn_down]
    in_specs.append(pl.BlockSpec((1, D_MODEL), lambda i, c: (0, 0)))
    args.append(final_norm.reshape(1, D_MODEL))
    return pl.pallas_call(
        functools.partial(_ffn_kernel, seq_len=seq_len, final=final),
        grid=(nb, n_steps),
        in_specs=in_specs,
        out_specs=pl.BlockSpec((ROWS, D_MODEL), lambda i, c: (i, 0)),
        out_shape=jax.ShapeDtypeStruct((n_tok, D_MODEL), F32),
        scratch_shapes=[pltpu.VMEM((ROWS, D_MODEL), BF16), pltpu.VMEM((ROWS, D_MODEL), F32)],
        compiler_params=pltpu.CompilerParams(
            dimension_semantics=("arbitrary", "arbitrary"), vmem_limit_bytes=VMEM_FFN
        ),
        name=f"ffn_l{layer}_t{seq_len}",
    )(*args)


def kernel(x_prompt, x_sample, c, state_hgrn, c_ctx, norm_mix, norm_ffn, w_ada, b_ada, w_in, lb_logits, hg_norm,
           w_branch_hg, w_branch_sg, w_branch_pool, w_out, sg_norm, sg_w, sg_b, pool_w, pool_scale, ffn_up,
           ffn_conv_w, ffn_conv_b, ffn_down, final_norm):
    n_ctx, t_ctx, _ = x_prompt.shape
    n_lat, t_lat, _ = x_sample.shape

    n_cond = 1 + n_lat
    pad = -n_cond % V7X_SUBLANES
    cvec = jnp.concatenate([c_ctx[None, :], c, jnp.zeros((pad, D_MODEL), F32)], axis=0)
    mod = _mod_call(cvec, w_ada, b_ada).reshape(DEPTH, n_cond + pad, N_MOD, D_MODEL)

    xs = _addpos_call(x_sample, _grid_pos_embed(t_lat)).reshape(n_lat * t_lat, D_MODEL)
    xp = x_prompt.reshape(n_ctx * t_ctx, D_MODEL)
    state0 = state_hgrn.reshape(n_lat * DEPTH * 2 * HG_HEADS, HG_DK, HG_DV)

    states = None
    for layer in range(DEPTH):
        final = layer == DEPTH - 1
        groups = []
        for x, m, t, s0 in ((xp, mod[layer, 0:1], t_ctx, None), (xs, mod[layer, 1:n_cond], t_lat, state0)):
            phg, s_fin = _hgrn_call(x, m, t, layer, norm_mix, w_in, lb_logits, hg_norm, w_branch_hg, s0, states)
            x1 = _mix_call(x, phg, m, t, layer, norm_mix, w_in, sg_norm, sg_w, sg_b, w_branch_sg, w_branch_pool,
                           pool_w, pool_scale, w_out)
            x2 = _ffn_call(x1, m, t, layer, final, norm_ffn, ffn_up, ffn_conv_w, ffn_conv_b, ffn_down, final_norm)
            groups.append((x2, s_fin))
        (xp, states), (xs, _) = groups

    y_prompt = xp.reshape(x_prompt.shape)
    y_sample = xs.reshape(x_sample.shape)
    return (y_prompt, y_sample, states)
```

```python
import functools

import jax
import jax.numpy as jnp
import numpy as np
from jax import lax
from jax.experimental import pallas as pl
from jax.experimental.pallas import tpu as pltpu

D_MODEL = 1024
DEPTH = 2
GRID_W = 64
POS_BASE = 10000.0
EPS = 1e-6
HG_HEADS = 4
HG_DK = 128
HG_DV = 128
HG_WIDTH = HG_HEADS * HG_DV
SG_GROUPS = 4
SG_WIDTH = 512
SG_GROUP_DIM = SG_WIDTH // SG_GROUPS
SG_CHUNK = 128
POOL_WINDOWS = (2, 4, 8, 16)
POOL_WIDTH = 512
POOL_GROUP_DIM = POOL_WIDTH // len(POOL_WINDOWS)
IN_COLS = 5 * HG_WIDTH + 2 * SG_WIDTH + POOL_WIDTH + 3 * D_MODEL
D_FF = 2816
N_MOD = 6

V7X_LANES = 128
V7X_SUBLANES = 8
V7X_MXU_DIM = 256
MIB = 2**20

ROWS = 1024
PART_ROWS = 512
MIX_PART = 256
HG_BLOCK = 256
HG_LEVELS = 8
IN_CHUNK = 512
FF_CHUNK = V7X_MXU_DIM
FF_SETS = 3
MOD_CHUNK = 3072
VMEM_SMALL = 32 * MIB
VMEM_HGRN = 48 * MIB
VMEM_MIX = 56 * MIB
VMEM_FFN = 52 * MIB

NEG_LOG2E = -1.4426950408889634

F32 = jnp.float32
BF16 = jnp.bfloat16


def _dot(a, b):
    return lax.dot_general(a, b, (((1,), (0,)), ((), ())), preferred_element_type=F32)


def _dot_nt(a, b):
    return lax.dot_general(a, b, (((1,), (1,)), ((), ())), preferred_element_type=F32)


def _dot_tn(a, b):
    return lax.dot_general(a, b, (((0,), (0,)), ((), ())), preferred_element_type=F32)


def _sigmoid(x):
    return 0.5 * jnp.tanh(0.5 * x) + 0.5


def _silu(x):
    return x * _sigmoid(x)


_GELU_C1 = 0.7978845608028654
_GELU_C2 = _GELU_C1 * 0.044715


def _gelu_tanh(x):
    half_x = 0.5 * x
    return half_x + half_x * jnp.tanh(x * (_GELU_C1 + _GELU_C2 * (x * x)))


def _rms(x, gain):
    return x * lax.rsqrt(jnp.mean(x * x, axis=-1, keepdims=True) + EPS) * gain


def _log1pexp(y):
    return jnp.maximum(y, 0.0) + jnp.log(1.0 + jnp.exp(-jnp.abs(y)))


def _norm_mod(x, gain, scale, shift):
    return (_rms(x, gain) * (1.0 + scale) + shift).astype(BF16)


def _part_rows(seq_len):
    return max(seq_len, PART_ROWS)


def _mod_kernel(c_ref, w_ref, b_ref, o_ref):
    c = _silu(c_ref[...]).astype(BF16)
    o_ref[...] = _dot(c, w_ref[...].astype(BF16)) + b_ref[...]


def _mod_call(cvec, w_ada, b_ada):
    n_rows = cvec.shape[0]
    n_cols = N_MOD * D_MODEL
    return pl.pallas_call(
        _mod_kernel,
        grid=(DEPTH, n_cols // MOD_CHUNK),
        in_specs=[
            pl.BlockSpec((n_rows, D_MODEL), lambda l, n: (0, 0)),
            pl.BlockSpec((None, D_MODEL, MOD_CHUNK), lambda l, n: (l, 0, n)),
            pl.BlockSpec((None, 1, MOD_CHUNK), lambda l, n: (l, 0, n)),
        ],
        out_specs=pl.BlockSpec((None, n_rows, MOD_CHUNK), lambda l, n: (l, 0, n)),
        out_shape=jax.ShapeDtypeStruct((DEPTH, n_rows, n_cols), F32),
        compiler_params=pltpu.CompilerParams(
            dimension_semantics=("arbitrary", "arbitrary"), vmem_limit_bytes=VMEM_SMALL
        ),
        name="adaln_mod",
    )(cvec, w_ada, b_ada.reshape(DEPTH, 1, n_cols))


def _addpos_kernel(x_ref, p_ref, o_ref):
    o_ref[...] = x_ref[...] + p_ref[...]


def _addpos_call(x, pos):
    b, t, d = x.shape
    return pl.pallas_call(
        _addpos_kernel,
        grid=(b,),
        in_specs=[pl.BlockSpec((None, t, d), lambda i: (i, 0, 0)), pl.BlockSpec((t, d), lambda i: (0, 0))],
        out_specs=pl.BlockSpec((None, t, d), lambda i: (i, 0, 0)),
        out_shape=jax.ShapeDtypeStruct(x.shape, x.dtype),
        compiler_params=pltpu.CompilerParams(dimension_semantics=("arbitrary",), vmem_limit_bytes=VMEM_SMALL),
        name="add_pos",
    )(x, pos)


def _grid_pos_embed(n_tokens):
    rows = n_tokens // GRID_W
    r = np.broadcast_to(np.arange(rows, dtype=np.float32)[:, None], (rows, GRID_W)).reshape(-1)
    col = np.broadcast_to(np.arange(GRID_W, dtype=np.float32)[None, :], (rows, GRID_W)).reshape(-1)
    quarter = D_MODEL // 4
    omega = (1.0 / (np.float32(POS_BASE) ** (np.arange(quarter, dtype=np.float32) / quarter))).astype(np.float32)
    ar = r[:, None] * omega[None, :]
    ac = col[:, None] * omega[None, :]
    return jnp.asarray(np.concatenate([np.sin(ar), np.cos(ar), np.sin(ac), np.cos(ac)], axis=-1), F32)


def _ref_rows(b, blk, r):
    n, c = b.shape
    if blk >= V7X_SUBLANES:
        x3 = b.reshape(n // blk, blk, c)
        return jnp.broadcast_to(x3[:, r : r + 1, :], x3.shape).reshape(n, c)
    x3 = b.reshape(n // V7X_SUBLANES, V7X_SUBLANES, c)
    sub = lax.broadcasted_iota(jnp.int32, x3.shape, 1)
    bases = list(range(0, V7X_SUBLANES, blk))
    out = jnp.broadcast_to(x3[:, bases[-1] + r : bases[-1] + r + 1, :], x3.shape)
    for base in reversed(bases[:-1]):
        out = jnp.where(sub < base + blk, jnp.broadcast_to(x3[:, base + r : base + r + 1, :], x3.shape), out)
    return out.reshape(n, c)


def _cum_logdecay(lf, tri):
    hi = lf.astype(BF16)
    r1 = lf - hi.astype(F32)
    mid = r1.astype(BF16)
    lo = (r1 - mid.astype(F32)).astype(BF16)
    return _dot(tri, hi) + _dot(tri, mid) + _dot(tri, lo)


def _hgrn_block(qs, ks, vs, bs, lvqs, sts):
    half = HG_BLOCK // 2
    lo, hi = slice(0, half), slice(half, HG_BLOCK)
    dirs = (True, False)
    vbs = [v.astype(BF16) for v in vs]
    qbs = [q.astype(BF16) for q in qs]
    kbs = [k.astype(BF16) for k in ks]
    diags = [[jnp.where(lvq == 0, _dot_nt(qb[h], kb[h]), 0.0) for h in (lo, hi)] for qb, kb, lvq in zip(qbs, kbs, lvqs)]
    for m in range(1, HG_LEVELS):
        blk = 2**m
        for d, forward in enumerate(dirs):
            b = bs[d]
            ref = _ref_rows(b, blk, blk // 2 - 1 if forward else blk // 2)
            e = jnp.exp2(jnp.abs(b - ref) * NEG_LOG2E).astype(BF16)
            qt, kt = qbs[d] * e, kbs[d] * e
            diags[d] = [jnp.where(lvqs[d] == m, _dot_nt(qt[h], kt[h]), a) for h, a in zip((lo, hi), diags[d])]
    outs = []
    for d, forward in enumerate(dirs):
        b, k, st, q, vb = bs[d], ks[d], sts[d], qs[d], vbs[d]
        mid = half - 1 if forward else half
        e = jnp.exp2(jnp.abs(b - b[mid : mid + 1, :]) * NEG_LOG2E).astype(BF16)
        qt, kt = qbs[d] * e, kbs[d] * e
        a_lo, a_hi = (a.astype(BF16) for a in diags[d])
        if forward:
            cross = _dot_nt(qt[hi], kt[lo]).astype(BF16)
            o = jnp.concatenate([_dot(a_lo, vb[lo]), _dot(cross, vb[lo]) + _dot(a_hi, vb[hi])], axis=0)
        else:
            cross = _dot_nt(qt[lo], kt[hi]).astype(BF16)
            o = jnp.concatenate([_dot(a_lo, vb[lo]) + _dot(cross, vb[hi]), _dot(a_hi, vb[hi])], axis=0)
        edge = b[HG_BLOCK - 1 : HG_BLOCK, :] if forward else b[0:1, :]
        k_end = (k * jnp.exp(edge - b)).astype(BF16)
        st_new = _dot_tn(vb, k_end)
        if st is not None:
            o = o + _dot_nt((q * jnp.exp(b)).astype(BF16), st.astype(BF16))
            st_new = st_new + st * jnp.exp(edge)
        outs.append((o, st_new))
    return outs


def _hgrn_kernel(*refs, layer, seq_len):
    carry = seq_len > HG_BLOCK
    it = iter(refs)
    x_ref, mod_ref, nmix_ref = next(it), next(it), next(it)
    w_refs = [next(it) for _ in range(5)]
    lbl_ref, hgn_ref, wbr_ref = next(it), next(it), next(it)
    s0_refs = [next(it), next(it)] if carry else None
    sprev_ref = next(it) if (not carry and layer > 0) else None
    phg_ref = next(it)
    sout_ref = None if carry else next(it)
    hb, wcat, q_s, v_s, g_s, kf_s, kb_s, bf_s, bb_s, o_s, y_s, lvf_s, lvb_s, trif_s, trib_s, st_s = it

    j = pl.program_id(1)

    def head(first):
        if first:
            t = lax.broadcasted_iota(jnp.int32, (HG_BLOCK // 2, HG_BLOCK // 2), 0)
            s = lax.broadcasted_iota(jnp.int32, (HG_BLOCK // 2, HG_BLOCK // 2), 1)
            x = t ^ s
            lv = jnp.zeros_like(x)
            for m in range(HG_LEVELS - 1):
                lv = lv + (x >= 2**m).astype(jnp.int32)
            lvf_s[...] = jnp.where(t >= s, lv, -1)
            lvb_s[...] = jnp.where(t <= s, lv, -1)
            t = lax.broadcasted_iota(jnp.int32, (HG_BLOCK, HG_BLOCK), 0)
            s = lax.broadcasted_iota(jnp.int32, (HG_BLOCK, HG_BLOCK), 1)
            trif_s[...] = (t >= s).astype(BF16)
            trib_s[...] = (t <= s).astype(BF16)
        for g, w_ref in enumerate(w_refs):
            wcat[:, g * HG_DK : (g + 1) * HG_DK] = w_ref[...].astype(BF16)
        a0, a1 = lbl_ref[0], lbl_ref[1]
        amax = jnp.maximum(a0, a1)
        e0, e1 = jnp.exp(a0 - amax), jnp.exp(a1 - amax)
        p0, p1 = e0 / (e0 + e1), e1 / (e0 + e1)
        lb = (p0 - p0) if layer == 0 else ((p0 + p1) - p0)
        log_lb = jnp.log(lb)

        wc = wcat[...]
        part = PART_ROWS
        n_parts = ROWS // part

        def zdot(p):
            rows = slice(p * part, (p + 1) * part)
            if first:
                hb[rows, :] = _norm_mod(x_ref[rows, :], nmix_ref[...], mod_ref[0, 1:2, :], mod_ref[0, 0:1, :])
            return _dot(hb[rows, :], wc)

        pending = zdot(0)
        for p in range(n_parts):
            z = pending
            if p + 1 < n_parts:
                pending = zdot(p + 1)
            rows = slice(p * part, (p + 1) * part)
            zq, zff, zfb, zi, zg = (z[:, g * HG_DK : (g + 1) * HG_DK] for g in range(5))
            q_s[rows, :] = _silu(zq) * HG_DK**-0.5
            v_s[rows, :] = zi
            g_s[rows, :] = _silu(zg)
            for d, (zf, k_s, b_s, tri_s) in enumerate(((zff, kf_s, bf_s, trif_s), (zfb, kb_s, bb_s, trib_s))):
                lf = _log1pexp(log_lb[d : d + 1, :] - zf) - _log1pexp(-zf)
                k_s[rows, :] = (1.0 - lb[d : d + 1, :]) * _sigmoid(-zf)
                tri = tri_s[...]
                for n in range(part // HG_BLOCK):
                    loc = slice(n * HG_BLOCK, (n + 1) * HG_BLOCK)
                    dst = slice(rows.start + n * HG_BLOCK, rows.start + (n + 1) * HG_BLOCK)
                    b_s[dst, :] = _cum_logdecay(lf[loc, :], tri)

        o_s[...] = jnp.zeros_like(o_s)
        n_blk = ROWS // HG_BLOCK
        if sprev_ref is not None:
            sout_ref[:, 0:layer] = sprev_ref[...]
        if carry:
            for d in range(2):
                st_s[d] = s0_refs[d][0].T

        def blocks(n, c):
            blks = (n, n_blk - 1 - n if carry else n)
            rows = [pl.ds(pl.multiple_of(blk * HG_BLOCK, HG_BLOCK), HG_BLOCK) for blk in blks]
            outs = _hgrn_block(
                [q_s[r, :] for r in rows], [kf_s[rows[0], :], kb_s[rows[1], :]], [v_s[r, :] for r in rows],
                [bf_s[rows[0], :], bb_s[rows[1], :]], [lvf_s[...], lvb_s[...]],
                [st_s[d] if carry else None for d in range(2)],
            )
            for d, (o, st_new) in enumerate(outs):
                o_s[rows[d], :] += o
                if carry:
                    st_s[d] = st_new
                else:
                    sout_ref[blks[d], layer, d, 0] = st_new.T
            return c

        lax.fori_loop(0, n_blk, blocks, 0, unroll=True)

        y_s[j] = (_rms(o_s[...], hgn_ref[...]) * g_s[...]).astype(BF16)

    @pl.when(j == 0)
    def _():
        head(True)

    @pl.when(j > 0)
    def _():
        head(False)

    @pl.when(j == HG_HEADS - 1)
    def _():
        y = jnp.concatenate([y_s[h] for h in range(HG_HEADS)], axis=1)
        phg_ref[...] = _dot(y, wbr_ref[...].astype(BF16))


def _hgrn_call(x, mod, seq_len, layer, norm_mix, w_in, lb_logits, hg_norm, w_branch_hg, state0, prev_states):
    n_tok = x.shape[0]
    nb = n_tok // ROWS
    carry = seq_len > HG_BLOCK
    per_seq_mod = mod.shape[0] > 1

    in_specs = [
        pl.BlockSpec((ROWS, D_MODEL), lambda i, j: (i, 0)),
        pl.BlockSpec((1, N_MOD, D_MODEL), (lambda i, j: (i, 0, 0)) if per_seq_mod else (lambda i, j: (0, 0, 0))),
        pl.BlockSpec((None, 1, D_MODEL), lambda i, j: (layer, 0, 0)),
    ]
    args = [x, mod, norm_mix.reshape(DEPTH, 1, D_MODEL)]
    for g in range(5):
        in_specs.append(pl.BlockSpec((None, D_MODEL, HG_DK), lambda i, j, g=g: (layer, 0, g * HG_HEADS + j)))
        args.append(w_in)
    in_specs += [
        pl.BlockSpec((DEPTH, 2, HG_DK), lambda i, j: (0, 0, j)),
        pl.BlockSpec((None, 1, HG_DV), lambda i, j: (layer, 0, 0)),
        pl.BlockSpec((None, HG_WIDTH, D_MODEL), lambda i, j: (layer, 0, 0), pipeline_mode=pl.Buffered(1)),
    ]
    args += [lb_logits, hg_norm.reshape(DEPTH, 1, HG_DV), w_branch_hg]
    if carry:
        assert seq_len == ROWS
        for d in range(2):
            in_specs.append(
                pl.BlockSpec((1, HG_DK, HG_DV), lambda i, j, d=d: (((i * DEPTH + layer) * 2 + d) * HG_HEADS + j, 0, 0))
            )
            args.append(state0)

    out_shape = [jax.ShapeDtypeStruct((n_tok, D_MODEL), F32)]
    out_specs = [pl.BlockSpec((ROWS, D_MODEL), lambda i, j: (i, 0))]
    if not carry:
        assert seq_len == HG_BLOCK
        n_seq = n_tok // seq_len
        seqs = ROWS // seq_len
        if layer > 0:
            in_specs.append(pl.BlockSpec((seqs, layer, 2, 1, HG_DK, HG_DV), lambda i, j: (i, 0, 0, j, 0, 0)))
            args.append(prev_states)
        out_shape.append(jax.ShapeDtypeStruct((n_seq, layer + 1, 2, HG_HEADS, HG_DK, HG_DV), F32))
        out_specs.append(pl.BlockSpec((seqs, layer + 1, 2, 1, HG_DK, HG_DV), lambda i, j: (i, 0, 0, j, 0, 0)))

    head = lambda dt=F32: pltpu.VMEM((ROWS, HG_DK), dt)
    scratch = [
        pltpu.VMEM((ROWS, D_MODEL), BF16),
        pltpu.VMEM((D_MODEL, 5 * HG_DK), BF16),
        head(), head(), head(),
        head(), head(), head(), head(),
        head(),
        pltpu.VMEM((HG_HEADS, ROWS, HG_DV), BF16),
        pltpu.VMEM((HG_BLOCK // 2, HG_BLOCK // 2), jnp.int32),
        pltpu.VMEM((HG_BLOCK // 2, HG_BLOCK // 2), jnp.int32),
        pltpu.VMEM((HG_BLOCK, HG_BLOCK), BF16),
        pltpu.VMEM((HG_BLOCK, HG_BLOCK), BF16),
        pltpu.VMEM((2, HG_DV, HG_DK), F32),
    ]
    outs = pl.pallas_call(
        functools.partial(_hgrn_kernel, layer=layer, seq_len=seq_len),
        grid=(nb, HG_HEADS),
        in_specs=in_specs,
        out_specs=out_specs,
        out_shape=out_shape,
        scratch_shapes=scratch,
        compiler_params=pltpu.CompilerParams(
            dimension_semantics=("arbitrary", "arbitrary"), vmem_limit_bytes=VMEM_HGRN
        ),
        name=f"hgrn_l{layer}_t{seq_len}",
    )(*args)
    return (outs[0], None) if carry else (outs[0], outs[1])


_MIX_ORDER = (10, 11, 5, 6, 7, 12, 13, 8, 9)


def _mix_col(k):
    idx = 0
    for n, c in enumerate(_MIX_ORDER):
        idx = idx + jnp.where(k == n, c, 0)
    return idx


def _window_mean_minus_self(p, tpos, seq_len, w):
    n = p.shape[0]
    half = w // 2

    def shifted(x, j):
        valid = (tpos + j >= 0) & (tpos + j < seq_len)
        return jnp.where(valid, pltpu.roll(x, (-j) % n, 0), 0.0)

    ahead, behind, length = p, p, 1
    while length < half:
        ahead = ahead + shifted(ahead, length)
        behind = behind + shifted(behind, -length)
        length *= 2
    acc = ahead + shifted(behind, -1)
    cnt = jnp.minimum(tpos + half, seq_len) - jnp.maximum(tpos - half, 0)
    return acc / cnt.astype(F32) - p


def _mix_kernel(
    x_ref, mod_ref, nmix_ref, w_ref, sgn_ref, sgw_ref, sgb_ref, wbsg_ref, wbpool_ref, poolw_ref, pscale_ref,
    wout_ref, phg_ref, o_ref, hb, u_s, br_s, mrg_s, *, seq_len,
):
    k = pl.program_id(1)
    half = D_MODEL // 2

    @pl.when(k == 0)
    def _():
        hb[...] = _norm_mod(x_ref[...], nmix_ref[...], mod_ref[0, 1:2, :], mod_ref[0, 0:1, :])

    def for_z_parts(consume, part=MIX_PART):
        w = w_ref[...].astype(BF16)
        n_parts = ROWS // part

        def zdot(p):
            return _dot(hb[p * part : (p + 1) * part, :], w)

        pending = zdot(0)
        for p in range(n_parts):
            z = pending
            if p + 1 < n_parts:
                pending = zdot(p + 1)
            consume(slice(p * part, (p + 1) * part), z)

    for step in (0, 1):

        @pl.when(k == step)
        def _(step=step):
            cols = slice(step * half, (step + 1) * half)

            def gate(rows, z):
                mrg_s[rows, cols] = _sigmoid(z)

            for_z_parts(gate)

    @pl.when(k == 2)
    def _():
        def store_u(rows, z):
            u_s[rows, :] = _gelu_tanh(z)

        for_z_parts(store_u)

    @pl.when(k == 3)
    def _():
        wbsg = wbsg_ref[...].astype(BF16)
        wgs = [sgw_ref[g].astype(BF16) for g in range(SG_GROUPS)]

        def spatial_gating(rows, z):
            v = _rms(_gelu_tanh(z), sgn_ref[...]).astype(BF16)
            for g in range(SG_GROUPS):
                bias = sgb_ref[:, g : g + 1]
                cols = slice(g * SG_GROUP_DIM, (g + 1) * SG_GROUP_DIM)
                for n in range((rows.stop - rows.start) // SG_CHUNK):
                    loc = slice(n * SG_CHUNK, (n + 1) * SG_CHUNK)
                    dst = slice(rows.start + n * SG_CHUNK, rows.start + (n + 1) * SG_CHUNK)
                    mixed = _dot(wgs[g], v[loc, cols]) + bias
                    br_s[dst, cols] = (u_s[dst, cols] * mixed).astype(BF16)
            mrg_s[rows, :] = mrg_s[rows, :] * _dot(br_s[rows, :], wbsg)

        for_z_parts(spatial_gating, PART_ROWS)

    @pl.when(k == 4)
    def _():
        part = _part_rows(seq_len)
        tpos = lax.broadcasted_iota(jnp.int32, (part, POOL_GROUP_DIM), 0) & (seq_len - 1)

        def pool(rows, z):
            for gi, w in enumerate(POOL_WINDOWS):
                cols = slice(gi * POOL_GROUP_DIM, (gi + 1) * POOL_GROUP_DIM)
                pooled = _window_mean_minus_self(z[:, cols], tpos, seq_len, w)
                out = _dot(pooled.astype(BF16), poolw_ref[gi].astype(BF16)) * pscale_ref[:, cols]
                br_s[rows, cols] = out.astype(BF16)

        for_z_parts(pool, part)

    for step in (5, 6):

        @pl.when(k == step)
        def _(step=step):
            cols = slice((step - 5) * half, (step - 4) * half)
            wbpool = wbpool_ref[:, cols].astype(BF16)

            def gate(rows, z):
                mrg_s[rows, cols] = mrg_s[rows, cols] + _sigmoid(z) * _dot(br_s[rows, :], wbpool)

            for_z_parts(gate)

    @pl.when(k == 7)
    def _():
        cols = slice(0, half)

        def gate(rows, z):
            mrg_s[rows, cols] = mrg_s[rows, cols] + _sigmoid(z) * phg_ref[rows, cols]

        for_z_parts(gate)

    @pl.when(k == 8)
    def _():
        cols = slice(half, D_MODEL)
        wout = wout_ref[...].astype(BF16)

        def gate_and_project(rows, z):
            mrg_s[rows, cols] = mrg_s[rows, cols] + _sigmoid(z) * phg_ref[rows, cols]
            y = _dot(mrg_s[rows, :].astype(BF16), wout)
            o_ref[rows, :] = x_ref[rows, :] + mod_ref[0, 2:3, :] * y

        for_z_parts(gate_and_project)


def _mix_call(x, phg, mod, seq_len, layer, norm_mix, w_in, sg_norm, sg_w, sg_b, w_branch_sg, w_branch_pool, pool_w,
              pool_scale, w_out):
    n_tok = x.shape[0]
    nb = n_tok // ROWS
    per_seq_mod = mod.shape[0] > 1
    const = pl.Buffered(1)
    assert seq_len & (seq_len - 1) == 0 and ROWS % seq_len == 0 and seq_len % SG_CHUNK == 0
    in_specs = [
        pl.BlockSpec((ROWS, D_MODEL), lambda i, k: (i, 0)),
        pl.BlockSpec((1, N_MOD, D_MODEL), (lambda i, k: (i, 0, 0)) if per_seq_mod else (lambda i, k: (0, 0, 0))),
        pl.BlockSpec((None, 1, D_MODEL), lambda i, k: (layer, 0, 0)),
        pl.BlockSpec((None, D_MODEL, IN_CHUNK), lambda i, k: (layer, 0, _mix_col(k))),
        pl.BlockSpec((None, 1, SG_WIDTH), lambda i, k: (layer, 0, 0)),
        pl.BlockSpec((None, SG_GROUPS, SG_CHUNK, SG_CHUNK), lambda i, k: (layer, 0, 0, 0)),
        pl.BlockSpec((None, SG_CHUNK, SG_GROUPS), lambda i, k: (layer, 0, 0)),
        pl.BlockSpec((None, SG_WIDTH, D_MODEL), lambda i, k: (layer, 0, 0), pipeline_mode=const),
        pl.BlockSpec((None, POOL_WIDTH, D_MODEL), lambda i, k: (layer, 0, 0), pipeline_mode=const),
        pl.BlockSpec((None, len(POOL_WINDOWS), POOL_GROUP_DIM, POOL_GROUP_DIM), lambda i, k: (layer, 0, 0, 0)),
        pl.BlockSpec((None, 1, POOL_WIDTH), lambda i, k: (layer, 0, 0)),
        pl.BlockSpec((None, D_MODEL, D_MODEL), lambda i, k: (layer, 0, 0), pipeline_mode=const),
        pl.BlockSpec((ROWS, D_MODEL), lambda i, k: (i, 0)),
    ]
    args = [
        x, mod, norm_mix.reshape(DEPTH, 1, D_MODEL), w_in, sg_norm.reshape(DEPTH, 1, SG_WIDTH), sg_w,
        jnp.swapaxes(sg_b, 1, 2), w_branch_sg, w_branch_pool, pool_w, pool_scale.reshape(DEPTH, 1, POOL_WIDTH),
        w_out, phg,
    ]
    scratch = [
        pltpu.VMEM((ROWS, D_MODEL), BF16),
        pltpu.VMEM((ROWS, SG_WIDTH), F32),
        pltpu.VMEM((ROWS, SG_WIDTH), BF16),
        pltpu.VMEM((ROWS, D_MODEL), F32),
    ]
    return pl.pallas_call(
        functools.partial(_mix_kernel, seq_len=seq_len),
        grid=(nb, len(_MIX_ORDER)),
        in_specs=in_specs,
        out_specs=pl.BlockSpec((ROWS, D_MODEL), lambda i, k: (i, 0)),
        out_shape=jax.ShapeDtypeStruct((n_tok, D_MODEL), F32),
        scratch_shapes=scratch,
        compiler_params=pltpu.CompilerParams(
            dimension_semantics=("arbitrary", "arbitrary"), vmem_limit_bytes=VMEM_MIX
        ),
        name=f"mix_l{layer}_t{seq_len}",
    )(*args)


def _ffn_kernel(x_ref, mod_ref, nffn_ref, *refs, seq_len, final):
    sets = [refs[7 * s : 7 * s + 7] for s in range(FF_SETS)]
    fin_ref, o_ref, hb, acc = refs[7 * FF_SETS :]
    c = pl.program_id(1)
    n_chunks = D_FF // FF_CHUNK
    n_steps = pl.cdiv(n_chunks, FF_SETS)

    part = ROWS
    tpos = lax.broadcasted_iota(jnp.int32, (part, FF_CHUNK), 0) & (seq_len - 1)
    has_prev = tpos >= 1
    has_next = tpos < seq_len - 1

    def conv(h, cw_ref, cb_ref):
        prev = jnp.where(has_prev, pltpu.roll(h, 1, 0), 0.0)
        nxt = jnp.where(has_next, pltpu.roll(h, part - 1, 0), 0.0)
        return prev * cw_ref[0:1, :] + h * cw_ref[1:2, :] + nxt * cw_ref[2:3, :] + cb_ref[...]

    def run(n_sets, first, last):
        ws = [(wa[...].astype(BF16), wb[...].astype(BF16), wd[...].astype(BF16)) for wa, wb, _, _, _, _, wd in sets]
        items = [(p, s) for p in range(ROWS // part) for s in range(n_sets)]

        def up(item):
            p, s = item
            rows = slice(p * part, (p + 1) * part)
            if first and s == 0:
                hb[rows, :] = _norm_mod(x_ref[rows, :], nffn_ref[...], mod_ref[0, 4:5, :], mod_ref[0, 3:4, :])
            h = hb[rows, :]
            return _dot(h, ws[s][0]), _dot(h, ws[s][1])

        pending = up(items[0])
        down = None
        for i, (p, s) in enumerate(items):
            ha, hb2 = pending
            if i + 1 < len(items):
                pending = up(items[i + 1])
            _, _, cwa_ref, cwb_ref, cba_ref, cbb_ref, _ = sets[s]
            a = conv(ha, cwa_ref, cba_ref)
            b = conv(hb2, cwb_ref, cbb_ref)
            d = _dot((_silu(a) * b).astype(BF16), ws[s][2])
            down = d if down is None else down + d
            if s == n_sets - 1:
                rows = slice(p * part, (p + 1) * part)
                total = down if first else acc[rows, :] + down
                if last:
                    y = x_ref[rows, :] + mod_ref[0, 5:6, :] * total
                    o_ref[rows, :] = _rms(y, fin_ref[...]) if final else y
                else:
                    acc[rows, :] = total
                down = None

    kinds = {}
    for step in range(n_steps):
        n_sets = len([s for s in range(FF_SETS) if step + s * n_steps < n_chunks])
        kinds.setdefault((n_sets, step == 0, step == n_steps - 1), []).append(step)
    for (n_sets, first, last), steps in kinds.items():
        cond = functools.reduce(lambda u, v: u | v, [c == st for st in steps])

        @pl.when(cond)
        def _(n_sets=n_sets, first=first, last=last):
            run(n_sets, first, last)


def _ffn_call(x, mod, seq_len, layer, final, norm_ffn, ffn_up, ffn_conv_w, ffn_conv_b, ffn_down, final_norm):
    n_tok = x.shape[0]
    nb = n_tok // ROWS
    nc = D_FF // FF_CHUNK
    n_steps = pl.cdiv(nc, FF_SETS)
    per_seq_mod = mod.shape[0] > 1
    conv_b = ffn_conv_b.reshape(DEPTH, 1, 2 * D_FF)
    in_specs = [
        pl.BlockSpec((ROWS, D_MODEL), lambda i, c: (i, 0)),
        pl.BlockSpec((1, N_MOD, D_MODEL), (lambda i, c: (i, 0, 0)) if per_seq_mod else (lambda i, c: (0, 0, 0))),
        pl.BlockSpec((None, 1, D_MODEL), lambda i, c: (layer, 0, 0)),
    ]
    args = [x, mod, norm_ffn.reshape(DEPTH, 1, D_MODEL)]
    for s in range(FF_SETS):
        chunk = lambda c, s=s: jnp.minimum(c + s * n_steps, nc - 1)
        in_specs += [
            pl.BlockSpec((None, D_MODEL, FF_CHUNK), lambda i, c, f=chunk: (layer, 0, f(c))),
            pl.BlockSpec((None, D_MODEL, FF_CHUNK), lambda i, c, f=chunk: (layer, 0, nc + f(c))),
            pl.BlockSpec((None, 3, FF_CHUNK), lambda i, c, f=chunk: (layer, 0, f(c))),
            pl.BlockSpec((None, 3, FF_CHUNK), lambda i, c, f=chunk: (layer, 0, nc + f(c))),
            pl.BlockSpec((None, 1, FF_CHUNK), lambda i, c, f=chunk: (layer, 0, f(c))),
            pl.BlockSpec((None, 1, FF_CHUNK), lambda i, c, f=chunk: (layer, 0, nc + f(c))),
            pl.BlockSpec((None, FF_CHUNK, D_MODEL), lambda i, c, f=chunk: (layer, f(c), 0)),
        ]
        args += [ffn_up, ffn_up, ffn_conv_w, ffn_conv_w, conv_b, conv_b, ffn_down]
    in_specs.append(pl.BlockSpec((1, D_MODEL), lambda i, c: (0, 0)))
    args.append(final_norm.reshape(1, D_MODEL))
    return pl.pallas_call(
        functools.partial(_ffn_kernel, seq_len=seq_len, final=final),
        grid=(nb, n_steps),
        in_specs=in_specs,
        out_specs=pl.BlockSpec((ROWS, D_MODEL), lambda i, c: (i, 0)),
        out_shape=jax.ShapeDtypeStruct((n_tok, D_MODEL), F32),
        scratch_shapes=[pltpu.VMEM((ROWS, D_MODEL), BF16), pltpu.VMEM((ROWS, D_MODEL), F32)],
        compiler_params=pltpu.CompilerParams(
            dimension_semantics=("arbitrary", "arbitrary"), vmem_limit_bytes=VMEM_FFN
        ),
        name=f"ffn_l{layer}_t{seq_len}",
    )(*args)


def kernel(x_prompt, x_sample, c, state_hgrn, c_ctx, norm_mix, norm_ffn, w_ada, b_ada, w_in, lb_logits, hg_norm,
           w_branch_hg, w_branch_sg, w_branch_pool, w_out, sg_norm, sg_w, sg_b, pool_w, pool_scale, ffn_up,
           ffn_conv_w, ffn_conv_b, ffn_down, final_norm):
    n_ctx, t_ctx, _ = x_prompt.shape
    n_lat, t_lat, _ = x_sample.shape

    n_cond = 1 + n_lat
    pad = -n_cond % V7X_SUBLANES
    cvec = jnp.concatenate([c_ctx[None, :], c, jnp.zeros((pad, D_MODEL), F32)], axis=0)
    mod = _mod_call(cvec, w_ada, b_ada).reshape(DEPTH, n_cond + pad, N_MOD, D_MODEL)

    xs = _addpos_call(x_sample, _grid_pos_embed(t_lat)).reshape(n_lat * t_lat, D_MODEL)
    xp = x_prompt.reshape(n_ctx * t_ctx, D_MODEL)
    state0 = state_hgrn.reshape(n_lat * DEPTH * 2 * HG_HEADS, HG_DK, HG_DV)

    states = None
    for layer in range(DEPTH):
        final = layer == DEPTH - 1
        groups = []
        for x, m, t, s0 in ((xp, mod[layer, 0:1], t_ctx, None), (xs, mod[layer, 1:n_cond], t_lat, state0)):
            phg, s_fin = _hgrn_call(x, m, t, layer, norm_mix, w_in, lb_logits, hg_norm, w_branch_hg, s0, states)
            x1 = _mix_call(x, phg, m, t, layer, norm_mix, w_in, sg_norm, sg_w, sg_b, w_branch_sg, w_branch_pool,
                           pool_w, pool_scale, w_out)
            x2 = _ffn_call(x1, m, t, layer, final, norm_ffn, ffn_up, ffn_conv_w, ffn_conv_b, ffn_down, final_norm)
            groups.append((x2, s_fin))
        (xp, states), (xs, _) = groups

    y_prompt = xp.reshape(x_prompt.shape)
    y_sample = xs.reshape(x_sample.shape)
    return (y_prompt, y_sample, states)
```

```python
import functools

import jax
import jax.numpy as jnp
import numpy as np
from jax import lax
from jax.experimental import pallas as pl
from jax.experimental.pallas import tpu as pltpu

D_MODEL = 1024
DEPTH = 2
GRID_W = 64
POS_BASE = 10000.0
EPS = 1e-6
HG_HEADS = 4
HG_DK = 128
HG_DV = 128
HG_WIDTH = HG_HEADS * HG_DV
SG_GROUPS = 4
SG_WIDTH = 512
SG_GROUP_DIM = SG_WIDTH // SG_GROUPS
SG_CHUNK = 128
POOL_WINDOWS = (2, 4, 8, 16)
POOL_WIDTH = 512
POOL_GROUP_DIM = POOL_WIDTH // len(POOL_WINDOWS)
IN_COLS = 5 * HG_WIDTH + 2 * SG_WIDTH + POOL_WIDTH + 3 * D_MODEL
D_FF = 2816
N_MOD = 6

V7X_LANES = 128
V7X_SUBLANES = 8
V7X_MXU_DIM = 256
MIB = 2**20

ROWS = 1024
PART_ROWS = 512
HG_BLOCK = 256
HG_LEVELS = 8
IN_CHUNK = 512
FF_CHUNK = V7X_MXU_DIM
FF_SETS = 3
MOD_CHUNK = 3072
VMEM_SMALL = 32 * MIB
VMEM_HGRN = 48 * MIB
VMEM_MIX = 56 * MIB
VMEM_FFN = 52 * MIB

LOG2E = 1.4426950408889634

F32 = jnp.float32
BF16 = jnp.bfloat16


def _dot(a, b):
    return lax.dot_general(a, b, (((1,), (0,)), ((), ())), preferred_element_type=F32)


def _dot_nt(a, b):
    return lax.dot_general(a, b, (((1,), (1,)), ((), ())), preferred_element_type=F32)


def _dot_tn(a, b):
    return lax.dot_general(a, b, (((0,), (0,)), ((), ())), preferred_element_type=F32)


def _sigmoid(x):
    return 0.5 * jnp.tanh(0.5 * x) + 0.5


def _silu(x):
    return x * _sigmoid(x)


_GELU_C1 = 0.7978845608028654
_GELU_C2 = _GELU_C1 * 0.044715


def _gelu_tanh(x):
    half_x = 0.5 * x
    return half_x + half_x * jnp.tanh(x * (_GELU_C1 + _GELU_C2 * (x * x)))


def _rms(x, gain):
    return x * lax.rsqrt(jnp.mean(x * x, axis=-1, keepdims=True) + EPS) * gain


def _log1pexp(y):
    return jnp.maximum(y, 0.0) + jnp.log(1.0 + jnp.exp(-jnp.abs(y)))


def _norm_mod(x, gain, scale, shift):
    return (_rms(x, gain) * (1.0 + scale) + shift).astype(BF16)


def _part_rows(seq_len):
    return max(seq_len, PART_ROWS)


def _mod_kernel(c_ref, w_ref, b_ref, o_ref):
    c = _silu(c_ref[...]).astype(BF16)
    o_ref[...] = _dot(c, w_ref[...].astype(BF16)) + b_ref[...]


def _mod_call(cvec, w_ada, b_ada):
    n_rows = cvec.shape[0]
    n_cols = N_MOD * D_MODEL
    return pl.pallas_call(
        _mod_kernel,
        grid=(DEPTH, n_cols // MOD_CHUNK),
        in_specs=[
            pl.BlockSpec((n_rows, D_MODEL), lambda l, n: (0, 0)),
            pl.BlockSpec((None, D_MODEL, MOD_CHUNK), lambda l, n: (l, 0, n)),
            pl.BlockSpec((None, 1, MOD_CHUNK), lambda l, n: (l, 0, n)),
        ],
        out_specs=pl.BlockSpec((None, n_rows, MOD_CHUNK), lambda l, n: (l, 0, n)),
        out_shape=jax.ShapeDtypeStruct((DEPTH, n_rows, n_cols), F32),
        compiler_params=pltpu.CompilerParams(
            dimension_semantics=("arbitrary", "arbitrary"), vmem_limit_bytes=VMEM_SMALL
        ),
        name="adaln_mod",
    )(cvec, w_ada, b_ada.reshape(DEPTH, 1, n_cols))


def _addpos_kernel(x_ref, p_ref, o_ref):
    o_ref[...] = x_ref[...] + p_ref[...]


def _addpos_call(x, pos):
    b, t, d = x.shape
    return pl.pallas_call(
        _addpos_kernel,
        grid=(b,),
        in_specs=[pl.BlockSpec((None, t, d), lambda i: (i, 0, 0)), pl.BlockSpec((t, d), lambda i: (0, 0))],
        out_specs=pl.BlockSpec((None, t, d), lambda i: (i, 0, 0)),
        out_shape=jax.ShapeDtypeStruct(x.shape, x.dtype),
        compiler_params=pltpu.CompilerParams(dimension_semantics=("arbitrary",), vmem_limit_bytes=VMEM_SMALL),
        name="add_pos",
    )(x, pos)


def _grid_pos_embed(n_tokens):
    rows = n_tokens // GRID_W
    r = np.broadcast_to(np.arange(rows, dtype=np.float32)[:, None], (rows, GRID_W)).reshape(-1)
    col = np.broadcast_to(np.arange(GRID_W, dtype=np.float32)[None, :], (rows, GRID_W)).reshape(-1)
    quarter = D_MODEL // 4
    omega = (1.0 / (np.float32(POS_BASE) ** (np.arange(quarter, dtype=np.float32) / quarter))).astype(np.float32)
    ar = r[:, None] * omega[None, :]
    ac = col[:, None] * omega[None, :]
    return jnp.asarray(np.concatenate([np.sin(ar), np.cos(ar), np.sin(ac), np.cos(ac)], axis=-1), F32)


def _ref_rows(b, blk, r):
    n, c = b.shape
    if blk >= V7X_SUBLANES:
        x3 = b.reshape(n // blk, blk, c)
        return jnp.broadcast_to(x3[:, r : r + 1, :], x3.shape).reshape(n, c)
    x3 = b.reshape(n // V7X_SUBLANES, V7X_SUBLANES, c)
    sub = lax.broadcasted_iota(jnp.int32, x3.shape, 1)
    bases = list(range(0, V7X_SUBLANES, blk))
    out = jnp.broadcast_to(x3[:, bases[-1] + r : bases[-1] + r + 1, :], x3.shape)
    for base in reversed(bases[:-1]):
        out = jnp.where(sub < base + blk, jnp.broadcast_to(x3[:, base + r : base + r + 1, :], x3.shape), out)
    return out.reshape(n, c)


def _cum_logdecay(lf, tri):
    hi = lf.astype(BF16)
    r1 = lf - hi.astype(F32)
    mid = r1.astype(BF16)
    lo = (r1 - mid.astype(F32)).astype(BF16)
    return _dot(tri, hi) + _dot(tri, mid) + _dot(tri, lo)


def _hgrn_block(qs, ks, vs, bs, lvqs, sgn_ref, sts):
    half = HG_BLOCK // 2
    lo, hi = slice(0, half), slice(half, HG_BLOCK)
    dirs = (True, False)
    vbs = [v.astype(BF16) for v in vs]
    qbs = [q.astype(BF16) for q in qs]
    kbs = [k.astype(BF16) for k in ks]
    diags = [[jnp.where(lvq == 0, _dot_nt(qb[h], kb[h]), 0.0) for h in (lo, hi)] for qb, kb, lvq in zip(qbs, kbs, lvqs)]
    for m in range(1, HG_LEVELS):
        blk = 2**m
        for d, forward in enumerate(dirs):
            b = bs[d]
            ref = _ref_rows(b, blk, blk // 2 - 1 if forward else blk // 2)
            e = jnp.exp2((b - ref) * sgn_ref[d, m - 1]).astype(BF16)
            qt, kt = qbs[d] * e, kbs[d] * e
            diags[d] = [jnp.where(lvqs[d] == m, _dot_nt(qt[h], kt[h]), a) for h, a in zip((lo, hi), diags[d])]
    outs = []
    for d, forward in enumerate(dirs):
        b, k, st, q, vb = bs[d], ks[d], sts[d], qs[d], vbs[d]
        mid = half - 1 if forward else half
        e = jnp.exp2((b - b[mid : mid + 1, :]) * sgn_ref[d, HG_LEVELS - 1]).astype(BF16)
        qt, kt = qbs[d] * e, kbs[d] * e
        a_lo, a_hi = (a.astype(BF16) for a in diags[d])
        if forward:
            cross = _dot_nt(qt[hi], kt[lo]).astype(BF16)
            o = jnp.concatenate([_dot(a_lo, vb[lo]), _dot(cross, vb[lo]) + _dot(a_hi, vb[hi])], axis=0)
        else:
            cross = _dot_nt(qt[lo], kt[hi]).astype(BF16)
            o = jnp.concatenate([_dot(a_lo, vb[lo]) + _dot(cross, vb[hi]), _dot(a_hi, vb[hi])], axis=0)
        edge = b[HG_BLOCK - 1 : HG_BLOCK, :] if forward else b[0:1, :]
        k_end = (k * jnp.exp(edge - b)).astype(BF16)
        st_new = _dot_tn(vb, k_end)
        if st is not None:
            o = o + _dot_nt((q * jnp.exp(b)).astype(BF16), st.astype(BF16))
            st_new = st_new + st * jnp.exp(edge)
        outs.append((o, st_new))
    return outs


def _hgrn_kernel(*refs, layer, seq_len):
    carry = seq_len > HG_BLOCK
    it = iter(refs)
    x_ref, mod_ref, nmix_ref = next(it), next(it), next(it)
    w_refs = [next(it) for _ in range(5)]
    lbl_ref, hgn_ref, wbr_ref = next(it), next(it), next(it)
    s0_refs = [next(it), next(it)] if carry else None
    sprev_ref = next(it) if (not carry and layer > 0) else None
    phg_ref = next(it)
    sout_ref = None if carry else next(it)
    hb, wcat, q_s, v_s, g_s, kf_s, kb_s, bf_s, bb_s, o_s, y_s, lvf_s, lvb_s, trif_s, trib_s, sgn_s, st_s = it

    j = pl.program_id(1)

    def head(first):
        if first:
            t = lax.broadcasted_iota(jnp.int32, (HG_BLOCK // 2, HG_BLOCK // 2), 0)
            s = lax.broadcasted_iota(jnp.int32, (HG_BLOCK // 2, HG_BLOCK // 2), 1)
            x = t ^ s
            lv = jnp.zeros_like(x)
            for m in range(HG_LEVELS - 1):
                lv = lv + (x >= 2**m).astype(jnp.int32)
            lvf_s[...] = jnp.where(t >= s, lv, -1)
            lvb_s[...] = jnp.where(t <= s, lv, -1)
            t = lax.broadcasted_iota(jnp.int32, (HG_BLOCK, HG_BLOCK), 0)
            s = lax.broadcasted_iota(jnp.int32, (HG_BLOCK, HG_BLOCK), 1)
            trif_s[...] = (t >= s).astype(BF16)
            trib_s[...] = (t <= s).astype(BF16)
            row = lax.broadcasted_iota(jnp.int32, (HG_BLOCK, HG_DK), 0)
            for m in range(1, HG_LEVELS + 1):
                upper = (row & 2 ** (m - 1)) != 0
                sgn_s[0, m - 1] = jnp.where(upper, LOG2E, -LOG2E)
                sgn_s[1, m - 1] = jnp.where(upper, -LOG2E, LOG2E)
        for g, w_ref in enumerate(w_refs):
            wcat[:, g * HG_DK : (g + 1) * HG_DK] = w_ref[...].astype(BF16)
        a0, a1 = lbl_ref[0], lbl_ref[1]
        amax = jnp.maximum(a0, a1)
        e0, e1 = jnp.exp(a0 - amax), jnp.exp(a1 - amax)
        p0, p1 = e0 / (e0 + e1), e1 / (e0 + e1)
        lb = (p0 - p0) if layer == 0 else ((p0 + p1) - p0)
        log_lb = jnp.log(lb)

        wc = wcat[...]
        part = PART_ROWS
        n_parts = ROWS // part

        def zdot(p):
            rows = slice(p * part, (p + 1) * part)
            if first:
                hb[rows, :] = _norm_mod(x_ref[rows, :], nmix_ref[...], mod_ref[0, 1:2, :], mod_ref[0, 0:1, :])
            return _dot(hb[rows, :], wc)

        pending = zdot(0)
        for p in range(n_parts):
            z = pending
            if p + 1 < n_parts:
                pending = zdot(p + 1)
            rows = slice(p * part, (p + 1) * part)
            zq, zff, zfb, zi, zg = (z[:, g * HG_DK : (g + 1) * HG_DK] for g in range(5))
            q_s[rows, :] = _silu(zq) * HG_DK**-0.5
            v_s[rows, :] = zi
            g_s[rows, :] = _silu(zg)
            for d, (zf, k_s, b_s, tri_s) in enumerate(((zff, kf_s, bf_s, trif_s), (zfb, kb_s, bb_s, trib_s))):
                t = jnp.exp(-jnp.abs(zf))
                lf = _log1pexp(log_lb[d : d + 1, :] - zf) - (jnp.maximum(-zf, 0.0) + jnp.log(1.0 + t))
                k_s[rows, :] = (1.0 - lb[d : d + 1, :]) * (jnp.where(zf >= 0.0, t, 1.0) / (1.0 + t))
                tri = tri_s[...]
                for n in range(part // HG_BLOCK):
                    loc = slice(n * HG_BLOCK, (n + 1) * HG_BLOCK)
                    dst = slice(rows.start + n * HG_BLOCK, rows.start + (n + 1) * HG_BLOCK)
                    b_s[dst, :] = _cum_logdecay(lf[loc, :], tri)

        o_s[...] = jnp.zeros_like(o_s)
        n_blk = ROWS // HG_BLOCK
        if sprev_ref is not None:
            sout_ref[:, 0:layer] = sprev_ref[...]
        if carry:
            for d in range(2):
                st_s[d] = s0_refs[d][0].T

        def blocks(n, c):
            blks = (n, n_blk - 1 - n if carry else n)
            rows = [pl.ds(pl.multiple_of(blk * HG_BLOCK, HG_BLOCK), HG_BLOCK) for blk in blks]
            outs = _hgrn_block(
                [q_s[r, :] for r in rows], [kf_s[rows[0], :], kb_s[rows[1], :]], [v_s[r, :] for r in rows],
                [bf_s[rows[0], :], bb_s[rows[1], :]], [lvf_s[...], lvb_s[...]], sgn_s,
                [st_s[d] if carry else None for d in range(2)],
            )
            for d, (o, st_new) in enumerate(outs):
                o_s[rows[d], :] += o
                if carry:
                    st_s[d] = st_new
                else:
                    sout_ref[blks[d], layer, d, 0] = st_new.T
            return c

        lax.fori_loop(0, n_blk, blocks, 0, unroll=True)

        y_s[j] = (_rms(o_s[...], hgn_ref[...]) * g_s[...]).astype(BF16)

    @pl.when(j == 0)
    def _():
        head(True)

    @pl.when(j > 0)
    def _():
        head(False)

    @pl.when(j == HG_HEADS - 1)
    def _():
        y = jnp.concatenate([y_s[h] for h in range(HG_HEADS)], axis=1)
        phg_ref[...] = _dot(y, wbr_ref[...].astype(BF16))


def _hgrn_call(x, mod, seq_len, layer, norm_mix, w_in, lb_logits, hg_norm, w_branch_hg, state0, prev_states):
    n_tok = x.shape[0]
    nb = n_tok // ROWS
    carry = seq_len > HG_BLOCK
    per_seq_mod = mod.shape[0] > 1

    in_specs = [
        pl.BlockSpec((ROWS, D_MODEL), lambda i, j: (i, 0)),
        pl.BlockSpec((1, N_MOD, D_MODEL), (lambda i, j: (i, 0, 0)) if per_seq_mod else (lambda i, j: (0, 0, 0))),
        pl.BlockSpec((None, 1, D_MODEL), lambda i, j: (layer, 0, 0)),
    ]
    args = [x, mod, norm_mix.reshape(DEPTH, 1, D_MODEL)]
    for g in range(5):
        in_specs.append(pl.BlockSpec((None, D_MODEL, HG_DK), lambda i, j, g=g: (layer, 0, g * HG_HEADS + j)))
        args.append(w_in)
    in_specs += [
        pl.BlockSpec((DEPTH, 2, HG_DK), lambda i, j: (0, 0, j)),
        pl.BlockSpec((None, 1, HG_DV), lambda i, j: (layer, 0, 0)),
        pl.BlockSpec((None, HG_WIDTH, D_MODEL), lambda i, j: (layer, 0, 0), pipeline_mode=pl.Buffered(1)),
    ]
    args += [lb_logits, hg_norm.reshape(DEPTH, 1, HG_DV), w_branch_hg]
    if carry:
        assert seq_len == ROWS
        for d in range(2):
            in_specs.append(
                pl.BlockSpec((1, HG_DK, HG_DV), lambda i, j, d=d: (((i * DEPTH + layer) * 2 + d) * HG_HEADS + j, 0, 0))
            )
            args.append(state0)

    out_shape = [jax.ShapeDtypeStruct((n_tok, D_MODEL), F32)]
    out_specs = [pl.BlockSpec((ROWS, D_MODEL), lambda i, j: (i, 0))]
    if not carry:
        assert seq_len == HG_BLOCK
        n_seq = n_tok // seq_len
        seqs = ROWS // seq_len
        if layer > 0:
            in_specs.append(pl.BlockSpec((seqs, layer, 2, 1, HG_DK, HG_DV), lambda i, j: (i, 0, 0, j, 0, 0)))
            args.append(prev_states)
        out_shape.append(jax.ShapeDtypeStruct((n_seq, layer + 1, 2, HG_HEADS, HG_DK, HG_DV), F32))
        out_specs.append(pl.BlockSpec((seqs, layer + 1, 2, 1, HG_DK, HG_DV), lambda i, j: (i, 0, 0, j, 0, 0)))

    head = lambda dt=F32: pltpu.VMEM((ROWS, HG_DK), dt)
    scratch = [
        pltpu.VMEM((ROWS, D_MODEL), BF16),
        pltpu.VMEM((D_MODEL, 5 * HG_DK), BF16),
        head(), head(), head(),
        head(), head(), head(), head(),
        head(),
        pltpu.VMEM((HG_HEADS, ROWS, HG_DV), BF16),
        pltpu.VMEM((HG_BLOCK // 2, HG_BLOCK // 2), jnp.int32),
        pltpu.VMEM((HG_BLOCK // 2, HG_BLOCK // 2), jnp.int32),
        pltpu.VMEM((HG_BLOCK, HG_BLOCK), BF16),
        pltpu.VMEM((HG_BLOCK, HG_BLOCK), BF16),
        pltpu.VMEM((2, HG_LEVELS, HG_BLOCK, HG_DK), F32),
        pltpu.VMEM((2, HG_DV, HG_DK), F32),
    ]
    outs = pl.pallas_call(
        functools.partial(_hgrn_kernel, layer=layer, seq_len=seq_len),
        grid=(nb, HG_HEADS),
        in_specs=in_specs,
        out_specs=out_specs,
        out_shape=out_shape,
        scratch_shapes=scratch,
        compiler_params=pltpu.CompilerParams(
            dimension_semantics=("arbitrary", "arbitrary"), vmem_limit_bytes=VMEM_HGRN
        ),
        name=f"hgrn_l{layer}_t{seq_len}",
    )(*args)
    return (outs[0], None) if carry else (outs[0], outs[1])


_MIX_ORDER = (10, 11, 5, 6, 7, 12, 13, 8, 9)


def _mix_col(k):
    idx = 0
    for n, c in enumerate(_MIX_ORDER):
        idx = idx + jnp.where(k == n, c, 0)
    return idx


def _window_mean_minus_self(p, tpos, seq_len, w):
    n = p.shape[0]
    half = w // 2

    def shifted(x, j):
        valid = (tpos + j >= 0) & (tpos + j < seq_len)
        return jnp.where(valid, pltpu.roll(x, (-j) % n, 0), 0.0)

    ahead, behind, length = p, p, 1
    while length < half:
        ahead = ahead + shifted(ahead, length)
        behind = behind + shifted(behind, -length)
        length *= 2
    acc = ahead + shifted(behind, -1)
    cnt = jnp.minimum(tpos + half, seq_len) - jnp.maximum(tpos - half, 0)
    return acc / cnt.astype(F32) - p


def _mix_kernel(
    x_ref, mod_ref, nmix_ref, w_ref, sgn_ref, sgw_ref, sgb_ref, wbsg_ref, wbpool_ref, poolw_ref, pscale_ref,
    wout_ref, phg_ref, o_ref, hb, u_s, br_s, mrg_s, *, seq_len,
):
    k = pl.program_id(1)
    half = D_MODEL // 2

    @pl.when(k == 0)
    def _():
        hb[...] = _norm_mod(x_ref[...], nmix_ref[...], mod_ref[0, 1:2, :], mod_ref[0, 0:1, :])

    def for_z_parts(consume, part=PART_ROWS):
        w = w_ref[...].astype(BF16)
        n_parts = ROWS // part

        def zdot(p):
            return _dot(hb[p * part : (p + 1) * part, :], w)

        pending = zdot(0)
        for p in range(n_parts):
            z = pending
            if p + 1 < n_parts:
                pending = zdot(p + 1)
            consume(slice(p * part, (p + 1) * part), z)

    for step in (0, 1):

        @pl.when(k == step)
        def _(step=step):
            cols = slice(step * half, (step + 1) * half)

            def gate(rows, z):
                mrg_s[rows, cols] = _sigmoid(z)

            for_z_parts(gate)

    @pl.when(k == 2)
    def _():
        def store_u(rows, z):
            u_s[rows, :] = _gelu_tanh(z)

        for_z_parts(store_u)

    @pl.when(k == 3)
    def _():
        wbsg = wbsg_ref[...].astype(BF16)
        wgs = [sgw_ref[g].astype(BF16) for g in range(SG_GROUPS)]

        def spatial_gating(rows, z):
            v = _rms(_gelu_tanh(z), sgn_ref[...]).astype(BF16)
            for g in range(SG_GROUPS):
                bias = sgb_ref[:, g : g + 1]
                cols = slice(g * SG_GROUP_DIM, (g + 1) * SG_GROUP_DIM)
                for n in range((rows.stop - rows.start) // SG_CHUNK):
                    loc = slice(n * SG_CHUNK, (n + 1) * SG_CHUNK)
                    dst = slice(rows.start + n * SG_CHUNK, rows.start + (n + 1) * SG_CHUNK)
                    mixed = _dot(wgs[g], v[loc, cols]) + bias
                    br_s[dst, cols] = (u_s[dst, cols] * mixed).astype(BF16)
            mrg_s[rows, :] = mrg_s[rows, :] * _dot(br_s[rows, :], wbsg)

        for_z_parts(spatial_gating)

    @pl.when(k == 4)
    def _():
        part = _part_rows(seq_len)
        tpos = lax.broadcasted_iota(jnp.int32, (part, POOL_GROUP_DIM), 0) & (seq_len - 1)

        def pool(rows, z):
            for gi, w in enumerate(POOL_WINDOWS):
                cols = slice(gi * POOL_GROUP_DIM, (gi + 1) * POOL_GROUP_DIM)
                pooled = _window_mean_minus_self(z[:, cols], tpos, seq_len, w)
                out = _dot(pooled.astype(BF16), poolw_ref[gi].astype(BF16)) * pscale_ref[:, cols]
                br_s[rows, cols] = out.astype(BF16)

        for_z_parts(pool, part)

    for step in (5, 6):

        @pl.when(k == step)
        def _(step=step):
            cols = slice((step - 5) * half, (step - 4) * half)
            wbpool = wbpool_ref[:, cols].astype(BF16)

            def gate(rows, z):
                mrg_s[rows, cols] = mrg_s[rows, cols] + _sigmoid(z) * _dot(br_s[rows, :], wbpool)

            for_z_parts(gate)

    @pl.when(k == 7)
    def _():
        cols = slice(0, half)

        def gate(rows, z):
            mrg_s[rows, cols] = mrg_s[rows, cols] + _sigmoid(z) * phg_ref[rows, cols]

        for_z_parts(gate)

    @pl.when(k == 8)
    def _():
        cols = slice(half, D_MODEL)
        wout = wout_ref[...].astype(BF16)

        def gate_and_project(rows, z):
            mrg_s[rows, cols] = mrg_s[rows, cols] + _sigmoid(z) * phg_ref[rows, cols]
            y = _dot(mrg_s[rows, :].astype(BF16), wout)
            o_ref[rows, :] = x_ref[rows, :] + mod_ref[0, 2:3, :] * y

        for_z_parts(gate_and_project)


def _mix_call(x, phg, mod, seq_len, layer, norm_mix, w_in, sg_norm, sg_w, sg_b, w_branch_sg, w_branch_pool, pool_w,
              pool_scale, w_out):
    n_tok = x.shape[0]
    nb = n_tok // ROWS
    per_seq_mod = mod.shape[0] > 1
    const = pl.Buffered(1)
    assert seq_len & (seq_len - 1) == 0 and ROWS % seq_len == 0 and seq_len % SG_CHUNK == 0
    in_specs = [
        pl.BlockSpec((ROWS, D_MODEL), lambda i, k: (i, 0)),
        pl.BlockSpec((1, N_MOD, D_MODEL), (lambda i, k: (i, 0, 0)) if per_seq_mod else (lambda i, k: (0, 0, 0))),
        pl.BlockSpec((None, 1, D_MODEL), lambda i, k: (layer, 0, 0)),
        pl.BlockSpec((None, D_MODEL, IN_CHUNK), lambda i, k: (layer, 0, _mix_col(k))),
        pl.BlockSpec((None, 1, SG_WIDTH), lambda i, k: (layer, 0, 0)),
        pl.BlockSpec((None, SG_GROUPS, SG_CHUNK, SG_CHUNK), lambda i, k: (layer, 0, 0, 0)),
        pl.BlockSpec((None, SG_CHUNK, SG_GROUPS), lambda i, k: (layer, 0, 0)),
        pl.BlockSpec((None, SG_WIDTH, D_MODEL), lambda i, k: (layer, 0, 0), pipeline_mode=const),
        pl.BlockSpec((None, POOL_WIDTH, D_MODEL), lambda i, k: (layer, 0, 0), pipeline_mode=const),
        pl.BlockSpec((None, len(POOL_WINDOWS), POOL_GROUP_DIM, POOL_GROUP_DIM), lambda i, k: (layer, 0, 0, 0)),
        pl.BlockSpec((None, 1, POOL_WIDTH), lambda i, k: (layer, 0, 0)),
        pl.BlockSpec((None, D_MODEL, D_MODEL), lambda i, k: (layer, 0, 0), pipeline_mode=const),
        pl.BlockSpec((ROWS, D_MODEL), lambda i, k: (i, 0)),
    ]
    args = [
        x, mod, norm_mix.reshape(DEPTH, 1, D_MODEL), w_in, sg_norm.reshape(DEPTH, 1, SG_WIDTH), sg_w,
        jnp.swapaxes(sg_b, 1, 2), w_branch_sg, w_branch_pool, pool_w, pool_scale.reshape(DEPTH, 1, POOL_WIDTH),
        w_out, phg,
    ]
    scratch = [
        pltpu.VMEM((ROWS, D_MODEL), BF16),
        pltpu.VMEM((ROWS, SG_WIDTH), F32),
        pltpu.VMEM((ROWS, SG_WIDTH), BF16),
        pltpu.VMEM((ROWS, D_MODEL), F32),
    ]
    return pl.pallas_call(
        functools.partial(_mix_kernel, seq_len=seq_len),
        grid=(nb, len(_MIX_ORDER)),
        in_specs=in_specs,
        out_specs=pl.BlockSpec((ROWS, D_MODEL), lambda i, k: (i, 0)),
        out_shape=jax.ShapeDtypeStruct((n_tok, D_MODEL), F32),
        scratch_shapes=scratch,
        compiler_params=pltpu.CompilerParams(
            dimension_semantics=("arbitrary", "arbitrary"), vmem_limit_bytes=VMEM_MIX
        ),
        name=f"mix_l{layer}_t{seq_len}",
    )(*args)


def _ffn_kernel(x_ref, mod_ref, nffn_ref, *refs, seq_len, final):
    sets = [refs[7 * s : 7 * s + 7] for s in range(FF_SETS)]
    fin_ref, o_ref, hb, acc = refs[7 * FF_SETS :]
    c = pl.program_id(1)
    n_chunks = D_FF // FF_CHUNK
    n_steps = pl.cdiv(n_chunks, FF_SETS)

    part = ROWS
    tpos = lax.broadcasted_iota(jnp.int32, (part, FF_CHUNK), 0) & (seq_len - 1)
    has_prev = tpos >= 1
    has_next = tpos < seq_len - 1

    def conv(h, cw_ref, cb_ref):
        prev = jnp.where(has_prev, pltpu.roll(h, 1, 0), 0.0)
        nxt = jnp.where(has_next, pltpu.roll(h, part - 1, 0), 0.0)
        return prev * cw_ref[0:1, :] + h * cw_ref[1:2, :] + nxt * cw_ref[2:3, :] + cb_ref[...]

    def run(n_sets, first, last):
        ws = [(wa[...].astype(BF16), wb[...].astype(BF16), wd[...].astype(BF16)) for wa, wb, _, _, _, _, wd in sets]
        items = [(p, s) for p in range(ROWS // part) for s in range(n_sets)]

        def up(item):
            p, s = item
            rows = slice(p * part, (p + 1) * part)
            if first and s == 0:
                hb[rows, :] = _norm_mod(x_ref[rows, :], nffn_ref[...], mod_ref[0, 4:5, :], mod_ref[0, 3:4, :])
            h = hb[rows, :]
            return _dot(h, ws[s][0]), _dot(h, ws[s][1])

        pending = up(items[0])
        down = None
        for i, (p, s) in enumerate(items):
            ha, hb2 = pending
            if i + 1 < len(items):
                pending = up(items[i + 1])
            _, _, cwa_ref, cwb_ref, cba_ref, cbb_ref, _ = sets[s]
            a = conv(ha, cwa_ref, cba_ref)
            b = conv(hb2, cwb_ref, cbb_ref)
            d = _dot((_silu(a) * b).astype(BF16), ws[s][2])
            down = d if down is None else down + d
            if s == n_sets - 1:
                rows = slice(p * part, (p + 1) * part)
                total = down if first else acc[rows, :] + down
                if last:
                    y = x_ref[rows, :] + mod_ref[0, 5:6, :] * total
                    o_ref[rows, :] = _rms(y, fin_ref[...]) if final else y
                else:
                    acc[rows, :] = total
                down = None

    kinds = {}
    for step in range(n_steps):
        n_sets = len([s for s in range(FF_SETS) if step + s * n_steps < n_chunks])
        kinds.setdefault((n_sets, step == 0, step == n_steps - 1), []).append(step)
    for (n_sets, first, last), steps in kinds.items():
        cond = functools.reduce(lambda u, v: u | v, [c == st for st in steps])

        @pl.when(cond)
        def _(n_sets=n_sets, first=first, last=last):
            run(n_sets, first, last)


def _ffn_call(x, mod, seq_len, layer, final, norm_ffn, ffn_up, ffn_conv_w, ffn_conv_b, ffn_down, final_norm):
    n_tok = x.shape[0]
    nb = n_tok // ROWS
    nc = D_FF // FF_CHUNK
    n_steps = pl.cdiv(nc, FF_SETS)
    per_seq_mod = mod.shape[0] > 1
    conv_b = ffn_conv_b.reshape(DEPTH, 1, 2 * D_FF)
    in_specs = [
        pl.BlockSpec((ROWS, D_MODEL), lambda i, c: (i, 0)),
        pl.BlockSpec((1, N_MOD, D_MODEL), (lambda i, c: (i, 0, 0)) if per_seq_mod else (lambda i, c: (0, 0, 0))),
        pl.BlockSpec((None, 1, D_MODEL), lambda i, c: (layer, 0, 0)),
    ]
    args = [x, mod, norm_ffn.reshape(DEPTH, 1, D_MODEL)]
    for s in range(FF_SETS):
        chunk = lambda c, s=s: jnp.minimum(c + s * n_steps, nc - 1)
        in_specs += [
            pl.BlockSpec((None, D_MODEL, FF_CHUNK), lambda i, c, f=chunk: (layer, 0, f(c))),
            pl.BlockSpec((None, D_MODEL, FF_CHUNK), lambda i, c, f=chunk: (layer, 0, nc + f(c))),
            pl.BlockSpec((None, 3, FF_CHUNK), lambda i, c, f=chunk: (layer, 0, f(c))),
            pl.BlockSpec((None, 3, FF_CHUNK), lambda i, c, f=chunk: (layer, 0, nc + f(c))),
            pl.BlockSpec((None, 1, FF_CHUNK), lambda i, c, f=chunk: (layer, 0, f(c))),
            pl.BlockSpec((None, 1, FF_CHUNK), lambda i, c, f=chunk: (layer, 0, nc + f(c))),
            pl.BlockSpec((None, FF_CHUNK, D_MODEL), lambda i, c, f=chunk: (layer, f(c), 0)),
        ]
        args += [ffn_up, ffn_up, ffn_conv_w, ffn_conv_w, conv_b, conv_b, ffn_down]
    in_specs.append(pl.BlockSpec((1, D_MODEL), lambda i, c: (0, 0)))
    args.append(final_norm.reshape(1, D_MODEL))
    return pl.pallas_call(
        functools.partial(_ffn_kernel, seq_len=seq_len, final=final),
        grid=(nb, n_steps),
        in_specs=in_specs,
        out_specs=pl.BlockSpec((ROWS, D_MODEL), lambda i, c: (i, 0)),
        out_shape=jax.ShapeDtypeStruct((n_tok, D_MODEL), F32),
        scratch_shapes=[pltpu.VMEM((ROWS, D_MODEL), BF16), pltpu.VMEM((ROWS, D_MODEL), F32)],
        compiler_params=pltpu.CompilerParams(
            dimension_semantics=("arbitrary", "arbitrary"), vmem_limit_bytes=VMEM_FFN
        ),
        name=f"ffn_l{layer}_t{seq_len}",
    )(*args)


def kernel(x_prompt, x_sample, c, state_hgrn, c_ctx, norm_mix, norm_ffn, w_ada, b_ada, w_in, lb_logits, hg_norm,
           w_branch_hg, w_branch_sg, w_branch_pool, w_out, sg_norm, sg_w, sg_b, pool_w, pool_scale, ffn_up,
           ffn_conv_w, ffn_conv_b, ffn_down, final_norm):
    n_ctx, t_ctx, _ = x_prompt.shape
    n_lat, t_lat, _ = x_sample.shape

    n_cond = 1 + n_lat
    pad = -n_cond % V7X_SUBLANES
    cvec = jnp.concatenate([c_ctx[None, :], c, jnp.zeros((pad, D_MODEL), F32)], axis=0)
    mod = _mod_call(cvec, w_ada, b_ada).reshape(DEPTH, n_cond + pad, N_MOD, D_MODEL)

    xs = _addpos_call(x_sample, _grid_pos_embed(t_lat)).reshape(n_lat * t_lat, D_MODEL)
    xp = x_prompt.reshape(n_ctx * t_ctx, D_MODEL)
    state0 = state_hgrn.reshape(n_lat * DEPTH * 2 * HG_HEADS, HG_DK, HG_DV)

    states = None
    for layer in range(DEPTH):
        final = layer == DEPTH - 1
        groups = []
        for x, m, t, s0 in ((xp, mod[layer, 0:1], t_ctx, None), (xs, mod[layer, 1:n_cond], t_lat, state0)):
            phg, s_fin = _hgrn_call(x, m, t, layer, norm_mix, w_in, lb_logits, hg_norm, w_branch_hg, s0, states)
            x1 = _mix_call(x, phg, m, t, layer, norm_mix, w_in, sg_norm, sg_w, sg_b, w_branch_sg, w_branch_pool,
                           pool_w, pool_scale, w_out)
            x2 = _ffn_call(x1, m, t, layer, final, norm_ffn, ffn_up, ffn_conv_w, ffn_conv_b, ffn_down, final_norm)
            groups.append((x2, s_fin))
        (xp, states), (xs, _) = groups

    y_prompt = xp.reshape(x_prompt.shape)
    y_sample = xs.reshape(x_sample.shape)
    return (y_prompt, y_sample, states)
```

```python
import functools

import jax
import jax.numpy as jnp
import numpy as np
from jax import lax
from jax.experimental import pallas as pl
from jax.experimental.pallas import tpu as pltpu

D_MODEL = 1024
DEPTH = 2
GRID_W = 64
POS_BASE = 10000.0
EPS = 1e-6
HG_HEADS = 4
HG_DK = 128
HG_DV = 128
HG_WIDTH = HG_HEADS * HG_DV
SG_GROUPS = 4
SG_WIDTH = 512
SG_GROUP_DIM = SG_WIDTH // SG_GROUPS
SG_CHUNK = 128
POOL_WINDOWS = (2, 4, 8, 16)
POOL_WIDTH = 512
POOL_GROUP_DIM = POOL_WIDTH // len(POOL_WINDOWS)
IN_COLS = 5 * HG_WIDTH + 2 * SG_WIDTH + POOL_WIDTH + 3 * D_MODEL
D_FF = 2816
N_MOD = 6

V7X_SUBLANES = 8
V7X_MXU_DIM = 256
MIB = 2**20

ROWS = 1024
PART_ROWS = 512
HG_BLOCK = 256
HG_LEVELS = 8
IN_CHUNK = 512
FF_CHUNK = V7X_MXU_DIM
FF_SETS = 3
MOD_CHUNK = 3072
VMEM_SMALL = 32 * MIB
VMEM_HGRN = 48 * MIB
VMEM_MIX = 56 * MIB
VMEM_FFN = 52 * MIB

LOG2E = 1.4426950408889634

F32 = jnp.float32
BF16 = jnp.bfloat16


def _dot(a, b):
    return lax.dot_general(a, b, (((1,), (0,)), ((), ())), preferred_element_type=F32)


def _dot_nt(a, b):
    return lax.dot_general(a, b, (((1,), (1,)), ((), ())), preferred_element_type=F32)


def _dot_tn(a, b):
    return lax.dot_general(a, b, (((0,), (0,)), ((), ())), preferred_element_type=F32)


def _sigmoid(x):
    return 0.5 * jnp.tanh(0.5 * x) + 0.5


def _silu(x):
    h = 0.5 * x
    return h * jnp.tanh(h) + h


_GELU_C1 = 0.7978845608028654
_GELU_C2 = _GELU_C1 * 0.044715


def _gelu_tanh(x):
    half_x = 0.5 * x
    return half_x + half_x * jnp.tanh(x * (_GELU_C1 + _GELU_C2 * (x * x)))


def _rms(x, gain):
    return x * lax.rsqrt(jnp.mean(x * x, axis=-1, keepdims=True) + EPS) * gain


def _log1pexp(y):
    return jnp.maximum(y, 0.0) + jnp.log(1.0 + jnp.exp(-jnp.abs(y)))


def _norm_mod(x, gain, scale, shift):
    return (_rms(x, gain) * (1.0 + scale) + shift).astype(BF16)


def _part_rows(seq_len):
    return max(seq_len, PART_ROWS)


def _mod_kernel(c_ref, w_ref, b_ref, o_ref):
    c = _silu(c_ref[...]).astype(BF16)
    o_ref[...] = _dot(c, w_ref[...].astype(BF16)) + b_ref[...]


def _mod_call(cvec, w_ada, b_ada):
    n_rows = cvec.shape[0]
    n_cols = N_MOD * D_MODEL
    return pl.pallas_call(
        _mod_kernel,
        grid=(DEPTH, n_cols // MOD_CHUNK),
        in_specs=[
            pl.BlockSpec((n_rows, D_MODEL), lambda l, n: (0, 0)),
            pl.BlockSpec((None, D_MODEL, MOD_CHUNK), lambda l, n: (l, 0, n)),
            pl.BlockSpec((None, 1, MOD_CHUNK), lambda l, n: (l, 0, n)),
        ],
        out_specs=pl.BlockSpec((None, n_rows, MOD_CHUNK), lambda l, n: (l, 0, n)),
        out_shape=jax.ShapeDtypeStruct((DEPTH, n_rows, n_cols), F32),
        compiler_params=pltpu.CompilerParams(
            dimension_semantics=("arbitrary", "arbitrary"), vmem_limit_bytes=VMEM_SMALL
        ),
        name="adaln_mod",
    )(cvec, w_ada, b_ada.reshape(DEPTH, 1, n_cols))


def _addpos_kernel(x_ref, p_ref, o_ref):
    o_ref[...] = x_ref[...] + p_ref[...]


def _addpos_call(x, pos):
    b, t, d = x.shape
    return pl.pallas_call(
        _addpos_kernel,
        grid=(b,),
        in_specs=[pl.BlockSpec((None, t, d), lambda i: (i, 0, 0)), pl.BlockSpec((t, d), lambda i: (0, 0))],
        out_specs=pl.BlockSpec((None, t, d), lambda i: (i, 0, 0)),
        out_shape=jax.ShapeDtypeStruct(x.shape, x.dtype),
        compiler_params=pltpu.CompilerParams(dimension_semantics=("arbitrary",), vmem_limit_bytes=VMEM_SMALL),
        name="add_pos",
    )(x, pos)


def _grid_pos_embed(n_tokens):
    rows = n_tokens // GRID_W
    r = np.broadcast_to(np.arange(rows, dtype=np.float32)[:, None], (rows, GRID_W)).reshape(-1)
    col = np.broadcast_to(np.arange(GRID_W, dtype=np.float32)[None, :], (rows, GRID_W)).reshape(-1)
    quarter = D_MODEL // 4
    omega = (1.0 / (np.float32(POS_BASE) ** (np.arange(quarter, dtype=np.float32) / quarter))).astype(np.float32)
    ar = r[:, None] * omega[None, :]
    ac = col[:, None] * omega[None, :]
    return jnp.asarray(np.concatenate([np.sin(ar), np.cos(ar), np.sin(ac), np.cos(ac)], axis=-1), F32)


def _ref_rows(b, blk, r):
    n, c = b.shape
    if blk >= V7X_SUBLANES:
        x3 = b.reshape(n // blk, blk, c)
        return jnp.broadcast_to(x3[:, r : r + 1, :], x3.shape).reshape(n, c)
    x3 = b.reshape(n // V7X_SUBLANES, V7X_SUBLANES, c)
    sub = lax.broadcasted_iota(jnp.int32, x3.shape, 1)
    bases = list(range(0, V7X_SUBLANES, blk))
    out = jnp.broadcast_to(x3[:, bases[-1] + r : bases[-1] + r + 1, :], x3.shape)
    for base in reversed(bases[:-1]):
        out = jnp.where(sub < base + blk, jnp.broadcast_to(x3[:, base + r : base + r + 1, :], x3.shape), out)
    return out.reshape(n, c)


def _cum_logdecay(lf, tri):
    hi = lf.astype(BF16)
    r1 = lf - hi.astype(F32)
    mid = r1.astype(BF16)
    lo = (r1 - mid.astype(F32)).astype(BF16)
    return _dot(tri, hi) + _dot(tri, mid) + _dot(tri, lo)


def _hgrn_block(qs, ks, vs, bs, lvqs, sgn_ref, sts):
    half = HG_BLOCK // 2
    lo, hi = slice(0, half), slice(half, HG_BLOCK)
    dirs = (True, False)
    vbs = [v.astype(BF16) for v in vs]
    qbs = [q.astype(BF16) for q in qs]
    kbs = [k.astype(BF16) for k in ks]
    diags = [[jnp.where(lvq == 0, _dot_nt(qb[h], kb[h]), 0.0) for h in (lo, hi)] for qb, kb, lvq in zip(qbs, kbs, lvqs)]
    for m in range(1, HG_LEVELS):
        blk = 2**m
        for d, forward in enumerate(dirs):
            b = bs[d]
            ref = _ref_rows(b, blk, blk // 2 - 1 if forward else blk // 2)
            e = jnp.exp2((b - ref) * sgn_ref[d, m - 1]).astype(BF16)
            qt, kt = qbs[d] * e, kbs[d] * e
            diags[d] = [jnp.where(lvqs[d] == m, _dot_nt(qt[h], kt[h]), a) for h, a in zip((lo, hi), diags[d])]
    outs = []
    for d, forward in enumerate(dirs):
        b, k, st, q, vb = bs[d], ks[d], sts[d], qs[d], vbs[d]
        mid = half - 1 if forward else half
        e = jnp.exp2((b - b[mid : mid + 1, :]) * sgn_ref[d, HG_LEVELS - 1]).astype(BF16)
        qt, kt = qbs[d] * e, kbs[d] * e
        a_lo, a_hi = (a.astype(BF16) for a in diags[d])
        if forward:
            cross = _dot_nt(qt[hi], kt[lo]).astype(BF16)
            o = jnp.concatenate([_dot(a_lo, vb[lo]), _dot(cross, vb[lo]) + _dot(a_hi, vb[hi])], axis=0)
        else:
            cross = _dot_nt(qt[lo], kt[hi]).astype(BF16)
            o = jnp.concatenate([_dot(a_lo, vb[lo]) + _dot(cross, vb[hi]), _dot(a_hi, vb[hi])], axis=0)
        edge = b[HG_BLOCK - 1 : HG_BLOCK, :] if forward else b[0:1, :]
        k_end = (k * jnp.exp(edge - b)).astype(BF16)
        st_new = _dot_tn(vb, k_end)
        if st is not None:
            o = o + _dot_nt((q * jnp.exp(b)).astype(BF16), st.astype(BF16))
            st_new = st_new + st * jnp.exp(edge)
        outs.append((o, st_new))
    return outs


def _hgrn_kernel(*refs, layer, seq_len):
    carry = seq_len > HG_BLOCK
    it = iter(refs)
    x_ref, mod_ref, nmix_ref = next(it), next(it), next(it)
    w_refs = [next(it) for _ in range(5)]
    lbl_ref, hgn_ref, wbr_ref = next(it), next(it), next(it)
    s0_refs = [next(it), next(it)] if carry else None
    sprev_ref = next(it) if (not carry and layer > 0) else None
    phg_ref = next(it)
    sout_ref = None if carry else next(it)
    hb, wcat, q_s, v_s, g_s, kf_s, kb_s, bf_s, bb_s, o_s, y_s, lvf_s, lvb_s, trif_s, trib_s, sgn_s, st_s = it

    j = pl.program_id(1)

    def head(first):
        if first:
            t = lax.broadcasted_iota(jnp.int32, (HG_BLOCK // 2, HG_BLOCK // 2), 0)
            s = lax.broadcasted_iota(jnp.int32, (HG_BLOCK // 2, HG_BLOCK // 2), 1)
            x = t ^ s
            lv = jnp.zeros_like(x)
            for m in range(HG_LEVELS - 1):
                lv = lv + (x >= 2**m).astype(jnp.int32)
            lvf_s[...] = jnp.where(t >= s, lv, -1)
            lvb_s[...] = jnp.where(t <= s, lv, -1)
            t = lax.broadcasted_iota(jnp.int32, (HG_BLOCK, HG_BLOCK), 0)
            s = lax.broadcasted_iota(jnp.int32, (HG_BLOCK, HG_BLOCK), 1)
            trif_s[...] = (t >= s).astype(BF16)
            trib_s[...] = (t <= s).astype(BF16)
            row = lax.broadcasted_iota(jnp.int32, (HG_BLOCK, HG_DK), 0)
            for m in range(1, HG_LEVELS + 1):
                upper = (row & 2 ** (m - 1)) != 0
                sgn_s[0, m - 1] = jnp.where(upper, LOG2E, -LOG2E)
                sgn_s[1, m - 1] = jnp.where(upper, -LOG2E, LOG2E)
        for g, w_ref in enumerate(w_refs):
            wcat[:, g * HG_DK : (g + 1) * HG_DK] = w_ref[...].astype(BF16)
        a0, a1 = lbl_ref[0], lbl_ref[1]
        amax = jnp.maximum(a0, a1)
        e0, e1 = jnp.exp(a0 - amax), jnp.exp(a1 - amax)
        p0, p1 = e0 / (e0 + e1), e1 / (e0 + e1)
        lb = (p0 - p0) if layer == 0 else ((p0 + p1) - p0)
        log_lb = jnp.log(lb)

        wc = wcat[...]
        part = PART_ROWS
        n_parts = ROWS // part

        def zdot(p):
            rows = slice(p * part, (p + 1) * part)
            if first:
                hb[rows, :] = _norm_mod(x_ref[rows, :], nmix_ref[...], mod_ref[0, 1:2, :], mod_ref[0, 0:1, :])
            return _dot(hb[rows, :], wc)

        pending = zdot(0)
        for p in range(n_parts):
            z = pending
            if p + 1 < n_parts:
                pending = zdot(p + 1)
            rows = slice(p * part, (p + 1) * part)
            zq, zff, zfb, zi, zg = (z[:, g * HG_DK : (g + 1) * HG_DK] for g in range(5))
            q_s[rows, :] = _silu(zq) * HG_DK**-0.5
            v_s[rows, :] = zi
            g_s[rows, :] = _silu(zg)
            for d, (zf, k_s, b_s, tri_s) in enumerate(((zff, kf_s, bf_s, trif_s), (zfb, kb_s, bb_s, trib_s))):
                t = jnp.exp(-jnp.abs(zf))
                lf = _log1pexp(log_lb[d : d + 1, :] - zf) - (jnp.maximum(-zf, 0.0) + jnp.log(1.0 + t))
                k_s[rows, :] = (1.0 - lb[d : d + 1, :]) * (jnp.where(zf >= 0.0, t, 1.0) / (1.0 + t))
                tri = tri_s[...]
                for n in range(part // HG_BLOCK):
                    loc = slice(n * HG_BLOCK, (n + 1) * HG_BLOCK)
                    dst = slice(rows.start + n * HG_BLOCK, rows.start + (n + 1) * HG_BLOCK)
                    b_s[dst, :] = _cum_logdecay(lf[loc, :], tri)

        o_s[...] = jnp.zeros_like(o_s)
        n_blk = ROWS // HG_BLOCK
        if sprev_ref is not None:
            sout_ref[:, 0:layer] = sprev_ref[...]
        if carry:
            for d in range(2):
                st_s[d] = s0_refs[d][0].T

        def blocks(n, c):
            blks = (n, n_blk - 1 - n if carry else n)
            rows = [pl.ds(pl.multiple_of(blk * HG_BLOCK, HG_BLOCK), HG_BLOCK) for blk in blks]
            outs = _hgrn_block(
                [q_s[r, :] for r in rows], [kf_s[rows[0], :], kb_s[rows[1], :]], [v_s[r, :] for r in rows],
                [bf_s[rows[0], :], bb_s[rows[1], :]], [lvf_s[...], lvb_s[...]], sgn_s,
                [st_s[d] if carry else None for d in range(2)],
            )
            for d, (o, st_new) in enumerate(outs):
                o_s[rows[d], :] += o
                if carry:
                    st_s[d] = st_new
                else:
                    sout_ref[blks[d], layer, d, 0] = st_new.T
            return c

        lax.fori_loop(0, n_blk, blocks, 0, unroll=True)

        y_s[j] = (_rms(o_s[...], hgn_ref[...]) * g_s[...]).astype(BF16)

    @pl.when(j == 0)
    def _():
        head(True)

    @pl.when(j > 0)
    def _():
        head(False)

    @pl.when(j == HG_HEADS - 1)
    def _():
        y = jnp.concatenate([y_s[h] for h in range(HG_HEADS)], axis=1)
        phg_ref[...] = _dot(y, wbr_ref[...].astype(BF16))


def _hgrn_call(x, mod, seq_len, layer, norm_mix, w_in, lb_logits, hg_norm, w_branch_hg, state0, prev_states):
    n_tok = x.shape[0]
    nb = n_tok // ROWS
    carry = seq_len > HG_BLOCK
    per_seq_mod = mod.shape[0] > 1

    in_specs = [
        pl.BlockSpec((ROWS, D_MODEL), lambda i, j: (i, 0)),
        pl.BlockSpec((1, N_MOD, D_MODEL), (lambda i, j: (i, 0, 0)) if per_seq_mod else (lambda i, j: (0, 0, 0))),
        pl.BlockSpec((None, 1, D_MODEL), lambda i, j: (layer, 0, 0)),
    ]
    args = [x, mod, norm_mix.reshape(DEPTH, 1, D_MODEL)]
    for g in range(5):
        in_specs.append(pl.BlockSpec((None, D_MODEL, HG_DK), lambda i, j, g=g: (layer, 0, g * HG_HEADS + j)))
        args.append(w_in)
    in_specs += [
        pl.BlockSpec((DEPTH, 2, HG_DK), lambda i, j: (0, 0, j)),
        pl.BlockSpec((None, 1, HG_DV), lambda i, j: (layer, 0, 0)),
        pl.BlockSpec((None, HG_WIDTH, D_MODEL), lambda i, j: (layer, 0, 0), pipeline_mode=pl.Buffered(1)),
    ]
    args += [lb_logits, hg_norm.reshape(DEPTH, 1, HG_DV), w_branch_hg]
    if carry:
        assert seq_len == ROWS
        for d in range(2):
            in_specs.append(
                pl.BlockSpec((1, HG_DK, HG_DV), lambda i, j, d=d: (((i * DEPTH + layer) * 2 + d) * HG_HEADS + j, 0, 0))
            )
            args.append(state0)

    out_shape = [jax.ShapeDtypeStruct((n_tok, D_MODEL), F32)]
    out_specs = [pl.BlockSpec((ROWS, D_MODEL), lambda i, j: (i, 0))]
    if not carry:
        assert seq_len == HG_BLOCK
        n_seq = n_tok // seq_len
        seqs = ROWS // seq_len
        if layer > 0:
            in_specs.append(pl.BlockSpec((seqs, layer, 2, 1, HG_DK, HG_DV), lambda i, j: (i, 0, 0, j, 0, 0)))
            args.append(prev_states)
        out_shape.append(jax.ShapeDtypeStruct((n_seq, layer + 1, 2, HG_HEADS, HG_DK, HG_DV), F32))
        out_specs.append(pl.BlockSpec((seqs, layer + 1, 2, 1, HG_DK, HG_DV), lambda i, j: (i, 0, 0, j, 0, 0)))

    head = lambda dt=F32: pltpu.VMEM((ROWS, HG_DK), dt)
    scratch = [
        pltpu.VMEM((ROWS, D_MODEL), BF16),
        pltpu.VMEM((D_MODEL, 5 * HG_DK), BF16),
        head(), head(), head(),
        head(), head(), head(), head(),
        head(),
        pltpu.VMEM((HG_HEADS, ROWS, HG_DV), BF16),
        pltpu.VMEM((HG_BLOCK // 2, HG_BLOCK // 2), jnp.int32),
        pltpu.VMEM((HG_BLOCK // 2, HG_BLOCK // 2), jnp.int32),
        pltpu.VMEM((HG_BLOCK, HG_BLOCK), BF16),
        pltpu.VMEM((HG_BLOCK, HG_BLOCK), BF16),
        pltpu.VMEM((2, HG_LEVELS, HG_BLOCK, HG_DK), F32),
        pltpu.VMEM((2, HG_DV, HG_DK), F32),
    ]
    outs = pl.pallas_call(
        functools.partial(_hgrn_kernel, layer=layer, seq_len=seq_len),
        grid=(nb, HG_HEADS),
        in_specs=in_specs,
        out_specs=out_specs,
        out_shape=out_shape,
        scratch_shapes=scratch,
        compiler_params=pltpu.CompilerParams(
            dimension_semantics=("arbitrary", "arbitrary"), vmem_limit_bytes=VMEM_HGRN
        ),
        name=f"hgrn_l{layer}_t{seq_len}",
    )(*args)
    return (outs[0], None) if carry else (outs[0], outs[1])


_MIX_ORDER = (10, 11, 5, 6, 7, 12, 13, 8, 9)


def _mix_col(k):
    idx = 0
    for n, c in enumerate(_MIX_ORDER):
        idx = idx + jnp.where(k == n, c, 0)
    return idx


def _window_mean_minus_self(p, tpos, seq_len, w):
    n = p.shape[0]
    half = w // 2

    def shifted(x, j):
        valid = (tpos + j >= 0) & (tpos + j < seq_len)
        return jnp.where(valid, pltpu.roll(x, (-j) % n, 0), 0.0)

    ahead, behind, length = p, p, 1
    while length < half:
        ahead = ahead + shifted(ahead, length)
        behind = behind + shifted(behind, -length)
        length *= 2
    acc = ahead + shifted(behind, -1)
    cnt = jnp.minimum(tpos + half, seq_len) - jnp.maximum(tpos - half, 0)
    return acc / cnt.astype(F32) - p


def _mix_kernel(
    x_ref, mod_ref, nmix_ref, w_ref, sgn_ref, sgw_ref, sgb_ref, wbsg_ref, wbpool_ref, poolw_ref, pscale_ref,
    wout_ref, phg_ref, o_ref, hb, u_s, br_s, mrg_s, *, seq_len,
):
    k = pl.program_id(1)
    half = D_MODEL // 2

    @pl.when(k == 0)
    def _():
        hb[...] = _norm_mod(x_ref[...], nmix_ref[...], mod_ref[0, 1:2, :], mod_ref[0, 0:1, :])

    def for_z_parts(consume, part=PART_ROWS):
        w = w_ref[...].astype(BF16)
        n_parts = ROWS // part

        def zdot(p):
            return _dot(hb[p * part : (p + 1) * part, :], w)

        pending = zdot(0)
        for p in range(n_parts):
            z = pending
            if p + 1 < n_parts:
                pending = zdot(p + 1)
            consume(slice(p * part, (p + 1) * part), z)

    for step in (0, 1):

        @pl.when(k == step)
        def _(step=step):
            cols = slice(step * half, (step + 1) * half)

            def gate(rows, z):
                mrg_s[rows, cols] = _sigmoid(z)

            for_z_parts(gate)

    @pl.when(k == 2)
    def _():
        def store_u(rows, z):
            u_s[rows, :] = _gelu_tanh(z)

        for_z_parts(store_u)

    @pl.when(k == 3)
    def _():
        wbsg = wbsg_ref[...].astype(BF16)
        wgs = [sgw_ref[g].astype(BF16) for g in range(SG_GROUPS)]

        def spatial_gating(rows, z):
            v = _rms(_gelu_tanh(z), sgn_ref[...]).astype(BF16)
            for g in range(SG_GROUPS):
                bias = sgb_ref[:, g : g + 1]
                cols = slice(g * SG_GROUP_DIM, (g + 1) * SG_GROUP_DIM)
                for n in range((rows.stop - rows.start) // SG_CHUNK):
                    loc = slice(n * SG_CHUNK, (n + 1) * SG_CHUNK)
                    dst = slice(rows.start + n * SG_CHUNK, rows.start + (n + 1) * SG_CHUNK)
                    mixed = _dot(wgs[g], v[loc, cols]) + bias
                    br_s[dst, cols] = (u_s[dst, cols] * mixed).astype(BF16)
            mrg_s[rows, :] = mrg_s[rows, :] * _dot(br_s[rows, :], wbsg)

        for_z_parts(spatial_gating)

    @pl.when(k == 4)
    def _():
        part = _part_rows(seq_len)
        tpos = lax.broadcasted_iota(jnp.int32, (part, POOL_GROUP_DIM), 0) & (seq_len - 1)

        def pool(rows, z):
            for gi, w in enumerate(POOL_WINDOWS):
                cols = slice(gi * POOL_GROUP_DIM, (gi + 1) * POOL_GROUP_DIM)
                pooled = _window_mean_minus_self(z[:, cols], tpos, seq_len, w)
                out = _dot(pooled.astype(BF16), poolw_ref[gi].astype(BF16)) * pscale_ref[:, cols]
                br_s[rows, cols] = out.astype(BF16)

        for_z_parts(pool, part)

    for step in (5, 6):

        @pl.when(k == step)
        def _(step=step):
            cols = slice((step - 5) * half, (step - 4) * half)
            wbpool = wbpool_ref[:, cols].astype(BF16)

            def gate(rows, z):
                mrg_s[rows, cols] = mrg_s[rows, cols] + _sigmoid(z) * _dot(br_s[rows, :], wbpool)

            for_z_parts(gate)

    @pl.when(k == 7)
    def _():
        cols = slice(0, half)

        def gate(rows, z):
            mrg_s[rows, cols] = mrg_s[rows, cols] + _sigmoid(z) * phg_ref[rows, cols]

        for_z_parts(gate)

    @pl.when(k == 8)
    def _():
        cols = slice(half, D_MODEL)
        wout = wout_ref[...].astype(BF16)

        def gate_and_project(rows, z):
            mrg_s[rows, cols] = mrg_s[rows, cols] + _sigmoid(z) * phg_ref[rows, cols]
            y = _dot(mrg_s[rows, :].astype(BF16), wout)
            o_ref[rows, :] = x_ref[rows, :] + mod_ref[0, 2:3, :] * y

        for_z_parts(gate_and_project)


def _mix_call(x, phg, mod, seq_len, layer, norm_mix, w_in, sg_norm, sg_w, sg_b, w_branch_sg, w_branch_pool, pool_w,
              pool_scale, w_out):
    n_tok = x.shape[0]
    nb = n_tok // ROWS
    per_seq_mod = mod.shape[0] > 1
    const = pl.Buffered(1)
    assert seq_len & (seq_len - 1) == 0 and ROWS % seq_len == 0 and seq_len % SG_CHUNK == 0
    in_specs = [
        pl.BlockSpec((ROWS, D_MODEL), lambda i, k: (i, 0)),
        pl.BlockSpec((1, N_MOD, D_MODEL), (lambda i, k: (i, 0, 0)) if per_seq_mod else (lambda i, k: (0, 0, 0))),
        pl.BlockSpec((None, 1, D_MODEL), lambda i, k: (layer, 0, 0)),
        pl.BlockSpec((None, D_MODEL, IN_CHUNK), lambda i, k: (layer, 0, _mix_col(k))),
        pl.BlockSpec((None, 1, SG_WIDTH), lambda i, k: (layer, 0, 0)),
        pl.BlockSpec((None, SG_GROUPS, SG_CHUNK, SG_CHUNK), lambda i, k: (layer, 0, 0, 0)),
        pl.BlockSpec((None, SG_CHUNK, SG_GROUPS), lambda i, k: (layer, 0, 0)),
        pl.BlockSpec((None, SG_WIDTH, D_MODEL), lambda i, k: (layer, 0, 0), pipeline_mode=const),
        pl.BlockSpec((None, POOL_WIDTH, D_MODEL), lambda i, k: (layer, 0, 0), pipeline_mode=const),
        pl.BlockSpec((None, len(POOL_WINDOWS), POOL_GROUP_DIM, POOL_GROUP_DIM), lambda i, k: (layer, 0, 0, 0)),
        pl.BlockSpec((None, 1, POOL_WIDTH), lambda i, k: (layer, 0, 0)),
        pl.BlockSpec((None, D_MODEL, D_MODEL), lambda i, k: (layer, 0, 0), pipeline_mode=const),
        pl.BlockSpec((ROWS, D_MODEL), lambda i, k: (i, 0)),
    ]
    args = [
        x, mod, norm_mix.reshape(DEPTH, 1, D_MODEL), w_in, sg_norm.reshape(DEPTH, 1, SG_WIDTH), sg_w,
        jnp.swapaxes(sg_b, 1, 2), w_branch_sg, w_branch_pool, pool_w, pool_scale.reshape(DEPTH, 1, POOL_WIDTH),
        w_out, phg,
    ]
    scratch = [
        pltpu.VMEM((ROWS, D_MODEL), BF16),
        pltpu.VMEM((ROWS, SG_WIDTH), F32),
        pltpu.VMEM((ROWS, SG_WIDTH), BF16),
        pltpu.VMEM((ROWS, D_MODEL), F32),
    ]
    return pl.pallas_call(
        functools.partial(_mix_kernel, seq_len=seq_len),
        grid=(nb, len(_MIX_ORDER)),
        in_specs=in_specs,
        out_specs=pl.BlockSpec((ROWS, D_MODEL), lambda i, k: (i, 0)),
        out_shape=jax.ShapeDtypeStruct((n_tok, D_MODEL), F32),
        scratch_shapes=scratch,
        compiler_params=pltpu.CompilerParams(
            dimension_semantics=("arbitrary", "arbitrary"), vmem_limit_bytes=VMEM_MIX
        ),
        name=f"mix_l{layer}_t{seq_len}",
    )(*args)


def _ffn_kernel(x_ref, mod_ref, nffn_ref, *refs, seq_len, final):
    sets = [refs[7 * s : 7 * s + 7] for s in range(FF_SETS)]
    fin_ref, o_ref, hb, acc = refs[7 * FF_SETS :]
    c = pl.program_id(1)
    n_chunks = D_FF // FF_CHUNK
    n_steps = pl.cdiv(n_chunks, FF_SETS)

    part = ROWS
    tpos = lax.broadcasted_iota(jnp.int32, (part, FF_CHUNK), 0) & (seq_len - 1)
    has_prev = tpos >= 1
    has_next = tpos < seq_len - 1

    def conv(h, cw_ref, cb_ref):
        prev = jnp.where(has_prev, pltpu.roll(h, 1, 0), 0.0)
        nxt = jnp.where(has_next, pltpu.roll(h, part - 1, 0), 0.0)
        return prev * cw_ref[0:1, :] + h * cw_ref[1:2, :] + nxt * cw_ref[2:3, :] + cb_ref[...]

    def run(n_sets, first, last):
        ws = [(wa[...].astype(BF16), wb[...].astype(BF16), wd[...].astype(BF16)) for wa, wb, _, _, _, _, wd in sets]
        items = [(p, s) for p in range(ROWS // part) for s in range(n_sets)]

        def up(item):
            p, s = item
            rows = slice(p * part, (p + 1) * part)
            if first and s == 0:
                hb[rows, :] = _norm_mod(x_ref[rows, :], nffn_ref[...], mod_ref[0, 4:5, :], mod_ref[0, 3:4, :])
            h = hb[rows, :]
            return _dot(h, ws[s][0]), _dot(h, ws[s][1])

        pending = up(items[0])
        down = None
        for i, (p, s) in enumerate(items):
            ha, hb2 = pending
            if i + 1 < len(items):
                pending = up(items[i + 1])
            _, _, cwa_ref, cwb_ref, cba_ref, cbb_ref, _ = sets[s]
            a = conv(ha, cwa_ref, cba_ref)
            b = conv(hb2, cwb_ref, cbb_ref)
            d = _dot((_silu(a) * b).astype(BF16), ws[s][2])
            down = d if down is None else down + d
            if s == n_sets - 1:
                rows = slice(p * part, (p + 1) * part)
                total = down if first else acc[rows, :] + down
                if last:
                    y = x_ref[rows, :] + mod_ref[0, 5:6, :] * total
                    o_ref[rows, :] = _rms(y, fin_ref[...]) if final else y
                else:
                    acc[rows, :] = total
                down = None

    kinds = {}
    for step in range(n_steps):
        n_sets = len([s for s in range(FF_SETS) if step + s * n_steps < n_chunks])
        kinds.setdefault((n_sets, step == 0, step == n_steps - 1), []).append(step)
    for (n_sets, first, last), steps in kinds.items():
        cond = functools.reduce(lambda u, v: u | v, [c == st for st in steps])

        @pl.when(cond)
        def _(n_sets=n_sets, first=first, last=last):
            run(n_sets, first, last)


def _ffn_call(x, mod, seq_len, layer, final, norm_ffn, ffn_up, ffn_conv_w, ffn_conv_b, ffn_down, final_norm):
    n_tok = x.shape[0]
    nb = n_tok // ROWS
    nc = D_FF // FF_CHUNK
    n_steps = pl.cdiv(nc, FF_SETS)
    per_seq_mod = mod.shape[0] > 1
    conv_b = ffn_conv_b.reshape(DEPTH, 1, 2 * D_FF)
    in_specs = [
        pl.BlockSpec((ROWS, D_MODEL), lambda i, c: (i, 0)),
        pl.BlockSpec((1, N_MOD, D_MODEL), (lambda i, c: (i, 0, 0)) if per_seq_mod else (lambda i, c: (0, 0, 0))),
        pl.BlockSpec((None, 1, D_MODEL), lambda i, c: (layer, 0, 0)),
    ]
    args = [x, mod, norm_ffn.reshape(DEPTH, 1, D_MODEL)]
    for s in range(FF_SETS):
        chunk = lambda c, s=s: jnp.minimum(c + s * n_steps, nc - 1)
        in_specs += [
            pl.BlockSpec((None, D_MODEL, FF_CHUNK), lambda i, c, f=chunk: (layer, 0, f(c))),
            pl.BlockSpec((None, D_MODEL, FF_CHUNK), lambda i, c, f=chunk: (layer, 0, nc + f(c))),
            pl.BlockSpec((None, 3, FF_CHUNK), lambda i, c, f=chunk: (layer, 0, f(c))),
            pl.BlockSpec((None, 3, FF_CHUNK), lambda i, c, f=chunk: (layer, 0, nc + f(c))),
            pl.BlockSpec((None, 1, FF_CHUNK), lambda i, c, f=chunk: (layer, 0, f(c))),
            pl.BlockSpec((None, 1, FF_CHUNK), lambda i, c, f=chunk: (layer, 0, nc + f(c))),
            pl.BlockSpec((None, FF_CHUNK, D_MODEL), lambda i, c, f=chunk: (layer, f(c), 0)),
        ]
        args += [ffn_up, ffn_up, ffn_conv_w, ffn_conv_w, conv_b, conv_b, ffn_down]
    in_specs.append(pl.BlockSpec((1, D_MODEL), lambda i, c: (0, 0)))
    args.append(final_norm.reshape(1, D_MODEL))
    return pl.pallas_call(
        functools.partial(_ffn_kernel, seq_len=seq_len, final=final),
        grid=(nb, n_steps),
        in_specs=in_specs,
        out_specs=pl.BlockSpec((ROWS, D_MODEL), lambda i, c: (i, 0)),
        out_shape=jax.ShapeDtypeStruct((n_tok, D_MODEL), F32),
        scratch_shapes=[pltpu.VMEM((ROWS, D_MODEL), BF16), pltpu.VMEM((ROWS, D_MODEL), F32)],
        compiler_params=pltpu.CompilerParams(
            dimension_semantics=("arbitrary", "arbitrary"), vmem_limit_bytes=VMEM_FFN
        ),
        name=f"ffn_l{layer}_t{seq_len}",
    )(*args)


def kernel(x_prompt, x_sample, c, state_hgrn, c_ctx, norm_mix, norm_ffn, w_ada, b_ada, w_in, lb_logits, hg_norm,
           w_branch_hg, w_branch_sg, w_branch_pool, w_out, sg_norm, sg_w, sg_b, pool_w, pool_scale, ffn_up,
           ffn_conv_w, ffn_conv_b, ffn_down, final_norm):
    assert w_in.shape == (DEPTH, D_MODEL, IN_COLS)
    n_ctx, t_ctx, _ = x_prompt.shape
    n_lat, t_lat, _ = x_sample.shape

    n_cond = 1 + n_lat
    pad = -n_cond % V7X_SUBLANES
    cvec = jnp.concatenate([c_ctx[None, :], c, jnp.zeros((pad, D_MODEL), F32)], axis=0)
    mod = _mod_call(cvec, w_ada, b_ada).reshape(DEPTH, n_cond + pad, N_MOD, D_MODEL)

    xs = _addpos_call(x_sample, _grid_pos_embed(t_lat)).reshape(n_lat * t_lat, D_MODEL)
    xp = x_prompt.reshape(n_ctx * t_ctx, D_MODEL)
    state0 = state_hgrn.reshape(n_lat * DEPTH * 2 * HG_HEADS, HG_DK, HG_DV)

    states = None
    for layer in range(DEPTH):
        final = layer == DEPTH - 1
        groups = []
        for x, m, t, s0 in ((xp, mod[layer, 0:1], t_ctx, None), (xs, mod[layer, 1:n_cond], t_lat, state0)):
            phg, s_fin = _hgrn_call(x, m, t, layer, norm_mix, w_in, lb_logits, hg_norm, w_branch_hg, s0, states)
            x1 = _mix_call(x, phg, m, t, layer, norm_mix, w_in, sg_norm, sg_w, sg_b, w_branch_sg, w_branch_pool,
                           pool_w, pool_scale, w_out)
            x2 = _ffn_call(x1, m, t, layer, final, norm_ffn, ffn_up, ffn_conv_w, ffn_conv_b, ffn_down, final_norm)
            groups.append((x2, s_fin))
        (xp, states), (xs, _) = groups

    y_prompt = xp.reshape(x_prompt.shape)
    y_sample = xs.reshape(x_sample.shape)
    return (y_prompt, y_sample, states)
```

```python
import functools

import jax
import jax.numpy as jnp
import numpy as np
from jax import lax
from jax.experimental import pallas as pl
from jax.experimental.pallas import tpu as pltpu

D_MODEL = 1024
DEPTH = 2
GRID_W = 64
POS_BASE = 10000.0
EPS = 1e-6
HG_HEADS = 4
HG_DK = 128
HG_DV = 128
HG_WIDTH = HG_HEADS * HG_DV
SG_GROUPS = 4
SG_WIDTH = 512
SG_GROUP_DIM = SG_WIDTH // SG_GROUPS
SG_CHUNK = 128
POOL_WINDOWS = (2, 4, 8, 16)
POOL_WIDTH = 512
POOL_GROUP_DIM = POOL_WIDTH // len(POOL_WINDOWS)
IN_COLS = 5 * HG_WIDTH + 2 * SG_WIDTH + POOL_WIDTH + 3 * D_MODEL
D_FF = 2816
N_MOD = 6

V7X_SUBLANES = 8
V7X_MXU_DIM = 256
MIB = 2**20

ROWS = 1024
PART_ROWS = 512
HG_BLOCK = 256
HG_LEVELS = 8
IN_CHUNK = 512
HG_GATE_COL = 5 * HG_WIDTH + 2 * SG_WIDTH + POOL_WIDTH
GATE_CHUNK = D_MODEL // HG_HEADS
FF_CHUNK = V7X_MXU_DIM
FF_SETS = 3
MOD_CHUNK = 3072
VMEM_SMALL = 32 * MIB
VMEM_HGRN = 48 * MIB
VMEM_MIX = 56 * MIB
VMEM_FFN = 52 * MIB

LOG2E = 1.4426950408889634

F32 = jnp.float32
BF16 = jnp.bfloat16


def _dot(a, b):
    return lax.dot_general(a, b, (((1,), (0,)), ((), ())), preferred_element_type=F32)


def _dot_nt(a, b):
    return lax.dot_general(a, b, (((1,), (1,)), ((), ())), preferred_element_type=F32)


def _dot_tn(a, b):
    return lax.dot_general(a, b, (((0,), (0,)), ((), ())), preferred_element_type=F32)


def _sigmoid(x):
    return 0.5 * jnp.tanh(0.5 * x) + 0.5


def _silu(x):
    h = 0.5 * x
    return h * jnp.tanh(h) + h


_GELU_C1 = 0.7978845608028654
_GELU_C2 = _GELU_C1 * 0.044715


def _gelu_tanh(x):
    half_x = 0.5 * x
    return half_x + half_x * jnp.tanh(x * (_GELU_C1 + _GELU_C2 * (x * x)))


def _rms(x, gain):
    return x * lax.rsqrt(jnp.mean(x * x, axis=-1, keepdims=True) + EPS) * gain


def _log1pexp(y):
    return jnp.maximum(y, 0.0) + jnp.log(1.0 + jnp.exp(-jnp.abs(y)))


def _norm_mod(x, gain, scale, shift):
    return (_rms(x, gain) * (1.0 + scale) + shift).astype(BF16)


def _part_rows(seq_len):
    return max(seq_len, PART_ROWS)


def _mod_kernel(c_ref, w_ref, b_ref, o_ref):
    c = _silu(c_ref[...]).astype(BF16)
    o_ref[...] = _dot(c, w_ref[...].astype(BF16)) + b_ref[...]


def _mod_call(cvec, w_ada, b_ada):
    n_rows = cvec.shape[0]
    n_cols = N_MOD * D_MODEL
    return pl.pallas_call(
        _mod_kernel,
        grid=(DEPTH, n_cols // MOD_CHUNK),
        in_specs=[
            pl.BlockSpec((n_rows, D_MODEL), lambda l, n: (0, 0)),
            pl.BlockSpec((None, D_MODEL, MOD_CHUNK), lambda l, n: (l, 0, n)),
            pl.BlockSpec((None, 1, MOD_CHUNK), lambda l, n: (l, 0, n)),
        ],
        out_specs=pl.BlockSpec((None, n_rows, MOD_CHUNK), lambda l, n: (l, 0, n)),
        out_shape=jax.ShapeDtypeStruct((DEPTH, n_rows, n_cols), F32),
        compiler_params=pltpu.CompilerParams(
            dimension_semantics=("arbitrary", "arbitrary"), vmem_limit_bytes=VMEM_SMALL
        ),
        name="adaln_mod",
    )(cvec, w_ada, b_ada.reshape(DEPTH, 1, n_cols))


def _addpos_kernel(x_ref, p_ref, o_ref):
    o_ref[...] = x_ref[...] + p_ref[...]


def _addpos_call(x, pos):
    b, t, d = x.shape
    return pl.pallas_call(
        _addpos_kernel,
        grid=(b,),
        in_specs=[pl.BlockSpec((None, t, d), lambda i: (i, 0, 0)), pl.BlockSpec((t, d), lambda i: (0, 0))],
        out_specs=pl.BlockSpec((None, t, d), lambda i: (i, 0, 0)),
        out_shape=jax.ShapeDtypeStruct(x.shape, x.dtype),
        compiler_params=pltpu.CompilerParams(dimension_semantics=("arbitrary",), vmem_limit_bytes=VMEM_SMALL),
        name="add_pos",
    )(x, pos)


def _grid_pos_embed(n_tokens):
    rows = n_tokens // GRID_W
    r = np.broadcast_to(np.arange(rows, dtype=np.float32)[:, None], (rows, GRID_W)).reshape(-1)
    col = np.broadcast_to(np.arange(GRID_W, dtype=np.float32)[None, :], (rows, GRID_W)).reshape(-1)
    quarter = D_MODEL // 4
    omega = (1.0 / (np.float32(POS_BASE) ** (np.arange(quarter, dtype=np.float32) / quarter))).astype(np.float32)
    ar = r[:, None] * omega[None, :]
    ac = col[:, None] * omega[None, :]
    return jnp.asarray(np.concatenate([np.sin(ar), np.cos(ar), np.sin(ac), np.cos(ac)], axis=-1), F32)


def _ref_rows(b, blk, r):
    n, c = b.shape
    if blk >= V7X_SUBLANES:
        x3 = b.reshape(n // blk, blk, c)
        return jnp.broadcast_to(x3[:, r : r + 1, :], x3.shape).reshape(n, c)
    x3 = b.reshape(n // V7X_SUBLANES, V7X_SUBLANES, c)
    sub = lax.broadcasted_iota(jnp.int32, x3.shape, 1)
    bases = list(range(0, V7X_SUBLANES, blk))
    out = jnp.broadcast_to(x3[:, bases[-1] + r : bases[-1] + r + 1, :], x3.shape)
    for base in reversed(bases[:-1]):
        out = jnp.where(sub < base + blk, jnp.broadcast_to(x3[:, base + r : base + r + 1, :], x3.shape), out)
    return out.reshape(n, c)


def _cum_logdecay(lf, tri):
    hi = lf.astype(BF16)
    r1 = lf - hi.astype(F32)
    mid = r1.astype(BF16)
    lo = (r1 - mid.astype(F32)).astype(BF16)
    return _dot(tri, hi) + _dot(tri, mid) + _dot(tri, lo)


def _hgrn_block(qs, ks, vs, bs, lvqs, sgn_ref, sts):
    half = HG_BLOCK // 2
    lo, hi = slice(0, half), slice(half, HG_BLOCK)
    dirs = (True, False)
    vbs = [v.astype(BF16) for v in vs]
    qbs = [q.astype(BF16) for q in qs]
    kbs = [k.astype(BF16) for k in ks]
    diags = [[jnp.where(lvq == 0, _dot_nt(qb[h], kb[h]), 0.0) for h in (lo, hi)] for qb, kb, lvq in zip(qbs, kbs, lvqs)]
    for m in range(1, HG_LEVELS):
        blk = 2**m
        for d, forward in enumerate(dirs):
            b = bs[d]
            ref = _ref_rows(b, blk, blk // 2 - 1 if forward else blk // 2)
            e = jnp.exp2((b - ref) * sgn_ref[d, m - 1]).astype(BF16)
            qt, kt = qbs[d] * e, kbs[d] * e
            diags[d] = [jnp.where(lvqs[d] == m, _dot_nt(qt[h], kt[h]), a) for h, a in zip((lo, hi), diags[d])]
    outs = []
    for d, forward in enumerate(dirs):
        b, k, st, q, vb = bs[d], ks[d], sts[d], qs[d], vbs[d]
        mid = half - 1 if forward else half
        e = jnp.exp2((b - b[mid : mid + 1, :]) * sgn_ref[d, HG_LEVELS - 1]).astype(BF16)
        qt, kt = qbs[d] * e, kbs[d] * e
        a_lo, a_hi = (a.astype(BF16) for a in diags[d])
        if forward:
            cross = _dot_nt(qt[hi], kt[lo]).astype(BF16)
            o = jnp.concatenate([_dot(a_lo, vb[lo]), _dot(cross, vb[lo]) + _dot(a_hi, vb[hi])], axis=0)
        else:
            cross = _dot_nt(qt[lo], kt[hi]).astype(BF16)
            o = jnp.concatenate([_dot(a_lo, vb[lo]) + _dot(cross, vb[hi]), _dot(a_hi, vb[hi])], axis=0)
        edge = b[HG_BLOCK - 1 : HG_BLOCK, :] if forward else b[0:1, :]
        k_end = (k * jnp.exp(edge - b)).astype(BF16)
        st_new = _dot_tn(vb, k_end)
        if st is not None:
            o = o + _dot_nt((q * jnp.exp(b)).astype(BF16), st.astype(BF16))
            st_new = st_new + st * jnp.exp(edge)
        outs.append((o, st_new))
    return outs


def _hgrn_kernel(*refs, layer, seq_len):
    carry = seq_len > HG_BLOCK
    it = iter(refs)
    x_ref, mod_ref, nmix_ref = next(it), next(it), next(it)
    w_refs = [next(it) for _ in range(5)]
    lbl_ref, hgn_ref, wbr_ref, wgate_ref = next(it), next(it), next(it), next(it)
    s0_refs = [next(it), next(it)] if carry else None
    sprev_ref = next(it) if (not carry and layer > 0) else None
    phg_ref = next(it)
    sout_ref = None if carry else next(it)
    hb, wcat, q_s, v_s, g_s, kf_s, kb_s, bf_s, bb_s, o_s, y_s, gate_s, lvf_s, lvb_s, trif_s, trib_s, sgn_s, st_s = it

    j = pl.program_id(1)

    def head(first):
        if first:
            t = lax.broadcasted_iota(jnp.int32, (HG_BLOCK // 2, HG_BLOCK // 2), 0)
            s = lax.broadcasted_iota(jnp.int32, (HG_BLOCK // 2, HG_BLOCK // 2), 1)
            x = t ^ s
            lv = jnp.zeros_like(x)
            for m in range(HG_LEVELS - 1):
                lv = lv + (x >= 2**m).astype(jnp.int32)
            lvf_s[...] = jnp.where(t >= s, lv, -1)
            lvb_s[...] = jnp.where(t <= s, lv, -1)
            t = lax.broadcasted_iota(jnp.int32, (HG_BLOCK, HG_BLOCK), 0)
            s = lax.broadcasted_iota(jnp.int32, (HG_BLOCK, HG_BLOCK), 1)
            trif_s[...] = (t >= s).astype(BF16)
            trib_s[...] = (t <= s).astype(BF16)
            row = lax.broadcasted_iota(jnp.int32, (HG_BLOCK, HG_DK), 0)
            for m in range(1, HG_LEVELS + 1):
                upper = (row & 2 ** (m - 1)) != 0
                sgn_s[0, m - 1] = jnp.where(upper, LOG2E, -LOG2E)
                sgn_s[1, m - 1] = jnp.where(upper, -LOG2E, LOG2E)
        for g, w_ref in enumerate(w_refs):
            wcat[:, g * HG_DK : (g + 1) * HG_DK] = w_ref[...].astype(BF16)
        a0, a1 = lbl_ref[0], lbl_ref[1]
        amax = jnp.maximum(a0, a1)
        e0, e1 = jnp.exp(a0 - amax), jnp.exp(a1 - amax)
        p0, p1 = e0 / (e0 + e1), e1 / (e0 + e1)
        lb = (p0 - p0) if layer == 0 else ((p0 + p1) - p0)
        log_lb = jnp.log(lb)

        wc = wcat[...]
        part = PART_ROWS
        n_parts = ROWS // part

        def zdot(p):
            rows = slice(p * part, (p + 1) * part)
            if first:
                hb[rows, :] = _norm_mod(x_ref[rows, :], nmix_ref[...], mod_ref[0, 1:2, :], mod_ref[0, 0:1, :])
            return _dot(hb[rows, :], wc)

        pending = zdot(0)
        for p in range(n_parts):
            z = pending
            if p + 1 < n_parts:
                pending = zdot(p + 1)
            rows = slice(p * part, (p + 1) * part)
            zq, zff, zfb, zi, zg = (z[:, g * HG_DK : (g + 1) * HG_DK] for g in range(5))
            q_s[rows, :] = _silu(zq) * HG_DK**-0.5
            v_s[rows, :] = zi
            g_s[rows, :] = _silu(zg)
            for d, (zf, k_s, b_s, tri_s) in enumerate(((zff, kf_s, bf_s, trif_s), (zfb, kb_s, bb_s, trib_s))):
                t = jnp.exp(-jnp.abs(zf))
                lf = _log1pexp(log_lb[d : d + 1, :] - zf) - (jnp.maximum(-zf, 0.0) + jnp.log(1.0 + t))
                k_s[rows, :] = (1.0 - lb[d : d + 1, :]) * (jnp.where(zf >= 0.0, t, 1.0) / (1.0 + t))
                tri = tri_s[...]
                for n in range(part // HG_BLOCK):
                    loc = slice(n * HG_BLOCK, (n + 1) * HG_BLOCK)
                    dst = slice(rows.start + n * HG_BLOCK, rows.start + (n + 1) * HG_BLOCK)
                    b_s[dst, :] = _cum_logdecay(lf[loc, :], tri)

        o_s[...] = jnp.zeros_like(o_s)
        gate_s[j] = _sigmoid(_dot(hb[...], wgate_ref[...].astype(BF16)))

        n_blk = ROWS // HG_BLOCK
        if sprev_ref is not None:
            sout_ref[:, 0:layer] = sprev_ref[...]
        if carry:
            for d in range(2):
                st_s[d] = s0_refs[d][0].T

        def blocks(n, c):
            blks = (n, n_blk - 1 - n if carry else n)
            rows = [pl.ds(pl.multiple_of(blk * HG_BLOCK, HG_BLOCK), HG_BLOCK) for blk in blks]
            outs = _hgrn_block(
                [q_s[r, :] for r in rows], [kf_s[rows[0], :], kb_s[rows[1], :]], [v_s[r, :] for r in rows],
                [bf_s[rows[0], :], bb_s[rows[1], :]], [lvf_s[...], lvb_s[...]], sgn_s,
                [st_s[d] if carry else None for d in range(2)],
            )
            for d, (o, st_new) in enumerate(outs):
                o_s[rows[d], :] += o
                if carry:
                    st_s[d] = st_new
                else:
                    sout_ref[blks[d], layer, d, 0] = st_new.T
            return c

        lax.fori_loop(0, n_blk, blocks, 0, unroll=True)

        y_s[j] = (_rms(o_s[...], hgn_ref[...]) * g_s[...]).astype(BF16)

    @pl.when(j == 0)
    def _():
        head(True)

    @pl.when(j > 0)
    def _():
        head(False)

    @pl.when(j == HG_HEADS - 1)
    def _():
        y = jnp.concatenate([y_s[h] for h in range(HG_HEADS)], axis=1)
        proj = _dot(y, wbr_ref[...].astype(BF16))
        for h in range(HG_HEADS):
            cols = slice(h * GATE_CHUNK, (h + 1) * GATE_CHUNK)
            phg_ref[:, cols] = gate_s[h] * proj[:, cols]


def _hgrn_call(x, mod, seq_len, layer, norm_mix, w_in, lb_logits, hg_norm, w_branch_hg, state0, prev_states):
    n_tok = x.shape[0]
    nb = n_tok // ROWS
    carry = seq_len > HG_BLOCK
    per_seq_mod = mod.shape[0] > 1

    in_specs = [
        pl.BlockSpec((ROWS, D_MODEL), lambda i, j: (i, 0)),
        pl.BlockSpec((1, N_MOD, D_MODEL), (lambda i, j: (i, 0, 0)) if per_seq_mod else (lambda i, j: (0, 0, 0))),
        pl.BlockSpec((None, 1, D_MODEL), lambda i, j: (layer, 0, 0)),
    ]
    args = [x, mod, norm_mix.reshape(DEPTH, 1, D_MODEL)]
    for g in range(5):
        in_specs.append(pl.BlockSpec((None, D_MODEL, HG_DK), lambda i, j, g=g: (layer, 0, g * HG_HEADS + j)))
        args.append(w_in)
    in_specs += [
        pl.BlockSpec((DEPTH, 2, HG_DK), lambda i, j: (0, 0, j)),
        pl.BlockSpec((None, 1, HG_DV), lambda i, j: (layer, 0, 0)),
        pl.BlockSpec((None, HG_WIDTH, D_MODEL), lambda i, j: (layer, 0, 0), pipeline_mode=pl.Buffered(1)),
        pl.BlockSpec((None, D_MODEL, GATE_CHUNK), lambda i, j: (layer, 0, HG_GATE_COL // GATE_CHUNK + j)),
    ]
    args += [lb_logits, hg_norm.reshape(DEPTH, 1, HG_DV), w_branch_hg, w_in]
    if carry:
        assert seq_len == ROWS
        for d in range(2):
            in_specs.append(
                pl.BlockSpec((1, HG_DK, HG_DV), lambda i, j, d=d: (((i * DEPTH + layer) * 2 + d) * HG_HEADS + j, 0, 0))
            )
            args.append(state0)

    out_shape = [jax.ShapeDtypeStruct((n_tok, D_MODEL), F32)]
    out_specs = [pl.BlockSpec((ROWS, D_MODEL), lambda i, j: (i, 0))]
    if not carry:
        assert seq_len == HG_BLOCK
        n_seq = n_tok // seq_len
        seqs = ROWS // seq_len
        if layer > 0:
            in_specs.append(pl.BlockSpec((seqs, layer, 2, 1, HG_DK, HG_DV), lambda i, j: (i, 0, 0, j, 0, 0)))
            args.append(prev_states)
        out_shape.append(jax.ShapeDtypeStruct((n_seq, layer + 1, 2, HG_HEADS, HG_DK, HG_DV), F32))
        out_specs.append(pl.BlockSpec((seqs, layer + 1, 2, 1, HG_DK, HG_DV), lambda i, j: (i, 0, 0, j, 0, 0)))

    head = lambda dt=F32: pltpu.VMEM((ROWS, HG_DK), dt)
    scratch = [
        pltpu.VMEM((ROWS, D_MODEL), BF16),
        pltpu.VMEM((D_MODEL, 5 * HG_DK), BF16),
        head(), head(), head(),
        head(), head(), head(), head(),
        head(),
        pltpu.VMEM((HG_HEADS, ROWS, HG_DV), BF16),
        pltpu.VMEM((HG_HEADS, ROWS, GATE_CHUNK), F32),
        pltpu.VMEM((HG_BLOCK // 2, HG_BLOCK // 2), jnp.int32),
        pltpu.VMEM((HG_BLOCK // 2, HG_BLOCK // 2), jnp.int32),
        pltpu.VMEM((HG_BLOCK, HG_BLOCK), BF16),
        pltpu.VMEM((HG_BLOCK, HG_BLOCK), BF16),
        pltpu.VMEM((2, HG_LEVELS, HG_BLOCK, HG_DK), F32),
        pltpu.VMEM((2, HG_DV, HG_DK), F32),
    ]
    outs = pl.pallas_call(
        functools.partial(_hgrn_kernel, layer=layer, seq_len=seq_len),
        grid=(nb, HG_HEADS),
        in_specs=in_specs,
        out_specs=out_specs,
        out_shape=out_shape,
        scratch_shapes=scratch,
        compiler_params=pltpu.CompilerParams(
            dimension_semantics=("arbitrary", "arbitrary"), vmem_limit_bytes=VMEM_HGRN
        ),
        name=f"hgrn_l{layer}_t{seq_len}",
    )(*args)
    return (outs[0], None) if carry else (outs[0], outs[1])


_MIX_ORDER = (10, 11, 5, 6, 7, 12, 13)


def _mix_col(k):
    idx = 0
    for n, c in enumerate(_MIX_ORDER):
        idx = idx + jnp.where(k == n, c, 0)
    return idx


def _window_mean_minus_self(p, tpos, seq_len, w):
    n = p.shape[0]
    half = w // 2

    def shifted(x, j):
        valid = (tpos + j >= 0) & (tpos + j < seq_len)
        return jnp.where(valid, pltpu.roll(x, (-j) % n, 0), 0.0)

    ahead, behind, length = p, p, 1
    while length < half:
        ahead = ahead + shifted(ahead, length)
        behind = behind + shifted(behind, -length)
        length *= 2
    acc = ahead + shifted(behind, -1)
    cnt = jnp.minimum(tpos + half, seq_len) - jnp.maximum(tpos - half, 0)
    return acc / cnt.astype(F32) - p


def _mix_kernel(
    x_ref, mod_ref, nmix_ref, w_ref, sgn_ref, sgw_ref, sgb_ref, wbsg_ref, wbpool_ref, poolw_ref, pscale_ref,
    wout_ref, phg_ref, o_ref, hb, u_s, br_s, mrg_s, *, seq_len,
):
    k = pl.program_id(1)
    half = D_MODEL // 2

    @pl.when(k == 0)
    def _():
        hb[...] = _norm_mod(x_ref[...], nmix_ref[...], mod_ref[0, 1:2, :], mod_ref[0, 0:1, :])

    def for_z_parts(consume, part=PART_ROWS):
        w = w_ref[...].astype(BF16)
        n_parts = ROWS // part

        def zdot(p):
            return _dot(hb[p * part : (p + 1) * part, :], w)

        pending = zdot(0)
        for p in range(n_parts):
            z = pending
            if p + 1 < n_parts:
                pending = zdot(p + 1)
            consume(slice(p * part, (p + 1) * part), z)

    for step in (0, 1):

        @pl.when(k == step)
        def _(step=step):
            cols = slice(step * half, (step + 1) * half)

            def gate(rows, z):
                mrg_s[rows, cols] = _sigmoid(z)

            for_z_parts(gate)

    @pl.when(k == 2)
    def _():
        def store_u(rows, z):
            u_s[rows, :] = _gelu_tanh(z)

        for_z_parts(store_u)

    @pl.when(k == 3)
    def _():
        wbsg = wbsg_ref[...].astype(BF16)
        wgs = [sgw_ref[g].astype(BF16) for g in range(SG_GROUPS)]

        def spatial_gating(rows, z):
            v = _rms(_gelu_tanh(z), sgn_ref[...]).astype(BF16)
            for g in range(SG_GROUPS):
                bias = sgb_ref[:, g : g + 1]
                cols = slice(g * SG_GROUP_DIM, (g + 1) * SG_GROUP_DIM)
                for n in range((rows.stop - rows.start) // SG_CHUNK):
                    loc = slice(n * SG_CHUNK, (n + 1) * SG_CHUNK)
                    dst = slice(rows.start + n * SG_CHUNK, rows.start + (n + 1) * SG_CHUNK)
                    mixed = _dot(wgs[g], v[loc, cols]) + bias
                    br_s[dst, cols] = (u_s[dst, cols] * mixed).astype(BF16)
            mrg_s[rows, :] = mrg_s[rows, :] * _dot(br_s[rows, :], wbsg)

        for_z_parts(spatial_gating)

    @pl.when(k == 4)
    def _():
        part = _part_rows(seq_len)
        tpos = lax.broadcasted_iota(jnp.int32, (part, POOL_GROUP_DIM), 0) & (seq_len - 1)

        def pool(rows, z):
            for gi, w in enumerate(POOL_WINDOWS):
                cols = slice(gi * POOL_GROUP_DIM, (gi + 1) * POOL_GROUP_DIM)
                pooled = _window_mean_minus_self(z[:, cols], tpos, seq_len, w)
                out = _dot(pooled.astype(BF16), poolw_ref[gi].astype(BF16)) * pscale_ref[:, cols]
                br_s[rows, cols] = out.astype(BF16)

        for_z_parts(pool, part)

    @pl.when(k == 5)
    def _():
        cols = slice(0, half)
        wbpool = wbpool_ref[:, cols].astype(BF16)

        def gate(rows, z):
            mrg_s[rows, cols] = mrg_s[rows, cols] + _sigmoid(z) * _dot(br_s[rows, :], wbpool)

        for_z_parts(gate)

    @pl.when(k == 6)
    def _():
        cols = slice(half, D_MODEL)
        wbpool = wbpool_ref[:, cols].astype(BF16)
        wout = wout_ref[...].astype(BF16)

        def gate_and_project(rows, z):
            mrg_s[rows, cols] = mrg_s[rows, cols] + _sigmoid(z) * _dot(br_s[rows, :], wbpool)
            merged = mrg_s[rows, :] + phg_ref[rows, :]
            y = _dot(merged.astype(BF16), wout)
            o_ref[rows, :] = x_ref[rows, :] + mod_ref[0, 2:3, :] * y

        for_z_parts(gate_and_project)


def _mix_call(x, phg, mod, seq_len, layer, norm_mix, w_in, sg_norm, sg_w, sg_b, w_branch_sg, w_branch_pool, pool_w,
              pool_scale, w_out):
    n_tok = x.shape[0]
    nb = n_tok // ROWS
    per_seq_mod = mod.shape[0] > 1
    const = pl.Buffered(1)
    assert seq_len & (seq_len - 1) == 0 and ROWS % seq_len == 0 and seq_len % SG_CHUNK == 0
    in_specs = [
        pl.BlockSpec((ROWS, D_MODEL), lambda i, k: (i, 0)),
        pl.BlockSpec((1, N_MOD, D_MODEL), (lambda i, k: (i, 0, 0)) if per_seq_mod else (lambda i, k: (0, 0, 0))),
        pl.BlockSpec((None, 1, D_MODEL), lambda i, k: (layer, 0, 0)),
        pl.BlockSpec((None, D_MODEL, IN_CHUNK), lambda i, k: (layer, 0, _mix_col(k))),
        pl.BlockSpec((None, 1, SG_WIDTH), lambda i, k: (layer, 0, 0)),
        pl.BlockSpec((None, SG_GROUPS, SG_CHUNK, SG_CHUNK), lambda i, k: (layer, 0, 0, 0)),
        pl.BlockSpec((None, SG_CHUNK, SG_GROUPS), lambda i, k: (layer, 0, 0)),
        pl.BlockSpec((None, SG_WIDTH, D_MODEL), lambda i, k: (layer, 0, 0), pipeline_mode=const),
        pl.BlockSpec((None, POOL_WIDTH, D_MODEL), lambda i, k: (layer, 0, 0), pipeline_mode=const),
        pl.BlockSpec((None, len(POOL_WINDOWS), POOL_GROUP_DIM, POOL_GROUP_DIM), lambda i, k: (layer, 0, 0, 0)),
        pl.BlockSpec((None, 1, POOL_WIDTH), lambda i, k: (layer, 0, 0)),
        pl.BlockSpec((None, D_MODEL, D_MODEL), lambda i, k: (layer, 0, 0), pipeline_mode=const),
        pl.BlockSpec((ROWS, D_MODEL), lambda i, k: (i, 0)),
    ]
    args = [
        x, mod, norm_mix.reshape(DEPTH, 1, D_MODEL), w_in, sg_norm.reshape(DEPTH, 1, SG_WIDTH), sg_w,
        jnp.swapaxes(sg_b, 1, 2), w_branch_sg, w_branch_pool, pool_w, pool_scale.reshape(DEPTH, 1, POOL_WIDTH),
        w_out, phg,
    ]
    scratch = [
        pltpu.VMEM((ROWS, D_MODEL), BF16),
        pltpu.VMEM((ROWS, SG_WIDTH), F32),
        pltpu.VMEM((ROWS, SG_WIDTH), BF16),
        pltpu.VMEM((ROWS, D_MODEL), F32),
    ]
    return pl.pallas_call(
        functools.partial(_mix_kernel, seq_len=seq_len),
        grid=(nb, len(_MIX_ORDER)),
        in_specs=in_specs,
        out_specs=pl.BlockSpec((ROWS, D_MODEL), lambda i, k: (i, 0)),
        out_shape=jax.ShapeDtypeStruct((n_tok, D_MODEL), F32),
        scratch_shapes=scratch,
        compiler_params=pltpu.CompilerParams(
            dimension_semantics=("arbitrary", "arbitrary"), vmem_limit_bytes=VMEM_MIX
        ),
        name=f"mix_l{layer}_t{seq_len}",
    )(*args)


def _ffn_kernel(x_ref, mod_ref, nffn_ref, *refs, seq_len, final):
    sets = [refs[7 * s : 7 * s + 7] for s in range(FF_SETS)]
    fin_ref, o_ref, hb, acc = refs[7 * FF_SETS :]
    c = pl.program_id(1)
    n_chunks = D_FF // FF_CHUNK
    n_steps = pl.cdiv(n_chunks, FF_SETS)

    part = ROWS
    tpos = lax.broadcasted_iota(jnp.int32, (part, FF_CHUNK), 0) & (seq_len - 1)
    has_prev = tpos >= 1
    has_next = tpos < seq_len - 1

    def conv(h, cw_ref, cb_ref):
        prev = jnp.where(has_prev, pltpu.roll(h, 1, 0), 0.0)
        nxt = jnp.where(has_next, pltpu.roll(h, part - 1, 0), 0.0)
        return prev * cw_ref[0:1, :] + h * cw_ref[1:2, :] + nxt * cw_ref[2:3, :] + cb_ref[...]

    def run(n_sets, first, last):
        ws = [(wa[...].astype(BF16), wb[...].astype(BF16), wd[...].astype(BF16)) for wa, wb, _, _, _, _, wd in sets]
        items = [(p, s) for p in range(ROWS // part) for s in range(n_sets)]

        def up(item):
            p, s = item
            rows = slice(p * part, (p + 1) * part)
            if first and s == 0:
                hb[rows, :] = _norm_mod(x_ref[rows, :], nffn_ref[...], mod_ref[0, 4:5, :], mod_ref[0, 3:4, :])
            h = hb[rows, :]
            return _dot(h, ws[s][0]), _dot(h, ws[s][1])

        pending = up(items[0])
        down = None
        for i, (p, s) in enumerate(items):
            ha, hb2 = pending
            if i + 1 < len(items):
                pending = up(items[i + 1])
            _, _, cwa_ref, cwb_ref, cba_ref, cbb_ref, _ = sets[s]
            a = conv(ha, cwa_ref, cba_ref)
            b = conv(hb2, cwb_ref, cbb_ref)
            d = _dot((_silu(a) * b).astype(BF16), ws[s][2])
            down = d if down is None else down + d
            if s == n_sets - 1:
                rows = slice(p * part, (p + 1) * part)
                total = down if first else acc[rows, :] + down
                if last:
                    y = x_ref[rows, :] + mod_ref[0, 5:6, :] * total
                    o_ref[rows, :] = _rms(y, fin_ref[...]) if final else y
                else:
                    acc[rows, :] = total
                down = None

    kinds = {}
    for step in range(n_steps):
        n_sets = len([s for s in range(FF_SETS) if step + s * n_steps < n_chunks])
        kinds.setdefault((n_sets, step == 0, step == n_steps - 1), []).append(step)
    for (n_sets, first, last), steps in kinds.items():
        cond = functools.reduce(lambda u, v: u | v, [c == st for st in steps])

        @pl.when(cond)
        def _(n_sets=n_sets, first=first, last=last):
            run(n_sets, first, last)


def _ffn_call(x, mod, seq_len, layer, final, norm_ffn, ffn_up, ffn_conv_w, ffn_conv_b, ffn_down, final_norm):
    n_tok = x.shape[0]
    nb = n_tok // ROWS
    nc = D_FF // FF_CHUNK
    n_steps = pl.cdiv(nc, FF_SETS)
    per_seq_mod = mod.shape[0] > 1
    conv_b = ffn_conv_b.reshape(DEPTH, 1, 2 * D_FF)
    in_specs = [
        pl.BlockSpec((ROWS, D_MODEL), lambda i, c: (i, 0)),
        pl.BlockSpec((1, N_MOD, D_MODEL), (lambda i, c: (i, 0, 0)) if per_seq_mod else (lambda i, c: (0, 0, 0))),
        pl.BlockSpec((None, 1, D_MODEL), lambda i, c: (layer, 0, 0)),
    ]
    args = [x, mod, norm_ffn.reshape(DEPTH, 1, D_MODEL)]
    for s in range(FF_SETS):
        chunk = lambda c, s=s: jnp.minimum(c + s * n_steps, nc - 1)
        in_specs += [
            pl.BlockSpec((None, D_MODEL, FF_CHUNK), lambda i, c, f=chunk: (layer, 0, f(c))),
            pl.BlockSpec((None, D_MODEL, FF_CHUNK), lambda i, c, f=chunk: (layer, 0, nc + f(c))),
            pl.BlockSpec((None, 3, FF_CHUNK), lambda i, c, f=chunk: (layer, 0, f(c))),
            pl.BlockSpec((None, 3, FF_CHUNK), lambda i, c, f=chunk: (layer, 0, nc + f(c))),
            pl.BlockSpec((None, 1, FF_CHUNK), lambda i, c, f=chunk: (layer, 0, f(c))),
            pl.BlockSpec((None, 1, FF_CHUNK), lambda i, c, f=chunk: (layer, 0, nc + f(c))),
            pl.BlockSpec((None, FF_CHUNK, D_MODEL), lambda i, c, f=chunk: (layer, f(c), 0)),
        ]
        args += [ffn_up, ffn_up, ffn_conv_w, ffn_conv_w, conv_b, conv_b, ffn_down]
    in_specs.append(pl.BlockSpec((1, D_MODEL), lambda i, c: (0, 0)))
    args.append(final_norm.reshape(1, D_MODEL))
    return pl.pallas_call(
        functools.partial(_ffn_kernel, seq_len=seq_len, final=final),
        grid=(nb, n_steps),
        in_specs=in_specs,
        out_specs=pl.BlockSpec((ROWS, D_MODEL), lambda i, c: (i, 0)),
        out_shape=jax.ShapeDtypeStruct((n_tok, D_MODEL), F32),
        scratch_shapes=[pltpu.VMEM((ROWS, D_MODEL), BF16), pltpu.VMEM((ROWS, D_MODEL), F32)],
        compiler_params=pltpu.CompilerParams(
            dimension_semantics=("arbitrary", "arbitrary"), vmem_limit_bytes=VMEM_FFN
        ),
        name=f"ffn_l{layer}_t{seq_len}",
    )(*args)


def kernel(x_prompt, x_sample, c, state_hgrn, c_ctx, norm_mix, norm_ffn, w_ada, b_ada, w_in, lb_logits, hg_norm,
           w_branch_hg, w_branch_sg, w_branch_pool, w_out, sg_norm, sg_w, sg_b, pool_w, pool_scale, ffn_up,
           ffn_conv_w, ffn_conv_b, ffn_down, final_norm):
    assert w_in.shape == (DEPTH, D_MODEL, IN_COLS)
    n_ctx, t_ctx, _ = x_prompt.shape
    n_lat, t_lat, _ = x_sample.shape

    n_cond = 1 + n_lat
    pad = -n_cond % V7X_SUBLANES
    cvec = jnp.concatenate([c_ctx[None, :], c, jnp.zeros((pad, D_MODEL), F32)], axis=0)
    mod = _mod_call(cvec, w_ada, b_ada).reshape(DEPTH, n_cond + pad, N_MOD, D_MODEL)

    xs = _addpos_call(x_sample, _grid_pos_embed(t_lat)).reshape(n_lat * t_lat, D_MODEL)
    xp = x_prompt.reshape(n_ctx * t_ctx, D_MODEL)
    state0 = state_hgrn.reshape(n_lat * DEPTH * 2 * HG_HEADS, HG_DK, HG_DV)

    states = None
    for layer in range(DEPTH):
        final = layer == DEPTH - 1
        groups = []
        for x, m, t, s0 in ((xp, mod[layer, 0:1], t_ctx, None), (xs, mod[layer, 1:n_cond], t_lat, state0)):
            phg, s_fin = _hgrn_call(x, m, t, layer, norm_mix, w_in, lb_logits, hg_norm, w_branch_hg, s0, states)
            x1 = _mix_call(x, phg, m, t, layer, norm_mix, w_in, sg_norm, sg_w, sg_b, w_branch_sg, w_branch_pool,
                           pool_w, pool_scale, w_out)
            x2 = _ffn_call(x1, m, t, layer, final, norm_ffn, ffn_up, ffn_conv_w, ffn_conv_b, ffn_down, final_norm)
            groups.append((x2, s_fin))
        (xp, states), (xs, _) = groups

    y_prompt = xp.reshape(x_prompt.shape)
    y_sample = xs.reshape(x_sample.shape)
    return (y_prompt, y_sample, states)
```

```python
import functools

import jax
import jax.numpy as jnp
import numpy as np
from jax import lax
from jax.experimental import pallas as pl
from jax.experimental.pallas import tpu as pltpu

D_MODEL = 1024
DEPTH = 2
GRID_W = 64
POS_BASE = 10000.0
EPS = 1e-6
HG_HEADS = 4
HG_DK = 128
HG_DV = 128
HG_WIDTH = HG_HEADS * HG_DV
SG_GROUPS = 4
SG_WIDTH = 512
SG_GROUP_DIM = SG_WIDTH // SG_GROUPS
SG_CHUNK = 128
POOL_WINDOWS = (2, 4, 8, 16)
POOL_WIDTH = 512
POOL_GROUP_DIM = POOL_WIDTH // len(POOL_WINDOWS)
IN_COLS = 5 * HG_WIDTH + 2 * SG_WIDTH + POOL_WIDTH + 3 * D_MODEL
D_FF = 2816
N_MOD = 6

V7X_SUBLANES = 8
V7X_MXU_DIM = 256
MIB = 2**20

ROWS = 1024
PART_ROWS = 512
HG_BLOCK = 256
HG_LEVELS = 8
IN_CHUNK = 512
HG_GATE_COL = 5 * HG_WIDTH + 2 * SG_WIDTH + POOL_WIDTH
GATE_CHUNK = D_MODEL // HG_HEADS
FF_CHUNK = V7X_MXU_DIM
FF_SETS = 3
MOD_CHUNK = 3072
VMEM_SMALL = 32 * MIB
VMEM_HGRN = 48 * MIB
VMEM_MIX = 56 * MIB
VMEM_FFN = 52 * MIB

LOG2E = 1.4426950408889634

F32 = jnp.float32
BF16 = jnp.bfloat16


def _dot(a, b):
    return lax.dot_general(a, b, (((1,), (0,)), ((), ())), preferred_element_type=F32)


def _dot_nt(a, b):
    return lax.dot_general(a, b, (((1,), (1,)), ((), ())), preferred_element_type=F32)


def _dot_tn(a, b):
    return lax.dot_general(a, b, (((0,), (0,)), ((), ())), preferred_element_type=F32)


def _sigmoid(x):
    return 0.5 * jnp.tanh(0.5 * x) + 0.5


def _silu(x):
    h = 0.5 * x
    return h * jnp.tanh(h) + h


_GELU_C1 = 0.7978845608028654
_GELU_C2 = _GELU_C1 * 0.044715


def _gelu_tanh(x):
    half_x = 0.5 * x
    return half_x + half_x * jnp.tanh(x * (_GELU_C1 + _GELU_C2 * (x * x)))


def _rms(x, gain):
    return x * lax.rsqrt(jnp.mean(x * x, axis=-1, keepdims=True) + EPS) * gain


def _log1pexp(y):
    return jnp.maximum(y, 0.0) + jnp.log(1.0 + jnp.exp(-jnp.abs(y)))


def _norm_mod(x, gain, scale, shift):
    return (_rms(x, gain) * (1.0 + scale) + shift).astype(BF16)


def _part_rows(seq_len):
    return max(seq_len, PART_ROWS)


def _mod_kernel(c_ref, w_ref, b_ref, o_ref):
    c = _silu(c_ref[...]).astype(BF16)
    o_ref[...] = _dot(c, w_ref[...].astype(BF16)) + b_ref[...]


def _mod_call(cvec, w_ada, b_ada):
    n_rows = cvec.shape[0]
    n_cols = N_MOD * D_MODEL
    return pl.pallas_call(
        _mod_kernel,
        grid=(DEPTH, n_cols // MOD_CHUNK),
        in_specs=[
            pl.BlockSpec((n_rows, D_MODEL), lambda l, n: (0, 0)),
            pl.BlockSpec((None, D_MODEL, MOD_CHUNK), lambda l, n: (l, 0, n)),
            pl.BlockSpec((None, 1, MOD_CHUNK), lambda l, n: (l, 0, n)),
        ],
        out_specs=pl.BlockSpec((None, n_rows, MOD_CHUNK), lambda l, n: (l, 0, n)),
        out_shape=jax.ShapeDtypeStruct((DEPTH, n_rows, n_cols), F32),
        compiler_params=pltpu.CompilerParams(
            dimension_semantics=("arbitrary", "arbitrary"), vmem_limit_bytes=VMEM_SMALL
        ),
        name="adaln_mod",
    )(cvec, w_ada, b_ada.reshape(DEPTH, 1, n_cols))


def _addpos_kernel(x_ref, p_ref, o_ref):
    o_ref[...] = x_ref[...] + p_ref[...]


def _addpos_call(x, pos):
    b, t, d = x.shape
    return pl.pallas_call(
        _addpos_kernel,
        grid=(b,),
        in_specs=[pl.BlockSpec((None, t, d), lambda i: (i, 0, 0)), pl.BlockSpec((t, d), lambda i: (0, 0))],
        out_specs=pl.BlockSpec((None, t, d), lambda i: (i, 0, 0)),
        out_shape=jax.ShapeDtypeStruct(x.shape, x.dtype),
        compiler_params=pltpu.CompilerParams(dimension_semantics=("arbitrary",), vmem_limit_bytes=VMEM_SMALL),
        name="add_pos",
    )(x, pos)


def _grid_pos_embed(n_tokens):
    rows = n_tokens // GRID_W
    r = np.broadcast_to(np.arange(rows, dtype=np.float32)[:, None], (rows, GRID_W)).reshape(-1)
    col = np.broadcast_to(np.arange(GRID_W, dtype=np.float32)[None, :], (rows, GRID_W)).reshape(-1)
    quarter = D_MODEL // 4
    omega = (1.0 / (np.float32(POS_BASE) ** (np.arange(quarter, dtype=np.float32) / quarter))).astype(np.float32)
    ar = r[:, None] * omega[None, :]
    ac = col[:, None] * omega[None, :]
    return jnp.asarray(np.concatenate([np.sin(ar), np.cos(ar), np.sin(ac), np.cos(ac)], axis=-1), F32)


def _ref_rows(b, blk, r):
    n, c = b.shape
    if blk >= V7X_SUBLANES:
        x3 = b.reshape(n // blk, blk, c)
        return jnp.broadcast_to(x3[:, r : r + 1, :], x3.shape).reshape(n, c)
    x3 = b.reshape(n // V7X_SUBLANES, V7X_SUBLANES, c)
    sub = lax.broadcasted_iota(jnp.int32, x3.shape, 1)
    bases = list(range(0, V7X_SUBLANES, blk))
    out = jnp.broadcast_to(x3[:, bases[-1] + r : bases[-1] + r + 1, :], x3.shape)
    for base in reversed(bases[:-1]):
        out = jnp.where(sub < base + blk, jnp.broadcast_to(x3[:, base + r : base + r + 1, :], x3.shape), out)
    return out.reshape(n, c)


def _cum_logdecay(lf, tri):
    hi = lf.astype(BF16)
    r1 = lf - hi.astype(F32)
    mid = r1.astype(BF16)
    lo = (r1 - mid.astype(F32)).astype(BF16)
    return _dot(tri, hi) + _dot(tri, mid) + _dot(tri, lo)


def _hgrn_block(qs, ks, vs, bs, lvqs, sgn_ref, sts):
    half = HG_BLOCK // 2
    lo, hi = slice(0, half), slice(half, HG_BLOCK)
    dirs = (True, False)
    vbs = [v.astype(BF16) for v in vs]
    qbs = [q.astype(BF16) for q in qs]
    kbs = [k.astype(BF16) for k in ks]
    diags = [[jnp.where(lvq == 0, _dot_nt(qb[h], kb[h]), 0.0) for h in (lo, hi)] for qb, kb, lvq in zip(qbs, kbs, lvqs)]
    for m in range(1, HG_LEVELS):
        blk = 2**m
        for d, forward in enumerate(dirs):
            b = bs[d]
            ref = _ref_rows(b, blk, blk // 2 - 1 if forward else blk // 2)
            e = jnp.exp2((b - ref) * sgn_ref[d, m - 1]).astype(BF16)
            qt, kt = qbs[d] * e, kbs[d] * e
            diags[d] = [jnp.where(lvqs[d] == m, _dot_nt(qt[h], kt[h]), a) for h, a in zip((lo, hi), diags[d])]
    outs = []
    for d, forward in enumerate(dirs):
        b, k, st, q, vb = bs[d], ks[d], sts[d], qs[d], vbs[d]
        mid = half - 1 if forward else half
        e = jnp.exp2((b - b[mid : mid + 1, :]) * sgn_ref[d, HG_LEVELS - 1]).astype(BF16)
        qt, kt = qbs[d] * e, kbs[d] * e
        a_lo, a_hi = (a.astype(BF16) for a in diags[d])
        if forward:
            cross = _dot_nt(qt[hi], kt[lo]).astype(BF16)
            o = jnp.concatenate([_dot(a_lo, vb[lo]), _dot(cross, vb[lo]) + _dot(a_hi, vb[hi])], axis=0)
        else:
            cross = _dot_nt(qt[lo], kt[hi]).astype(BF16)
            o = jnp.concatenate([_dot(a_lo, vb[lo]) + _dot(cross, vb[hi]), _dot(a_hi, vb[hi])], axis=0)
        edge = b[HG_BLOCK - 1 : HG_BLOCK, :] if forward else b[0:1, :]
        k_end = (k * jnp.exp(edge - b)).astype(BF16)
        st_new = _dot_tn(vb, k_end)
        if st is not None:
            o = o + _dot_nt((q * jnp.exp(b)).astype(BF16), st.astype(BF16))
            st_new = st_new + st * jnp.exp(edge)
        outs.append((o, st_new))
    return outs


def _hgrn_kernel(*refs, layer, seq_len):
    carry = seq_len > HG_BLOCK
    it = iter(refs)
    x_ref, mod_ref, nmix_ref = next(it), next(it), next(it)
    w_refs = [next(it) for _ in range(5)]
    lbl_ref, hgn_ref, wbr_ref, wgate_ref = next(it), next(it), next(it), next(it)
    s0_refs = [next(it), next(it)] if carry else None
    sprev_ref = next(it) if (not carry and layer > 0) else None
    phg_ref = next(it)
    sout_ref = None if carry else next(it)
    hb, wcat, q_s, v_s, g_s, kf_s, kb_s, bf_s, bb_s, o_s, y_s, gate_s, lvf_s, lvb_s, trif_s, trib_s, sgn_s, st_s = it

    j = pl.program_id(1)

    def head(first):
        if first:
            t = lax.broadcasted_iota(jnp.int32, (HG_BLOCK // 2, HG_BLOCK // 2), 0)
            s = lax.broadcasted_iota(jnp.int32, (HG_BLOCK // 2, HG_BLOCK // 2), 1)
            x = t ^ s
            lv = jnp.zeros_like(x)
            for m in range(HG_LEVELS - 1):
                lv = lv + (x >= 2**m).astype(jnp.int32)
            lvf_s[...] = jnp.where(t >= s, lv, -1)
            lvb_s[...] = jnp.where(t <= s, lv, -1)
            t = lax.broadcasted_iota(jnp.int32, (HG_BLOCK, HG_BLOCK), 0)
            s = lax.broadcasted_iota(jnp.int32, (HG_BLOCK, HG_BLOCK), 1)
            trif_s[...] = (t >= s).astype(BF16)
            trib_s[...] = (t <= s).astype(BF16)
            row = lax.broadcasted_iota(jnp.int32, (HG_BLOCK, HG_DK), 0)
            for m in range(1, HG_LEVELS + 1):
                upper = (row & 2 ** (m - 1)) != 0
                sgn_s[0, m - 1] = jnp.where(upper, LOG2E, -LOG2E)
                sgn_s[1, m - 1] = jnp.where(upper, -LOG2E, LOG2E)
        for g, w_ref in enumerate(w_refs):
            wcat[:, g * HG_DK : (g + 1) * HG_DK] = w_ref[...].astype(BF16)
        a0, a1 = lbl_ref[0], lbl_ref[1]
        amax = jnp.maximum(a0, a1)
        e0, e1 = jnp.exp(a0 - amax), jnp.exp(a1 - amax)
        p0, p1 = e0 / (e0 + e1), e1 / (e0 + e1)
        lb = (p0 - p0) if layer == 0 else ((p0 + p1) - p0)
        log_lb = jnp.log(lb)

        wc = wcat[...]
        part = PART_ROWS
        n_parts = ROWS // part

        def zdot(p):
            rows = slice(p * part, (p + 1) * part)
            if first:
                hb[rows, :] = _norm_mod(x_ref[rows, :], nmix_ref[...], mod_ref[0, 1:2, :], mod_ref[0, 0:1, :])
            return _dot(hb[rows, :], wc)

        pending = zdot(0)
        for p in range(n_parts):
            z = pending
            if p + 1 < n_parts:
                pending = zdot(p + 1)
            rows = slice(p * part, (p + 1) * part)
            zq, zff, zfb, zi, zg = (z[:, g * HG_DK : (g + 1) * HG_DK] for g in range(5))
            q_s[rows, :] = _silu(zq) * HG_DK**-0.5
            v_s[rows, :] = zi
            g_s[rows, :] = _silu(zg)
            for d, (zf, k_s, b_s, tri_s) in enumerate(((zff, kf_s, bf_s, trif_s), (zfb, kb_s, bb_s, trib_s))):
                t = jnp.exp(-jnp.abs(zf))
                lf = _log1pexp(log_lb[d : d + 1, :] - zf) - (jnp.maximum(-zf, 0.0) + jnp.log(1.0 + t))
                k_s[rows, :] = (1.0 - lb[d : d + 1, :]) * (jnp.where(zf >= 0.0, t, 1.0) / (1.0 + t))
                tri = tri_s[...]
                for n in range(part // HG_BLOCK):
                    loc = slice(n * HG_BLOCK, (n + 1) * HG_BLOCK)
                    dst = slice(rows.start + n * HG_BLOCK, rows.start + (n + 1) * HG_BLOCK)
                    b_s[dst, :] = _cum_logdecay(lf[loc, :], tri)

        o_s[...] = jnp.zeros_like(o_s)
        n_blk = ROWS // HG_BLOCK
        if sprev_ref is not None:
            sout_ref[:, 0:layer] = sprev_ref[...]
        if carry:
            for d in range(2):
                st_s[d] = s0_refs[d][0].T

        def blocks(n, c):
            blks = (n, n_blk - 1 - n if carry else n)
            rows = [pl.ds(pl.multiple_of(blk * HG_BLOCK, HG_BLOCK), HG_BLOCK) for blk in blks]
            outs = _hgrn_block(
                [q_s[r, :] for r in rows], [kf_s[rows[0], :], kb_s[rows[1], :]], [v_s[r, :] for r in rows],
                [bf_s[rows[0], :], bb_s[rows[1], :]], [lvf_s[...], lvb_s[...]], sgn_s,
                [st_s[d] if carry else None for d in range(2)],
            )
            for d, (o, st_new) in enumerate(outs):
                o_s[rows[d], :] += o
                if carry:
                    st_s[d] = st_new
                else:
                    sout_ref[blks[d], layer, d, 0] = st_new.T
            return c

        lax.fori_loop(0, n_blk, blocks, 0, unroll=True)

        gate_s[j] = _sigmoid(_dot(hb[...], wgate_ref[...].astype(BF16)))
        y_s[j] = (_rms(o_s[...], hgn_ref[...]) * g_s[...]).astype(BF16)

    @pl.when(j == 0)
    def _():
        head(True)

    @pl.when(j > 0)
    def _():
        head(False)

    @pl.when(j == HG_HEADS - 1)
    def _():
        y = jnp.concatenate([y_s[h] for h in range(HG_HEADS)], axis=1)
        proj = _dot(y, wbr_ref[...].astype(BF16))
        for h in range(HG_HEADS):
            cols = slice(h * GATE_CHUNK, (h + 1) * GATE_CHUNK)
            phg_ref[:, cols] = gate_s[h] * proj[:, cols]


def _hgrn_call(x, mod, seq_len, layer, norm_mix, w_in, lb_logits, hg_norm, w_branch_hg, state0, prev_states):
    n_tok = x.shape[0]
    nb = n_tok // ROWS
    carry = seq_len > HG_BLOCK
    per_seq_mod = mod.shape[0] > 1

    in_specs = [
        pl.BlockSpec((ROWS, D_MODEL), lambda i, j: (i, 0)),
        pl.BlockSpec((1, N_MOD, D_MODEL), (lambda i, j: (i, 0, 0)) if per_seq_mod else (lambda i, j: (0, 0, 0))),
        pl.BlockSpec((None, 1, D_MODEL), lambda i, j: (layer, 0, 0)),
    ]
    args = [x, mod, norm_mix.reshape(DEPTH, 1, D_MODEL)]
    for g in range(5):
        in_specs.append(pl.BlockSpec((None, D_MODEL, HG_DK), lambda i, j, g=g: (layer, 0, g * HG_HEADS + j)))
        args.append(w_in)
    in_specs += [
        pl.BlockSpec((DEPTH, 2, HG_DK), lambda i, j: (0, 0, j)),
        pl.BlockSpec((None, 1, HG_DV), lambda i, j: (layer, 0, 0)),
        pl.BlockSpec((None, HG_WIDTH, D_MODEL), lambda i, j: (layer, 0, 0), pipeline_mode=pl.Buffered(1)),
        pl.BlockSpec((None, D_MODEL, GATE_CHUNK), lambda i, j: (layer, 0, HG_GATE_COL // GATE_CHUNK + j)),
    ]
    args += [lb_logits, hg_norm.reshape(DEPTH, 1, HG_DV), w_branch_hg, w_in]
    if carry:
        assert seq_len == ROWS
        for d in range(2):
            in_specs.append(
                pl.BlockSpec((1, HG_DK, HG_DV), lambda i, j, d=d: (((i * DEPTH + layer) * 2 + d) * HG_HEADS + j, 0, 0))
            )
            args.append(state0)

    out_shape = [jax.ShapeDtypeStruct((n_tok, D_MODEL), F32)]
    out_specs = [pl.BlockSpec((ROWS, D_MODEL), lambda i, j: (i, 0))]
    if not carry:
        assert seq_len == HG_BLOCK
        n_seq = n_tok // seq_len
        seqs = ROWS // seq_len
        if layer > 0:
            in_specs.append(pl.BlockSpec((seqs, layer, 2, 1, HG_DK, HG_DV), lambda i, j: (i, 0, 0, j, 0, 0)))
            args.append(prev_states)
        out_shape.append(jax.ShapeDtypeStruct((n_seq, layer + 1, 2, HG_HEADS, HG_DK, HG_DV), F32))
        out_specs.append(pl.BlockSpec((seqs, layer + 1, 2, 1, HG_DK, HG_DV), lambda i, j: (i, 0, 0, j, 0, 0)))

    head = lambda dt=F32: pltpu.VMEM((ROWS, HG_DK), dt)
    scratch = [
        pltpu.VMEM((ROWS, D_MODEL), BF16),
        pltpu.VMEM((D_MODEL, 5 * HG_DK), BF16),
        head(), head(), head(),
        head(), head(), head(), head(),
        head(),
        pltpu.VMEM((HG_HEADS, ROWS, HG_DV), BF16),
        pltpu.VMEM((HG_HEADS, ROWS, GATE_CHUNK), F32),
        pltpu.VMEM((HG_BLOCK // 2, HG_BLOCK // 2), jnp.int32),
        pltpu.VMEM((HG_BLOCK // 2, HG_BLOCK // 2), jnp.int32),
        pltpu.VMEM((HG_BLOCK, HG_BLOCK), BF16),
        pltpu.VMEM((HG_BLOCK, HG_BLOCK), BF16),
        pltpu.VMEM((2, HG_LEVELS, HG_BLOCK, HG_DK), F32),
        pltpu.VMEM((2, HG_DV, HG_DK), F32),
    ]
    outs = pl.pallas_call(
        functools.partial(_hgrn_kernel, layer=layer, seq_len=seq_len),
        grid=(nb, HG_HEADS),
        in_specs=in_specs,
        out_specs=out_specs,
        out_shape=out_shape,
        scratch_shapes=scratch,
        compiler_params=pltpu.CompilerParams(
            dimension_semantics=("arbitrary", "arbitrary"), vmem_limit_bytes=VMEM_HGRN
        ),
        name=f"hgrn_l{layer}_t{seq_len}",
    )(*args)
    return (outs[0], None) if carry else (outs[0], outs[1])


_MIX_ORDER = (10, 11, 5, 6, 7, 12, 13)


def _mix_col(k):
    idx = 0
    for n, c in enumerate(_MIX_ORDER):
        idx = idx + jnp.where(k == n, c, 0)
    return idx


def _window_mean_minus_self(p, tpos, seq_len, w):
    n = p.shape[0]
    half = w // 2

    def shifted(x, j):
        valid = (tpos + j >= 0) & (tpos + j < seq_len)
        return jnp.where(valid, pltpu.roll(x, (-j) % n, 0), 0.0)

    ahead, behind, length = p, p, 1
    while length < half:
        ahead = ahead + shifted(ahead, length)
        behind = behind + shifted(behind, -length)
        length *= 2
    acc = ahead + shifted(behind, -1)
    cnt = jnp.minimum(tpos + half, seq_len) - jnp.maximum(tpos - half, 0)
    return acc / cnt.astype(F32) - p


def _mix_kernel(
    x_ref, mod_ref, nmix_ref, w_ref, sgn_ref, sgw_ref, sgb_ref, wbsg_ref, wbpool_ref, poolw_ref, pscale_ref,
    wout_ref, phg_ref, o_ref, hb, u_s, br_s, mrg_s, *, seq_len,
):
    k = pl.program_id(1)
    half = D_MODEL // 2

    @pl.when(k == 0)
    def _():
        hb[...] = _norm_mod(x_ref[...], nmix_ref[...], mod_ref[0, 1:2, :], mod_ref[0, 0:1, :])

    def for_z_parts(consume, part=PART_ROWS):
        w = w_ref[...].astype(BF16)
        n_parts = ROWS // part

        def zdot(p):
            return _dot(hb[p * part : (p + 1) * part, :], w)

        pending = zdot(0)
        for p in range(n_parts):
            z = pending
            if p + 1 < n_parts:
                pending = zdot(p + 1)
            consume(slice(p * part, (p + 1) * part), z)

    for step in (0, 1):

        @pl.when(k == step)
        def _(step=step):
            cols = slice(step * half, (step + 1) * half)

            def gate(rows, z):
                mrg_s[rows, cols] = _sigmoid(z)

            for_z_parts(gate)

    @pl.when(k == 2)
    def _():
        def store_u(rows, z):
            u_s[rows, :] = _gelu_tanh(z)

        for_z_parts(store_u)

    @pl.when(k == 3)
    def _():
        wbsg = wbsg_ref[...].astype(BF16)
        wgs = [sgw_ref[g].astype(BF16) for g in range(SG_GROUPS)]

        def spatial_gating(rows, z):
            v = _rms(_gelu_tanh(z), sgn_ref[...]).astype(BF16)
            for g in range(SG_GROUPS):
                bias = sgb_ref[:, g : g + 1]
                cols = slice(g * SG_GROUP_DIM, (g + 1) * SG_GROUP_DIM)
                for n in range((rows.stop - rows.start) // SG_CHUNK):
                    loc = slice(n * SG_CHUNK, (n + 1) * SG_CHUNK)
                    dst = slice(rows.start + n * SG_CHUNK, rows.start + (n + 1) * SG_CHUNK)
                    mixed = _dot(wgs[g], v[loc, cols]) + bias
                    br_s[dst, cols] = (u_s[dst, cols] * mixed).astype(BF16)
            mrg_s[rows, :] = mrg_s[rows, :] * _dot(br_s[rows, :], wbsg)

        for_z_parts(spatial_gating)

    @pl.when(k == 4)
    def _():
        part = _part_rows(seq_len)
        tpos = lax.broadcasted_iota(jnp.int32, (part, POOL_GROUP_DIM), 0) & (seq_len - 1)

        def pool(rows, z):
            for gi, w in enumerate(POOL_WINDOWS):
                cols = slice(gi * POOL_GROUP_DIM, (gi + 1) * POOL_GROUP_DIM)
                pooled = _window_mean_minus_self(z[:, cols], tpos, seq_len, w)
                out = _dot(pooled.astype(BF16), poolw_ref[gi].astype(BF16)) * pscale_ref[:, cols]
                br_s[rows, cols] = out.astype(BF16)

        for_z_parts(pool, part)

    @pl.when(k == 5)
    def _():
        cols = slice(0, half)
        wbpool = wbpool_ref[:, cols].astype(BF16)

        def gate(rows, z):
            mrg_s[rows, cols] = mrg_s[rows, cols] + _sigmoid(z) * _dot(br_s[rows, :], wbpool)

        for_z_parts(gate)

    @pl.when(k == 6)
    def _():
        cols = slice(half, D_MODEL)
        wbpool = wbpool_ref[:, cols].astype(BF16)
        wout = wout_ref[...].astype(BF16)

        def gate_and_project(rows, z):
            mrg_s[rows, cols] = mrg_s[rows, cols] + _sigmoid(z) * _dot(br_s[rows, :], wbpool)
            merged = mrg_s[rows, :] + phg_ref[rows, :]
            y = _dot(merged.astype(BF16), wout)
            o_ref[rows, :] = x_ref[rows, :] + mod_ref[0, 2:3, :] * y

        for_z_parts(gate_and_project)


def _mix_call(x, phg, mod, seq_len, layer, norm_mix, w_in, sg_norm, sg_w, sg_b, w_branch_sg, w_branch_pool, pool_w,
              pool_scale, w_out):
    n_tok = x.shape[0]
    nb = n_tok // ROWS
    per_seq_mod = mod.shape[0] > 1
    const = pl.Buffered(1)
    assert seq_len & (seq_len - 1) == 0 and ROWS % seq_len == 0 and seq_len % SG_CHUNK == 0
    in_specs = [
        pl.BlockSpec((ROWS, D_MODEL), lambda i, k: (i, 0)),
        pl.BlockSpec((1, N_MOD, D_MODEL), (lambda i, k: (i, 0, 0)) if per_seq_mod else (lambda i, k: (0, 0, 0))),
        pl.BlockSpec((None, 1, D_MODEL), lambda i, k: (layer, 0, 0)),
        pl.BlockSpec((None, D_MODEL, IN_CHUNK), lambda i, k: (layer, 0, _mix_col(k))),
        pl.BlockSpec((None, 1, SG_WIDTH), lambda i, k: (layer, 0, 0)),
        pl.BlockSpec((None, SG_GROUPS, SG_CHUNK, SG_CHUNK), lambda i, k: (layer, 0, 0, 0)),
        pl.BlockSpec((None, SG_CHUNK, SG_GROUPS), lambda i, k: (layer, 0, 0)),
        pl.BlockSpec((None, SG_WIDTH, D_MODEL), lambda i, k: (layer, 0, 0), pipeline_mode=const),
        pl.BlockSpec((None, POOL_WIDTH, D_MODEL), lambda i, k: (layer, 0, 0), pipeline_mode=const),
        pl.BlockSpec((None, len(POOL_WINDOWS), POOL_GROUP_DIM, POOL_GROUP_DIM), lambda i, k: (layer, 0, 0, 0)),
        pl.BlockSpec((None, 1, POOL_WIDTH), lambda i, k: (layer, 0, 0)),
        pl.BlockSpec((None, D_MODEL, D_MODEL), lambda i, k: (layer, 0, 0), pipeline_mode=const),
        pl.BlockSpec((ROWS, D_MODEL), lambda i, k: (i, 0)),
    ]
    args = [
        x, mod, norm_mix.reshape(DEPTH, 1, D_MODEL), w_in, sg_norm.reshape(DEPTH, 1, SG_WIDTH), sg_w,
        jnp.swapaxes(sg_b, 1, 2), w_branch_sg, w_branch_pool, pool_w, pool_scale.reshape(DEPTH, 1, POOL_WIDTH),
        w_out, phg,
    ]
    scratch = [
        pltpu.VMEM((ROWS, D_MODEL), BF16),
        pltpu.VMEM((ROWS, SG_WIDTH), F32),
        pltpu.VMEM((ROWS, SG_WIDTH), BF16),
        pltpu.VMEM((ROWS, D_MODEL), F32),
    ]
    return pl.pallas_call(
        functools.partial(_mix_kernel, seq_len=seq_len),
        grid=(nb, len(_MIX_ORDER)),
        in_specs=in_specs,
        out_specs=pl.BlockSpec((ROWS, D_MODEL), lambda i, k: (i, 0)),
        out_shape=jax.ShapeDtypeStruct((n_tok, D_MODEL), F32),
        scratch_shapes=scratch,
        compiler_params=pltpu.CompilerParams(
            dimension_semantics=("arbitrary", "arbitrary"), vmem_limit_bytes=VMEM_MIX
        ),
        name=f"mix_l{layer}_t{seq_len}",
    )(*args)


def _ffn_kernel(x_ref, mod_ref, nffn_ref, *refs, seq_len, final):
    sets = [refs[7 * s : 7 * s + 7] for s in range(FF_SETS)]
    fin_ref, o_ref, hb, acc = refs[7 * FF_SETS :]
    c = pl.program_id(1)
    n_chunks = D_FF // FF_CHUNK
    n_steps = pl.cdiv(n_chunks, FF_SETS)

    part = ROWS
    tpos = lax.broadcasted_iota(jnp.int32, (part, FF_CHUNK), 0) & (seq_len - 1)
    has_prev = tpos >= 1
    has_next = tpos < seq_len - 1

    def conv(h, cw_ref, cb_ref):
        prev = jnp.where(has_prev, pltpu.roll(h, 1, 0), 0.0)
        nxt = jnp.where(has_next, pltpu.roll(h, part - 1, 0), 0.0)
        return prev * cw_ref[0:1, :] + h * cw_ref[1:2, :] + nxt * cw_ref[2:3, :] + cb_ref[...]

    def run(n_sets, first, last):
        ws = [(wa[...].astype(BF16), wb[...].astype(BF16), wd[...].astype(BF16)) for wa, wb, _, _, _, _, wd in sets]
        items = [(p, s) for p in range(ROWS // part) for s in range(n_sets)]

        def up(item):
            p, s = item
            rows = slice(p * part, (p + 1) * part)
            if first and s == 0:
                hb[rows, :] = _norm_mod(x_ref[rows, :], nffn_ref[...], mod_ref[0, 4:5, :], mod_ref[0, 3:4, :])
            h = hb[rows, :]
            return _dot(h, ws[s][0]), _dot(h, ws[s][1])

        pending = up(items[0])
        down = None
        for i, (p, s) in enumerate(items):
            ha, hb2 = pending
            if i + 1 < len(items):
                pending = up(items[i + 1])
            _, _, cwa_ref, cwb_ref, cba_ref, cbb_ref, _ = sets[s]
            a = conv(ha, cwa_ref, cba_ref)
            b = conv(hb2, cwb_ref, cbb_ref)
            d = _dot((_silu(a) * b).astype(BF16), ws[s][2])
            down = d if down is None else down + d
            if s == n_sets - 1:
                rows = slice(p * part, (p + 1) * part)
                total = down if first else acc[rows, :] + down
                if last:
                    y = x_ref[rows, :] + mod_ref[0, 5:6, :] * total
                    o_ref[rows, :] = _rms(y, fin_ref[...]) if final else y
                else:
                    acc[rows, :] = total
                down = None

    kinds = {}
    for step in range(n_steps):
        n_sets = len([s for s in range(FF_SETS) if step + s * n_steps < n_chunks])
        kinds.setdefault((n_sets, step == 0, step == n_steps - 1), []).append(step)
    for (n_sets, first, last), steps in kinds.items():
        cond = functools.reduce(lambda u, v: u | v, [c == st for st in steps])

        @pl.when(cond)
        def _(n_sets=n_sets, first=first, last=last):
            run(n_sets, first, last)


def _ffn_call(x, mod, seq_len, layer, final, norm_ffn, ffn_up, ffn_conv_w, ffn_conv_b, ffn_down, final_norm):
    n_tok = x.shape[0]
    nb = n_tok // ROWS
    nc = D_FF // FF_CHUNK
    n_steps = pl.cdiv(nc, FF_SETS)
    per_seq_mod = mod.shape[0] > 1
    conv_b = ffn_conv_b.reshape(DEPTH, 1, 2 * D_FF)
    in_specs = [
        pl.BlockSpec((ROWS, D_MODEL), lambda i, c: (i, 0)),
        pl.BlockSpec((1, N_MOD, D_MODEL), (lambda i, c: (i, 0, 0)) if per_seq_mod else (lambda i, c: (0, 0, 0))),
        pl.BlockSpec((None, 1, D_MODEL), lambda i, c: (layer, 0, 0)),
    ]
    args = [x, mod, norm_ffn.reshape(DEPTH, 1, D_MODEL)]
    for s in range(FF_SETS):
        chunk = lambda c, s=s: jnp.minimum(c + s * n_steps, nc - 1)
        in_specs += [
            pl.BlockSpec((None, D_MODEL, FF_CHUNK), lambda i, c, f=chunk: (layer, 0, f(c))),
            pl.BlockSpec((None, D_MODEL, FF_CHUNK), lambda i, c, f=chunk: (layer, 0, nc + f(c))),
            pl.BlockSpec((None, 3, FF_CHUNK), lambda i, c, f=chunk: (layer, 0, f(c))),
            pl.BlockSpec((None, 3, FF_CHUNK), lambda i, c, f=chunk: (layer, 0, nc + f(c))),
            pl.BlockSpec((None, 1, FF_CHUNK), lambda i, c, f=chunk: (layer, 0, f(c))),
            pl.BlockSpec((None, 1, FF_CHUNK), lambda i, c, f=chunk: (layer, 0, nc + f(c))),
            pl.BlockSpec((None, FF_CHUNK, D_MODEL), lambda i, c, f=chunk: (layer, f(c), 0)),
        ]
        args += [ffn_up, ffn_up, ffn_conv_w, ffn_conv_w, conv_b, conv_b, ffn_down]
    in_specs.append(pl.BlockSpec((1, D_MODEL), lambda i, c: (0, 0)))
    args.append(final_norm.reshape(1, D_MODEL))
    return pl.pallas_call(
        functools.partial(_ffn_kernel, seq_len=seq_len, final=final),
        grid=(nb, n_steps),
        in_specs=in_specs,
        out_specs=pl.BlockSpec((ROWS, D_MODEL), lambda i, c: (i, 0)),
        out_shape=jax.ShapeDtypeStruct((n_tok, D_MODEL), F32),
        scratch_shapes=[pltpu.VMEM((ROWS, D_MODEL), BF16), pltpu.VMEM((ROWS, D_MODEL), F32)],
        compiler_params=pltpu.CompilerParams(
            dimension_semantics=("arbitrary", "arbitrary"), vmem_limit_bytes=VMEM_FFN
        ),
        name=f"ffn_l{layer}_t{seq_len}",
    )(*args)


def kernel(x_prompt, x_sample, c, state_hgrn, c_ctx, norm_mix, norm_ffn, w_ada, b_ada, w_in, lb_logits, hg_norm,
           w_branch_hg, w_branch_sg, w_branch_pool, w_out, sg_norm, sg_w, sg_b, pool_w, pool_scale, ffn_up,
           ffn_conv_w, ffn_conv_b, ffn_down, final_norm):
    assert w_in.shape == (DEPTH, D_MODEL, IN_COLS)
    n_ctx, t_ctx, _ = x_prompt.shape
    n_lat, t_lat, _ = x_sample.shape

    n_cond = 1 + n_lat
    pad = -n_cond % V7X_SUBLANES
    cvec = jnp.concatenate([c_ctx[None, :], c, jnp.zeros((pad, D_MODEL), F32)], axis=0)
    mod = _mod_call(cvec, w_ada, b_ada).reshape(DEPTH, n_cond + pad, N_MOD, D_MODEL)

    xs = _addpos_call(x_sample, _grid_pos_embed(t_lat)).reshape(n_lat * t_lat, D_MODEL)
    xp = x_prompt.reshape(n_ctx * t_ctx, D_MODEL)
    state0 = state_hgrn.reshape(n_lat * DEPTH * 2 * HG_HEADS, HG_DK, HG_DV)

    states = None
    for layer in range(DEPTH):
        final = layer == DEPTH - 1
        groups = []
        for x, m, t, s0 in ((xp, mod[layer, 0:1], t_ctx, None), (xs, mod[layer, 1:n_cond], t_lat, state0)):
            phg, s_fin = _hgrn_call(x, m, t, layer, norm_mix, w_in, lb_logits, hg_norm, w_branch_hg, s0, states)
            x1 = _mix_call(x, phg, m, t, layer, norm_mix, w_in, sg_norm, sg_w, sg_b, w_branch_sg, w_branch_pool,
                           pool_w, pool_scale, w_out)
            x2 = _ffn_call(x1, m, t, layer, final, norm_ffn, ffn_up, ffn_conv_w, ffn_conv_b, ffn_down, final_norm)
            groups.append((x2, s_fin))
        (xp, states), (xs, _) = groups

    y_prompt = xp.reshape(x_prompt.shape)
    y_sample = xs.reshape(x_sample.shape)
    return (y_prompt, y_sample, states)
```

```python
import functools

import jax
import jax.numpy as jnp
import numpy as np
from jax import lax
from jax.experimental import pallas as pl
from jax.experimental.pallas import tpu as pltpu

D_MODEL = 1024
DEPTH = 2
GRID_W = 64
POS_BASE = 10000.0
EPS = 1e-6
HG_HEADS = 4
HG_DK = 128
HG_DV = 128
HG_WIDTH = HG_HEADS * HG_DV
SG_GROUPS = 4
SG_WIDTH = 512
SG_GROUP_DIM = SG_WIDTH // SG_GROUPS
SG_CHUNK = 128
POOL_WINDOWS = (2, 4, 8, 16)
POOL_WIDTH = 512
POOL_GROUP_DIM = POOL_WIDTH // len(POOL_WINDOWS)
IN_COLS = 5 * HG_WIDTH + 2 * SG_WIDTH + POOL_WIDTH + 3 * D_MODEL
D_FF = 2816
N_MOD = 6

V7X_SUBLANES = 8
V7X_MXU_DIM = 256
MIB = 2**20

ROWS = 1024
PART_ROWS = 512
HG_BLOCK = 256
HG_LEVELS = 8
IN_CHUNK = 512
HG_GATE_COL = 5 * HG_WIDTH + 2 * SG_WIDTH + POOL_WIDTH
GATE_CHUNK = D_MODEL // HG_HEADS
FF_CHUNK = V7X_MXU_DIM
FF_SETS = 3
MOD_CHUNK = 3072
VMEM_SMALL = 32 * MIB
VMEM_HGRN = 52 * MIB
VMEM_MIX = 56 * MIB
VMEM_FFN = 52 * MIB

LOG2E = 1.4426950408889634

F32 = jnp.float32
BF16 = jnp.bfloat16


def _dot(a, b):
    return lax.dot_general(a, b, (((1,), (0,)), ((), ())), preferred_element_type=F32)


def _dot_nt(a, b):
    return lax.dot_general(a, b, (((1,), (1,)), ((), ())), preferred_element_type=F32)


def _dot_tn(a, b):
    return lax.dot_general(a, b, (((0,), (0,)), ((), ())), preferred_element_type=F32)


def _sigmoid(x):
    return 0.5 * jnp.tanh(0.5 * x) + 0.5


def _silu(x):
    h = 0.5 * x
    return h * jnp.tanh(h) + h


_GELU_C1 = 0.7978845608028654
_GELU_C2 = _GELU_C1 * 0.044715


def _gelu_tanh(x):
    half_x = 0.5 * x
    return half_x + half_x * jnp.tanh(x * (_GELU_C1 + _GELU_C2 * (x * x)))


def _rms(x, gain):
    return x * lax.rsqrt(jnp.mean(x * x, axis=-1, keepdims=True) + EPS) * gain


def _log1pexp(y):
    return jnp.maximum(y, 0.0) + jnp.log(1.0 + jnp.exp(-jnp.abs(y)))


def _norm_mod(x, gain, scale, shift):
    return (_rms(x, gain) * (1.0 + scale) + shift).astype(BF16)


def _part_rows(seq_len):
    return max(seq_len, PART_ROWS)


def _mod_kernel(c_ref, w_ref, b_ref, o_ref):
    c = _silu(c_ref[...]).astype(BF16)
    o_ref[...] = _dot(c, w_ref[...].astype(BF16)) + b_ref[...]


def _mod_call(cvec, w_ada, b_ada):
    n_rows = cvec.shape[0]
    n_cols = N_MOD * D_MODEL
    return pl.pallas_call(
        _mod_kernel,
        grid=(DEPTH, n_cols // MOD_CHUNK),
        in_specs=[
            pl.BlockSpec((n_rows, D_MODEL), lambda l, n: (0, 0)),
            pl.BlockSpec((None, D_MODEL, MOD_CHUNK), lambda l, n: (l, 0, n)),
            pl.BlockSpec((None, 1, MOD_CHUNK), lambda l, n: (l, 0, n)),
        ],
        out_specs=pl.BlockSpec((None, n_rows, MOD_CHUNK), lambda l, n: (l, 0, n)),
        out_shape=jax.ShapeDtypeStruct((DEPTH, n_rows, n_cols), F32),
        compiler_params=pltpu.CompilerParams(
            dimension_semantics=("arbitrary", "arbitrary"), vmem_limit_bytes=VMEM_SMALL
        ),
        name="adaln_mod",
    )(cvec, w_ada, b_ada.reshape(DEPTH, 1, n_cols))


def _addpos_kernel(x_ref, p_ref, o_ref):
    o_ref[...] = x_ref[...] + p_ref[...]


def _addpos_call(x, pos):
    b, t, d = x.shape
    return pl.pallas_call(
        _addpos_kernel,
        grid=(b,),
        in_specs=[pl.BlockSpec((None, t, d), lambda i: (i, 0, 0)), pl.BlockSpec((t, d), lambda i: (0, 0))],
        out_specs=pl.BlockSpec((None, t, d), lambda i: (i, 0, 0)),
        out_shape=jax.ShapeDtypeStruct(x.shape, x.dtype),
        compiler_params=pltpu.CompilerParams(dimension_semantics=("arbitrary",), vmem_limit_bytes=VMEM_SMALL),
        name="add_pos",
    )(x, pos)


def _grid_pos_embed(n_tokens):
    rows = n_tokens // GRID_W
    r = np.broadcast_to(np.arange(rows, dtype=np.float32)[:, None], (rows, GRID_W)).reshape(-1)
    col = np.broadcast_to(np.arange(GRID_W, dtype=np.float32)[None, :], (rows, GRID_W)).reshape(-1)
    quarter = D_MODEL // 4
    omega = (1.0 / (np.float32(POS_BASE) ** (np.arange(quarter, dtype=np.float32) / quarter))).astype(np.float32)
    ar = r[:, None] * omega[None, :]
    ac = col[:, None] * omega[None, :]
    return jnp.asarray(np.concatenate([np.sin(ar), np.cos(ar), np.sin(ac), np.cos(ac)], axis=-1), F32)


def _ref_rows(b, blk, r):
    n, c = b.shape
    if blk >= V7X_SUBLANES:
        x3 = b.reshape(n // blk, blk, c)
        return jnp.broadcast_to(x3[:, r : r + 1, :], x3.shape).reshape(n, c)
    x3 = b.reshape(n // V7X_SUBLANES, V7X_SUBLANES, c)
    sub = lax.broadcasted_iota(jnp.int32, x3.shape, 1)
    bases = list(range(0, V7X_SUBLANES, blk))
    out = jnp.broadcast_to(x3[:, bases[-1] + r : bases[-1] + r + 1, :], x3.shape)
    for base in reversed(bases[:-1]):
        out = jnp.where(sub < base + blk, jnp.broadcast_to(x3[:, base + r : base + r + 1, :], x3.shape), out)
    return out.reshape(n, c)


def _cum_logdecay(lf, tri):
    hi = lf.astype(BF16)
    r1 = lf - hi.astype(F32)
    mid = r1.astype(BF16)
    lo = (r1 - mid.astype(F32)).astype(BF16)
    return _dot(tri, hi) + _dot(tri, mid) + _dot(tri, lo)


def _hgrn_block(qs, ks, vs, bs, lvqs, sgn_ref, sts):
    half = HG_BLOCK // 2
    lo, hi = slice(0, half), slice(half, HG_BLOCK)
    dirs = (True, False)
    vbs = [v.astype(BF16) for v in vs]
    qbs = [q.astype(BF16) for q in qs]
    kbs = [k.astype(BF16) for k in ks]
    diags = [[jnp.where(lvq == 0, _dot_nt(qb[h], kb[h]), 0.0) for h in (lo, hi)] for qb, kb, lvq in zip(qbs, kbs, lvqs)]
    for m in range(1, HG_LEVELS):
        blk = 2**m
        for d, forward in enumerate(dirs):
            b = bs[d]
            ref = _ref_rows(b, blk, blk // 2 - 1 if forward else blk // 2)
            e = jnp.exp2((b - ref) * sgn_ref[d, m - 1]).astype(BF16)
            qt, kt = qbs[d] * e, kbs[d] * e
            diags[d] = [jnp.where(lvqs[d] == m, _dot_nt(qt[h], kt[h]), a) for h, a in zip((lo, hi), diags[d])]
    outs = []
    for d, forward in enumerate(dirs):
        b, k, st, q, vb = bs[d], ks[d], sts[d], qs[d], vbs[d]
        mid = half - 1 if forward else half
        e = jnp.exp2((b - b[mid : mid + 1, :]) * sgn_ref[d, HG_LEVELS - 1]).astype(BF16)
        qt, kt = qbs[d] * e, kbs[d] * e
        a_lo, a_hi = (a.astype(BF16) for a in diags[d])
        if forward:
            cross = _dot_nt(qt[hi], kt[lo]).astype(BF16)
            o = jnp.concatenate([_dot(a_lo, vb[lo]), _dot(cross, vb[lo]) + _dot(a_hi, vb[hi])], axis=0)
        else:
            cross = _dot_nt(qt[lo], kt[hi]).astype(BF16)
            o = jnp.concatenate([_dot(a_lo, vb[lo]) + _dot(cross, vb[hi]), _dot(a_hi, vb[hi])], axis=0)
        edge = b[HG_BLOCK - 1 : HG_BLOCK, :] if forward else b[0:1, :]
        k_end = (k * jnp.exp(edge - b)).astype(BF16)
        st_new = _dot_tn(vb, k_end)
        if st is not None:
            o = o + _dot_nt((q * jnp.exp(b)).astype(BF16), st.astype(BF16))
            st_new = st_new + st * jnp.exp(edge)
        outs.append((o, st_new))
    return outs


def _hgrn_kernel(*refs, layer, seq_len):
    carry = seq_len > HG_BLOCK
    it = iter(refs)
    x_ref, mod_ref, nmix_ref = next(it), next(it), next(it)
    w_refs = [next(it) for _ in range(5)]
    lbl_ref, hgn_ref, wbr_ref, wgate_ref, wu_ref = next(it), next(it), next(it), next(it), next(it)
    s0_refs = [next(it), next(it)] if carry else None
    sprev_ref = next(it) if (not carry and layer > 0) else None
    phg_ref, uo_ref = next(it), next(it)
    sout_ref = None if carry else next(it)
    hb, wcat, q_s, v_s, g_s, kf_s, kb_s, bf_s, bb_s, o_s, y_s, gate_s, u_sc, lvf_s, lvb_s, trif_s, trib_s, sgn_s, st_s = it

    j = pl.program_id(1)

    def head(first):
        if first:
            t = lax.broadcasted_iota(jnp.int32, (HG_BLOCK // 2, HG_BLOCK // 2), 0)
            s = lax.broadcasted_iota(jnp.int32, (HG_BLOCK // 2, HG_BLOCK // 2), 1)
            x = t ^ s
            lv = jnp.zeros_like(x)
            for m in range(HG_LEVELS - 1):
                lv = lv + (x >= 2**m).astype(jnp.int32)
            lvf_s[...] = jnp.where(t >= s, lv, -1)
            lvb_s[...] = jnp.where(t <= s, lv, -1)
            t = lax.broadcasted_iota(jnp.int32, (HG_BLOCK, HG_BLOCK), 0)
            s = lax.broadcasted_iota(jnp.int32, (HG_BLOCK, HG_BLOCK), 1)
            trif_s[...] = (t >= s).astype(BF16)
            trib_s[...] = (t <= s).astype(BF16)
            row = lax.broadcasted_iota(jnp.int32, (HG_BLOCK, HG_DK), 0)
            for m in range(1, HG_LEVELS + 1):
                upper = (row & 2 ** (m - 1)) != 0
                sgn_s[0, m - 1] = jnp.where(upper, LOG2E, -LOG2E)
                sgn_s[1, m - 1] = jnp.where(upper, -LOG2E, LOG2E)
        for g, w_ref in enumerate(w_refs):
            wcat[:, g * HG_DK : (g + 1) * HG_DK] = w_ref[...].astype(BF16)
        a0, a1 = lbl_ref[0], lbl_ref[1]
        amax = jnp.maximum(a0, a1)
        e0, e1 = jnp.exp(a0 - amax), jnp.exp(a1 - amax)
        p0, p1 = e0 / (e0 + e1), e1 / (e0 + e1)
        lb = (p0 - p0) if layer == 0 else ((p0 + p1) - p0)
        log_lb = jnp.log(lb)

        wc = wcat[...]
        part = PART_ROWS
        n_parts = ROWS // part

        def zdot(p):
            rows = slice(p * part, (p + 1) * part)
            if first:
                hb[rows, :] = _norm_mod(x_ref[rows, :], nmix_ref[...], mod_ref[0, 1:2, :], mod_ref[0, 0:1, :])
            return _dot(hb[rows, :], wc)

        pending = zdot(0)
        for p in range(n_parts):
            z = pending
            if p + 1 < n_parts:
                pending = zdot(p + 1)
            rows = slice(p * part, (p + 1) * part)
            zq, zff, zfb, zi, zg = (z[:, g * HG_DK : (g + 1) * HG_DK] for g in range(5))
            q_s[rows, :] = _silu(zq) * HG_DK**-0.5
            v_s[rows, :] = zi
            g_s[rows, :] = _silu(zg)
            for d, (zf, k_s, b_s, tri_s) in enumerate(((zff, kf_s, bf_s, trif_s), (zfb, kb_s, bb_s, trib_s))):
                t = jnp.exp(-jnp.abs(zf))
                lf = _log1pexp(log_lb[d : d + 1, :] - zf) - (jnp.maximum(-zf, 0.0) + jnp.log(1.0 + t))
                k_s[rows, :] = (1.0 - lb[d : d + 1, :]) * (jnp.where(zf >= 0.0, t, 1.0) / (1.0 + t))
                tri = tri_s[...]
                for n in range(part // HG_BLOCK):
                    loc = slice(n * HG_BLOCK, (n + 1) * HG_BLOCK)
                    dst = slice(rows.start + n * HG_BLOCK, rows.start + (n + 1) * HG_BLOCK)
                    b_s[dst, :] = _cum_logdecay(lf[loc, :], tri)

        o_s[...] = jnp.zeros_like(o_s)
        n_blk = ROWS // HG_BLOCK
        if sprev_ref is not None:
            sout_ref[:, 0:layer] = sprev_ref[...]
        if carry:
            for d in range(2):
                st_s[d] = s0_refs[d][0].T

        def blocks(n, c):
            blks = (n, n_blk - 1 - n if carry else n)
            rows = [pl.ds(pl.multiple_of(blk * HG_BLOCK, HG_BLOCK), HG_BLOCK) for blk in blks]
            outs = _hgrn_block(
                [q_s[r, :] for r in rows], [kf_s[rows[0], :], kb_s[rows[1], :]], [v_s[r, :] for r in rows],
                [bf_s[rows[0], :], bb_s[rows[1], :]], [lvf_s[...], lvb_s[...]], sgn_s,
                [st_s[d] if carry else None for d in range(2)],
            )
            for d, (o, st_new) in enumerate(outs):
                o_s[rows[d], :] += o
                if carry:
                    st_s[d] = st_new
                else:
                    sout_ref[blks[d], layer, d, 0] = st_new.T
            return c

        lax.fori_loop(0, n_blk, blocks, 0, unroll=True)

        gate_s[j] = _sigmoid(_dot(hb[...], wgate_ref[...].astype(BF16)))
        u_sc[j] = _gelu_tanh(_dot(hb[...], wu_ref[...].astype(BF16)))
        y_s[j] = (_rms(o_s[...], hgn_ref[...]) * g_s[...]).astype(BF16)

    @pl.when(j == 0)
    def _():
        head(True)

    @pl.when(j > 0)
    def _():
        head(False)

    @pl.when(j == HG_HEADS - 1)
    def _():
        y = jnp.concatenate([y_s[h] for h in range(HG_HEADS)], axis=1)
        proj = _dot(y, wbr_ref[...].astype(BF16))
        for h in range(HG_HEADS):
            cols = slice(h * GATE_CHUNK, (h + 1) * GATE_CHUNK)
            phg_ref[:, cols] = gate_s[h] * proj[:, cols]
            uo_ref[:, h * HG_DK : (h + 1) * HG_DK] = u_sc[h]


def _hgrn_call(x, mod, seq_len, layer, norm_mix, w_in, lb_logits, hg_norm, w_branch_hg, state0, prev_states):
    n_tok = x.shape[0]
    nb = n_tok // ROWS
    carry = seq_len > HG_BLOCK
    per_seq_mod = mod.shape[0] > 1

    in_specs = [
        pl.BlockSpec((ROWS, D_MODEL), lambda i, j: (i, 0)),
        pl.BlockSpec((1, N_MOD, D_MODEL), (lambda i, j: (i, 0, 0)) if per_seq_mod else (lambda i, j: (0, 0, 0))),
        pl.BlockSpec((None, 1, D_MODEL), lambda i, j: (layer, 0, 0)),
    ]
    args = [x, mod, norm_mix.reshape(DEPTH, 1, D_MODEL)]
    for g in range(5):
        in_specs.append(pl.BlockSpec((None, D_MODEL, HG_DK), lambda i, j, g=g: (layer, 0, g * HG_HEADS + j)))
        args.append(w_in)
    in_specs += [
        pl.BlockSpec((DEPTH, 2, HG_DK), lambda i, j: (0, 0, j)),
        pl.BlockSpec((None, 1, HG_DV), lambda i, j: (layer, 0, 0)),
        pl.BlockSpec((None, HG_WIDTH, D_MODEL), lambda i, j: (layer, 0, 0), pipeline_mode=pl.Buffered(1)),
        pl.BlockSpec((None, D_MODEL, GATE_CHUNK), lambda i, j: (layer, 0, HG_GATE_COL // GATE_CHUNK + j)),
        pl.BlockSpec((None, D_MODEL, HG_DK), lambda i, j: (layer, 0, 5 * HG_HEADS + j)),
    ]
    args += [lb_logits, hg_norm.reshape(DEPTH, 1, HG_DV), w_branch_hg, w_in, w_in]
    if carry:
        assert seq_len == ROWS
        for d in range(2):
            in_specs.append(
                pl.BlockSpec((1, HG_DK, HG_DV), lambda i, j, d=d: (((i * DEPTH + layer) * 2 + d) * HG_HEADS + j, 0, 0))
            )
            args.append(state0)

    out_shape = [jax.ShapeDtypeStruct((n_tok, D_MODEL), F32), jax.ShapeDtypeStruct((n_tok, SG_WIDTH), F32)]
    out_specs = [pl.BlockSpec((ROWS, D_MODEL), lambda i, j: (i, 0)), pl.BlockSpec((ROWS, SG_WIDTH), lambda i, j: (i, 0))]
    if not carry:
        assert seq_len == HG_BLOCK
        n_seq = n_tok // seq_len
        seqs = ROWS // seq_len
        if layer > 0:
            in_specs.append(pl.BlockSpec((seqs, layer, 2, 1, HG_DK, HG_DV), lambda i, j: (i, 0, 0, j, 0, 0)))
            args.append(prev_states)
        out_shape.append(jax.ShapeDtypeStruct((n_seq, layer + 1, 2, HG_HEADS, HG_DK, HG_DV), F32))
        out_specs.append(pl.BlockSpec((seqs, layer + 1, 2, 1, HG_DK, HG_DV), lambda i, j: (i, 0, 0, j, 0, 0)))

    head = lambda dt=F32: pltpu.VMEM((ROWS, HG_DK), dt)
    scratch = [
        pltpu.VMEM((ROWS, D_MODEL), BF16),
        pltpu.VMEM((D_MODEL, 5 * HG_DK), BF16),
        head(), head(), head(),
        head(), head(), head(), head(),
        head(),
        pltpu.VMEM((HG_HEADS, ROWS, HG_DV), BF16),
        pltpu.VMEM((HG_HEADS, ROWS, GATE_CHUNK), F32),
        pltpu.VMEM((HG_HEADS, ROWS, HG_DK), F32),
        pltpu.VMEM((HG_BLOCK // 2, HG_BLOCK // 2), jnp.int32),
        pltpu.VMEM((HG_BLOCK // 2, HG_BLOCK // 2), jnp.int32),
        pltpu.VMEM((HG_BLOCK, HG_BLOCK), BF16),
        pltpu.VMEM((HG_BLOCK, HG_BLOCK), BF16),
        pltpu.VMEM((2, HG_LEVELS, HG_BLOCK, HG_DK), F32),
        pltpu.VMEM((2, HG_DV, HG_DK), F32),
    ]
    outs = pl.pallas_call(
        functools.partial(_hgrn_kernel, layer=layer, seq_len=seq_len),
        grid=(nb, HG_HEADS),
        in_specs=in_specs,
        out_specs=out_specs,
        out_shape=out_shape,
        scratch_shapes=scratch,
        compiler_params=pltpu.CompilerParams(
            dimension_semantics=("arbitrary", "arbitrary"), vmem_limit_bytes=VMEM_HGRN
        ),
        name=f"hgrn_l{layer}_t{seq_len}",
    )(*args)
    return (outs[0], outs[1], None) if carry else tuple(outs)


_MIX_ORDER = (10, 11, 6, 7, 12, 13)


def _mix_col(k):
    idx = 0
    for n, c in enumerate(_MIX_ORDER):
        idx = idx + jnp.where(k == n, c, 0)
    return idx


def _window_mean_minus_self(p, tpos, seq_len, w):
    n = p.shape[0]
    half = w // 2

    def shifted(x, j):
        valid = (tpos + j >= 0) & (tpos + j < seq_len)
        return jnp.where(valid, pltpu.roll(x, (-j) % n, 0), 0.0)

    ahead, behind, length = p, p, 1
    while length < half:
        ahead = ahead + shifted(ahead, length)
        behind = behind + shifted(behind, -length)
        length *= 2
    acc = ahead + shifted(behind, -1)
    cnt = jnp.minimum(tpos + half, seq_len) - jnp.maximum(tpos - half, 0)
    return acc / cnt.astype(F32) - p


def _mix_kernel(
    x_ref, mod_ref, nmix_ref, w_ref, sgn_ref, sgw_ref, sgb_ref, wbsg_ref, wbpool_ref, poolw_ref, pscale_ref,
    wout_ref, phg_ref, u_s, o_ref, hb, br_s, mrg_s, *, seq_len,
):
    k = pl.program_id(1)
    half = D_MODEL // 2

    @pl.when(k == 0)
    def _():
        hb[...] = _norm_mod(x_ref[...], nmix_ref[...], mod_ref[0, 1:2, :], mod_ref[0, 0:1, :])

    def for_z_parts(consume, part=PART_ROWS):
        w = w_ref[...].astype(BF16)
        n_parts = ROWS // part

        def zdot(p):
            return _dot(hb[p * part : (p + 1) * part, :], w)

        pending = zdot(0)
        for p in range(n_parts):
            z = pending
            if p + 1 < n_parts:
                pending = zdot(p + 1)
            consume(slice(p * part, (p + 1) * part), z)

    for step in (0, 1):

        @pl.when(k == step)
        def _(step=step):
            cols = slice(step * half, (step + 1) * half)

            def gate(rows, z):
                mrg_s[rows, cols] = _sigmoid(z)

            for_z_parts(gate)

    @pl.when(k == 2)
    def _():
        wbsg = wbsg_ref[...].astype(BF16)
        wgs = [sgw_ref[g].astype(BF16) for g in range(SG_GROUPS)]

        def spatial_gating(rows, z):
            v = _rms(_gelu_tanh(z), sgn_ref[...]).astype(BF16)
            for g in range(SG_GROUPS):
                bias = sgb_ref[:, g : g + 1]
                cols = slice(g * SG_GROUP_DIM, (g + 1) * SG_GROUP_DIM)
                for n in range((rows.stop - rows.start) // SG_CHUNK):
                    loc = slice(n * SG_CHUNK, (n + 1) * SG_CHUNK)
                    dst = slice(rows.start + n * SG_CHUNK, rows.start + (n + 1) * SG_CHUNK)
                    mixed = _dot(wgs[g], v[loc, cols]) + bias
                    br_s[dst, cols] = (u_s[dst, cols] * mixed).astype(BF16)
            mrg_s[rows, :] = mrg_s[rows, :] * _dot(br_s[rows, :], wbsg)

        for_z_parts(spatial_gating)

    @pl.when(k == 3)
    def _():
        part = _part_rows(seq_len)
        tpos = lax.broadcasted_iota(jnp.int32, (part, POOL_GROUP_DIM), 0) & (seq_len - 1)

        def pool(rows, z):
            for gi, w in enumerate(POOL_WINDOWS):
                cols = slice(gi * POOL_GROUP_DIM, (gi + 1) * POOL_GROUP_DIM)
                pooled = _window_mean_minus_self(z[:, cols], tpos, seq_len, w)
                out = _dot(pooled.astype(BF16), poolw_ref[gi].astype(BF16)) * pscale_ref[:, cols]
                br_s[rows, cols] = out.astype(BF16)

        for_z_parts(pool, part)

    @pl.when(k == 4)
    def _():
        cols = slice(0, half)
        wbpool = wbpool_ref[:, cols].astype(BF16)

        def gate(rows, z):
            mrg_s[rows, cols] = mrg_s[rows, cols] + _sigmoid(z) * _dot(br_s[rows, :], wbpool)

        for_z_parts(gate)

    @pl.when(k == 5)
    def _():
        cols = slice(half, D_MODEL)
        wbpool = wbpool_ref[:, cols].astype(BF16)
        wout = wout_ref[...].astype(BF16)

        def gate_and_project(rows, z):
            mrg_s[rows, cols] = mrg_s[rows, cols] + _sigmoid(z) * _dot(br_s[rows, :], wbpool)
            merged = mrg_s[rows, :] + phg_ref[rows, :]
            y = _dot(merged.astype(BF16), wout)
            o_ref[rows, :] = x_ref[rows, :] + mod_ref[0, 2:3, :] * y

        for_z_parts(gate_and_project)


def _mix_call(x, phg, u, mod, seq_len, layer, norm_mix, w_in, sg_norm, sg_w, sg_b, w_branch_sg, w_branch_pool, pool_w,
              pool_scale, w_out):
    n_tok = x.shape[0]
    nb = n_tok // ROWS
    per_seq_mod = mod.shape[0] > 1
    const = pl.Buffered(1)
    assert seq_len & (seq_len - 1) == 0 and ROWS % seq_len == 0 and seq_len % SG_CHUNK == 0
    in_specs = [
        pl.BlockSpec((ROWS, D_MODEL), lambda i, k: (i, 0)),
        pl.BlockSpec((1, N_MOD, D_MODEL), (lambda i, k: (i, 0, 0)) if per_seq_mod else (lambda i, k: (0, 0, 0))),
        pl.BlockSpec((None, 1, D_MODEL), lambda i, k: (layer, 0, 0)),
        pl.BlockSpec((None, D_MODEL, IN_CHUNK), lambda i, k: (layer, 0, _mix_col(k))),
        pl.BlockSpec((None, 1, SG_WIDTH), lambda i, k: (layer, 0, 0)),
        pl.BlockSpec((None, SG_GROUPS, SG_CHUNK, SG_CHUNK), lambda i, k: (layer, 0, 0, 0)),
        pl.BlockSpec((None, SG_CHUNK, SG_GROUPS), lambda i, k: (layer, 0, 0)),
        pl.BlockSpec((None, SG_WIDTH, D_MODEL), lambda i, k: (layer, 0, 0), pipeline_mode=const),
        pl.BlockSpec((None, POOL_WIDTH, D_MODEL), lambda i, k: (layer, 0, 0), pipeline_mode=const),
        pl.BlockSpec((None, len(POOL_WINDOWS), POOL_GROUP_DIM, POOL_GROUP_DIM), lambda i, k: (layer, 0, 0, 0)),
        pl.BlockSpec((None, 1, POOL_WIDTH), lambda i, k: (layer, 0, 0)),
        pl.BlockSpec((None, D_MODEL, D_MODEL), lambda i, k: (layer, 0, 0), pipeline_mode=const),
        pl.BlockSpec((ROWS, D_MODEL), lambda i, k: (i, 0)),
        pl.BlockSpec((ROWS, SG_WIDTH), lambda i, k: (i, 0)),
    ]
    args = [
        x, mod, norm_mix.reshape(DEPTH, 1, D_MODEL), w_in, sg_norm.reshape(DEPTH, 1, SG_WIDTH), sg_w,
        jnp.swapaxes(sg_b, 1, 2), w_branch_sg, w_branch_pool, pool_w, pool_scale.reshape(DEPTH, 1, POOL_WIDTH),
        w_out, phg, u,
    ]
    scratch = [
        pltpu.VMEM((ROWS, D_MODEL), BF16),
        pltpu.VMEM((ROWS, SG_WIDTH), BF16),
        pltpu.VMEM((ROWS, D_MODEL), F32),
    ]
    return pl.pallas_call(
        functools.partial(_mix_kernel, seq_len=seq_len),
        grid=(nb, len(_MIX_ORDER)),
        in_specs=in_specs,
        out_specs=pl.BlockSpec((ROWS, D_MODEL), lambda i, k: (i, 0)),
        out_shape=jax.ShapeDtypeStruct((n_tok, D_MODEL), F32),
        scratch_shapes=scratch,
        compiler_params=pltpu.CompilerParams(
            dimension_semantics=("arbitrary", "arbitrary"), vmem_limit_bytes=VMEM_MIX
        ),
        name=f"mix_l{layer}_t{seq_len}",
    )(*args)


def _ffn_kernel(x_ref, mod_ref, nffn_ref, *refs, seq_len, final):
    sets = [refs[7 * s : 7 * s + 7] for s in range(FF_SETS)]
    fin_ref, o_ref, hb, acc = refs[7 * FF_SETS :]
    c = pl.program_id(1)
    n_chunks = D_FF // FF_CHUNK
    n_steps = pl.cdiv(n_chunks, FF_SETS)

    part = ROWS
    tpos = lax.broadcasted_iota(jnp.int32, (part, FF_CHUNK), 0) & (seq_len - 1)
    has_prev = tpos >= 1
    has_next = tpos < seq_len - 1

    def conv(h, cw_ref, cb_ref):
        prev = jnp.where(has_prev, pltpu.roll(h, 1, 0), 0.0)
        nxt = jnp.where(has_next, pltpu.roll(h, part - 1, 0), 0.0)
        return prev * cw_ref[0:1, :] + h * cw_ref[1:2, :] + nxt * cw_ref[2:3, :] + cb_ref[...]

    def run(n_sets, first, last):
        ws = [(wa[...].astype(BF16), wb[...].astype(BF16), wd[...].astype(BF16)) for wa, wb, _, _, _, _, wd in sets]
        items = [(p, s) for p in range(ROWS // part) for s in range(n_sets)]

        def up(item):
            p, s = item
            rows = slice(p * part, (p + 1) * part)
            if first and s == 0:
                hb[rows, :] = _norm_mod(x_ref[rows, :], nffn_ref[...], mod_ref[0, 4:5, :], mod_ref[0, 3:4, :])
            h = hb[rows, :]
            return _dot(h, ws[s][0]), _dot(h, ws[s][1])

        pending = up(items[0])
        down = None
        for i, (p, s) in enumerate(items):
            ha, hb2 = pending
            if i + 1 < len(items):
                pending = up(items[i + 1])
            _, _, cwa_ref, cwb_ref, cba_ref, cbb_ref, _ = sets[s]
            a = conv(ha, cwa_ref, cba_ref)
            b = conv(hb2, cwb_ref, cbb_ref)
            d = _dot((_silu(a) * b).astype(BF16), ws[s][2])
            down = d if down is None else down + d
            if s == n_sets - 1:
                rows = slice(p * part, (p + 1) * part)
                total = down if first else acc[rows, :] + down
                if last:
                    y = x_ref[rows, :] + mod_ref[0, 5:6, :] * total
                    o_ref[rows, :] = _rms(y, fin_ref[...]) if final else y
                else:
                    acc[rows, :] = total
                down = None

    kinds = {}
    for step in range(n_steps):
        n_sets = len([s for s in range(FF_SETS) if step + s * n_steps < n_chunks])
        kinds.setdefault((n_sets, step == 0, step == n_steps - 1), []).append(step)
    for (n_sets, first, last), steps in kinds.items():
        cond = functools.reduce(lambda u, v: u | v, [c == st for st in steps])

        @pl.when(cond)
        def _(n_sets=n_sets, first=first, last=last):
            run(n_sets, first, last)


def _ffn_call(x, mod, seq_len, layer, final, norm_ffn, ffn_up, ffn_conv_w, ffn_conv_b, ffn_down, final_norm):
    n_tok = x.shape[0]
    nb = n_tok // ROWS
    nc = D_FF // FF_CHUNK
    n_steps = pl.cdiv(nc, FF_SETS)
    per_seq_mod = mod.shape[0] > 1
    conv_b = ffn_conv_b.reshape(DEPTH, 1, 2 * D_FF)
    in_specs = [
        pl.BlockSpec((ROWS, D_MODEL), lambda i, c: (i, 0)),
        pl.BlockSpec((1, N_MOD, D_MODEL), (lambda i, c: (i, 0, 0)) if per_seq_mod else (lambda i, c: (0, 0, 0))),
        pl.BlockSpec((None, 1, D_MODEL), lambda i, c: (layer, 0, 0)),
    ]
    args = [x, mod, norm_ffn.reshape(DEPTH, 1, D_MODEL)]
    for s in range(FF_SETS):
        chunk = lambda c, s=s: jnp.minimum(c + s * n_steps, nc - 1)
        in_specs += [
            pl.BlockSpec((None, D_MODEL, FF_CHUNK), lambda i, c, f=chunk: (layer, 0, f(c))),
            pl.BlockSpec((None, D_MODEL, FF_CHUNK), lambda i, c, f=chunk: (layer, 0, nc + f(c))),
            pl.BlockSpec((None, 3, FF_CHUNK), lambda i, c, f=chunk: (layer, 0, f(c))),
            pl.BlockSpec((None, 3, FF_CHUNK), lambda i, c, f=chunk: (layer, 0, nc + f(c))),
            pl.BlockSpec((None, 1, FF_CHUNK), lambda i, c, f=chunk: (layer, 0, f(c))),
            pl.BlockSpec((None, 1, FF_CHUNK), lambda i, c, f=chunk: (layer, 0, nc + f(c))),
            pl.BlockSpec((None, FF_CHUNK, D_MODEL), lambda i, c, f=chunk: (layer, f(c), 0)),
        ]
        args += [ffn_up, ffn_up, ffn_conv_w, ffn_conv_w, conv_b, conv_b, ffn_down]
    in_specs.append(pl.BlockSpec((1, D_MODEL), lambda i, c: (0, 0)))
    args.append(final_norm.reshape(1, D_MODEL))
    return pl.pallas_call(
        functools.partial(_ffn_kernel, seq_len=seq_len, final=final),
        grid=(nb, n_steps),
        in_specs=in_specs,
        out_specs=pl.BlockSpec((ROWS, D_MODEL), lambda i, c: (i, 0)),
        out_shape=jax.ShapeDtypeStruct((n_tok, D_MODEL), F32),
        scratch_shapes=[pltpu.VMEM((ROWS, D_MODEL), BF16), pltpu.VMEM((ROWS, D_MODEL), F32)],
        compiler_params=pltpu.CompilerParams(
            dimension_semantics=("arbitrary", "arbitrary"), vmem_limit_bytes=VMEM_FFN
        ),
        name=f"ffn_l{layer}_t{seq_len}",
    )(*args)


def kernel(x_prompt, x_sample, c, state_hgrn, c_ctx, norm_mix, norm_ffn, w_ada, b_ada, w_in, lb_logits, hg_norm,
           w_branch_hg, w_branch_sg, w_branch_pool, w_out, sg_norm, sg_w, sg_b, pool_w, pool_scale, ffn_up,
           ffn_conv_w, ffn_conv_b, ffn_down, final_norm):
    assert w_in.shape == (DEPTH, D_MODEL, IN_COLS)
    n_ctx, t_ctx, _ = x_prompt.shape
    n_lat, t_lat, _ = x_sample.shape

    n_cond = 1 + n_lat
    pad = -n_cond % V7X_SUBLANES
    cvec = jnp.concatenate([c_ctx[None, :], c, jnp.zeros((pad, D_MODEL), F32)], axis=0)
    mod = _mod_call(cvec, w_ada, b_ada).reshape(DEPTH, n_cond + pad, N_MOD, D_MODEL)

    xs = _addpos_call(x_sample, _grid_pos_embed(t_lat)).reshape(n_lat * t_lat, D_MODEL)
    xp = x_prompt.reshape(n_ctx * t_ctx, D_MODEL)
    state0 = state_hgrn.reshape(n_lat * DEPTH * 2 * HG_HEADS, HG_DK, HG_DV)

    states = None
    for layer in range(DEPTH):
        final = layer == DEPTH - 1
        groups = []
        for x, m, t, s0 in ((xp, mod[layer, 0:1], t_ctx, None), (xs, mod[layer, 1:n_cond], t_lat, state0)):
            phg, u, s_fin = _hgrn_call(x, m, t, layer, norm_mix, w_in, lb_logits, hg_norm, w_branch_hg, s0, states)
            x1 = _mix_call(x, phg, u, m, t, layer, norm_mix, w_in, sg_norm, sg_w, sg_b, w_branch_sg, w_branch_pool,
                           pool_w, pool_scale, w_out)
            x2 = _ffn_call(x1, m, t, layer, final, norm_ffn, ffn_up, ffn_conv_w, ffn_conv_b, ffn_down, final_norm)
            groups.append((x2, s_fin))
        (xp, states), (xs, _) = groups

    y_prompt = xp.reshape(x_prompt.shape)
    y_sample = xs.reshape(x_sample.shape)
    return (y_prompt, y_sample, states)
```
